```python
import jax, jax.numpy as jnp
from jax import lax
import numpy as np

D_MODEL = 1024
BATCH = 4
SEQ = 8192
DEPTH = 1

CTX_LEN = 256
GRID_W = 64
HG_DK = 128
HG_DV = 128
HG_HEADS = (D_MODEL // 2) // HG_DV
ML_DK = 128
ML_DV = 128
ML_HEADS = (D_MODEL // 2) // ML_DV
HG_KW = HG_HEADS * HG_DK
HG_VW = HG_HEADS * HG_DV
ML_KW = ML_HEADS * ML_DK
ML_VW = ML_HEADS * ML_DV
MIX_WIDTH = HG_VW + ML_VW
HG_CHUNK = 32
ML_CHUNK = 64
CONV_K = 3
N_EXPERTS = 16
EC_CAPACITY = 2
EXPERT_FF = 1024
NORM_EPS = 1e-6

HG_Q = 0
HG_I = HG_Q + HG_KW
HG_G = HG_I + HG_VW
HG_FF = HG_G + HG_VW
ML_Q = HG_FF + 2 * HG_KW
ML_V = ML_Q + 2 * ML_KW
ML_O = ML_V + ML_VW
ML_GATES = ML_O + ML_VW
PROJ_OUT = ML_GATES + 4 * ML_HEADS

kernel_name = "hgrn2_mlstm_ec_moe_diffusion_block"


def rms_norm(x, w):
    xf = x.astype(jnp.float32)
    y = xf * lax.rsqrt(jnp.mean(xf * xf, axis=-1, keepdims=True) + NORM_EPS)
    return (y * w).astype(x.dtype)


def _head_rms_norm(t, w, n_heads):
    B, T, W = t.shape
    tf = t.astype(jnp.float32).reshape(B, T, n_heads, -1)
    tf = tf * lax.rsqrt(jnp.mean(tf * tf, axis=-1, keepdims=True) + NORM_EPS)
    return (tf.reshape(B, T, W) * w).astype(t.dtype)


def _head_layer_norm(t, w, n_heads):
    B, T, W = t.shape
    tf = t.astype(jnp.float32).reshape(B, T, n_heads, -1)
    tf = tf - jnp.mean(tf, axis=-1, keepdims=True)
    tf = tf * lax.rsqrt(jnp.mean(tf * tf, axis=-1, keepdims=True) + NORM_EPS)
    return (tf.reshape(B, T, W) * w).astype(t.dtype)


def _to_heads(t, n_heads):
    B, T, _ = t.shape
    return t.reshape(B, T, n_heads, -1).transpose(0, 2, 1, 3)


def _from_heads(t):
    B, H, T, d = t.shape
    return t.transpose(0, 2, 1, 3).reshape(B, T, H * d)


def _to_chunks(t, L):
    B, H, T = t.shape[:3]
    return jnp.moveaxis(t.reshape(B, H, T // L, L, *t.shape[3:]), 2, 0)


def _from_chunks(t):
    N, B, H, L = t.shape[:4]
    return jnp.moveaxis(t, 0, 2).reshape(B, H, N * L, *t.shape[4:])


def hgrn2_scan(q, k, v, log_f, s0):
    L = HG_CHUNK
    mask = jnp.tril(jnp.ones((L, L), dtype=bool))
    f32 = jnp.float32

    def step(s, inp):
        qc, kc, vc, gc = inp
        a = jnp.cumsum(gc, axis=2)
        diff = a[:, :, :, None, :] - a[:, :, None, :, :]
        decay = jnp.exp(jnp.where(mask[:, :, None], diff, -jnp.inf))
        scores = jnp.einsum('bhid,bhjd,bhijd->bhij', qc, kc, decay)
        o = jnp.einsum('bhij,bhjv->bhiv', scores, vc) + jnp.einsum('bhid,bhdv->bhiv', qc * jnp.exp(a), s)
        a_last = a[:, :, -1:, :]
        s_new = jnp.exp(a_last[:, :, 0, :])[..., None] * s + jnp.einsum('bhjd,bhjv->bhdv', kc * jnp.exp(a_last - a), vc)
        return s_new, o

    xs = (_to_chunks(q.astype(f32), L), _to_chunks(k.astype(f32), L), _to_chunks(v.astype(f32), L), _to_chunks(log_f.astype(f32), L))
    s_fin, o = lax.scan(step, s0, xs)
    return _from_chunks(o).astype(v.dtype), s_fin


def mlstm_scan(q, k, v, log_i, log_f, state):
    L = ML_CHUNK
    mask = jnp.tril(jnp.ones((L, L), dtype=bool))
    f32 = jnp.float32

    def step(carry, inp):
        c, n, m = carry
        qc, kc, vc, ic, fc = inp
        b = jnp.cumsum(fc, axis=-1)
        d = jnp.where(mask, b[..., :, None] - b[..., None, :] + ic[..., None, :], -jnp.inf)
        inter = b + m[..., None]
        m_row = jnp.maximum(inter, jnp.max(d, axis=-1))
        w_inter = jnp.exp(inter - m_row)
        qk = jnp.einsum('bhid,bhjd->bhij', qc, kc) * jnp.exp(d - m_row[..., None])
        num = jnp.einsum('bhij,bhjv->bhiv', qk, vc) + w_inter[..., None] * jnp.einsum('bhid,bhdv->bhiv', qc, c)
        den = jnp.sum(qk, axis=-1) + w_inter * jnp.einsum('bhid,bhd->bhi', qc, n)
        h = num / jnp.maximum(jnp.abs(den), jnp.exp(-m_row))[..., None]
        m_new = m_row[..., -1]
        wk = jnp.exp(b[..., -1:] - b + ic - m_new[..., None])
        ws = jnp.exp(b[..., -1] + m - m_new)
        c_new = ws[..., None, None] * c + jnp.einsum('bhj,bhjd,bhjv->bhdv', wk, kc, vc)
        n_new = ws[..., None] * n + jnp.einsum('bhj,bhjd->bhd', wk, kc)
        return (c_new, n_new, m_new), h

    xs = (_to_chunks(q.astype(f32), L), _to_chunks(k.astype(f32), L), _to_chunks(v.astype(f32), L),
          _to_chunks(log_i.astype(f32), L), _to_chunks(log_f.astype(f32), L))
    st, h = lax.scan(step, state, xs)
    return _from_chunks(h).astype(v.dtype), st


def _short_conv(t, conv_w, conv_b, on_grid):
    B, T, C = t.shape
    w = conv_w.astype(t.dtype)
    if on_grid:
        rows = T // GRID_W
        y = lax.conv_general_dilated(t.reshape(B, rows, GRID_W, C), w[:, :, None, :], (1, 1), 'SAME',
                                     dimension_numbers=('NHWC', 'HWIO', 'NHWC'), feature_group_count=C)
        y = y.reshape(B, T, C)
    else:
        y = lax.conv_general_dilated(t, w[CONV_K // 2][:, None, :], (1,), 'SAME',
                                     dimension_numbers=('NWC', 'WIO', 'NWC'), feature_group_count=C)
    return jax.nn.silu(y + conv_b)


def _prep(u, w_in, conv_w, conv_b, lb, ml_gate_b, on_grid):
    B, T, _ = u.shape
    p = jnp.einsum('btd,dk->btk', u, w_in)
    f_logit = p[..., HG_FF:ML_Q].astype(jnp.float32).reshape(B, T, 2, HG_KW)
    f = lb + (1.0 - lb) * jax.nn.sigmoid(f_logit)
    qk = _short_conv(p[..., ML_Q:ML_V], conv_w, conv_b, on_grid)
    gates = (p[..., ML_GATES:].astype(jnp.float32) + ml_gate_b).reshape(B, T, 2, 2, ML_HEADS)
    gates = jnp.transpose(gates, (2, 3, 0, 4, 1))
    return dict(
        hg_q=_to_heads(jax.nn.silu(p[..., HG_Q:HG_I]), HG_HEADS),
        hg_v=_to_heads(p[..., HG_I:HG_G], HG_HEADS),
        hg_g=p[..., HG_G:HG_FF],
        hg_k=[_to_heads(1.0 - f[:, :, d], HG_HEADS) for d in range(2)],
        hg_logf=[_to_heads(jnp.log(f[:, :, d]), HG_HEADS) for d in range(2)],
        ml_q=_to_heads(qk[..., :ML_KW], ML_HEADS),
        ml_k=_to_heads(qk[..., ML_KW:], ML_HEADS) * (ML_DK ** -0.5),
        ml_v=_to_heads(p[..., ML_V:ML_O], ML_HEADS),
        ml_o=p[..., ML_O:ML_GATES],
        ml_logi=gates[:, 0],
        ml_logf=jax.nn.log_sigmoid(gates[:, 1]),
    )


def _scan_direction(pr, d, hg_s0, ml_s0):
    fl = (lambda a: jnp.flip(a, axis=2)) if d == 1 else (lambda a: a)
    hg_o, hg_s = hgrn2_scan(fl(pr['hg_q']), fl(pr['hg_k'][d]), fl(pr['hg_v']), fl(pr['hg_logf'][d]), hg_s0)
    ml_h, ml_s = mlstm_scan(fl(pr['ml_q']), fl(pr['ml_k']), fl(pr['ml_v']), fl(pr['ml_logi'][d]), fl(pr['ml_logf'][d]), ml_s0)
    return fl(hg_o), hg_s, fl(ml_h), ml_s


def _zero_states(B):
    f32 = jnp.float32
    hg = jnp.zeros((B, HG_HEADS, HG_DK, HG_DV), f32)
    ml = (jnp.zeros((B, ML_HEADS, ML_DK, ML_DV), f32), jnp.zeros((B, ML_HEADS, ML_DK), f32), jnp.zeros((B, ML_HEADS), f32))
    return hg, ml


def _mixer_out(pr, hg_o, ml_h, hg_norm_w, ml_norm_w, w_out):
    hg = _head_rms_norm(_from_heads(hg_o), hg_norm_w, HG_HEADS) * jax.nn.silu(pr['hg_g'])
    ml = _head_layer_norm(_from_heads(ml_h), ml_norm_w, ML_HEADS) * jax.nn.sigmoid(pr['ml_o'])
    return jnp.einsum('btk,kd->btd', jnp.concatenate([hg, ml], axis=-1), w_out)


def ec_moe(u, router_w, w_gate, w_up, w_down):
    B, n, D = u.shape
    cap = EC_CAPACITY * n // N_EXPERTS
    aff = jax.nn.softmax(jnp.einsum('bnd,de->bne', u, router_w).astype(jnp.float32), axis=-1)
    gate, idx = lax.top_k(jnp.swapaxes(aff, 1, 2), cap)
    xs = jax.vmap(lambda t, i: t[i])(u, idx)
    h = jax.nn.silu(jnp.einsum('becd,edf->becf', xs, w_gate)) * jnp.einsum('becd,edf->becf', xs, w_up)
    y = jnp.einsum('becf,efd->becd', h, w_down) * gate[..., None].astype(u.dtype)
    return jax.vmap(lambda ye, i: jnp.zeros((n, D), ye.dtype).at[i.reshape(-1)].add(ye.reshape(-1, D)))(y, idx)


def setup_inputs(seed: int = 0) -> dict:
    key = jax.random.key(seed)
    ks = jax.random.split(key, 24)
    f32 = jnp.float32
    nrm = lambda k, shape, s: s * jax.random.normal(k, shape, f32)
    D = D_MODEL
    gate_base = jnp.stack([jnp.zeros((ML_HEADS,), f32), jnp.linspace(3.0, 6.0, ML_HEADS, dtype=f32)])
    ml_gate_b = (jnp.broadcast_to(gate_base, (DEPTH, 2, 2, ML_HEADS)) + nrm(ks[10], (DEPTH, 2, 2, ML_HEADS), 0.1)).reshape(DEPTH, 4 * ML_HEADS)
    return {
        "x": nrm(ks[0], (BATCH, SEQ, D), 1.0),
        "c": nrm(ks[1], (BATCH, D), 1.0),
        "ctx": nrm(ks[2], (BATCH, CTX_LEN, D), 1.0),
        "c_ctx": nrm(ks[3], (D,), 1.0),
        "ada_w": nrm(ks[4], (DEPTH, D, 6 * D), 0.5 * D ** -0.5),
        "ada_b": nrm(ks[5], (DEPTH, 6 * D), 0.02),
        "norm1_w": 1.0 + nrm(ks[6], (DEPTH, D), 0.02),
        "w_in": nrm(ks[7], (DEPTH, D, PROJ_OUT), D ** -0.5),
        "conv_w": nrm(ks[8], (DEPTH, CONV_K, CONV_K, 2 * ML_KW), 1.0 / CONV_K),
        "conv_b": nrm(ks[9], (DEPTH, 2 * ML_KW), 0.02),
        "hg_lb_logits": nrm(ks[11], (DEPTH + 1, 2, HG_KW), 0.5),
        "ml_gate_b": ml_gate_b,
        "hg_norm_w": 1.0 + nrm(ks[12], (DEPTH, HG_VW), 0.02),
        "ml_norm_w": 1.0 + nrm(ks[13], (DEPTH, ML_VW), 0.02),
        "w_out": nrm(ks[14], (DEPTH, MIX_WIDTH, D), MIX_WIDTH ** -0.5),
        "norm2_w": 1.0 + nrm(ks[15], (DEPTH, D), 0.02),
        "router_w": nrm(ks[16], (DEPTH, D, N_EXPERTS), D ** -0.5),
        "exp_w_gate": nrm(ks[17], (DEPTH, N_EXPERTS, D, EXPERT_FF), D ** -0.5),
        "exp_w_up": nrm(ks[18], (DEPTH, N_EXPERTS, D, EXPERT_FF), D ** -0.5),
        "exp_w_down": nrm(ks[19], (DEPTH, N_EXPERTS, EXPERT_FF, D), EXPERT_FF ** -0.5),
        "final_norm_w": 1.0 + nrm(ks[20], (D,), 0.02),
    }


def reference(x, c, ctx, c_ctx, ada_w, ada_b, norm1_w, w_in, conv_w, conv_b, hg_lb_logits, ml_gate_b,
              hg_norm_w, ml_norm_w, w_out, norm2_w, router_w, exp_w_gate, exp_w_up, exp_w_down, final_norm_w):
    B = x.shape[0]
    lb_all = jnp.cumsum(jax.nn.softmax(hg_lb_logits.astype(jnp.float32), axis=0), axis=0)
    silu_c = jax.nn.silu(c)
    silu_cc = jax.nn.silu(c_ctx)
    for l in range(DEPTH):
        last = l == DEPTH - 1
        mod_x = jnp.split(silu_c @ ada_w[l] + ada_b[l], 6, axis=-1)
        mod_c = jnp.split(silu_cc @ ada_w[l] + ada_b[l], 6, axis=-1)
        sh1, sc1, g1, sh2, sc2, g2 = [m[:, None, :] for m in mod_x]
        csh1, csc1, cg1, csh2, csc2, cg2 = mod_c
        lb = lb_all[l]
        u_x = rms_norm(x, norm1_w[l]) * (1.0 + sc1) + sh1
        u_c = rms_norm(ctx, norm1_w[l]) * (1.0 + csc1) + csh1
        pr_c = _prep(u_c, w_in[l], conv_w[l], conv_b[l], lb, ml_gate_b[l], False)
        pr_x = _prep(u_x, w_in[l], conv_w[l], conv_b[l], lb, ml_gate_b[l], True)
        hg0, ml0 = _zero_states(B)
        c_hg_f, c_hs_f, c_ml_f, c_ms_f = _scan_direction(pr_c, 0, hg0, ml0)
        c_hg_b, c_hs_b, c_ml_b, c_ms_b = _scan_direction(pr_c, 1, hg0, ml0)
        x_hg_f, _, x_ml_f, _ = _scan_direction(pr_x, 0, c_hs_f, c_ms_f)
        x_hg_b, _, x_ml_b, _ = _scan_direction(pr_x, 1, c_hs_b, c_ms_b)
        x = x + g1 * _mixer_out(pr_x, x_hg_f + x_hg_b, x_ml_f + x_ml_b, hg_norm_w[l], ml_norm_w[l], w_out[l])
        if not last:
            ctx = ctx + cg1 * _mixer_out(pr_c, c_hg_f + c_hg_b, c_ml_f + c_ml_b, hg_norm_w[l], ml_norm_w[l], w_out[l])
        v_x = rms_norm(x, norm2_w[l]) * (1.0 + sc2) + sh2
        x = x + g2 * ec_moe(v_x, router_w[l], exp_w_gate[l], exp_w_up[l], exp_w_down[l])
        if not last:
            v_c = rms_norm(ctx, norm2_w[l]) * (1.0 + csc2) + csh2
            ctx = ctx + cg2 * ec_moe(v_c, router_w[l], exp_w_gate[l], exp_w_up[l], exp_w_down[l])
    return rms_norm(x, final_norm_w)
```

```python
import functools

import numpy as np
import jax
import jax.numpy as jnp
from jax import lax
from jax.experimental import pallas as pl
from jax.experimental.pallas import tpu as pltpu

F32 = jnp.float32
BF16 = jnp.bfloat16
I32 = jnp.int32
HIGHEST = lax.Precision.HIGHEST
NORM_EPS = 1e-6

HEAD_DIM = 128
N_HEADS = 4
HEADS_W = N_HEADS * HEAD_DIM
GRID_W = 64
N_EXPERTS = 16
EC_CAPACITY = 2
HG_CHUNK = 64
ML_CHUNK = 128
SCAN_STEP = 256
ONEHOT_BLK = 256
CONV_HALO = 72
TOPK_BISECTIONS = 64
VMEM_LIMIT = 56 * 1024 * 1024

_NT = (((1,), (1,)), ((), ()))
_TN = (((0,), (0,)), ((), ()))


def _dot(a, b, dims=None, precision=None):
    if dims is None:
        return jnp.dot(a, b, preferred_element_type=F32, precision=precision)
    return lax.dot_general(a, b, dims, preferred_element_type=F32, precision=precision)


def _sigmoid(x):
    return jax.nn.sigmoid(x)


def _silu(x):
    return x * jax.nn.sigmoid(x)


def _params(sem):
    return pltpu.CompilerParams(dimension_semantics=sem, vmem_limit_bytes=VMEM_LIMIT)


def _mod_body(r_ref, w_ref, b_ref, o_ref):
    r = r_ref[...]
    o_ref[...] = _dot(_silu(r), w_ref[...], precision=HIGHEST) + b_ref[...]


def _modulation(rows, w, b):
    d, n = w.shape
    tn = n // 4
    return pl.pallas_call(
        _mod_body,
        grid=(n // tn,),
        in_specs=[pl.BlockSpec((8, d), lambda j: (0, 0)),
                  pl.BlockSpec((d, tn), lambda j: (0, j)),
                  pl.BlockSpec((1, tn), lambda j: (0, j))],
        out_specs=pl.BlockSpec((8, tn), lambda j: (0, j)),
        out_shape=jax.ShapeDtypeStruct((8, n), F32),
        compiler_params=_params(("arbitrary",)),
        name="modulation",
    )(rows, w, b)


def _log_sigmoid(x):
    return jnp.minimum(x, 0.0) - jnp.log(1.0 + jnp.exp(-jnp.abs(x)))


def _inproj_body(x_ref, sc_ref, sh_ref, nw_ref, w_ref, wg_ref, gb_ref, lbl_ref, *refs):
    hgq_ref, hgv_ref, hgg_ref, hgk_ref, hglf_ref, mlqk_ref, mlv_ref, mlo_ref, gates_ref = refs[-9:]
    kw = HEADS_W
    x = x_ref[0]
    y = x * lax.rsqrt(jnp.mean(x * x, axis=-1, keepdims=True) + NORM_EPS) * nw_ref[...]
    u = y * (1.0 + sc_ref[0]) + sh_ref[0]
    ub = u.astype(BF16)

    def proj(c0, c1):
        return _dot(ub, w_ref[:, c0:c1])

    hgq_ref[0] = _silu(proj(0, kw)).astype(BF16)
    hgv_ref[0] = proj(kw, 2 * kw).astype(BF16)
    hgg_ref[0] = _silu(proj(2 * kw, 3 * kw)).astype(BF16)

    lbl = lbl_ref[...]
    mx = jnp.max(lbl, axis=0)
    ex = jnp.exp(lbl - mx[None])
    lb = ex[0] / jnp.sum(ex, axis=0)
    for d in range(2):
        p = proj((3 + d) * kw, (4 + d) * kw)
        lbd = lb[d:d + 1]
        f = lbd + (1.0 - lbd) * _sigmoid(p)
        hgk_ref[0, :, d * kw:(d + 1) * kw] = (1.0 - f).astype(BF16)
        hglf_ref[0, :, d * kw:(d + 1) * kw] = jnp.log(f)

    mlqk_ref[0, :, 0:kw] = proj(5 * kw, 6 * kw)
    mlqk_ref[0, :, kw:2 * kw] = proj(6 * kw, 7 * kw)
    mlv_ref[0] = proj(7 * kw, 8 * kw).astype(BF16)
    mlo_ref[0] = _sigmoid(proj(8 * kw, 9 * kw)).astype(BF16)

    g = _dot(wg_ref[...], u, _NT, precision=HIGHEST) + gb_ref[...]
    row = lax.broadcasted_iota(I32, g.shape, 0)
    gates_ref[0] = jnp.where((row % 8) >= N_HEADS, _log_sigmoid(g), g)


def _inproj_shapes(bsz, tall):
    kw = HEADS_W
    return [
        jax.ShapeDtypeStruct((bsz, tall, kw), BF16),
        jax.ShapeDtypeStruct((bsz, tall, kw), BF16),
        jax.ShapeDtypeStruct((bsz, tall, kw), BF16),
        jax.ShapeDtypeStruct((bsz, tall, 2 * kw), BF16),
        jax.ShapeDtypeStruct((bsz, tall, 2 * kw), F32),
        jax.ShapeDtypeStruct((bsz, tall, 2 * kw), F32),
        jax.ShapeDtypeStruct((bsz, tall, kw), BF16),
        jax.ShapeDtypeStruct((bsz, tall, kw), BF16),
        jax.ShapeDtypeStruct((bsz, 4 * N_HEADS, tall), F32),
    ]


def _inproj(tokens, scale, shift, nw, w_main, w_gt, gate_b, lb_logits, tall, tm, blk0, prev=None):
    bsz, n, d = tokens.shape
    kw = HEADS_W
    nt = n // tm
    per_sample = scale.shape[0] == bsz
    mod_map = (lambda b, i: (b, 0, 0)) if per_sample else (lambda b, i: (0, 0, 0))
    const2 = lambda b, i: (0, 0)
    in_specs = [
        pl.BlockSpec((1, tm, d), lambda b, i: (b, i, 0)),
        pl.BlockSpec((1, 1, d), mod_map),
        pl.BlockSpec((1, 1, d), mod_map),
        pl.BlockSpec((1, d), const2),
        pl.BlockSpec(w_main.shape, const2),
        pl.BlockSpec(w_gt.shape, const2),
        pl.BlockSpec(gate_b.shape, const2),
        pl.BlockSpec(lb_logits.shape, lambda b, i: (0, 0, 0)),
    ]
    args = [tokens, scale, shift, nw, w_main, w_gt, gate_b, lb_logits]
    aliases = {}
    if prev is not None:
        for k, a in enumerate(prev):
            in_specs.append(pl.BlockSpec(memory_space=pl.ANY))
            aliases[len(args)] = k
            args.append(a)
    row_map = lambda b, i: (b, blk0 + i, 0)
    widths = [kw, kw, kw, 2 * kw, 2 * kw, 2 * kw, kw, kw]
    out_specs = [pl.BlockSpec((1, tm, w), row_map) for w in widths]
    out_specs.append(pl.BlockSpec((1, 4 * N_HEADS, tm), lambda b, i: (b, 0, blk0 + i)))
    return pl.pallas_call(
        _inproj_body,
        grid=(bsz, nt),
        in_specs=in_specs,
        out_specs=out_specs,
        out_shape=_inproj_shapes(bsz, tall),
        input_output_aliases=aliases,
        compiler_params=_params(("arbitrary", "arbitrary")),
        name="inproj_ctx" if prev is not None else "inproj_x",
    )(*args)


def _conv_body(x_ref, w_ref, b_ref, o_ref, pad_ref, cpad_ref, *, t, nctx, scale_from):
    halo = CONV_HALO
    rows = 512
    win = rows + 2 * halo
    ch = x_ref.shape[-1]
    scale = jnp.where(pl.program_id(1) >= scale_from, HEAD_DIM ** -0.5, 1.0).astype(F32)
    w = w_ref[...]
    bias = b_ref[...]

    pad_ref[0:halo, :] = jnp.zeros((halo, ch), F32)
    pad_ref[halo + t:halo + t + halo, :] = jnp.zeros((halo, ch), F32)
    pad_ref[halo:halo + t, :] = x_ref[0, 0:t, :]
    col = (lax.broadcasted_iota(I32, (win, ch), 0) + (GRID_W - halo % GRID_W)) % GRID_W
    left_ok = col > 0
    right_ok = col < GRID_W - 1

    def chunk(c, carry):
        o = pl.multiple_of(c * rows, rows)
        xw = pad_ref[pl.ds(o, win), :]
        xm = jnp.where(left_ok, pltpu.roll(xw, 1, 0), 0.0)
        xp = jnp.where(right_ok, pltpu.roll(xw, win - 1, 0), 0.0)
        z = [xm * w[dr, 0:1] + xw * w[dr, 1:2] + xp * w[dr, 2:3] for dr in range(3)]
        y = (z[1][halo:halo + rows]
             + z[0][halo - GRID_W:halo - GRID_W + rows]
             + z[2][halo + GRID_W:halo + GRID_W + rows])
        o_ref[0, pl.ds(o, rows), :] = (_silu(y + bias) * scale).astype(o_ref.dtype)
        return carry

    lax.fori_loop(0, t // rows, chunk, 0)

    cpad_ref[0:8, :] = jnp.zeros((8, ch), F32)
    cpad_ref[8 + nctx:16 + nctx, :] = jnp.zeros((8, ch), F32)
    cpad_ref[8:8 + nctx, :] = x_ref[0, t:t + nctx, :]
    xw = cpad_ref[...]
    n = nctx + 16
    y = (pltpu.roll(xw, 1, 0) * w[1, 0:1] + xw * w[1, 1:2] + pltpu.roll(xw, n - 1, 0) * w[1, 2:3])[8:8 + nctx]
    o_ref[0, t:t + nctx, :] = (_silu(y + bias) * scale).astype(o_ref.dtype)


def _conv(qk_pre, conv_w, conv_b, t, nctx):
    bsz, tall, c = qk_pre.shape
    ch = 128
    body = functools.partial(_conv_body, t=t, nctx=nctx, scale_from=(c // 2) // ch)
    return pl.pallas_call(
        body,
        grid=(bsz, c // ch),
        in_specs=[pl.BlockSpec((1, tall, ch), lambda b, j: (b, 0, j)),
                  pl.BlockSpec((3, 3, ch), lambda b, j: (0, 0, j)),
                  pl.BlockSpec((1, ch), lambda b, j: (0, j))],
        out_specs=pl.BlockSpec((1, tall, ch), lambda b, j: (b, 0, j)),
        out_shape=jax.ShapeDtypeStruct((bsz, tall, c), BF16),
        scratch_shapes=[pltpu.VMEM((t + 2 * CONV_HALO, ch), F32),
                        pltpu.VMEM((nctx + 16, ch), F32)],
        compiler_params=_params(("arbitrary", "arbitrary")),
        name="qk_conv",
    )(qk_pre, conv_w, conv_b)


def _fwd_blk(s, nx):
    return jnp.where(s == 0, nx, s - 1)


def _bwd_blk(s, nx):
    return jnp.where(s == 0, nx, nx - s)


def _hg_constants(rev):
    c = HG_CHUNK
    i = np.arange(c)[:, None]
    j = np.arange(c)[None, :]
    blocks = [(j >= i) if rev else (j <= i), (j < i) if rev else (j > i)]
    masks = [i == j]
    qsel = []
    m = c // 2
    while m >= 1:
        b0 = (i // (2 * m)) * (2 * m)
        same = (i // (2 * m)) == (j // (2 * m))
        if rev:
            beta = b0 + m
            qrow = (i % (2 * m)) < m
            g = np.where(qrow, (j >= i) & (j < beta), (j >= beta) & (j < i))
            mask = same & qrow & ((j % (2 * m)) >= m)
        else:
            beta = b0 + m - 1
            qrow = (i % (2 * m)) >= m
            g = np.where(qrow, (j > beta) & (j <= i), (j > i) & (j <= beta))
            mask = same & qrow & ((j % (2 * m)) < m)
        blocks.append(g)
        masks.append(mask)
        qsel.append(np.broadcast_to(qrow, (c, HEAD_DIM)))
        m //= 2
    g = np.concatenate(blocks, axis=0).astype(np.float32)
    g3 = np.concatenate([g, g, g], axis=1)
    return (jnp.asarray(g3, BF16), jnp.asarray(np.stack(masks), F32), jnp.asarray(np.stack(qsel), F32))


def _hg_unit(rev, r0, q_ref, k_ref, v_ref, lf_ref, g_ref, msk_ref, qs_ref, o_ref, st_ref):
    c = HG_CHUNK
    lf = lf_ref[0, pl.ds(r0, c), :]
    p1 = lf.astype(BF16)
    r1 = lf - p1.astype(F32)
    p2 = r1.astype(BF16)
    p3 = (r1 - p2.astype(F32)).astype(BF16)
    dall = _dot(g_ref[...], jnp.concatenate([p1, p2, p3], axis=0))
    nlev = msk_ref.shape[0] - 1
    for h in range(N_HEADS):
        cs = slice(h * HEAD_DIM, (h + 1) * HEAD_DIM)
        qb = q_ref[0, pl.ds(r0, c), cs]
        kb = k_ref[0, pl.ds(r0, c), cs]
        vb = v_ref[0, pl.ds(r0, c), cs]
        qf = qb.astype(F32)
        kf = kb.astype(F32)
        a = dall[0:c, cs]
        a_end = dall[c:2 * c, cs]
        att = jnp.where(msk_ref[0] > 0, _dot(qb, kb, _NT), 0.0)
        for l in range(nlev):
            e = jnp.exp(dall[(l + 2) * c:(l + 3) * c, cs])
            z = (jnp.where(qs_ref[l] > 0, qf, kf) * e).astype(BF16)
            att = att + jnp.where(msk_ref[l + 1] > 0, _dot(z, z, _NT), 0.0)
        st = st_ref[h]
        qbar = (qf * jnp.exp(a)).astype(BF16)
        o = _dot(att.astype(BF16), vb) + _dot(qbar, st.astype(BF16), _NT)
        o_ref[0, pl.ds(r0, c), cs] = o
        khat = (kf * jnp.exp(a_end)).astype(BF16)
        a_tot = a[0:1] if rev else a[c - 1:c]
        st_ref[h] = st * jnp.exp(a_tot) + _dot(vb, khat, _TN)


def _hg_body(qf_ref, kf_ref, vf_ref, lff_ref, qb_ref, kb_ref, vb_ref, lfb_ref,
             gf_ref, mf_ref, sf_ref, gb_ref, mb_ref, sb_ref,
             of_ref, ob_ref, st_ref):
    @pl.when(pl.program_id(1) == 0)
    def _():
        st_ref[...] = jnp.zeros(st_ref.shape, F32)

    nsub = SCAN_STEP // HG_CHUNK

    def sub(c, carry):
        r0 = pl.multiple_of(c * HG_CHUNK, HG_CHUNK)
        _hg_unit(False, r0, qf_ref, kf_ref, vf_ref, lff_ref, gf_ref, mf_ref, sf_ref, of_ref, st_ref.at[0])
        r1 = pl.multiple_of((nsub - 1 - c) * HG_CHUNK, HG_CHUNK)
        _hg_unit(True, r1, qb_ref, kb_ref, vb_ref, lfb_ref, gb_ref, mb_ref, sb_ref, ob_ref, st_ref.at[1])
        return carry

    lax.fori_loop(0, nsub, sub, 0)


def _hgrn2(hgq, hgk, hgv, hglf, nx):
    bsz, tall, kw = hgq.shape
    steps = tall // SCAN_STEP
    cf = _hg_constants(False)
    cb = _hg_constants(True)
    blk = (1, SCAN_STEP, kw)
    fwd = lambda col: (lambda b, s: (b, _fwd_blk(s, nx), col))
    bwd = lambda col: (lambda b, s: (b, _bwd_blk(s, nx), col))
    const = lambda a: pl.BlockSpec(a.shape, lambda b, s: (0,) * a.ndim)
    in_specs = [pl.BlockSpec(blk, fwd(0)), pl.BlockSpec(blk, fwd(0)), pl.BlockSpec(blk, fwd(0)), pl.BlockSpec(blk, fwd(0)),
                pl.BlockSpec(blk, bwd(0)), pl.BlockSpec(blk, bwd(1)), pl.BlockSpec(blk, bwd(0)), pl.BlockSpec(blk, bwd(1))]
    in_specs += [const(a) for a in cf + cb]
    out_sds = jax.ShapeDtypeStruct((bsz, tall, kw), F32)
    return pl.pallas_call(
        _hg_body,
        grid=(bsz, steps),
        in_specs=in_specs,
        out_specs=[pl.BlockSpec(blk, fwd(0)), pl.BlockSpec(blk, bwd(0))],
        out_shape=[out_sds, out_sds],
        scratch_shapes=[pltpu.VMEM((2, N_HEADS, HEAD_DIM, HEAD_DIM), F32)],
        compiler_params=_params(("arbitrary", "arbitrary")),
        name="hgrn2_scan",
    )(hgq, hgk, hgv, hglf, hgq, hgk, hgv, hglf, *cf, *cb)


def _ml_unit(rev, r0, d, q_ref, k_ref, v_ref, g_ref, o_ref, ct_ref, n_ref, m_ref):
    c = ML_CHUNK
    ii = lax.broadcasted_iota(I32, (c, c), 0)
    jj = lax.broadcasted_iota(I32, (c, c), 1)
    tri = (jj >= ii) if rev else (jj <= ii)
    eye = ii == jj
    last = 0 if rev else c - 1
    gates = g_ref[0, :, pl.ds(r0, c)]
    for h in range(N_HEADS):
        cs = slice(h * HEAD_DIM, (h + 1) * HEAD_DIM)
        qb = q_ref[0, pl.ds(r0, c), cs]
        kb = k_ref[0, pl.ds(r0, c), cs]
        vb = v_ref[0, pl.ds(r0, c), cs]
        qf = qb.astype(F32)
        kf = kb.astype(F32)
        irow = gates[d * 8 + h:d * 8 + h + 1]
        frow = gates[d * 8 + N_HEADS + h:d * 8 + N_HEADS + h + 1]
        bcol = jnp.sum(jnp.where(tri, frow, 0.0), axis=1, keepdims=True)
        brow = jnp.sum(jnp.where(eye, bcol, 0.0), axis=0, keepdims=True)
        icol = jnp.sum(jnp.where(eye, irow, 0.0), axis=1, keepdims=True)
        dm = jnp.where(tri, bcol - brow + irow, -jnp.inf)
        mprev = m_ref[h][:, 0:1]
        inter = bcol + mprev
        mrow = jnp.maximum(inter, jnp.max(dm, axis=1, keepdims=True))
        w_inter = jnp.exp(inter - mrow)
        qk = _dot(qb, kb, _NT) * jnp.exp(dm - mrow)
        ct = ct_ref[h]
        nrow = n_ref[h]
        num = _dot(qk.astype(BF16), vb) + w_inter * _dot(qb, ct.astype(BF16), _NT)
        den = jnp.sum(qk, axis=1, keepdims=True) + w_inter * jnp.sum(qf * nrow, axis=1, keepdims=True)
        o_ref[0, pl.ds(r0, c), cs] = num / jnp.maximum(jnp.abs(den), jnp.exp(-mrow))
        mnew = mrow[last:last + 1]
        blast = bcol[last:last + 1]
        wk = jnp.exp(blast - bcol + icol - mnew)
        ws = jnp.exp(blast + mprev - mnew)
        kh = kf * wk
        ct_ref[h] = ws * ct + _dot(vb, kh.astype(BF16), _TN)
        n_ref[h] = ws * nrow + jnp.sum(kh, axis=0, keepdims=True)
        m_ref[h] = jnp.broadcast_to(mnew, (1, HEAD_DIM))


def _ml_body(qf_ref, kf_ref, vf_ref, gf_ref, qb_ref, kb_ref, vb_ref, gb_ref,
             of_ref, ob_ref, ct_ref, n_ref, m_ref):
    @pl.when(pl.program_id(1) == 0)
    def _():
        ct_ref[...] = jnp.zeros(ct_ref.shape, F32)
        n_ref[...] = jnp.zeros(n_ref.shape, F32)
        m_ref[...] = jnp.zeros(m_ref.shape, F32)

    nsub = SCAN_STEP // ML_CHUNK

    def sub(c, carry):
        r0 = pl.multiple_of(c * ML_CHUNK, ML_CHUNK)
        _ml_unit(False, r0, 0, qf_ref, kf_ref, vf_ref, gf_ref, of_ref, ct_ref.at[0], n_ref.at[0], m_ref.at[0])
        r1 = pl.multiple_of((nsub - 1 - c) * ML_CHUNK, ML_CHUNK)
        _ml_unit(True, r1, 1, qb_ref, kb_ref, vb_ref, gb_ref, ob_ref, ct_ref.at[1], n_ref.at[1], m_ref.at[1])
        return carry

    lax.fori_loop(0, nsub, sub, 0)


def _mlstm(mlqk, mlv, gates, nx):
    bsz, tall, kw = mlv.shape
    steps = tall // SCAN_STEP
    blk = (1, SCAN_STEP, kw)
    gblk = (1, 4 * N_HEADS, SCAN_STEP)
    fwd = lambda col: (lambda b, s: (b, _fwd_blk(s, nx), col))
    bwd = lambda col: (lambda b, s: (b, _bwd_blk(s, nx), col))
    in_specs = [pl.BlockSpec(blk, fwd(0)), pl.BlockSpec(blk, fwd(1)), pl.BlockSpec(blk, fwd(0)),
                pl.BlockSpec(gblk, lambda b, s: (b, 0, _fwd_blk(s, nx))),
                pl.BlockSpec(blk, bwd(0)), pl.BlockSpec(blk, bwd(1)), pl.BlockSpec(blk, bwd(0)),
                pl.BlockSpec(gblk, lambda b, s: (b, 0, _bwd_blk(s, nx)))]
    out_sds = jax.ShapeDtypeStruct((bsz, tall, kw), F32)
    return pl.pallas_call(
        _ml_body,
        grid=(bsz, steps),
        in_specs=in_specs,
        out_specs=[pl.BlockSpec(blk, fwd(0)), pl.BlockSpec(blk, bwd(0))],
        out_shape=[out_sds, out_sds],
        scratch_shapes=[pltpu.VMEM((2, N_HEADS, HEAD_DIM, HEAD_DIM), F32),
                        pltpu.VMEM((2, N_HEADS, 1, HEAD_DIM), F32),
                        pltpu.VMEM((2, N_HEADS, 1, HEAD_DIM), F32)],
        compiler_params=_params(("arbitrary", "arbitrary")),
        name="mlstm_scan",
    )(mlqk, mlqk, mlv, gates, mlqk, mlqk, mlv, gates)


def _out_body(hof_ref, hob_ref, mhf_ref, mhb_ref, hgg_ref, mlo_ref, x_ref, g1_ref, sc2_ref, sh2_ref,
              hnw_ref, mnw_ref, wout_ref, n2w_ref, rwt_ref, x1_ref, vt_ref, aff_ref):
    hg = hof_ref[0] + hob_ref[0]
    ml = mhf_ref[0] + mhb_ref[0]
    hparts, mparts = [], []
    for h in range(N_HEADS):
        cs = slice(h * HEAD_DIM, (h + 1) * HEAD_DIM)
        t = hg[:, cs]
        hparts.append(t * lax.rsqrt(jnp.mean(t * t, axis=-1, keepdims=True) + NORM_EPS))
        t = ml[:, cs]
        t = t - jnp.mean(t, axis=-1, keepdims=True)
        mparts.append(t * lax.rsqrt(jnp.mean(t * t, axis=-1, keepdims=True) + NORM_EPS))
    hgn = jnp.concatenate(hparts, axis=-1) * hnw_ref[...] * hgg_ref[0].astype(F32)
    mln = jnp.concatenate(mparts, axis=-1) * mnw_ref[...] * mlo_ref[0].astype(F32)
    mix = jnp.concatenate([hgn, mln], axis=-1).astype(BF16)
    x1 = x_ref[0] + g1_ref[0] * _dot(mix, wout_ref[...])
    x1_ref[0] = x1
    v = x1 * lax.rsqrt(jnp.mean(x1 * x1, axis=-1, keepdims=True) + NORM_EPS) * n2w_ref[...]
    v = v * (1.0 + sc2_ref[0]) + sh2_ref[0]
    vt = v.T
    vt_ref[0] = vt.astype(BF16)
    logits = _dot(rwt_ref[...], vt, precision=HIGHEST)
    ex = jnp.exp(logits - jnp.max(logits, axis=0, keepdims=True))
    aff_ref[0] = ex / jnp.sum(ex, axis=0, keepdims=True)


def _mixer_out(hg_f, hg_b, ml_f, ml_b, hgg, mlo, x, g1, sc2, sh2, hnw, mnw, w_out, n2w, rwt, tm):
    bsz, t, d = x.shape
    kw = HEADS_W
    ne = rwt.shape[0]
    row = lambda b, i: (b, i, 0)
    mod = lambda b, i: (b, 0, 0)
    const2 = lambda b, i: (0, 0)
    act = pl.BlockSpec((1, tm, kw), row)
    in_specs = [act, act, act, act, act, act,
                pl.BlockSpec((1, tm, d), row),
                pl.BlockSpec((1, 1, d), mod), pl.BlockSpec((1, 1, d), mod), pl.BlockSpec((1, 1, d), mod),
                pl.BlockSpec((1, kw), const2), pl.BlockSpec((1, kw), const2),
                pl.BlockSpec(w_out.shape, const2), pl.BlockSpec((1, d), const2), pl.BlockSpec(rwt.shape, const2)]
    return pl.pallas_call(
        _out_body,
        grid=(bsz, t // tm),
        in_specs=in_specs,
        out_specs=[pl.BlockSpec((1, tm, d), row),
                   pl.BlockSpec((1, d, tm), lambda b, i: (b, 0, i)),
                   pl.BlockSpec((1, ne, tm), lambda b, i: (b, 0, i))],
        out_shape=[jax.ShapeDtypeStruct((bsz, t, d), F32),
                   jax.ShapeDtypeStruct((bsz, d, t), BF16),
                   jax.ShapeDtypeStruct((bsz, ne, t), F32)],
        compiler_params=_params(("arbitrary", "arbitrary")),
        name="mixer_out",
    )(hg_f, hg_b, ml_f, ml_b, hgg, mlo, x, g1, sc2, sh2, hnw, mnw, w_out, n2w, rwt)


def _prefix_count(maskf, u_ref, ones_ref, bl_ref):
    e, nb, ln = maskf.shape
    x = maskf.reshape(e * nb, ln)
    xb = x.astype(BF16)
    incl = _dot(xb, u_ref[...])
    tot = _dot(xb, ones_ref[...])
    off = _dot(bl_ref[...], tot.astype(BF16))
    return (incl - x + off).reshape(e, nb, ln), off.reshape(e, nb, ln)


def _topk_body(aff_ref, u_ref, ones_ref, bl_ref, pos_ref, cnt_ref, *, cap):
    x = aff_ref[0]
    ne = x.shape[0]

    def count(m):
        return jnp.sum(jnp.sum(jnp.where(m, 1.0, 0.0), axis=1, keepdims=True), axis=2, keepdims=True)

    def halve(_, carry):
        lo, hi = carry
        mid = 0.5 * (lo + hi)
        up = count(x > mid) >= cap
        return jnp.where(up, mid, lo), jnp.where(up, hi, mid)

    lo, hi = lax.fori_loop(0, TOPK_BISECTIONS, halve,
                           (jnp.full((ne, 1, 1), -1.0, F32), jnp.full((ne, 1, 1), 1.0, F32)))
    gt = jnp.where(x > hi, 1.0, 0.0)
    eq = jnp.where(x > lo, 1.0, 0.0) - gt
    need = cap - count(x > hi)
    eq_rank, _ = _prefix_count(eq, u_ref, ones_ref, bl_ref)
    sel = gt + eq * jnp.where(eq_rank < need, 1.0, 0.0)
    pos, off = _prefix_count(sel, u_ref, ones_ref, bl_ref)
    pos_ref[0] = jnp.where(sel > 0, pos, -1.0).astype(I32)
    cnt_ref[0] = off.astype(I32)


def _topk(aff4, cap):
    bsz, ne, nb, ln = aff4.shape
    k = np.arange(ln)
    u = jnp.asarray(k[:, None] <= k[None, :], BF16)
    ones = jnp.ones((ln, ln), BF16)
    r = np.arange(ne * nb)
    bl = jnp.asarray(((r[:, None] // nb) == (r[None, :] // nb)) & (r[None, :] < r[:, None]), BF16)
    blk = pl.BlockSpec((1, ne, nb, ln), lambda b: (b, 0, 0, 0))
    const = lambda a: pl.BlockSpec(a.shape, lambda b: (0, 0))
    sds = jax.ShapeDtypeStruct((bsz, ne, nb, ln), I32)
    return pl.pallas_call(
        functools.partial(_topk_body, cap=cap),
        grid=(bsz,),
        in_specs=[blk, const(u), const(ones), const(bl)],
        out_specs=[blk, blk],
        out_shape=[sds, sds],
        compiler_params=_params(("arbitrary",)),
        name="expert_topk",
    )(aff4, u, ones, bl)


def _onehot_t(pos_row, s):
    n = ONEHOT_BLK
    slot = lax.broadcasted_iota(I32, (n, pos_row.shape[-1]), 0) + s * n
    return jnp.where(slot == pos_row, 1.0, 0.0).astype(BF16)


def _moe1_body(cnt_ref, pos_ref, vt_ref, wg_ref, wu_ref, wd_ref, yt_ref, xs_ref, *, tb, cap):
    n = ONEHOT_BLK
    b, e, t = pl.program_id(0), pl.program_id(1), pl.program_id(2)
    row = b * pl.num_programs(1) + e

    @pl.when(t == 0)
    def _():
        xs_ref[...] = jnp.zeros(xs_ref.shape, F32)

    for j in range(tb // n):
        blk = t * (tb // 128) + j * (n // 128)
        c0 = cnt_ref[row, blk]
        c1 = cnt_ref[row, blk + n // 128]
        for s in range(cap // n):
            @pl.when((c0 < (s + 1) * n) & (c1 > s * n))
            def _(j=j, s=s):
                pt = _onehot_t(pos_ref[0, 0, :, j * n:(j + 1) * n], s)
                xs_ref[:, s * n:(s + 1) * n] += _dot(vt_ref[0, :, j * n:(j + 1) * n], pt, _NT)

    @pl.when(t == pl.num_programs(2) - 1)
    def _():
        xs = xs_ref[...].astype(BF16)
        hg = _dot(wg_ref[0], xs)
        hu = _dot(wu_ref[0], xs)
        ht = (_silu(hg) * hu).astype(BF16)
        yt_ref[0, 0] = _dot(wd_ref[0], ht).astype(BF16)


def _moe_experts(cnt, pos, vt, wgt, wut, wdt, cap, tb):
    bsz, d, t = vt.shape
    ne, f, _ = wgt.shape
    grid_spec = pltpu.PrefetchScalarGridSpec(
        num_scalar_prefetch=1,
        grid=(bsz, ne, t // tb),
        in_specs=[pl.BlockSpec((1, 1, 1, tb), lambda b, e, i, c: (b, e, 0, i)),
                  pl.BlockSpec((1, d, tb), lambda b, e, i, c: (b, 0, i)),
                  pl.BlockSpec((1, f, d), lambda b, e, i, c: (e, 0, 0)),
                  pl.BlockSpec((1, f, d), lambda b, e, i, c: (e, 0, 0)),
                  pl.BlockSpec((1, d, f), lambda b, e, i, c: (e, 0, 0))],
        out_specs=pl.BlockSpec((1, 1, d, cap), lambda b, e, i, c: (b, e, 0, 0)),
        scratch_shapes=[pltpu.VMEM((d, cap), F32)],
    )
    return pl.pallas_call(
        functools.partial(_moe1_body, tb=tb, cap=cap),
        grid_spec=grid_spec,
        out_shape=jax.ShapeDtypeStruct((bsz, ne, d, cap), BF16),
        compiler_params=_params(("arbitrary", "arbitrary", "arbitrary")),
        name="moe_experts",
    )(cnt, pos, vt, wgt, wut, wdt)


def _moe2_body(cnt_ref, pos_ref, aff_ref, yt_ref, out_ref, *, tb, cap):
    n = ONEHOT_BLK
    b, t, e = pl.program_id(0), pl.program_id(1), pl.program_id(2)
    row = b * pl.num_programs(2) + e

    @pl.when(e == 0)
    def _():
        out_ref[...] = jnp.zeros(out_ref.shape, F32)

    for j in range(tb // n):
        blk = t * (tb // 128) + j * (n // 128)
        c0 = cnt_ref[row, blk]
        c1 = cnt_ref[row, blk + n // 128]
        for s in range(cap // n):
            @pl.when((c0 < (s + 1) * n) & (c1 > s * n))
            def _(j=j, s=s):
                ts = slice(j * n, (j + 1) * n)
                pt = _onehot_t(pos_ref[0, 0, :, ts], s)
                contrib = _dot(yt_ref[0, 0, :, s * n:(s + 1) * n], pt)
                out_ref[0, :, ts] += contrib * aff_ref[0, 0, :, ts]


def _moe_combine(cnt, pos, aff, yt, cap, tb):
    bsz, ne, d, _ = yt.shape
    t = pos.shape[-1]
    grid_spec = pltpu.PrefetchScalarGridSpec(
        num_scalar_prefetch=1,
        grid=(bsz, t // tb, ne),
        in_specs=[pl.BlockSpec((1, 1, 1, tb), lambda b, i, e, c: (b, e, 0, i)),
                  pl.BlockSpec((1, 1, 1, tb), lambda b, i, e, c: (b, e, 0, i)),
                  pl.BlockSpec((1, 1, d, cap), lambda b, i, e, c: (b, e, 0, 0))],
        out_specs=pl.BlockSpec((1, d, tb), lambda b, i, e, c: (b, 0, i)),
    )
    return pl.pallas_call(
        functools.partial(_moe2_body, tb=tb, cap=cap),
        grid_spec=grid_spec,
        out_shape=jax.ShapeDtypeStruct((bsz, d, t), F32),
        compiler_params=_params(("arbitrary", "arbitrary", "arbitrary")),
        name="moe_combine",
    )(cnt, pos, aff, yt)


def _final_body(x1_ref, mt_ref, g2_ref, fw_ref, o_ref):
    x2 = x1_ref[0] + g2_ref[0] * mt_ref[0].T
    o_ref[0] = x2 * lax.rsqrt(jnp.mean(x2 * x2, axis=-1, keepdims=True) + NORM_EPS) * fw_ref[...]


def _final(x1, moe_t, g2, fw, tm):
    bsz, t, d = x1.shape
    return pl.pallas_call(
        _final_body,
        grid=(bsz, t // tm),
        in_specs=[pl.BlockSpec((1, tm, d), lambda b, i: (b, i, 0)),
                  pl.BlockSpec((1, d, tm), lambda b, i: (b, 0, i)),
                  pl.BlockSpec((1, 1, d), lambda b, i: (b, 0, 0)),
                  pl.BlockSpec((1, d), lambda b, i: (0, 0))],
        out_specs=pl.BlockSpec((1, tm, d), lambda b, i: (b, i, 0)),
        out_shape=jax.ShapeDtypeStruct((bsz, t, d), F32),
        compiler_params=_params(("arbitrary", "arbitrary")),
        name="final_norm",
    )(x1, moe_t, g2, fw)


def kernel(x, c, ctx, c_ctx, ada_w, ada_b, norm1_w, w_in, conv_w, conv_b, hg_lb_logits, ml_gate_b,
           hg_norm_w, ml_norm_w, w_out, norm2_w, router_w, exp_w_gate, exp_w_up, exp_w_down, final_norm_w):
    bsz, t, d = x.shape
    nctx = ctx.shape[1]
    assert ada_w.shape[0] == 1, "single-layer block"
    assert nctx == SCAN_STEP and t % SCAN_STEP == 0 and bsz + 1 <= 8
    tall = t + nctx
    nx = t // SCAN_STEP
    kw = HEADS_W
    ne = router_w.shape[-1]
    cap = EC_CAPACITY * t // ne
    tm = min(512, t)
    tb = min(2048, t)

    rows = jnp.concatenate([c, c_ctx[None], jnp.zeros((7 - bsz, d), F32)], axis=0)
    mod = _modulation(rows, ada_w[0], ada_b[0][None])
    mx = [m[:, None, :] for m in jnp.split(mod[:bsz], 6, axis=-1)]
    mc = [m[:, None, :] for m in jnp.split(mod[bsz:bsz + 1], 6, axis=-1)]
    sh1, sc1, g1, sh2, sc2, g2 = mx
    csh1, csc1 = mc[0], mc[1]

    main_w = 9 * kw
    w_main = w_in[0][:, :main_w].astype(BF16)
    w_gt = w_in[0][:, main_w:].T
    gate_b = ml_gate_b[0][:, None]
    nw1 = norm1_w[0][None]
    outs = _inproj(x, sc1, sh1, nw1, w_main, w_gt, gate_b, hg_lb_logits, tall, tm, 0)
    outs = _inproj(ctx, csc1, csh1, nw1, w_main, w_gt, gate_b, hg_lb_logits, tall, nctx, t // nctx, prev=outs)
    hgq, hgv, hgg, hgk, hglf, mlqk_pre, mlv, mlo, gates = outs

    mlqk = _conv(mlqk_pre, conv_w[0], conv_b[0][None], t, nctx)
    hg_f, hg_b = _hgrn2(hgq, hgk, hgv, hglf, nx)
    ml_f, ml_b = _mlstm(mlqk, mlv, gates, nx)

    x1, vt, aff = _mixer_out(hg_f, hg_b, ml_f, ml_b, hgg, mlo, x, g1, sc2, sh2,
                             hg_norm_w[0][None], ml_norm_w[0][None], w_out[0].astype(BF16),
                             norm2_w[0][None], router_w[0].T, tm)

    pos4, cnt4 = _topk(aff.reshape(bsz, ne, t // 128, 128), cap)
    pos = pos4.reshape(bsz, ne, 1, t)
    cnt = jnp.concatenate([cnt4[..., 0].reshape(bsz * ne, t // 128),
                           jnp.full((bsz * ne, 1), cap, I32)], axis=1)
    wgt = jnp.swapaxes(exp_w_gate[0], 1, 2).astype(BF16)
    wut = jnp.swapaxes(exp_w_up[0], 1, 2).astype(BF16)
    wdt = jnp.swapaxes(exp_w_down[0], 1, 2).astype(BF16)
    yt = _moe_experts(cnt, pos, vt, wgt, wut, wdt, cap, tb)
    moe_t = _moe_combine(cnt, pos, aff.reshape(bsz, ne, 1, t), yt, cap, tb)
    return _final(x1, moe_t, g2, final_norm_w[None], tm)
```

```python
import dataclasses
import functools

import numpy as np
import jax
import jax.numpy as jnp
from jax import lax
from jax.experimental import pallas as pl
from jax.experimental.pallas import tpu as pltpu
from jax.experimental.pallas import tpu_sc as plsc

F32 = jnp.float32
BF16 = jnp.bfloat16
I32 = jnp.int32
HIGHEST = lax.Precision.HIGHEST
NORM_EPS = 1e-6

HEAD_DIM = 128
N_HEADS = 4
HEADS_W = N_HEADS * HEAD_DIM
GRID_W = 64
N_EXPERTS = 16
EC_CAPACITY = 2
HG_CHUNK = 64
ML_CHUNK = 128
SCAN_STEP = 256
ONEHOT_BLK = 256
CONV_HALO = 72
SC_GATHER_ROWS = 64
TOPK_BISECTIONS = 64
VMEM_LIMIT = 56 * 1024 * 1024

_NT = (((1,), (1,)), ((), ()))
_TN = (((0,), (0,)), ((), ()))


def _dot(a, b, dims=None, precision=None):
    if dims is None:
        return jnp.dot(a, b, preferred_element_type=F32, precision=precision)
    return lax.dot_general(a, b, dims, preferred_element_type=F32, precision=precision)


def _sigmoid(x):
    return jax.nn.sigmoid(x)


def _pack_halves(x):
    w = x.shape[-1] // 2
    bits = lax.bitcast_convert_type(x.astype(BF16).astype(F32), I32)
    return lax.shift_right_logical(bits[:, :w], 16) | bits[:, w:]


def _unpack_halves(p):
    lo = lax.bitcast_convert_type(lax.shift_left(p, 16), F32)
    hi = lax.bitcast_convert_type(p & jnp.int32(-65536), F32)
    return jnp.concatenate([lo, hi], axis=-1).astype(BF16)


def _silu(x):
    return x * jax.nn.sigmoid(x)


def _params(sem):
    return pltpu.CompilerParams(dimension_semantics=sem, vmem_limit_bytes=VMEM_LIMIT)


def _mod_body(r_ref, w_ref, b_ref, o_ref):
    r = r_ref[...]
    o_ref[...] = _dot(_silu(r), w_ref[...], precision=HIGHEST) + b_ref[...]


def _modulation(rows, w, b):
    d, n = w.shape
    tn = n // 4
    return pl.pallas_call(
        _mod_body,
        grid=(n // tn,),
        in_specs=[pl.BlockSpec((8, d), lambda j: (0, 0)),
                  pl.BlockSpec((d, tn), lambda j: (0, j)),
                  pl.BlockSpec((1, tn), lambda j: (0, j))],
        out_specs=pl.BlockSpec((8, tn), lambda j: (0, j)),
        out_shape=jax.ShapeDtypeStruct((8, n), F32),
        compiler_params=_params(("arbitrary",)),
        name="modulation",
    )(rows, w, b)


def _log_sigmoid(x):
    return jnp.minimum(x, 0.0) - jnp.log(1.0 + jnp.exp(-jnp.abs(x)))


def _inproj_body(x_ref, sc_ref, sh_ref, nw_ref, w_ref, wg_ref, gb_ref, lbl_ref, *refs):
    hgq_ref, hgv_ref, hgg_ref, hgk_ref, hglf_ref, mlqk_ref, mlv_ref, mlo_ref, gates_ref = refs[-9:]
    kw = HEADS_W
    x = x_ref[0]
    y = x * lax.rsqrt(jnp.mean(x * x, axis=-1, keepdims=True) + NORM_EPS) * nw_ref[...]
    u = y * (1.0 + sc_ref[0]) + sh_ref[0]
    ub = u.astype(BF16)

    def proj(c0, c1):
        return _dot(ub, w_ref[:, c0:c1])

    hgq_ref[0] = _silu(proj(0, kw)).astype(BF16)
    hgv_ref[0] = proj(kw, 2 * kw).astype(BF16)
    hgg_ref[0] = _silu(proj(2 * kw, 3 * kw)).astype(BF16)

    lbl = lbl_ref[...]
    mx = jnp.max(lbl, axis=0)
    ex = jnp.exp(lbl - mx[None])
    lb = ex[0] / jnp.sum(ex, axis=0)
    for d in range(2):
        p = proj((3 + d) * kw, (4 + d) * kw)
        lbd = lb[d:d + 1]
        f = lbd + (1.0 - lbd) * _sigmoid(p)
        hgk_ref[0, :, d * kw:(d + 1) * kw] = (1.0 - f).astype(BF16)
        hglf_ref[0, :, d * kw:(d + 1) * kw] = jnp.log(f)

    mlqk_ref[0, :, 0:kw] = proj(5 * kw, 6 * kw)
    mlqk_ref[0, :, kw:2 * kw] = proj(6 * kw, 7 * kw)
    mlv_ref[0] = proj(7 * kw, 8 * kw).astype(BF16)
    mlo_ref[0] = _sigmoid(proj(8 * kw, 9 * kw)).astype(BF16)

    g = _dot(wg_ref[...], u, _NT, precision=HIGHEST) + gb_ref[...]
    row = lax.broadcasted_iota(I32, g.shape, 0)
    gates_ref[0] = jnp.where((row % 8) >= N_HEADS, _log_sigmoid(g), g)


def _inproj_shapes(bsz, tall):
    kw = HEADS_W
    return [
        jax.ShapeDtypeStruct((bsz, tall, kw), BF16),
        jax.ShapeDtypeStruct((bsz, tall, kw), BF16),
        jax.ShapeDtypeStruct((bsz, tall, kw), BF16),
        jax.ShapeDtypeStruct((bsz, tall, 2 * kw), BF16),
        jax.ShapeDtypeStruct((bsz, tall, 2 * kw), F32),
        jax.ShapeDtypeStruct((bsz, tall, 2 * kw), F32),
        jax.ShapeDtypeStruct((bsz, tall, kw), BF16),
        jax.ShapeDtypeStruct((bsz, tall, kw), BF16),
        jax.ShapeDtypeStruct((bsz, 4 * N_HEADS, tall), F32),
    ]


def _inproj(tokens, scale, shift, nw, w_main, w_gt, gate_b, lb_logits, tall, tm, blk0, prev=None):
    bsz, n, d = tokens.shape
    kw = HEADS_W
    nt = n // tm
    per_sample = scale.shape[0] == bsz
    mod_map = (lambda b, i: (b, 0, 0)) if per_sample else (lambda b, i: (0, 0, 0))
    const2 = lambda b, i: (0, 0)
    in_specs = [
        pl.BlockSpec((1, tm, d), lambda b, i: (b, i, 0)),
        pl.BlockSpec((1, 1, d), mod_map),
        pl.BlockSpec((1, 1, d), mod_map),
        pl.BlockSpec((1, d), const2),
        pl.BlockSpec(w_main.shape, const2),
        pl.BlockSpec(w_gt.shape, const2),
        pl.BlockSpec(gate_b.shape, const2),
        pl.BlockSpec(lb_logits.shape, lambda b, i: (0, 0, 0)),
    ]
    args = [tokens, scale, shift, nw, w_main, w_gt, gate_b, lb_logits]
    aliases = {}
    if prev is not None:
        for k, a in enumerate(prev):
            in_specs.append(pl.BlockSpec(memory_space=pl.ANY))
            aliases[len(args)] = k
            args.append(a)
    row_map = lambda b, i: (b, blk0 + i, 0)
    widths = [kw, kw, kw, 2 * kw, 2 * kw, 2 * kw, kw, kw]
    out_specs = [pl.BlockSpec((1, tm, w), row_map) for w in widths]
    out_specs.append(pl.BlockSpec((1, 4 * N_HEADS, tm), lambda b, i: (b, 0, blk0 + i)))
    return pl.pallas_call(
        _inproj_body,
        grid=(bsz, nt),
        in_specs=in_specs,
        out_specs=out_specs,
        out_shape=_inproj_shapes(bsz, tall),
        input_output_aliases=aliases,
        compiler_params=_params(("arbitrary", "arbitrary")),
        name="inproj_ctx" if prev is not None else "inproj_x",
    )(*args)


def _conv_body(x_ref, w_ref, b_ref, o_ref, pad_ref, cpad_ref, *, t, nctx, scale_from):
    halo = CONV_HALO
    rows = 512
    win = rows + 2 * halo
    ch = x_ref.shape[-1]
    scale = jnp.where(pl.program_id(1) >= scale_from, HEAD_DIM ** -0.5, 1.0).astype(F32)
    w = w_ref[...]
    bias = b_ref[...]

    pad_ref[0:halo, :] = jnp.zeros((halo, ch), F32)
    pad_ref[halo + t:halo + t + halo, :] = jnp.zeros((halo, ch), F32)
    pad_ref[halo:halo + t, :] = x_ref[0, 0:t, :]
    col = (lax.broadcasted_iota(I32, (win, ch), 0) + (GRID_W - halo % GRID_W)) % GRID_W
    left_ok = col > 0
    right_ok = col < GRID_W - 1

    def chunk(c, carry):
        o = pl.multiple_of(c * rows, rows)
        xw = pad_ref[pl.ds(o, win), :]
        xm = jnp.where(left_ok, pltpu.roll(xw, 1, 0), 0.0)
        xp = jnp.where(right_ok, pltpu.roll(xw, win - 1, 0), 0.0)
        z = [xm * w[dr, 0:1] + xw * w[dr, 1:2] + xp * w[dr, 2:3] for dr in range(3)]
        y = (z[1][halo:halo + rows]
             + z[0][halo - GRID_W:halo - GRID_W + rows]
             + z[2][halo + GRID_W:halo + GRID_W + rows])
        o_ref[0, pl.ds(o, rows), :] = (_silu(y + bias) * scale).astype(o_ref.dtype)
        return carry

    lax.fori_loop(0, t // rows, chunk, 0)

    cpad_ref[0:8, :] = jnp.zeros((8, ch), F32)
    cpad_ref[8 + nctx:16 + nctx, :] = jnp.zeros((8, ch), F32)
    cpad_ref[8:8 + nctx, :] = x_ref[0, t:t + nctx, :]
    xw = cpad_ref[...]
    n = nctx + 16
    y = (pltpu.roll(xw, 1, 0) * w[1, 0:1] + xw * w[1, 1:2] + pltpu.roll(xw, n - 1, 0) * w[1, 2:3])[8:8 + nctx]
    o_ref[0, t:t + nctx, :] = (_silu(y + bias) * scale).astype(o_ref.dtype)


def _conv(qk_pre, conv_w, conv_b, t, nctx):
    bsz, tall, c = qk_pre.shape
    ch = 128
    body = functools.partial(_conv_body, t=t, nctx=nctx, scale_from=(c // 2) // ch)
    return pl.pallas_call(
        body,
        grid=(bsz, c // ch),
        in_specs=[pl.BlockSpec((1, tall, ch), lambda b, j: (b, 0, j)),
                  pl.BlockSpec((3, 3, ch), lambda b, j: (0, 0, j)),
                  pl.BlockSpec((1, ch), lambda b, j: (0, j))],
        out_specs=pl.BlockSpec((1, tall, ch), lambda b, j: (b, 0, j)),
        out_shape=jax.ShapeDtypeStruct((bsz, tall, c), BF16),
        scratch_shapes=[pltpu.VMEM((t + 2 * CONV_HALO, ch), F32),
                        pltpu.VMEM((nctx + 16, ch), F32)],
        compiler_params=_params(("arbitrary", "arbitrary")),
        name="qk_conv",
    )(qk_pre, conv_w, conv_b)


def _fwd_blk(s, nx):
    return jnp.where(s == 0, nx, s - 1)


def _bwd_blk(s, nx):
    return jnp.where(s == 0, nx, nx - s)


def _hg_constants(rev):
    c = HG_CHUNK
    i = np.arange(c)[:, None]
    j = np.arange(c)[None, :]
    blocks = [(j >= i) if rev else (j <= i), (j < i) if rev else (j > i)]
    masks = [i == j]
    qsel = []
    m = c // 2
    while m >= 1:
        b0 = (i // (2 * m)) * (2 * m)
        same = (i // (2 * m)) == (j // (2 * m))
        if rev:
            beta = b0 + m
            qrow = (i % (2 * m)) < m
            g = np.where(qrow, (j >= i) & (j < beta), (j >= beta) & (j < i))
            mask = same & qrow & ((j % (2 * m)) >= m)
        else:
            beta = b0 + m - 1
            qrow = (i % (2 * m)) >= m
            g = np.where(qrow, (j > beta) & (j <= i), (j > i) & (j <= beta))
            mask = same & qrow & ((j % (2 * m)) < m)
        blocks.append(g)
        masks.append(mask)
        qsel.append(np.broadcast_to(qrow, (c, HEAD_DIM)))
        m //= 2
    g = np.concatenate(blocks, axis=0).astype(np.float32)
    g3 = np.concatenate([g, g, g], axis=1)
    return (jnp.asarray(g3, BF16), jnp.asarray(np.stack(masks), F32), jnp.asarray(np.stack(qsel), F32))


def _hg_unit(rev, r0, q_ref, k_ref, v_ref, lf_ref, g_ref, msk_ref, qs_ref, o_ref, st_ref):
    c = HG_CHUNK
    lf = lf_ref[0, pl.ds(r0, c), :]
    p1 = lf.astype(BF16)
    r1 = lf - p1.astype(F32)
    p2 = r1.astype(BF16)
    p3 = (r1 - p2.astype(F32)).astype(BF16)
    dall = _dot(g_ref[...], jnp.concatenate([p1, p2, p3], axis=0))
    nlev = msk_ref.shape[0] - 1
    for h in range(N_HEADS):
        cs = slice(h * HEAD_DIM, (h + 1) * HEAD_DIM)
        qb = q_ref[0, pl.ds(r0, c), cs]
        kb = k_ref[0, pl.ds(r0, c), cs]
        vb = v_ref[0, pl.ds(r0, c), cs]
        qf = qb.astype(F32)
        kf = kb.astype(F32)
        a = dall[0:c, cs]
        a_end = dall[c:2 * c, cs]
        att = jnp.where(msk_ref[0] > 0, _dot(qb, kb, _NT), 0.0)
        for l in range(nlev):
            e = jnp.exp(dall[(l + 2) * c:(l + 3) * c, cs])
            z = (jnp.where(qs_ref[l] > 0, qf, kf) * e).astype(BF16)
            att = att + jnp.where(msk_ref[l + 1] > 0, _dot(z, z, _NT), 0.0)
        st = st_ref[h]
        qbar = (qf * jnp.exp(a)).astype(BF16)
        o = _dot(att.astype(BF16), vb) + _dot(qbar, st.astype(BF16), _NT)
        o_ref[0, pl.ds(r0, c), cs] = o
        khat = (kf * jnp.exp(a_end)).astype(BF16)
        a_tot = a[0:1] if rev else a[c - 1:c]
        st_ref[h] = st * jnp.exp(a_tot) + _dot(vb, khat, _TN)


def _hg_body(qf_ref, kf_ref, vf_ref, lff_ref, qb_ref, kb_ref, vb_ref, lfb_ref,
             gf_ref, mf_ref, sf_ref, gb_ref, mb_ref, sb_ref,
             of_ref, ob_ref, st_ref):
    @pl.when(pl.program_id(1) == 0)
    def _():
        st_ref[...] = jnp.zeros(st_ref.shape, F32)

    nsub = SCAN_STEP // HG_CHUNK

    def sub(c, carry):
        r0 = pl.multiple_of(c * HG_CHUNK, HG_CHUNK)
        _hg_unit(False, r0, qf_ref, kf_ref, vf_ref, lff_ref, gf_ref, mf_ref, sf_ref, of_ref, st_ref.at[0])
        r1 = pl.multiple_of((nsub - 1 - c) * HG_CHUNK, HG_CHUNK)
        _hg_unit(True, r1, qb_ref, kb_ref, vb_ref, lfb_ref, gb_ref, mb_ref, sb_ref, ob_ref, st_ref.at[1])
        return carry

    lax.fori_loop(0, nsub, sub, 0)


def _hgrn2(hgq, hgk, hgv, hglf, nx):
    bsz, tall, kw = hgq.shape
    steps = tall // SCAN_STEP
    cf = _hg_constants(False)
    cb = _hg_constants(True)
    blk = (1, SCAN_STEP, kw)
    fwd = lambda col: (lambda b, s: (b, _fwd_blk(s, nx), col))
    bwd = lambda col: (lambda b, s: (b, _bwd_blk(s, nx), col))
    const = lambda a: pl.BlockSpec(a.shape, lambda b, s: (0,) * a.ndim)
    in_specs = [pl.BlockSpec(blk, fwd(0)), pl.BlockSpec(blk, fwd(0)), pl.BlockSpec(blk, fwd(0)), pl.BlockSpec(blk, fwd(0)),
                pl.BlockSpec(blk, bwd(0)), pl.BlockSpec(blk, bwd(1)), pl.BlockSpec(blk, bwd(0)), pl.BlockSpec(blk, bwd(1))]
    in_specs += [const(a) for a in cf + cb]
    out_sds = jax.ShapeDtypeStruct((bsz, tall, kw), F32)
    return pl.pallas_call(
        _hg_body,
        grid=(bsz, steps),
        in_specs=in_specs,
        out_specs=[pl.BlockSpec(blk, fwd(0)), pl.BlockSpec(blk, bwd(0))],
        out_shape=[out_sds, out_sds],
        scratch_shapes=[pltpu.VMEM((2, N_HEADS, HEAD_DIM, HEAD_DIM), F32)],
        compiler_params=_params(("arbitrary", "arbitrary")),
        name="hgrn2_scan",
    )(hgq, hgk, hgv, hglf, hgq, hgk, hgv, hglf, *cf, *cb)


def _ml_unit(rev, r0, d, q_ref, k_ref, v_ref, g_ref, o_ref, ct_ref, n_ref, m_ref):
    c = ML_CHUNK
    ii = lax.broadcasted_iota(I32, (c, c), 0)
    jj = lax.broadcasted_iota(I32, (c, c), 1)
    tri = (jj >= ii) if rev else (jj <= ii)
    eye = ii == jj
    last = 0 if rev else c - 1
    gates = g_ref[0, :, pl.ds(r0, c)]
    for h in range(N_HEADS):
        cs = slice(h * HEAD_DIM, (h + 1) * HEAD_DIM)
        qb = q_ref[0, pl.ds(r0, c), cs]
        kb = k_ref[0, pl.ds(r0, c), cs]
        vb = v_ref[0, pl.ds(r0, c), cs]
        qf = qb.astype(F32)
        kf = kb.astype(F32)
        irow = gates[d * 8 + h:d * 8 + h + 1]
        frow = gates[d * 8 + N_HEADS + h:d * 8 + N_HEADS + h + 1]
        bcol = jnp.sum(jnp.where(tri, frow, 0.0), axis=1, keepdims=True)
        brow = jnp.sum(jnp.where(eye, bcol, 0.0), axis=0, keepdims=True)
        icol = jnp.sum(jnp.where(eye, irow, 0.0), axis=1, keepdims=True)
        dm = jnp.where(tri, bcol - brow + irow, -jnp.inf)
        mprev = m_ref[h][:, 0:1]
        inter = bcol + mprev
        mrow = jnp.maximum(inter, jnp.max(dm, axis=1, keepdims=True))
        w_inter = jnp.exp(inter - mrow)
        qk = _dot(qb, kb, _NT) * jnp.exp(dm - mrow)
        ct = ct_ref[h]
        nrow = n_ref[h]
        num = _dot(qk.astype(BF16), vb) + w_inter * _dot(qb, ct.astype(BF16), _NT)
        den = jnp.sum(qk, axis=1, keepdims=True) + w_inter * jnp.sum(qf * nrow, axis=1, keepdims=True)
        o_ref[0, pl.ds(r0, c), cs] = num / jnp.maximum(jnp.abs(den), jnp.exp(-mrow))
        mnew = mrow[last:last + 1]
        blast = bcol[last:last + 1]
        wk = jnp.exp(blast - bcol + icol - mnew)
        ws = jnp.exp(blast + mprev - mnew)
        kh = kf * wk
        ct_ref[h] = ws * ct + _dot(vb, kh.astype(BF16), _TN)
        n_ref[h] = ws * nrow + jnp.sum(kh, axis=0, keepdims=True)
        m_ref[h] = jnp.broadcast_to(mnew, (1, HEAD_DIM))


def _ml_body(qf_ref, kf_ref, vf_ref, gf_ref, qb_ref, kb_ref, vb_ref, gb_ref,
             of_ref, ob_ref, ct_ref, n_ref, m_ref):
    @pl.when(pl.program_id(1) == 0)
    def _():
        ct_ref[...] = jnp.zeros(ct_ref.shape, F32)
        n_ref[...] = jnp.zeros(n_ref.shape, F32)
        m_ref[...] = jnp.zeros(m_ref.shape, F32)

    nsub = SCAN_STEP // ML_CHUNK

    def sub(c, carry):
        r0 = pl.multiple_of(c * ML_CHUNK, ML_CHUNK)
        _ml_unit(False, r0, 0, qf_ref, kf_ref, vf_ref, gf_ref, of_ref, ct_ref.at[0], n_ref.at[0], m_ref.at[0])
        r1 = pl.multiple_of((nsub - 1 - c) * ML_CHUNK, ML_CHUNK)
        _ml_unit(True, r1, 1, qb_ref, kb_ref, vb_ref, gb_ref, ob_ref, ct_ref.at[1], n_ref.at[1], m_ref.at[1])
        return carry

    lax.fori_loop(0, nsub, sub, 0)


def _mlstm(mlqk, mlv, gates, nx):
    bsz, tall, kw = mlv.shape
    steps = tall // SCAN_STEP
    blk = (1, SCAN_STEP, kw)
    gblk = (1, 4 * N_HEADS, SCAN_STEP)
    fwd = lambda col: (lambda b, s: (b, _fwd_blk(s, nx), col))
    bwd = lambda col: (lambda b, s: (b, _bwd_blk(s, nx), col))
    in_specs = [pl.BlockSpec(blk, fwd(0)), pl.BlockSpec(blk, fwd(1)), pl.BlockSpec(blk, fwd(0)),
                pl.BlockSpec(gblk, lambda b, s: (b, 0, _fwd_blk(s, nx))),
                pl.BlockSpec(blk, bwd(0)), pl.BlockSpec(blk, bwd(1)), pl.BlockSpec(blk, bwd(0)),
                pl.BlockSpec(gblk, lambda b, s: (b, 0, _bwd_blk(s, nx)))]
    out_sds = jax.ShapeDtypeStruct((bsz, tall, kw), F32)
    return pl.pallas_call(
        _ml_body,
        grid=(bsz, steps),
        in_specs=in_specs,
        out_specs=[pl.BlockSpec(blk, fwd(0)), pl.BlockSpec(blk, bwd(0))],
        out_shape=[out_sds, out_sds],
        scratch_shapes=[pltpu.VMEM((2, N_HEADS, HEAD_DIM, HEAD_DIM), F32),
                        pltpu.VMEM((2, N_HEADS, 1, HEAD_DIM), F32),
                        pltpu.VMEM((2, N_HEADS, 1, HEAD_DIM), F32)],
        compiler_params=_params(("arbitrary", "arbitrary")),
        name="mlstm_scan",
    )(mlqk, mlqk, mlv, gates, mlqk, mlqk, mlv, gates)


def _out_body(hof_ref, hob_ref, mhf_ref, mhb_ref, hgg_ref, mlo_ref, x_ref, g1_ref, sc2_ref, sh2_ref,
              hnw_ref, mnw_ref, wout_ref, n2w_ref, rwt_ref, x1_ref, vt_ref, aff_ref):
    hg = hof_ref[0] + hob_ref[0]
    ml = mhf_ref[0] + mhb_ref[0]
    hparts, mparts = [], []
    for h in range(N_HEADS):
        cs = slice(h * HEAD_DIM, (h + 1) * HEAD_DIM)
        t = hg[:, cs]
        hparts.append(t * lax.rsqrt(jnp.mean(t * t, axis=-1, keepdims=True) + NORM_EPS))
        t = ml[:, cs]
        t = t - jnp.mean(t, axis=-1, keepdims=True)
        mparts.append(t * lax.rsqrt(jnp.mean(t * t, axis=-1, keepdims=True) + NORM_EPS))
    hgn = jnp.concatenate(hparts, axis=-1) * hnw_ref[...] * hgg_ref[0].astype(F32)
    mln = jnp.concatenate(mparts, axis=-1) * mnw_ref[...] * mlo_ref[0].astype(F32)
    mix = jnp.concatenate([hgn, mln], axis=-1).astype(BF16)
    x1 = x_ref[0] + g1_ref[0] * _dot(mix, wout_ref[...])
    x1_ref[0] = x1
    v = x1 * lax.rsqrt(jnp.mean(x1 * x1, axis=-1, keepdims=True) + NORM_EPS) * n2w_ref[...]
    v = v * (1.0 + sc2_ref[0]) + sh2_ref[0]
    vt_ref[0] = _pack_halves(v)
    logits = _dot(rwt_ref[...], v, _NT, precision=HIGHEST)
    ex = jnp.exp(logits - jnp.max(logits, axis=0, keepdims=True))
    aff_ref[0] = ex / jnp.sum(ex, axis=0, keepdims=True)


def _mixer_out(hg_f, hg_b, ml_f, ml_b, hgg, mlo, x, g1, sc2, sh2, hnw, mnw, w_out, n2w, rwt, tm):
    bsz, t, d = x.shape
    kw = HEADS_W
    ne = rwt.shape[0]
    row = lambda b, i: (b, i, 0)
    mod = lambda b, i: (b, 0, 0)
    const2 = lambda b, i: (0, 0)
    act = pl.BlockSpec((1, tm, kw), row)
    in_specs = [act, act, act, act, act, act,
                pl.BlockSpec((1, tm, d), row),
                pl.BlockSpec((1, 1, d), mod), pl.BlockSpec((1, 1, d), mod), pl.BlockSpec((1, 1, d), mod),
                pl.BlockSpec((1, kw), const2), pl.BlockSpec((1, kw), const2),
                pl.BlockSpec(w_out.shape, const2), pl.BlockSpec((1, d), const2), pl.BlockSpec(rwt.shape, const2)]
    return pl.pallas_call(
        _out_body,
        grid=(bsz, t // tm),
        in_specs=in_specs,
        out_specs=[pl.BlockSpec((1, tm, d), row),
                   pl.BlockSpec((1, tm, d // 2), row),
                   pl.BlockSpec((1, ne, tm), lambda b, i: (b, 0, i))],
        out_shape=[jax.ShapeDtypeStruct((bsz, t, d), F32),
                   jax.ShapeDtypeStruct((bsz, t, d // 2), I32),
                   jax.ShapeDtypeStruct((bsz, ne, t), F32)],
        compiler_params=_params(("arbitrary", "arbitrary")),
        name="mixer_out",
    )(hg_f, hg_b, ml_f, ml_b, hgg, mlo, x, g1, sc2, sh2, hnw, mnw, w_out, n2w, rwt)


def _prefix_count(maskf, u_ref, ones_ref, bl_ref):
    e, nb, ln = maskf.shape
    x = maskf.reshape(e * nb, ln)
    xb = x.astype(BF16)
    incl = _dot(xb, u_ref[...])
    tot = _dot(xb, ones_ref[...])
    off = _dot(bl_ref[...], tot.astype(BF16))
    return (incl - x + off).reshape(e, nb, ln), off.reshape(e, nb, ln)


def _topk_body(aff_ref, u_ref, ones_ref, bl_ref, pos_ref, cnt_ref, *, cap):
    x = aff_ref[0]
    ne = x.shape[0]

    def count(m):
        return jnp.sum(jnp.sum(jnp.where(m, 1.0, 0.0), axis=1, keepdims=True), axis=2, keepdims=True)

    def halve(_, carry):
        lo, hi = carry
        mid = 0.5 * (lo + hi)
        up = count(x > mid) >= cap
        return jnp.where(up, mid, lo), jnp.where(up, hi, mid)

    lo, hi = lax.fori_loop(0, TOPK_BISECTIONS, halve,
                           (jnp.full((ne, 1, 1), -1.0, F32), jnp.full((ne, 1, 1), 1.0, F32)))
    gt = jnp.where(x > hi, 1.0, 0.0)
    eq = jnp.where(x > lo, 1.0, 0.0) - gt
    need = cap - count(x > hi)
    eq_rank, _ = _prefix_count(eq, u_ref, ones_ref, bl_ref)
    sel = gt + eq * jnp.where(eq_rank < need, 1.0, 0.0)
    pos, off = _prefix_count(sel, u_ref, ones_ref, bl_ref)
    pos_ref[0] = jnp.where(sel > 0, pos, -1.0).astype(I32)
    cnt_ref[0] = off.astype(I32)


def _topk(aff4, cap):
    bsz, ne, nb, ln = aff4.shape
    k = np.arange(ln)
    u = jnp.asarray(k[:, None] <= k[None, :], BF16)
    ones = jnp.ones((ln, ln), BF16)
    r = np.arange(ne * nb)
    bl = jnp.asarray(((r[:, None] // nb) == (r[None, :] // nb)) & (r[None, :] < r[:, None]), BF16)
    blk = pl.BlockSpec((1, ne, nb, ln), lambda b: (b, 0, 0, 0))
    const = lambda a: pl.BlockSpec(a.shape, lambda b: (0, 0))
    sds = jax.ShapeDtypeStruct((bsz, ne, nb, ln), I32)
    return pl.pallas_call(
        functools.partial(_topk_body, cap=cap),
        grid=(bsz,),
        in_specs=[blk, const(u), const(ones), const(bl)],
        out_specs=[blk, blk],
        out_shape=[sds, sds],
        compiler_params=_params(("arbitrary",)),
        name="expert_topk",
    )(aff4, u, ones, bl)


def _sc_invert_gather(table, pos_flat, npairs, ne, t, cap):
    info = plsc.get_sparse_core_info()
    nc, lanes = info.num_cores, info.num_lanes
    nw = nc * info.num_subcores
    per_w = npairs // nw
    width = table.shape[1]
    chunk = SC_GATHER_ROWS
    mesh = plsc.VectorSubcoreMesh(core_axis_name="c", subcore_axis_name="s")
    cp = dataclasses.replace(pltpu.CompilerParams(), needs_layout_passes=False)

    @functools.partial(
        pl.kernel, mesh=mesh, compiler_params=cp,
        out_type=[jax.ShapeDtypeStruct((npairs * cap, width), table.dtype),
                  jax.ShapeDtypeStruct((npairs * cap,), I32)],
        scratch_types=[pltpu.VMEM((t,), I32), pltpu.VMEM((cap,), I32), pltpu.VMEM((chunk,), I32),
                       pltpu.VMEM((chunk, width), table.dtype), pltpu.SemaphoreType.DMA],
    )
    def body(table_hbm, pos_hbm, out_hbm, idx_hbm, pos_v, idx_v, ich_v, rows_v, sem):
        wid = lax.axis_index("s") * nc + lax.axis_index("c")

        @pl.loop(0, per_w)
        def _(kk):
            p = wid * per_w + kk
            tok0 = (p // ne) * t
            pltpu.sync_copy(pos_hbm.at[pl.ds(pl.multiple_of(p * t, t), t)], pos_v)

            @pl.loop(0, t // lanes)
            def _(i):
                v = pos_v[pl.ds(pl.multiple_of(i * lanes, lanes), lanes)]
                tok = lax.iota(I32, lanes) + (i * lanes + tok0)
                plsc.store_scatter(idx_v, [v], tok, mask=v >= 0)

            pltpu.sync_copy(idx_v, idx_hbm.at[pl.ds(pl.multiple_of(p * cap, cap), cap)])

            @pl.loop(0, cap // chunk)
            def _(c):
                for q in range(chunk // lanes):
                    src = pl.ds(pl.multiple_of(c * chunk + q * lanes, lanes), lanes)
                    ich_v[pl.ds(q * lanes, lanes)] = idx_v[src]
                pltpu.async_copy(table_hbm.at[ich_v], rows_v, sem).wait()
                pltpu.sync_copy(rows_v, out_hbm.at[pl.ds(pl.multiple_of(p * cap + c * chunk, chunk), chunk)])

    return body(table, pos_flat)


def _onehot_t(pos_row, s):
    n = ONEHOT_BLK
    slot = lax.broadcasted_iota(I32, (n, pos_row.shape[-1]), 0) + s * n
    return jnp.where(slot == pos_row, 1.0, 0.0).astype(BF16)


def _moe1_body(xs_ref, wg_ref, wu_ref, wd_ref, yt_ref):
    xs = _unpack_halves(xs_ref[0])
    hg = _dot(xs, wg_ref[0])
    hu = _dot(xs, wu_ref[0])
    h = (_silu(hg) * hu).astype(BF16)
    yt_ref[0, 0] = _dot(wd_ref[0], h, _NT).astype(BF16)


def _moe_experts(xs, wg, wu, wdt, bsz):
    npairs, cap, half = xs.shape
    ne, d, f = wg.shape
    return pl.pallas_call(
        _moe1_body,
        grid=(bsz, ne),
        in_specs=[pl.BlockSpec((1, cap, half), lambda b, e: (b * ne + e, 0, 0)),
                  pl.BlockSpec((1, d, f), lambda b, e: (e, 0, 0)),
                  pl.BlockSpec((1, d, f), lambda b, e: (e, 0, 0)),
                  pl.BlockSpec((1, d, f), lambda b, e: (e, 0, 0))],
        out_specs=pl.BlockSpec((1, 1, d, cap), lambda b, e: (b, e, 0, 0)),
        out_shape=jax.ShapeDtypeStruct((bsz, ne, d, cap), BF16),
        compiler_params=_params(("arbitrary", "arbitrary")),
        name="moe_experts",
    )(xs, wg, wu, wdt)


def _moe2_body(cnt_ref, pos_ref, aff_ref, yt_ref, out_ref, *, tb, cap):
    n = ONEHOT_BLK
    b, t, e = pl.program_id(0), pl.program_id(1), pl.program_id(2)
    row = b * pl.num_programs(2) + e

    @pl.when(e == 0)
    def _():
        out_ref[...] = jnp.zeros(out_ref.shape, F32)

    for j in range(tb // n):
        blk = t * (tb // 128) + j * (n // 128)
        c0 = cnt_ref[row, blk]
        c1 = cnt_ref[row, blk + n // 128]
        for s in range(cap // n):
            @pl.when((c0 < (s + 1) * n) & (c1 > s * n))
            def _(j=j, s=s):
                ts = slice(j * n, (j + 1) * n)
                pt = _onehot_t(pos_ref[0, 0, :, ts], s)
                contrib = _dot(yt_ref[0, 0, :, s * n:(s + 1) * n], pt)
                out_ref[0, :, ts] += contrib * aff_ref[0, 0, :, ts]


def _moe_combine(cnt, pos, aff, yt, cap, tb):
    bsz, ne, d, _ = yt.shape
    t = pos.shape[-1]
    grid_spec = pltpu.PrefetchScalarGridSpec(
        num_scalar_prefetch=1,
        grid=(bsz, t // tb, ne),
        in_specs=[pl.BlockSpec((1, 1, 1, tb), lambda b, i, e, c: (b, e, 0, i)),
                  pl.BlockSpec((1, 1, 1, tb), lambda b, i, e, c: (b, e, 0, i)),
                  pl.BlockSpec((1, 1, d, cap), lambda b, i, e, c: (b, e, 0, 0))],
        out_specs=pl.BlockSpec((1, d, tb), lambda b, i, e, c: (b, 0, i)),
    )
    return pl.pallas_call(
        functools.partial(_moe2_body, tb=tb, cap=cap),
        grid_spec=grid_spec,
        out_shape=jax.ShapeDtypeStruct((bsz, d, t), F32),
        compiler_params=_params(("arbitrary", "arbitrary", "arbitrary")),
        name="moe_combine",
    )(cnt, pos, aff, yt)


def _final_body(x1_ref, mt_ref, g2_ref, fw_ref, o_ref):
    x2 = x1_ref[0] + g2_ref[0] * mt_ref[0].T
    o_ref[0] = x2 * lax.rsqrt(jnp.mean(x2 * x2, axis=-1, keepdims=True) + NORM_EPS) * fw_ref[...]


def _final(x1, moe_t, g2, fw, tm):
    bsz, t, d = x1.shape
    return pl.pallas_call(
        _final_body,
        grid=(bsz, t // tm),
        in_specs=[pl.BlockSpec((1, tm, d), lambda b, i: (b, i, 0)),
                  pl.BlockSpec((1, d, tm), lambda b, i: (b, 0, i)),
                  pl.BlockSpec((1, 1, d), lambda b, i: (b, 0, 0)),
                  pl.BlockSpec((1, d), lambda b, i: (0, 0))],
        out_specs=pl.BlockSpec((1, tm, d), lambda b, i: (b, i, 0)),
        out_shape=jax.ShapeDtypeStruct((bsz, t, d), F32),
        compiler_params=_params(("arbitrary", "arbitrary")),
        name="final_norm",
    )(x1, moe_t, g2, fw)


def kernel(x, c, ctx, c_ctx, ada_w, ada_b, norm1_w, w_in, conv_w, conv_b, hg_lb_logits, ml_gate_b,
           hg_norm_w, ml_norm_w, w_out, norm2_w, router_w, exp_w_gate, exp_w_up, exp_w_down, final_norm_w):
    bsz, t, d = x.shape
    nctx = ctx.shape[1]
    assert ada_w.shape[0] == 1, "single-layer block"
    assert nctx == SCAN_STEP and t % SCAN_STEP == 0 and bsz + 1 <= 8
    tall = t + nctx
    nx = t // SCAN_STEP
    kw = HEADS_W
    ne = router_w.shape[-1]
    cap = EC_CAPACITY * t // ne
    tm = min(512, t)
    tb = min(2048, t)

    rows = jnp.concatenate([c, c_ctx[None], jnp.zeros((7 - bsz, d), F32)], axis=0)
    mod = _modulation(rows, ada_w[0], ada_b[0][None])
    mx = [m[:, None, :] for m in jnp.split(mod[:bsz], 6, axis=-1)]
    mc = [m[:, None, :] for m in jnp.split(mod[bsz:bsz + 1], 6, axis=-1)]
    sh1, sc1, g1, sh2, sc2, g2 = mx
    csh1, csc1 = mc[0], mc[1]

    main_w = 9 * kw
    w_main = w_in[0][:, :main_w].astype(BF16)
    w_gt = w_in[0][:, main_w:].T
    gate_b = ml_gate_b[0][:, None]
    nw1 = norm1_w[0][None]
    outs = _inproj(x, sc1, sh1, nw1, w_main, w_gt, gate_b, hg_lb_logits, tall, tm, 0)
    outs = _inproj(ctx, csc1, csh1, nw1, w_main, w_gt, gate_b, hg_lb_logits, tall, nctx, t // nctx, prev=outs)
    hgq, hgv, hgg, hgk, hglf, mlqk_pre, mlv, mlo, gates = outs

    mlqk = _conv(mlqk_pre, conv_w[0], conv_b[0][None], t, nctx)
    hg_f, hg_b = _hgrn2(hgq, hgk, hgv, hglf, nx)
    ml_f, ml_b = _mlstm(mlqk, mlv, gates, nx)

    x1, vpk, aff = _mixer_out(hg_f, hg_b, ml_f, ml_b, hgg, mlo, x, g1, sc2, sh2,
                             hg_norm_w[0][None], ml_norm_w[0][None], w_out[0].astype(BF16),
                             norm2_w[0][None], router_w[0].T, tm)

    pos4, cnt4 = _topk(aff.reshape(bsz, ne, t // 128, 128), cap)
    pos = pos4.reshape(bsz, ne, 1, t)
    cnt = jnp.concatenate([cnt4[..., 0].reshape(bsz * ne, t // 128),
                           jnp.full((bsz * ne, 1), cap, I32)], axis=1)
    xs, _ = _sc_invert_gather(vpk.reshape(bsz * t, d // 2), pos4.reshape(-1), bsz * ne, ne, t, cap)
    wg = exp_w_gate[0].astype(BF16)
    wu = exp_w_up[0].astype(BF16)
    wdt = jnp.swapaxes(exp_w_down[0], 1, 2).astype(BF16)
    yt = _moe_experts(xs.reshape(bsz * ne, cap, d // 2), wg, wu, wdt, bsz)
    moe_t = _moe_combine(cnt, pos, aff.reshape(bsz, ne, 1, t), yt, cap, tb)
    return _final(x1, moe_t, g2, final_norm_w[None], tm)
```

```python
import dataclasses
import functools

import numpy as np
import jax
import jax.numpy as jnp
from jax import lax
from jax.experimental import pallas as pl
from jax.experimental.pallas import tpu as pltpu
from jax.experimental.pallas import tpu_sc as plsc

F32 = jnp.float32
BF16 = jnp.bfloat16
I32 = jnp.int32
HIGHEST = lax.Precision.HIGHEST
NORM_EPS = 1e-6

HEAD_DIM = 128
N_HEADS = 4
HEADS_W = N_HEADS * HEAD_DIM
GRID_W = 64
N_EXPERTS = 16
EC_CAPACITY = 2
HG_CHUNK = 64
ML_CHUNK = 128
SCAN_STEP = 256
ONEHOT_BLK = 256
CONV_HALO = 72
SC_GATHER_ROWS = 64
TOPK_BISECTIONS = 64
VMEM_LIMIT = 56 * 1024 * 1024

_NT = (((1,), (1,)), ((), ()))
_TN = (((0,), (0,)), ((), ()))


def _dot(a, b, dims=None, precision=None):
    if dims is None:
        return jnp.dot(a, b, preferred_element_type=F32, precision=precision)
    return lax.dot_general(a, b, dims, preferred_element_type=F32, precision=precision)


def _sigmoid(x):
    return jax.nn.sigmoid(x)


def _pack_halves(x):
    w = x.shape[-1] // 2
    bits = lax.bitcast_convert_type(x.astype(BF16).astype(F32), I32)
    return lax.shift_right_logical(bits[:, :w], 16) | bits[:, w:]


def _unpack_halves(p):
    lo = lax.bitcast_convert_type(lax.shift_left(p, 16), F32)
    hi = lax.bitcast_convert_type(p & jnp.int32(-65536), F32)
    return jnp.concatenate([lo, hi], axis=-1).astype(BF16)


def _silu(x):
    return x * jax.nn.sigmoid(x)


def _params(sem):
    return pltpu.CompilerParams(dimension_semantics=sem, vmem_limit_bytes=VMEM_LIMIT)


def _mod_body(r_ref, w_ref, b_ref, o_ref):
    r = r_ref[...]
    o_ref[...] = _dot(_silu(r), w_ref[...], precision=HIGHEST) + b_ref[...]


def _modulation(rows, w, b):
    d, n = w.shape
    tn = n // 4
    return pl.pallas_call(
        _mod_body,
        grid=(n // tn,),
        in_specs=[pl.BlockSpec((8, d), lambda j: (0, 0)),
                  pl.BlockSpec((d, tn), lambda j: (0, j)),
                  pl.BlockSpec((1, tn), lambda j: (0, j))],
        out_specs=pl.BlockSpec((8, tn), lambda j: (0, j)),
        out_shape=jax.ShapeDtypeStruct((8, n), F32),
        compiler_params=_params(("arbitrary",)),
        name="modulation",
    )(rows, w, b)


def _log_sigmoid(x):
    return jnp.minimum(x, 0.0) - jnp.log(1.0 + jnp.exp(-jnp.abs(x)))


def _inproj_body(x_ref, sc_ref, sh_ref, nw_ref, w_ref, wg_ref, gb_ref, lbl_ref, *refs):
    hgq_ref, hgv_ref, hgg_ref, hgk_ref, hglf_ref, mlqk_ref, mlv_ref, mlo_ref, gates_ref = refs[-9:]
    kw = HEADS_W
    x = x_ref[0]
    y = x * lax.rsqrt(jnp.mean(x * x, axis=-1, keepdims=True) + NORM_EPS) * nw_ref[...]
    u = y * (1.0 + sc_ref[0]) + sh_ref[0]
    ub = u.astype(BF16)

    def proj(c0, c1):
        return _dot(ub, w_ref[:, c0:c1])

    hgq_ref[0] = _silu(proj(0, kw)).astype(BF16)
    hgv_ref[0] = proj(kw, 2 * kw).astype(BF16)
    hgg_ref[0] = _silu(proj(2 * kw, 3 * kw)).astype(BF16)

    lbl = lbl_ref[...]
    mx = jnp.max(lbl, axis=0)
    ex = jnp.exp(lbl - mx[None])
    lb = ex[0] / jnp.sum(ex, axis=0)
    for d in range(2):
        p = proj((3 + d) * kw, (4 + d) * kw)
        lbd = lb[d:d + 1]
        f = lbd + (1.0 - lbd) * _sigmoid(p)
        hgk_ref[0, :, d * kw:(d + 1) * kw] = (1.0 - f).astype(BF16)
        hglf_ref[0, :, d * kw:(d + 1) * kw] = jnp.log(f)

    mlqk_ref[0, :, 0:kw] = proj(5 * kw, 6 * kw)
    mlqk_ref[0, :, kw:2 * kw] = proj(6 * kw, 7 * kw)
    mlv_ref[0] = proj(7 * kw, 8 * kw).astype(BF16)
    mlo_ref[0] = _sigmoid(proj(8 * kw, 9 * kw)).astype(BF16)

    g = _dot(wg_ref[...], u, _NT, precision=HIGHEST) + gb_ref[...]
    row = lax.broadcasted_iota(I32, g.shape, 0)
    gates_ref[0] = jnp.where((row % 8) >= N_HEADS, _log_sigmoid(g), g)


def _inproj_shapes(bsz, tall):
    kw = HEADS_W
    return [
        jax.ShapeDtypeStruct((bsz, tall, kw), BF16),
        jax.ShapeDtypeStruct((bsz, tall, kw), BF16),
        jax.ShapeDtypeStruct((bsz, tall, kw), BF16),
        jax.ShapeDtypeStruct((bsz, tall, 2 * kw), BF16),
        jax.ShapeDtypeStruct((bsz, tall, 2 * kw), F32),
        jax.ShapeDtypeStruct((bsz, tall, 2 * kw), F32),
        jax.ShapeDtypeStruct((bsz, tall, kw), BF16),
        jax.ShapeDtypeStruct((bsz, tall, kw), BF16),
        jax.ShapeDtypeStruct((bsz, 4 * N_HEADS, tall), F32),
    ]


def _inproj(tokens, scale, shift, nw, w_main, w_gt, gate_b, lb_logits, tall, tm, blk0, prev=None):
    bsz, n, d = tokens.shape
    kw = HEADS_W
    nt = n // tm
    per_sample = scale.shape[0] == bsz
    mod_map = (lambda b, i: (b, 0, 0)) if per_sample else (lambda b, i: (0, 0, 0))
    const2 = lambda b, i: (0, 0)
    in_specs = [
        pl.BlockSpec((1, tm, d), lambda b, i: (b, i, 0)),
        pl.BlockSpec((1, 1, d), mod_map),
        pl.BlockSpec((1, 1, d), mod_map),
        pl.BlockSpec((1, d), const2),
        pl.BlockSpec(w_main.shape, const2),
        pl.BlockSpec(w_gt.shape, const2),
        pl.BlockSpec(gate_b.shape, const2),
        pl.BlockSpec(lb_logits.shape, lambda b, i: (0, 0, 0)),
    ]
    args = [tokens, scale, shift, nw, w_main, w_gt, gate_b, lb_logits]
    aliases = {}
    if prev is not None:
        for k, a in enumerate(prev):
            in_specs.append(pl.BlockSpec(memory_space=pl.ANY))
            aliases[len(args)] = k
            args.append(a)
    row_map = lambda b, i: (b, blk0 + i, 0)
    widths = [kw, kw, kw, 2 * kw, 2 * kw, 2 * kw, kw, kw]
    out_specs = [pl.BlockSpec((1, tm, w), row_map) for w in widths]
    out_specs.append(pl.BlockSpec((1, 4 * N_HEADS, tm), lambda b, i: (b, 0, blk0 + i)))
    return pl.pallas_call(
        _inproj_body,
        grid=(bsz, nt),
        in_specs=in_specs,
        out_specs=out_specs,
        out_shape=_inproj_shapes(bsz, tall),
        input_output_aliases=aliases,
        compiler_params=_params(("arbitrary", "arbitrary")),
        name="inproj_ctx" if prev is not None else "inproj_x",
    )(*args)


def _conv_body(x_ref, w_ref, b_ref, o_ref, pad_ref, cpad_ref, *, t, nctx, scale_from):
    halo = CONV_HALO
    rows = 512
    win = rows + 2 * halo
    ch = x_ref.shape[-1]
    scale = jnp.where(pl.program_id(1) >= scale_from, HEAD_DIM ** -0.5, 1.0).astype(F32)
    w = w_ref[...]
    bias = b_ref[...]

    pad_ref[0:halo, :] = jnp.zeros((halo, ch), F32)
    pad_ref[halo + t:halo + t + halo, :] = jnp.zeros((halo, ch), F32)
    pad_ref[halo:halo + t, :] = x_ref[0, 0:t, :]
    col = (lax.broadcasted_iota(I32, (win, ch), 0) + (GRID_W - halo % GRID_W)) % GRID_W
    left_ok = col > 0
    right_ok = col < GRID_W - 1

    def chunk(c, carry):
        o = pl.multiple_of(c * rows, rows)
        xw = pad_ref[pl.ds(o, win), :]
        xm = jnp.where(left_ok, pltpu.roll(xw, 1, 0), 0.0)
        xp = jnp.where(right_ok, pltpu.roll(xw, win - 1, 0), 0.0)
        z = [xm * w[dr, 0:1] + xw * w[dr, 1:2] + xp * w[dr, 2:3] for dr in range(3)]
        y = (z[1][halo:halo + rows]
             + z[0][halo - GRID_W:halo - GRID_W + rows]
             + z[2][halo + GRID_W:halo + GRID_W + rows])
        o_ref[0, pl.ds(o, rows), :] = (_silu(y + bias) * scale).astype(o_ref.dtype)
        return carry

    lax.fori_loop(0, t // rows, chunk, 0)

    cpad_ref[0:8, :] = jnp.zeros((8, ch), F32)
    cpad_ref[8 + nctx:16 + nctx, :] = jnp.zeros((8, ch), F32)
    cpad_ref[8:8 + nctx, :] = x_ref[0, t:t + nctx, :]
    xw = cpad_ref[...]
    n = nctx + 16
    y = (pltpu.roll(xw, 1, 0) * w[1, 0:1] + xw * w[1, 1:2] + pltpu.roll(xw, n - 1, 0) * w[1, 2:3])[8:8 + nctx]
    o_ref[0, t:t + nctx, :] = (_silu(y + bias) * scale).astype(o_ref.dtype)


def _conv(qk_pre, conv_w, conv_b, t, nctx):
    bsz, tall, c = qk_pre.shape
    ch = 128
    body = functools.partial(_conv_body, t=t, nctx=nctx, scale_from=(c // 2) // ch)
    return pl.pallas_call(
        body,
        grid=(bsz, c // ch),
        in_specs=[pl.BlockSpec((1, tall, ch), lambda b, j: (b, 0, j)),
                  pl.BlockSpec((3, 3, ch), lambda b, j: (0, 0, j)),
                  pl.BlockSpec((1, ch), lambda b, j: (0, j))],
        out_specs=pl.BlockSpec((1, tall, ch), lambda b, j: (b, 0, j)),
        out_shape=jax.ShapeDtypeStruct((bsz, tall, c), BF16),
        scratch_shapes=[pltpu.VMEM((t + 2 * CONV_HALO, ch), F32),
                        pltpu.VMEM((nctx + 16, ch), F32)],
        compiler_params=_params(("arbitrary", "arbitrary")),
        name="qk_conv",
    )(qk_pre, conv_w, conv_b)


def _fwd_blk(s, nx):
    return jnp.where(s == 0, nx, s - 1)


def _bwd_blk(s, nx):
    return jnp.where(s == 0, nx, nx - s)


def _hg_constants(rev):
    c = HG_CHUNK
    i = np.arange(c)[:, None]
    j = np.arange(c)[None, :]
    blocks = [(j >= i) if rev else (j <= i), (j < i) if rev else (j > i)]
    masks = [i == j]
    qsel = []
    m = c // 2
    while m >= 1:
        b0 = (i // (2 * m)) * (2 * m)
        same = (i // (2 * m)) == (j // (2 * m))
        if rev:
            beta = b0 + m
            qrow = (i % (2 * m)) < m
            g = np.where(qrow, (j >= i) & (j < beta), (j >= beta) & (j < i))
            mask = same & qrow & ((j % (2 * m)) >= m)
        else:
            beta = b0 + m - 1
            qrow = (i % (2 * m)) >= m
            g = np.where(qrow, (j > beta) & (j <= i), (j > i) & (j <= beta))
            mask = same & qrow & ((j % (2 * m)) < m)
        blocks.append(g)
        masks.append(mask)
        qsel.append(np.broadcast_to(qrow, (c, HEAD_DIM)))
        m //= 2
    g = np.concatenate(blocks, axis=0).astype(np.float32)
    g3 = np.concatenate([g, g, g], axis=1)
    return (jnp.asarray(g3, BF16), jnp.asarray(np.stack(masks), F32), jnp.asarray(np.stack(qsel), F32))


def _hg_unit(rev, r0, q_ref, k_ref, v_ref, lf_ref, g_ref, msk_ref, qs_ref, o_ref, st_ref):
    c = HG_CHUNK
    lf = lf_ref[0, pl.ds(r0, c), :]
    p1 = lf.astype(BF16)
    r1 = lf - p1.astype(F32)
    p2 = r1.astype(BF16)
    p3 = (r1 - p2.astype(F32)).astype(BF16)
    dall = _dot(g_ref[...], jnp.concatenate([p1, p2, p3], axis=0))
    nlev = msk_ref.shape[0] - 1
    for h in range(N_HEADS):
        cs = slice(h * HEAD_DIM, (h + 1) * HEAD_DIM)
        qb = q_ref[0, pl.ds(r0, c), cs]
        kb = k_ref[0, pl.ds(r0, c), cs]
        vb = v_ref[0, pl.ds(r0, c), cs]
        qf = qb.astype(F32)
        kf = kb.astype(F32)
        a = dall[0:c, cs]
        a_end = dall[c:2 * c, cs]
        att = jnp.where(msk_ref[0] > 0, _dot(qb, kb, _NT), 0.0)
        for l in range(nlev):
            e = jnp.exp(dall[(l + 2) * c:(l + 3) * c, cs])
            z = (jnp.where(qs_ref[l] > 0, qf, kf) * e).astype(BF16)
            att = att + jnp.where(msk_ref[l + 1] > 0, _dot(z, z, _NT), 0.0)
        st = st_ref[h]
        qbar = (qf * jnp.exp(a)).astype(BF16)
        o = _dot(att.astype(BF16), vb) + _dot(qbar, st.astype(BF16), _NT)
        o_ref[0, pl.ds(r0, c), cs] = o
        khat = (kf * jnp.exp(a_end)).astype(BF16)
        a_tot = a[0:1] if rev else a[c - 1:c]
        st_ref[h] = st * jnp.exp(a_tot) + _dot(vb, khat, _TN)


def _hg_body(qf_ref, kf_ref, vf_ref, lff_ref, qb_ref, kb_ref, vb_ref, lfb_ref,
             gf_ref, mf_ref, sf_ref, gb_ref, mb_ref, sb_ref,
             of_ref, ob_ref, st_ref):
    @pl.when(pl.program_id(1) == 0)
    def _():
        st_ref[...] = jnp.zeros(st_ref.shape, F32)

    nsub = SCAN_STEP // HG_CHUNK

    def sub(c, carry):
        r0 = pl.multiple_of(c * HG_CHUNK, HG_CHUNK)
        _hg_unit(False, r0, qf_ref, kf_ref, vf_ref, lff_ref, gf_ref, mf_ref, sf_ref, of_ref, st_ref.at[0])
        r1 = pl.multiple_of((nsub - 1 - c) * HG_CHUNK, HG_CHUNK)
        _hg_unit(True, r1, qb_ref, kb_ref, vb_ref, lfb_ref, gb_ref, mb_ref, sb_ref, ob_ref, st_ref.at[1])
        return carry

    lax.fori_loop(0, nsub, sub, 0)


def _hgrn2(hgq, hgk, hgv, hglf, nx):
    bsz, tall, kw = hgq.shape
    steps = tall // SCAN_STEP
    cf = _hg_constants(False)
    cb = _hg_constants(True)
    blk = (1, SCAN_STEP, kw)
    fwd = lambda col: (lambda b, s: (b, _fwd_blk(s, nx), col))
    bwd = lambda col: (lambda b, s: (b, _bwd_blk(s, nx), col))
    const = lambda a: pl.BlockSpec(a.shape, lambda b, s: (0,) * a.ndim)
    in_specs = [pl.BlockSpec(blk, fwd(0)), pl.BlockSpec(blk, fwd(0)), pl.BlockSpec(blk, fwd(0)), pl.BlockSpec(blk, fwd(0)),
                pl.BlockSpec(blk, bwd(0)), pl.BlockSpec(blk, bwd(1)), pl.BlockSpec(blk, bwd(0)), pl.BlockSpec(blk, bwd(1))]
    in_specs += [const(a) for a in cf + cb]
    out_sds = jax.ShapeDtypeStruct((bsz, tall, kw), F32)
    return pl.pallas_call(
        _hg_body,
        grid=(bsz, steps),
        in_specs=in_specs,
        out_specs=[pl.BlockSpec(blk, fwd(0)), pl.BlockSpec(blk, bwd(0))],
        out_shape=[out_sds, out_sds],
        scratch_shapes=[pltpu.VMEM((2, N_HEADS, HEAD_DIM, HEAD_DIM), F32)],
        compiler_params=_params(("arbitrary", "arbitrary")),
        name="hgrn2_scan",
    )(hgq, hgk, hgv, hglf, hgq, hgk, hgv, hglf, *cf, *cb)


def _ml_unit(rev, r0, d, q_ref, k_ref, v_ref, g_ref, o_ref, ct_ref, n_ref, m_ref):
    c = ML_CHUNK
    ii = lax.broadcasted_iota(I32, (c, c), 0)
    jj = lax.broadcasted_iota(I32, (c, c), 1)
    tri = (jj >= ii) if rev else (jj <= ii)
    eye = ii == jj
    last = 0 if rev else c - 1
    gates = g_ref[0, :, pl.ds(r0, c)]
    for h in range(N_HEADS):
        cs = slice(h * HEAD_DIM, (h + 1) * HEAD_DIM)
        qb = q_ref[0, pl.ds(r0, c), cs]
        kb = k_ref[0, pl.ds(r0, c), cs]
        vb = v_ref[0, pl.ds(r0, c), cs]
        qf = qb.astype(F32)
        kf = kb.astype(F32)
        irow = gates[d * 8 + h:d * 8 + h + 1]
        frow = gates[d * 8 + N_HEADS + h:d * 8 + N_HEADS + h + 1]
        bcol = jnp.sum(jnp.where(tri, frow, 0.0), axis=1, keepdims=True)
        brow = jnp.sum(jnp.where(eye, bcol, 0.0), axis=0, keepdims=True)
        icol = jnp.sum(jnp.where(eye, irow, 0.0), axis=1, keepdims=True)
        dm = jnp.where(tri, bcol - brow + irow, -jnp.inf)
        mprev = m_ref[h][:, 0:1]
        inter = bcol + mprev
        mrow = jnp.maximum(inter, jnp.max(dm, axis=1, keepdims=True))
        w_inter = jnp.exp(inter - mrow)
        qk = _dot(qb, kb, _NT) * jnp.exp(dm - mrow)
        ct = ct_ref[h]
        nrow = n_ref[h]
        num = _dot(qk.astype(BF16), vb) + w_inter * _dot(qb, ct.astype(BF16), _NT)
        den = jnp.sum(qk, axis=1, keepdims=True) + w_inter * jnp.sum(qf * nrow, axis=1, keepdims=True)
        o_ref[0, pl.ds(r0, c), cs] = num / jnp.maximum(jnp.abs(den), jnp.exp(-mrow))
        mnew = mrow[last:last + 1]
        blast = bcol[last:last + 1]
        wk = jnp.exp(blast - bcol + icol - mnew)
        ws = jnp.exp(blast + mprev - mnew)
        kh = kf * wk
        ct_ref[h] = ws * ct + _dot(vb, kh.astype(BF16), _TN)
        n_ref[h] = ws * nrow + jnp.sum(kh, axis=0, keepdims=True)
        m_ref[h] = jnp.broadcast_to(mnew, (1, HEAD_DIM))


def _ml_body(qf_ref, kf_ref, vf_ref, gf_ref, qb_ref, kb_ref, vb_ref, gb_ref,
             of_ref, ob_ref, ct_ref, n_ref, m_ref):
    @pl.when(pl.program_id(1) == 0)
    def _():
        ct_ref[...] = jnp.zeros(ct_ref.shape, F32)
        n_ref[...] = jnp.zeros(n_ref.shape, F32)
        m_ref[...] = jnp.zeros(m_ref.shape, F32)

    nsub = SCAN_STEP // ML_CHUNK

    def sub(c, carry):
        r0 = pl.multiple_of(c * ML_CHUNK, ML_CHUNK)
        _ml_unit(False, r0, 0, qf_ref, kf_ref, vf_ref, gf_ref, of_ref, ct_ref.at[0], n_ref.at[0], m_ref.at[0])
        r1 = pl.multiple_of((nsub - 1 - c) * ML_CHUNK, ML_CHUNK)
        _ml_unit(True, r1, 1, qb_ref, kb_ref, vb_ref, gb_ref, ob_ref, ct_ref.at[1], n_ref.at[1], m_ref.at[1])
        return carry

    lax.fori_loop(0, nsub, sub, 0)


def _mlstm(mlqk, mlv, gates, nx):
    bsz, tall, kw = mlv.shape
    steps = tall // SCAN_STEP
    blk = (1, SCAN_STEP, kw)
    gblk = (1, 4 * N_HEADS, SCAN_STEP)
    fwd = lambda col: (lambda b, s: (b, _fwd_blk(s, nx), col))
    bwd = lambda col: (lambda b, s: (b, _bwd_blk(s, nx), col))
    in_specs = [pl.BlockSpec(blk, fwd(0)), pl.BlockSpec(blk, fwd(1)), pl.BlockSpec(blk, fwd(0)),
                pl.BlockSpec(gblk, lambda b, s: (b, 0, _fwd_blk(s, nx))),
                pl.BlockSpec(blk, bwd(0)), pl.BlockSpec(blk, bwd(1)), pl.BlockSpec(blk, bwd(0)),
                pl.BlockSpec(gblk, lambda b, s: (b, 0, _bwd_blk(s, nx)))]
    out_sds = jax.ShapeDtypeStruct((bsz, tall, kw), F32)
    return pl.pallas_call(
        _ml_body,
        grid=(bsz, steps),
        in_specs=in_specs,
        out_specs=[pl.BlockSpec(blk, fwd(0)), pl.BlockSpec(blk, bwd(0))],
        out_shape=[out_sds, out_sds],
        scratch_shapes=[pltpu.VMEM((2, N_HEADS, HEAD_DIM, HEAD_DIM), F32),
                        pltpu.VMEM((2, N_HEADS, 1, HEAD_DIM), F32),
                        pltpu.VMEM((2, N_HEADS, 1, HEAD_DIM), F32)],
        compiler_params=_params(("arbitrary", "arbitrary")),
        name="mlstm_scan",
    )(mlqk, mlqk, mlv, gates, mlqk, mlqk, mlv, gates)


def _out_body(hof_ref, hob_ref, mhf_ref, mhb_ref, hgg_ref, mlo_ref, x_ref, g1_ref, sc2_ref, sh2_ref,
              hnw_ref, mnw_ref, wout_ref, n2w_ref, rwt_ref, x1_ref, vt_ref, aff_ref):
    hg = hof_ref[0] + hob_ref[0]
    ml = mhf_ref[0] + mhb_ref[0]
    hparts, mparts = [], []
    for h in range(N_HEADS):
        cs = slice(h * HEAD_DIM, (h + 1) * HEAD_DIM)
        t = hg[:, cs]
        hparts.append(t * lax.rsqrt(jnp.mean(t * t, axis=-1, keepdims=True) + NORM_EPS))
        t = ml[:, cs]
        t = t - jnp.mean(t, axis=-1, keepdims=True)
        mparts.append(t * lax.rsqrt(jnp.mean(t * t, axis=-1, keepdims=True) + NORM_EPS))
    hgn = jnp.concatenate(hparts, axis=-1) * hnw_ref[...] * hgg_ref[0].astype(F32)
    mln = jnp.concatenate(mparts, axis=-1) * mnw_ref[...] * mlo_ref[0].astype(F32)
    mix = jnp.concatenate([hgn, mln], axis=-1).astype(BF16)
    x1 = x_ref[0] + g1_ref[0] * _dot(mix, wout_ref[...])
    x1_ref[0] = x1
    v = x1 * lax.rsqrt(jnp.mean(x1 * x1, axis=-1, keepdims=True) + NORM_EPS) * n2w_ref[...]
    v = v * (1.0 + sc2_ref[0]) + sh2_ref[0]
    vt_ref[0] = _pack_halves(v)
    logits = _dot(rwt_ref[...], v, _NT, precision=HIGHEST)
    ex = jnp.exp(logits - jnp.max(logits, axis=0, keepdims=True))
    aff_ref[0] = ex / jnp.sum(ex, axis=0, keepdims=True)


def _mixer_out(hg_f, hg_b, ml_f, ml_b, hgg, mlo, x, g1, sc2, sh2, hnw, mnw, w_out, n2w, rwt, tm):
    bsz, t, d = x.shape
    kw = HEADS_W
    ne = rwt.shape[0]
    row = lambda b, i: (b, i, 0)
    mod = lambda b, i: (b, 0, 0)
    const2 = lambda b, i: (0, 0)
    act = pl.BlockSpec((1, tm, kw), row)
    in_specs = [act, act, act, act, act, act,
                pl.BlockSpec((1, tm, d), row),
                pl.BlockSpec((1, 1, d), mod), pl.BlockSpec((1, 1, d), mod), pl.BlockSpec((1, 1, d), mod),
                pl.BlockSpec((1, kw), const2), pl.BlockSpec((1, kw), const2),
                pl.BlockSpec(w_out.shape, const2), pl.BlockSpec((1, d), const2), pl.BlockSpec(rwt.shape, const2)]
    return pl.pallas_call(
        _out_body,
        grid=(bsz, t // tm),
        in_specs=in_specs,
        out_specs=[pl.BlockSpec((1, tm, d), row),
                   pl.BlockSpec((1, tm, d // 2), row),
                   pl.BlockSpec((1, ne, tm), lambda b, i: (b, 0, i))],
        out_shape=[jax.ShapeDtypeStruct((bsz, t, d), F32),
                   jax.ShapeDtypeStruct((bsz, t, d // 2), I32),
                   jax.ShapeDtypeStruct((bsz, ne, t), F32)],
        compiler_params=_params(("arbitrary", "arbitrary")),
        name="mixer_out",
    )(hg_f, hg_b, ml_f, ml_b, hgg, mlo, x, g1, sc2, sh2, hnw, mnw, w_out, n2w, rwt)


def _prefix_count(maskf, u_ref, ones_ref, bl_ref):
    e, nb, ln = maskf.shape
    x = maskf.reshape(e * nb, ln)
    xb = x.astype(BF16)
    incl = _dot(xb, u_ref[...])
    tot = _dot(xb, ones_ref[...])
    off = _dot(bl_ref[...], tot.astype(BF16))
    return (incl - x + off).reshape(e, nb, ln), off.reshape(e, nb, ln)


def _topk_body(aff_ref, u_ref, ones_ref, bl_ref, bl1_ref, pos_ref, base_ref, *, cap):
    x = aff_ref[0]
    ne = x.shape[0]

    def count(m):
        return jnp.sum(jnp.sum(jnp.where(m, 1.0, 0.0), axis=1, keepdims=True), axis=2, keepdims=True)

    def halve(_, carry):
        lo, hi = carry
        mid = 0.5 * (lo + hi)
        up = count(x > mid) >= cap
        return jnp.where(up, mid, lo), jnp.where(up, hi, mid)

    lo, hi = lax.fori_loop(0, TOPK_BISECTIONS, halve,
                           (jnp.full((ne, 1, 1), -1.0, F32), jnp.full((ne, 1, 1), 1.0, F32)))
    gt = jnp.where(x > hi, 1.0, 0.0)
    eq = jnp.where(x > lo, 1.0, 0.0) - gt
    need = cap - count(x > hi)
    eq_rank, _ = _prefix_count(eq, u_ref, ones_ref, bl_ref)
    sel = gt + eq * jnp.where(eq_rank < need, 1.0, 0.0)
    pos, _ = _prefix_count(sel, u_ref, ones_ref, bl_ref)
    pos_ref[0] = jnp.where(sel > 0, pos, -1.0).astype(I32)
    nsel = jnp.sum(sel, axis=0)
    nb16 = nsel.astype(BF16)
    incl = _dot(nb16, u_ref[...])
    tot = _dot(nb16, ones_ref[...])
    off = _dot(bl1_ref[...], tot, precision=HIGHEST)
    base_ref[0] = (incl - nsel + off).astype(I32)


def _topk(aff4, cap):
    bsz, ne, nb, ln = aff4.shape
    k = np.arange(ln)
    u = jnp.asarray(k[:, None] <= k[None, :], BF16)
    ones = jnp.ones((ln, ln), BF16)
    r = np.arange(ne * nb)
    bl = jnp.asarray(((r[:, None] // nb) == (r[None, :] // nb)) & (r[None, :] < r[:, None]), BF16)
    r1 = np.arange(nb)
    bl1 = jnp.asarray(r1[None, :] < r1[:, None], F32)
    blk = pl.BlockSpec((1, ne, nb, ln), lambda b: (b, 0, 0, 0))
    const = lambda a: pl.BlockSpec(a.shape, lambda b: (0, 0))
    return pl.pallas_call(
        functools.partial(_topk_body, cap=cap),
        grid=(bsz,),
        in_specs=[blk, const(u), const(ones), const(bl), const(bl1)],
        out_specs=[blk, pl.BlockSpec((1, nb, ln), lambda b: (b, 0, 0))],
        out_shape=[jax.ShapeDtypeStruct((bsz, ne, nb, ln), I32), jax.ShapeDtypeStruct((bsz, nb, ln), I32)],
        compiler_params=_params(("arbitrary",)),
        name="expert_topk",
    )(aff4, u, ones, bl, bl1)


def _sc_setup():
    info = plsc.get_sparse_core_info()
    mesh = plsc.VectorSubcoreMesh(core_axis_name="c", subcore_axis_name="s")
    params = dataclasses.replace(pltpu.CompilerParams(), needs_layout_passes=False)
    return info.num_cores, info.num_cores * info.num_subcores, info.num_lanes, mesh, params


def _sc_worker_id(nc):
    return lax.axis_index("s") * nc + lax.axis_index("c")


def _sc_gather_chunks(table_hbm, out_hbm, idx_v, ich_v, rows_v, sem, out_row0, n, lanes):
    chunk = SC_GATHER_ROWS

    @pl.loop(0, n // chunk)
    def _(c):
        for q in range(chunk // lanes):
            src = pl.ds(pl.multiple_of(c * chunk + q * lanes, lanes), lanes)
            ich_v[pl.ds(q * lanes, lanes)] = idx_v[src]
        pltpu.async_copy(table_hbm.at[ich_v], rows_v, sem).wait()
        pltpu.sync_copy(rows_v, out_hbm.at[pl.ds(pl.multiple_of(out_row0 + c * chunk, chunk), chunk)])


def _sc_invert_gather(table, pos_flat, aff_flat, npairs, ne, t, cap):
    nc, nw, lanes, mesh, params = _sc_setup()
    per_w = npairs // nw
    width = table.shape[1]
    chunk = SC_GATHER_ROWS

    @functools.partial(
        pl.kernel, mesh=mesh, compiler_params=params,
        out_type=[jax.ShapeDtypeStruct((npairs * cap, width), table.dtype),
                  jax.ShapeDtypeStruct((npairs * cap,), F32)],
        scratch_types=[pltpu.VMEM((t,), I32), pltpu.VMEM((t,), F32), pltpu.VMEM((cap,), I32), pltpu.VMEM((cap,), F32),
                       pltpu.VMEM((chunk,), I32), pltpu.VMEM((chunk, width), table.dtype), pltpu.SemaphoreType.DMA],
    )
    def body(table_hbm, pos_hbm, aff_hbm, out_hbm, gate_hbm, pos_v, aff_v, idx_v, gate_v, ich_v, rows_v, sem):
        wid = _sc_worker_id(nc)

        @pl.loop(0, per_w)
        def _(kk):
            p = wid * per_w + kk
            tok0 = (p // ne) * t
            pltpu.sync_copy(pos_hbm.at[pl.ds(pl.multiple_of(p * t, t), t)], pos_v)
            pltpu.sync_copy(aff_hbm.at[pl.ds(pl.multiple_of(p * t, t), t)], aff_v)

            @pl.loop(0, t // lanes)
            def _(i):
                v = pos_v[pl.ds(pl.multiple_of(i * lanes, lanes), lanes)]
                tok = lax.iota(I32, lanes) + i * lanes
                plsc.store_scatter(idx_v, [v], tok, mask=v >= 0)

            @pl.loop(0, cap // lanes)
            def _(j):
                sl = pl.ds(pl.multiple_of(j * lanes, lanes), lanes)
                ii = idx_v[sl]
                gate_v[sl] = plsc.load_gather(aff_v, [ii])
                idx_v[sl] = ii + tok0

            pltpu.sync_copy(gate_v, gate_hbm.at[pl.ds(pl.multiple_of(p * cap, cap), cap)])
            _sc_gather_chunks(table_hbm, out_hbm, idx_v, ich_v, rows_v, sem, p * cap, cap, lanes)

    return body(table, pos_flat, aff_flat)


def _sc_assignment_order(pos_flat, base_flat, bsz, ne, t, cap):
    nc, nw, lanes, mesh, params = _sc_setup()
    na = ne * cap
    per_b = nw // bsz
    rng = na // per_b

    @functools.partial(
        pl.kernel, mesh=mesh, compiler_params=params,
        out_type=[jax.ShapeDtypeStruct((bsz * na,), I32), jax.ShapeDtypeStruct((bsz * na,), I32)],
        scratch_types=[pltpu.VMEM((t,), I32), pltpu.VMEM((t,), I32), pltpu.VMEM((t,), I32),
                       pltpu.VMEM((rng,), I32), pltpu.VMEM((rng,), I32)],
    )
    def body(pos_hbm, base_hbm, perm_hbm, tok_hbm, pos_v, base_v, rank_v, perm_v, tok_v):
        wid = _sc_worker_id(nc)
        b = wid // per_b
        a0 = (wid % per_b) * rng
        pltpu.sync_copy(base_hbm.at[pl.ds(pl.multiple_of(b * t, t), t)], base_v)

        @pl.loop(0, t // lanes)
        def _(i):
            rank_v[pl.ds(pl.multiple_of(i * lanes, lanes), lanes)] = jnp.zeros((lanes,), I32)

        @pl.loop(0, ne)
        def _(e):
            p = b * ne + e
            pltpu.sync_copy(pos_hbm.at[pl.ds(pl.multiple_of(p * t, t), t)], pos_v)

            @pl.loop(0, t // lanes)
            def _(i):
                sl = pl.ds(pl.multiple_of(i * lanes, lanes), lanes)
                v = pos_v[sl]
                r = rank_v[sl]
                a = base_v[sl] + r - a0
                sel = v >= 0
                mine = sel & (a >= 0) & (a < rng)
                plsc.store_scatter(perm_v, [a], v + p * cap, mask=mine)
                plsc.store_scatter(tok_v, [a], lax.iota(I32, lanes) + i * lanes, mask=mine)
                rank_v[sl] = r + jnp.where(sel, 1, 0)

        dst = pl.ds(pl.multiple_of(b * na + a0, rng), rng)
        pltpu.sync_copy(perm_v, perm_hbm.at[dst])
        pltpu.sync_copy(tok_v, tok_hbm.at[dst])

    return body(pos_flat, base_flat)


def _sc_gather_rows(table, idx):
    nc, nw, lanes, mesh, params = _sc_setup()
    n = idx.shape[0]
    per_w = n // nw
    width = table.shape[1]
    chunk = SC_GATHER_ROWS

    @functools.partial(
        pl.kernel, mesh=mesh, compiler_params=params,
        out_type=jax.ShapeDtypeStruct((n, width), table.dtype),
        scratch_types=[pltpu.VMEM((per_w,), I32), pltpu.VMEM((chunk,), I32),
                       pltpu.VMEM((chunk, width), table.dtype), pltpu.SemaphoreType.DMA],
    )
    def body(table_hbm, idx_hbm, out_hbm, idx_v, ich_v, rows_v, sem):
        row0 = _sc_worker_id(nc) * per_w
        pltpu.sync_copy(idx_hbm.at[pl.ds(pl.multiple_of(row0, per_w), per_w)], idx_v)
        _sc_gather_chunks(table_hbm, out_hbm, idx_v, ich_v, rows_v, sem, row0, per_w, lanes)

    return body(table, idx)


def _moe1_body(xs_ref, gate_ref, wg_ref, wu_ref, wd_ref, y_ref):
    xs = _unpack_halves(xs_ref[0])
    hg = _dot(xs, wg_ref[0])
    hu = _dot(xs, wu_ref[0])
    h = (_silu(hg) * hu).astype(BF16)
    y_ref[0] = _pack_halves(_dot(h, wd_ref[0]) * gate_ref[0])


def _moe_experts(xs, gate, wg, wu, wd, bsz):
    npairs, cap, half = xs.shape
    ne, d, f = wg.shape
    pair = lambda b, e: (b * ne + e, 0, 0)
    expert = lambda b, e: (e, 0, 0)
    return pl.pallas_call(
        _moe1_body,
        grid=(bsz, ne),
        in_specs=[pl.BlockSpec((1, cap, half), pair),
                  pl.BlockSpec((1, cap, 1), pair),
                  pl.BlockSpec((1, d, f), expert),
                  pl.BlockSpec((1, d, f), expert),
                  pl.BlockSpec((1, f, d), expert)],
        out_specs=pl.BlockSpec((1, cap, half), pair),
        out_shape=jax.ShapeDtypeStruct((npairs, cap, half), I32),
        compiler_params=_params(("arbitrary", "arbitrary")),
        name="moe_experts",
    )(xs, gate, wg, wu, wd)


def _combine_body(abase_ref, ys_hbm, tok_ref, x1_ref, g2_ref, fw_ref, o_ref, buf_ref, acc_ref, sem_ref, *, na):
    n = ONEHOT_BLK
    b, i = pl.program_id(0), pl.program_id(1)
    lo = abase_ref[b, i]
    hi = abase_ref[b, i + 1]
    c0 = lo // n
    nchunk = jnp.where(hi > lo, (hi + n - 1) // n - c0, 0)

    def copy(c, slot):
        row0 = pl.multiple_of(b * na + (c0 + c) * n, n)
        return pltpu.make_async_copy(ys_hbm.at[pl.ds(row0, n)], buf_ref.at[slot], sem_ref.at[slot])

    acc_ref[...] = jnp.zeros(acc_ref.shape, F32)

    @pl.when(nchunk > 0)
    def _():
        copy(0, 0).start()

    tok_ids = lax.broadcasted_iota(I32, (n, n), 0) + i * n

    def step(c, carry):
        slot = c % 2
        copy(c, slot).wait()

        @pl.when(c + 1 < nchunk)
        def _():
            copy(c + 1, 1 - slot).start()

        seg = jnp.where(tok_ids == tok_ref[0, pl.ds(c0 + c, 1), :], 1.0, 0.0).astype(BF16)
        acc_ref[...] += _dot(seg, _unpack_halves(buf_ref[slot]))
        return carry

    lax.fori_loop(0, nchunk, step, 0)
    x2 = x1_ref[0] + g2_ref[0] * acc_ref[...]
    o_ref[0] = x2 * lax.rsqrt(jnp.mean(x2 * x2, axis=-1, keepdims=True) + NORM_EPS) * fw_ref[...]


def _combine_final(abase, ys, tok, x1, g2, fw, na):
    bsz, t, d = x1.shape
    n = ONEHOT_BLK
    grid_spec = pltpu.PrefetchScalarGridSpec(
        num_scalar_prefetch=1,
        grid=(bsz, t // n),
        in_specs=[pl.BlockSpec(memory_space=pl.ANY),
                  pl.BlockSpec((1, na // n, n), lambda b, i, a: (b, 0, 0)),
                  pl.BlockSpec((1, n, d), lambda b, i, a: (b, i, 0)),
                  pl.BlockSpec((1, 1, d), lambda b, i, a: (b, 0, 0)),
                  pl.BlockSpec((1, d), lambda b, i, a: (0, 0))],
        out_specs=pl.BlockSpec((1, n, d), lambda b, i, a: (b, i, 0)),
        scratch_shapes=[pltpu.VMEM((2, n, d // 2), I32), pltpu.VMEM((n, d), F32), pltpu.SemaphoreType.DMA((2,))],
    )
    return pl.pallas_call(
        functools.partial(_combine_body, na=na),
        grid_spec=grid_spec,
        out_shape=jax.ShapeDtypeStruct((bsz, t, d), F32),
        compiler_params=_params(("arbitrary", "arbitrary")),
        name="combine_final",
    )(abase, ys, tok, x1, g2, fw)


def kernel(x, c, ctx, c_ctx, ada_w, ada_b, norm1_w, w_in, conv_w, conv_b, hg_lb_logits, ml_gate_b,
           hg_norm_w, ml_norm_w, w_out, norm2_w, router_w, exp_w_gate, exp_w_up, exp_w_down, final_norm_w):
    bsz, t, d = x.shape
    nctx = ctx.shape[1]
    assert ada_w.shape[0] == 1, "single-layer block"
    assert nctx == SCAN_STEP and t % SCAN_STEP == 0 and bsz + 1 <= 8
    tall = t + nctx
    nx = t // SCAN_STEP
    kw = HEADS_W
    ne = router_w.shape[-1]
    cap = EC_CAPACITY * t // ne
    tm = min(512, t)

    rows = jnp.concatenate([c, c_ctx[None], jnp.zeros((7 - bsz, d), F32)], axis=0)
    mod = _modulation(rows, ada_w[0], ada_b[0][None])
    mx = [m[:, None, :] for m in jnp.split(mod[:bsz], 6, axis=-1)]
    mc = [m[:, None, :] for m in jnp.split(mod[bsz:bsz + 1], 6, axis=-1)]
    sh1, sc1, g1, sh2, sc2, g2 = mx
    csh1, csc1 = mc[0], mc[1]

    main_w = 9 * kw
    w_main = w_in[0][:, :main_w].astype(BF16)
    w_gt = w_in[0][:, main_w:].T
    gate_b = ml_gate_b[0][:, None]
    nw1 = norm1_w[0][None]
    outs = _inproj(x, sc1, sh1, nw1, w_main, w_gt, gate_b, hg_lb_logits, tall, tm, 0)
    outs = _inproj(ctx, csc1, csh1, nw1, w_main, w_gt, gate_b, hg_lb_logits, tall, nctx, t // nctx, prev=outs)
    hgq, hgv, hgg, hgk, hglf, mlqk_pre, mlv, mlo, gates = outs

    mlqk = _conv(mlqk_pre, conv_w[0], conv_b[0][None], t, nctx)
    hg_f, hg_b = _hgrn2(hgq, hgk, hgv, hglf, nx)
    ml_f, ml_b = _mlstm(mlqk, mlv, gates, nx)

    x1, vpk, aff = _mixer_out(hg_f, hg_b, ml_f, ml_b, hgg, mlo, x, g1, sc2, sh2,
                             hg_norm_w[0][None], ml_norm_w[0][None], w_out[0].astype(BF16),
                             norm2_w[0][None], router_w[0].T, tm)

    pos4, base = _topk(aff.reshape(bsz, ne, t // 128, 128), cap)
    pos_flat = pos4.reshape(-1)
    xs, gate = _sc_invert_gather(vpk.reshape(bsz * t, d // 2), pos_flat, aff.reshape(-1), bsz * ne, ne, t, cap)
    perm, tok = _sc_assignment_order(pos_flat, base.reshape(-1), bsz, ne, t, cap)
    y = _moe_experts(xs.reshape(bsz * ne, cap, d // 2), gate.reshape(bsz * ne, cap, 1),
                     exp_w_gate[0].astype(BF16), exp_w_up[0].astype(BF16), exp_w_down[0].astype(BF16), bsz)
    na = ne * cap
    ys = _sc_gather_rows(y.reshape(bsz * na, d // 2), perm)
    abase = jnp.concatenate([base.reshape(bsz, t)[:, ::ONEHOT_BLK], jnp.full((bsz, 1), na, I32)], axis=1)
    return _combine_final(abase, ys, tok.reshape(bsz, na // ONEHOT_BLK, ONEHOT_BLK), x1, g2, final_norm_w[None], na)
```

```python
import dataclasses
import functools

import numpy as np
import jax
import jax.numpy as jnp
from jax import lax
from jax.experimental import pallas as pl
from jax.experimental.pallas import tpu as pltpu
from jax.experimental.pallas import tpu_sc as plsc

F32 = jnp.float32
BF16 = jnp.bfloat16
I32 = jnp.int32
HIGHEST = lax.Precision.HIGHEST
NORM_EPS = 1e-6

HEAD_DIM = 128
N_HEADS = 4
HEADS_W = N_HEADS * HEAD_DIM
GRID_W = 64
N_EXPERTS = 16
EC_CAPACITY = 2
HG_CHUNK = 64
ML_CHUNK = 128
SCAN_STEP = 256
ONEHOT_BLK = 256
CONV_HALO = 72
SC_GATHER_ROWS = 64
COMBINE_RING = 6
TOPK_BISECTIONS = 64
VMEM_LIMIT = 56 * 1024 * 1024

_NT = (((1,), (1,)), ((), ()))
_TN = (((0,), (0,)), ((), ()))


def _dot(a, b, dims=None, precision=None):
    if dims is None:
        return jnp.dot(a, b, preferred_element_type=F32, precision=precision)
    return lax.dot_general(a, b, dims, preferred_element_type=F32, precision=precision)


def _sigmoid(x):
    return jax.nn.sigmoid(x)


def _pack_halves(x):
    w = x.shape[-1] // 2
    bits = lax.bitcast_convert_type(x.astype(BF16).astype(F32), I32)
    return lax.shift_right_logical(bits[:, :w], 16) | bits[:, w:]


def _unpack_halves(p):
    lo = lax.bitcast_convert_type(lax.shift_left(p, 16), F32)
    hi = lax.bitcast_convert_type(p & jnp.int32(-65536), F32)
    return jnp.concatenate([lo, hi], axis=-1).astype(BF16)


def _silu(x):
    return x * jax.nn.sigmoid(x)


def _params(sem, flags=None):
    return pltpu.CompilerParams(dimension_semantics=sem, vmem_limit_bytes=VMEM_LIMIT, flags=flags)


def _mod_body(r_ref, w_ref, b_ref, o_ref):
    r = r_ref[...]
    o_ref[...] = _dot(_silu(r), w_ref[...], precision=HIGHEST) + b_ref[...]


def _modulation(rows, w, b):
    d, n = w.shape
    tn = n // 4
    return pl.pallas_call(
        _mod_body,
        grid=(n // tn,),
        in_specs=[pl.BlockSpec((8, d), lambda j: (0, 0)),
                  pl.BlockSpec((d, tn), lambda j: (0, j)),
                  pl.BlockSpec((1, tn), lambda j: (0, j))],
        out_specs=pl.BlockSpec((8, tn), lambda j: (0, j)),
        out_shape=jax.ShapeDtypeStruct((8, n), F32),
        compiler_params=_params(("arbitrary",)),
        name="modulation",
    )(rows, w, b)


def _log_sigmoid(x):
    return jnp.minimum(x, 0.0) - jnp.log(1.0 + jnp.exp(-jnp.abs(x)))


def _inproj_body(x_ref, sc_ref, sh_ref, nw_ref, w_ref, wg_ref, gb_ref, lbl_ref, *refs):
    hgq_ref, hgv_ref, hgg_ref, hgk_ref, hglf_ref, mlqk_ref, mlv_ref, mlo_ref, gates_ref = refs[-9:]
    kw = HEADS_W
    x = x_ref[0]
    y = x * lax.rsqrt(jnp.mean(x * x, axis=-1, keepdims=True) + NORM_EPS) * nw_ref[...]
    u = y * (1.0 + sc_ref[0]) + sh_ref[0]
    ub = u.astype(BF16)

    def proj(c0, c1):
        return _dot(ub, w_ref[:, c0:c1])

    hgq_ref[0] = _silu(proj(0, kw)).astype(BF16)
    hgv_ref[0] = proj(kw, 2 * kw).astype(BF16)
    hgg_ref[0] = _silu(proj(2 * kw, 3 * kw)).astype(BF16)

    lbl = lbl_ref[...]
    mx = jnp.max(lbl, axis=0)
    ex = jnp.exp(lbl - mx[None])
    lb = ex[0] / jnp.sum(ex, axis=0)
    for d in range(2):
        p = proj((3 + d) * kw, (4 + d) * kw)
        lbd = lb[d:d + 1]
        f = lbd + (1.0 - lbd) * _sigmoid(p)
        hgk_ref[0, :, d * kw:(d + 1) * kw] = (1.0 - f).astype(BF16)
        hglf_ref[0, :, d * kw:(d + 1) * kw] = jnp.log(f)

    mlqk_ref[0, :, 0:kw] = proj(5 * kw, 6 * kw)
    mlqk_ref[0, :, kw:2 * kw] = proj(6 * kw, 7 * kw)
    mlv_ref[0] = proj(7 * kw, 8 * kw).astype(BF16)
    mlo_ref[0] = _sigmoid(proj(8 * kw, 9 * kw)).astype(BF16)

    g = _dot(wg_ref[...], u, _NT, precision=HIGHEST) + gb_ref[...]
    row = lax.broadcasted_iota(I32, g.shape, 0)
    gates_ref[0] = jnp.where((row % 8) >= N_HEADS, _log_sigmoid(g), g)


def _inproj_shapes(bsz, tall):
    kw = HEADS_W
    return [
        jax.ShapeDtypeStruct((bsz, tall, kw), BF16),
        jax.ShapeDtypeStruct((bsz, tall, kw), BF16),
        jax.ShapeDtypeStruct((bsz, tall, kw), BF16),
        jax.ShapeDtypeStruct((bsz, tall, 2 * kw), BF16),
        jax.ShapeDtypeStruct((bsz, tall, 2 * kw), F32),
        jax.ShapeDtypeStruct((bsz, tall, 2 * kw), F32),
        jax.ShapeDtypeStruct((bsz, tall, kw), BF16),
        jax.ShapeDtypeStruct((bsz, tall, kw), BF16),
        jax.ShapeDtypeStruct((bsz, 4 * N_HEADS, tall), F32),
    ]


def _inproj(tokens, scale, shift, nw, w_main, w_gt, gate_b, lb_logits, tall, tm, blk0, prev=None):
    bsz, n, d = tokens.shape
    kw = HEADS_W
    nt = n // tm
    per_sample = scale.shape[0] == bsz
    mod_map = (lambda b, i: (b, 0, 0)) if per_sample else (lambda b, i: (0, 0, 0))
    const2 = lambda b, i: (0, 0)
    in_specs = [
        pl.BlockSpec((1, tm, d), lambda b, i: (b, i, 0)),
        pl.BlockSpec((1, 1, d), mod_map),
        pl.BlockSpec((1, 1, d), mod_map),
        pl.BlockSpec((1, d), const2),
        pl.BlockSpec(w_main.shape, const2),
        pl.BlockSpec(w_gt.shape, const2),
        pl.BlockSpec(gate_b.shape, const2),
        pl.BlockSpec(lb_logits.shape, lambda b, i: (0, 0, 0)),
    ]
    args = [tokens, scale, shift, nw, w_main, w_gt, gate_b, lb_logits]
    aliases = {}
    if prev is not None:
        for k, a in enumerate(prev):
            in_specs.append(pl.BlockSpec(memory_space=pl.ANY))
            aliases[len(args)] = k
            args.append(a)
    row_map = lambda b, i: (b, blk0 + i, 0)
    widths = [kw, kw, kw, 2 * kw, 2 * kw, 2 * kw, kw, kw]
    out_specs = [pl.BlockSpec((1, tm, w), row_map) for w in widths]
    out_specs.append(pl.BlockSpec((1, 4 * N_HEADS, tm), lambda b, i: (b, 0, blk0 + i)))
    return pl.pallas_call(
        _inproj_body,
        grid=(bsz, nt),
        in_specs=in_specs,
        out_specs=out_specs,
        out_shape=_inproj_shapes(bsz, tall),
        input_output_aliases=aliases,
        compiler_params=_params(("arbitrary", "arbitrary")),
        name="inproj_ctx" if prev is not None else "inproj_x",
    )(*args)


def _conv_body(x_ref, w_ref, b_ref, o_ref, pad_ref, cpad_ref, *, t, nctx, scale_from):
    halo = CONV_HALO
    rows = 512
    win = rows + 2 * halo
    ch = x_ref.shape[-1]
    scale = jnp.where(pl.program_id(1) >= scale_from, HEAD_DIM ** -0.5, 1.0).astype(F32)
    w = w_ref[...]
    bias = b_ref[...]

    pad_ref[0:halo, :] = jnp.zeros((halo, ch), F32)
    pad_ref[halo + t:halo + t + halo, :] = jnp.zeros((halo, ch), F32)
    pad_ref[halo:halo + t, :] = x_ref[0, 0:t, :]
    col = (lax.broadcasted_iota(I32, (win, ch), 0) + (GRID_W - halo % GRID_W)) % GRID_W
    left_ok = col > 0
    right_ok = col < GRID_W - 1

    def chunk(c, carry):
        o = pl.multiple_of(c * rows, rows)
        xw = pad_ref[pl.ds(o, win), :]
        xm = jnp.where(left_ok, pltpu.roll(xw, 1, 0), 0.0)
        xp = jnp.where(right_ok, pltpu.roll(xw, win - 1, 0), 0.0)
        z = [xm * w[dr, 0:1] + xw * w[dr, 1:2] + xp * w[dr, 2:3] for dr in range(3)]
        y = (z[1][halo:halo + rows]
             + z[0][halo - GRID_W:halo - GRID_W + rows]
             + z[2][halo + GRID_W:halo + GRID_W + rows])
        o_ref[0, pl.ds(o, rows), :] = (_silu(y + bias) * scale).astype(o_ref.dtype)
        return carry

    lax.fori_loop(0, t // rows, chunk, 0)

    cpad_ref[0:8, :] = jnp.zeros((8, ch), F32)
    cpad_ref[8 + nctx:16 + nctx, :] = jnp.zeros((8, ch), F32)
    cpad_ref[8:8 + nctx, :] = x_ref[0, t:t + nctx, :]
    xw = cpad_ref[...]
    n = nctx + 16
    y = (pltpu.roll(xw, 1, 0) * w[1, 0:1] + xw * w[1, 1:2] + pltpu.roll(xw, n - 1, 0) * w[1, 2:3])[8:8 + nctx]
    o_ref[0, t:t + nctx, :] = (_silu(y + bias) * scale).astype(o_ref.dtype)


def _conv(qk_pre, conv_w, conv_b, t, nctx):
    bsz, tall, c = qk_pre.shape
    ch = 128
    body = functools.partial(_conv_body, t=t, nctx=nctx, scale_from=(c // 2) // ch)
    return pl.pallas_call(
        body,
        grid=(bsz, c // ch),
        in_specs=[pl.BlockSpec((1, tall, ch), lambda b, j: (b, 0, j)),
                  pl.BlockSpec((3, 3, ch), lambda b, j: (0, 0, j)),
                  pl.BlockSpec((1, ch), lambda b, j: (0, j))],
        out_specs=pl.BlockSpec((1, tall, ch), lambda b, j: (b, 0, j)),
        out_shape=jax.ShapeDtypeStruct((bsz, tall, c), BF16),
        scratch_shapes=[pltpu.VMEM((t + 2 * CONV_HALO, ch), F32),
                        pltpu.VMEM((nctx + 16, ch), F32)],
        compiler_params=_params(("arbitrary", "arbitrary")),
        name="qk_conv",
    )(qk_pre, conv_w, conv_b)


def _fwd_blk(s, nx):
    return jnp.where(s == 0, nx, s - 1)


def _bwd_blk(s, nx):
    return jnp.where(s == 0, nx, nx - s)


def _hg_constants(rev):
    c = HG_CHUNK
    i = np.arange(c)[:, None]
    j = np.arange(c)[None, :]
    blocks = [(j >= i) if rev else (j <= i), (j < i) if rev else (j > i)]
    masks = [i == j]
    qsel = []
    m = c // 2
    while m >= 1:
        b0 = (i // (2 * m)) * (2 * m)
        same = (i // (2 * m)) == (j // (2 * m))
        if rev:
            beta = b0 + m
            qrow = (i % (2 * m)) < m
            g = np.where(qrow, (j >= i) & (j < beta), (j >= beta) & (j < i))
            mask = same & qrow & ((j % (2 * m)) >= m)
        else:
            beta = b0 + m - 1
            qrow = (i % (2 * m)) >= m
            g = np.where(qrow, (j > beta) & (j <= i), (j > i) & (j <= beta))
            mask = same & qrow & ((j % (2 * m)) < m)
        blocks.append(g)
        masks.append(mask)
        qsel.append(np.broadcast_to(qrow, (c, HEAD_DIM)))
        m //= 2
    g = np.concatenate(blocks, axis=0).astype(np.float32)
    g3 = np.concatenate([g, g, g], axis=1)
    return (jnp.asarray(g3, BF16), jnp.asarray(np.stack(masks), F32), jnp.asarray(np.stack(qsel), F32))


def _hg_unit(rev, r0, q_ref, k_ref, v_ref, lf_ref, g_ref, msk_ref, qs_ref, o_ref, st_ref):
    c = HG_CHUNK
    lf = lf_ref[0, pl.ds(r0, c), :]
    p1 = lf.astype(BF16)
    r1 = lf - p1.astype(F32)
    p2 = r1.astype(BF16)
    p3 = (r1 - p2.astype(F32)).astype(BF16)
    dall = _dot(g_ref[...], jnp.concatenate([p1, p2, p3], axis=0))
    nlev = msk_ref.shape[0] - 1
    for h in range(N_HEADS):
        cs = slice(h * HEAD_DIM, (h + 1) * HEAD_DIM)
        qb = q_ref[0, pl.ds(r0, c), cs]
        kb = k_ref[0, pl.ds(r0, c), cs]
        vb = v_ref[0, pl.ds(r0, c), cs]
        qf = qb.astype(F32)
        kf = kb.astype(F32)
        a = dall[0:c, cs]
        a_end = dall[c:2 * c, cs]
        att = jnp.where(msk_ref[0] > 0, _dot(qb, kb, _NT), 0.0)
        for l in range(nlev):
            e = jnp.exp(dall[(l + 2) * c:(l + 3) * c, cs])
            z = (jnp.where(qs_ref[l] > 0, qf, kf) * e).astype(BF16)
            att = att + jnp.where(msk_ref[l + 1] > 0, _dot(z, z, _NT), 0.0)
        st = st_ref[h]
        qbar = (qf * jnp.exp(a)).astype(BF16)
        o = _dot(att.astype(BF16), vb) + _dot(qbar, st.astype(BF16), _NT)
        o_ref[0, pl.ds(r0, c), cs] = o
        khat = (kf * jnp.exp(a_end)).astype(BF16)
        a_tot = a[0:1] if rev else a[c - 1:c]
        st_ref[h] = st * jnp.exp(a_tot) + _dot(vb, khat, _TN)


def _hg_body(qf_ref, kf_ref, vf_ref, lff_ref, qb_ref, kb_ref, vb_ref, lfb_ref,
             gf_ref, mf_ref, sf_ref, gb_ref, mb_ref, sb_ref,
             of_ref, ob_ref, st_ref):
    @pl.when(pl.program_id(1) == 0)
    def _():
        st_ref[...] = jnp.zeros(st_ref.shape, F32)

    nsub = SCAN_STEP // HG_CHUNK

    def sub(c, carry):
        r0 = pl.multiple_of(c * HG_CHUNK, HG_CHUNK)
        _hg_unit(False, r0, qf_ref, kf_ref, vf_ref, lff_ref, gf_ref, mf_ref, sf_ref, of_ref, st_ref.at[0])
        r1 = pl.multiple_of((nsub - 1 - c) * HG_CHUNK, HG_CHUNK)
        _hg_unit(True, r1, qb_ref, kb_ref, vb_ref, lfb_ref, gb_ref, mb_ref, sb_ref, ob_ref, st_ref.at[1])
        return carry

    lax.fori_loop(0, nsub, sub, 0)


def _hgrn2(hgq, hgk, hgv, hglf, nx):
    bsz, tall, kw = hgq.shape
    steps = tall // SCAN_STEP
    cf = _hg_constants(False)
    cb = _hg_constants(True)
    blk = (1, SCAN_STEP, kw)
    fwd = lambda col: (lambda b, s: (b, _fwd_blk(s, nx), col))
    bwd = lambda col: (lambda b, s: (b, _bwd_blk(s, nx), col))
    const = lambda a: pl.BlockSpec(a.shape, lambda b, s: (0,) * a.ndim)
    in_specs = [pl.BlockSpec(blk, fwd(0)), pl.BlockSpec(blk, fwd(0)), pl.BlockSpec(blk, fwd(0)), pl.BlockSpec(blk, fwd(0)),
                pl.BlockSpec(blk, bwd(0)), pl.BlockSpec(blk, bwd(1)), pl.BlockSpec(blk, bwd(0)), pl.BlockSpec(blk, bwd(1))]
    in_specs += [const(a) for a in cf + cb]
    out_sds = jax.ShapeDtypeStruct((bsz, tall, kw), F32)
    return pl.pallas_call(
        _hg_body,
        grid=(bsz, steps),
        in_specs=in_specs,
        out_specs=[pl.BlockSpec(blk, fwd(0)), pl.BlockSpec(blk, bwd(0))],
        out_shape=[out_sds, out_sds],
        scratch_shapes=[pltpu.VMEM((2, N_HEADS, HEAD_DIM, HEAD_DIM), F32)],
        compiler_params=_params(("arbitrary", "arbitrary")),
        name="hgrn2_scan",
    )(hgq, hgk, hgv, hglf, hgq, hgk, hgv, hglf, *cf, *cb)


def _ml_constants(rev):
    k = np.arange(ML_CHUNK)
    tri = (k[:, None] >= k[None, :]) if rev else (k[:, None] <= k[None, :])
    return jnp.asarray(np.concatenate([tri, tri, tri], axis=0), BF16)


def _ml_unit(rev, r0, d, q_ref, k_ref, v_ref, g_ref, tri3_ref, o_ref, st_ref, m_ref):
    c = ML_CHUNK
    ii = lax.broadcasted_iota(I32, (c, c), 0)
    jj = lax.broadcasted_iota(I32, (c, c), 1)
    tri_t = (ii >= jj) if rev else (ii <= jj)
    last = 0 if rev else c - 1
    gates = g_ref[0, :, pl.ds(r0, c)]
    p1 = gates.astype(BF16)
    r1 = gates - p1.astype(F32)
    p2 = r1.astype(BF16)
    p3 = (r1 - p2.astype(F32)).astype(BF16)
    csum = _dot(jnp.concatenate([p1, p2, p3], axis=1), tri3_ref[...])
    ones = jnp.ones((c, HEAD_DIM), BF16)
    for h in range(N_HEADS):
        cs = slice(h * HEAD_DIM, (h + 1) * HEAD_DIM)
        qb = q_ref[0, pl.ds(r0, c), cs]
        kb = k_ref[0, pl.ds(r0, c), cs]
        v1 = jnp.concatenate([v_ref[0, pl.ds(r0, c), cs], ones], axis=1)
        irow = gates[d * 8 + h:d * 8 + h + 1]
        brow = csum[d * 8 + N_HEADS + h:d * 8 + N_HEADS + h + 1]
        rrow = irow - brow
        rcol = jnp.concatenate([rrow, jnp.zeros((7, c), F32)], axis=0).T[:, 0:1]
        st = st_ref[h]
        mprev = m_ref[h][:, 0:1]
        rmat = jnp.where(tri_t, rcol, -jnp.inf)
        grow = jnp.maximum(jnp.max(rmat, axis=0, keepdims=True), mprev)
        qk = _dot(kb, qb, _NT) * jnp.exp(rmat - grow)
        both = _dot(v1, qk.astype(BF16), _TN) + jnp.exp(mprev - grow) * _dot(st.astype(BF16), qb, _NT)
        den = both[HEAD_DIM:HEAD_DIM + 1]
        inv = 1.0 / jnp.maximum(jnp.abs(den), jnp.exp(-(brow + grow)))
        o_ref[0, pl.ds(r0, c), cs] = (both[:HEAD_DIM] * inv).T
        blast = brow[:, last:last + 1]
        mnew = blast + grow[:, last:last + 1]
        kh = (kb.astype(F32) * jnp.exp(blast + rcol - mnew)).astype(BF16)
        st_ref[h] = jnp.exp(blast + mprev - mnew) * st + _dot(v1, kh, _TN)
        m_ref[h] = jnp.broadcast_to(mnew, (1, HEAD_DIM))


def _ml_body(qf_ref, kf_ref, vf_ref, gf_ref, qb_ref, kb_ref, vb_ref, gb_ref, tf_ref, tb_ref,
             of_ref, ob_ref, st_ref, m_ref):
    @pl.when(pl.program_id(1) == 0)
    def _():
        st_ref[...] = jnp.zeros(st_ref.shape, F32)
        m_ref[...] = jnp.zeros(m_ref.shape, F32)

    nsub = SCAN_STEP // ML_CHUNK
    for c in range(nsub):
        _ml_unit(False, c * ML_CHUNK, 0, qf_ref, kf_ref, vf_ref, gf_ref, tf_ref, of_ref, st_ref.at[0], m_ref.at[0])
        r1 = (nsub - 1 - c) * ML_CHUNK
        _ml_unit(True, r1, 1, qb_ref, kb_ref, vb_ref, gb_ref, tb_ref, ob_ref, st_ref.at[1], m_ref.at[1])


def _mlstm(mlqk, mlv, gates, nx):
    bsz, tall, kw = mlv.shape
    steps = tall // SCAN_STEP
    blk = (1, SCAN_STEP, kw)
    gblk = (1, 4 * N_HEADS, SCAN_STEP)
    tf, tb = _ml_constants(False), _ml_constants(True)
    fwd = lambda col: (lambda b, s: (b, _fwd_blk(s, nx), col))
    bwd = lambda col: (lambda b, s: (b, _bwd_blk(s, nx), col))
    const = pl.BlockSpec(tf.shape, lambda b, s: (0, 0))
    in_specs = [pl.BlockSpec(blk, fwd(0)), pl.BlockSpec(blk, fwd(1)), pl.BlockSpec(blk, fwd(0)),
                pl.BlockSpec(gblk, lambda b, s: (b, 0, _fwd_blk(s, nx))),
                pl.BlockSpec(blk, bwd(0)), pl.BlockSpec(blk, bwd(1)), pl.BlockSpec(blk, bwd(0)),
                pl.BlockSpec(gblk, lambda b, s: (b, 0, _bwd_blk(s, nx))), const, const]
    out_sds = jax.ShapeDtypeStruct((bsz, tall, kw), F32)
    return pl.pallas_call(
        _ml_body,
        grid=(bsz, steps),
        in_specs=in_specs,
        out_specs=[pl.BlockSpec(blk, fwd(0)), pl.BlockSpec(blk, bwd(0))],
        out_shape=[out_sds, out_sds],
        scratch_shapes=[pltpu.VMEM((2, N_HEADS, 2 * HEAD_DIM, HEAD_DIM), F32),
                        pltpu.VMEM((2, N_HEADS, 1, HEAD_DIM), F32)],
        compiler_params=_params(("arbitrary", "arbitrary")),
        name="mlstm_scan",
    )(mlqk, mlqk, mlv, gates, mlqk, mlqk, mlv, gates, tf, tb)


def _out_body(hof_ref, hob_ref, mhf_ref, mhb_ref, hgg_ref, mlo_ref, x_ref, g1_ref, sc2_ref, sh2_ref,
              hnw_ref, mnw_ref, wout_ref, n2w_ref, rwt_ref, x1_ref, vt_ref, aff_ref):
    hg = hof_ref[0] + hob_ref[0]
    ml = mhf_ref[0] + mhb_ref[0]
    hparts, mparts = [], []
    for h in range(N_HEADS):
        cs = slice(h * HEAD_DIM, (h + 1) * HEAD_DIM)
        t = hg[:, cs]
        hparts.append(t * lax.rsqrt(jnp.mean(t * t, axis=-1, keepdims=True) + NORM_EPS))
        t = ml[:, cs]
        t = t - jnp.mean(t, axis=-1, keepdims=True)
        mparts.append(t * lax.rsqrt(jnp.mean(t * t, axis=-1, keepdims=True) + NORM_EPS))
    hgn = jnp.concatenate(hparts, axis=-1) * hnw_ref[...] * hgg_ref[0].astype(F32)
    mln = jnp.concatenate(mparts, axis=-1) * mnw_ref[...] * mlo_ref[0].astype(F32)
    mix = jnp.concatenate([hgn, mln], axis=-1).astype(BF16)
    x1 = x_ref[0] + g1_ref[0] * _dot(mix, wout_ref[...])
    x1_ref[0] = x1
    v = x1 * lax.rsqrt(jnp.mean(x1 * x1, axis=-1, keepdims=True) + NORM_EPS) * n2w_ref[...]
    v = v * (1.0 + sc2_ref[0]) + sh2_ref[0]
    vt_ref[0] = _pack_halves(v)
    logits = _dot(rwt_ref[...], v, _NT, precision=HIGHEST)
    ex = jnp.exp(logits - jnp.max(logits, axis=0, keepdims=True))
    aff_ref[0] = ex / jnp.sum(ex, axis=0, keepdims=True)


def _mixer_out(hg_f, hg_b, ml_f, ml_b, hgg, mlo, x, g1, sc2, sh2, hnw, mnw, w_out, n2w, rwt, tm):
    bsz, t, d = x.shape
    kw = HEADS_W
    ne = rwt.shape[0]
    row = lambda b, i: (b, i, 0)
    mod = lambda b, i: (b, 0, 0)
    const2 = lambda b, i: (0, 0)
    act = pl.BlockSpec((1, tm, kw), row)
    in_specs = [act, act, act, act, act, act,
                pl.BlockSpec((1, tm, d), row),
                pl.BlockSpec((1, 1, d), mod), pl.BlockSpec((1, 1, d), mod), pl.BlockSpec((1, 1, d), mod),
                pl.BlockSpec((1, kw), const2), pl.BlockSpec((1, kw), const2),
                pl.BlockSpec(w_out.shape, const2), pl.BlockSpec((1, d), const2), pl.BlockSpec(rwt.shape, const2)]
    return pl.pallas_call(
        _out_body,
        grid=(bsz, t // tm),
        in_specs=in_specs,
        out_specs=[pl.BlockSpec((1, tm, d), row),
                   pl.BlockSpec((1, tm, d // 2), row),
                   pl.BlockSpec((1, ne, tm), lambda b, i: (b, 0, i))],
        out_shape=[jax.ShapeDtypeStruct((bsz, t, d), F32),
                   jax.ShapeDtypeStruct((bsz, t, d // 2), I32),
                   jax.ShapeDtypeStruct((bsz, ne, t), F32)],
        compiler_params=_params(("arbitrary", "arbitrary")),
        name="mixer_out",
    )(hg_f, hg_b, ml_f, ml_b, hgg, mlo, x, g1, sc2, sh2, hnw, mnw, w_out, n2w, rwt)


def _prefix_count(maskf, u_ref, ones_ref, bl_ref):
    e, nb, ln = maskf.shape
    x = maskf.reshape(e * nb, ln)
    xb = x.astype(BF16)
    incl = _dot(xb, u_ref[...])
    tot = _dot(xb, ones_ref[...])
    off = _dot(bl_ref[...], tot.astype(BF16))
    return (incl - x + off).reshape(e, nb, ln), off.reshape(e, nb, ln)


def _topk_body(aff_ref, u_ref, ones_ref, bl_ref, bl1_ref, pos_ref, base_ref, *, cap):
    x = aff_ref[0]
    ne = x.shape[0]

    def count(m):
        return jnp.sum(jnp.sum(jnp.where(m, 1.0, 0.0), axis=1, keepdims=True), axis=2, keepdims=True)

    def halve(_, carry):
        lo, hi = carry
        mid = 0.5 * (lo + hi)
        up = count(x > mid) >= cap
        return jnp.where(up, mid, lo), jnp.where(up, hi, mid)

    lo, hi = lax.fori_loop(0, TOPK_BISECTIONS, halve,
                           (jnp.full((ne, 1, 1), -1.0, F32), jnp.full((ne, 1, 1), 1.0, F32)))
    gt = jnp.where(x > hi, 1.0, 0.0)
    eq = jnp.where(x > lo, 1.0, 0.0) - gt
    need = cap - count(x > hi)
    eq_rank, _ = _prefix_count(eq, u_ref, ones_ref, bl_ref)
    sel = gt + eq * jnp.where(eq_rank < need, 1.0, 0.0)
    pos, _ = _prefix_count(sel, u_ref, ones_ref, bl_ref)
    pos_ref[0] = jnp.where(sel > 0, pos, -1.0).astype(I32)
    nsel = jnp.sum(sel, axis=0)
    nb16 = nsel.astype(BF16)
    incl = _dot(nb16, u_ref[...])
    tot = _dot(nb16, ones_ref[...])
    off = _dot(bl1_ref[...], tot, precision=HIGHEST)
    base_ref[0] = (incl - nsel + off).astype(I32)


def _topk(aff4, cap):
    bsz, ne, nb, ln = aff4.shape
    k = np.arange(ln)
    u = jnp.asarray(k[:, None] <= k[None, :], BF16)
    ones = jnp.ones((ln, ln), BF16)
    r = np.arange(ne * nb)
    bl = jnp.asarray(((r[:, None] // nb) == (r[None, :] // nb)) & (r[None, :] < r[:, None]), BF16)
    r1 = np.arange(nb)
    bl1 = jnp.asarray(r1[None, :] < r1[:, None], F32)
    blk = pl.BlockSpec((1, ne, nb, ln), lambda b: (b, 0, 0, 0))
    const = lambda a: pl.BlockSpec(a.shape, lambda b: (0, 0))
    return pl.pallas_call(
        functools.partial(_topk_body, cap=cap),
        grid=(bsz,),
        in_specs=[blk, const(u), const(ones), const(bl), const(bl1)],
        out_specs=[blk, pl.BlockSpec((1, nb, ln), lambda b: (b, 0, 0))],
        out_shape=[jax.ShapeDtypeStruct((bsz, ne, nb, ln), I32), jax.ShapeDtypeStruct((bsz, nb, ln), I32)],
        compiler_params=_params(("arbitrary",)),
        name="expert_topk",
    )(aff4, u, ones, bl, bl1)


def _sc_setup():
    info = plsc.get_sparse_core_info()
    mesh = plsc.VectorSubcoreMesh(core_axis_name="c", subcore_axis_name="s")
    params = dataclasses.replace(pltpu.CompilerParams(), needs_layout_passes=False)
    return info.num_cores, info.num_cores * info.num_subcores, info.num_lanes, mesh, params


def _sc_worker_id(nc):
    return lax.axis_index("s") * nc + lax.axis_index("c")


def _sc_gather_chunks(table_hbm, out_hbm, idx_v, ich_v, rows_v, sem, out_row0, n, lanes):
    chunk = SC_GATHER_ROWS

    @pl.loop(0, n // chunk)
    def _(c):
        for q in range(chunk // lanes):
            src = pl.ds(pl.multiple_of(c * chunk + q * lanes, lanes), lanes)
            ich_v[pl.ds(q * lanes, lanes)] = idx_v[src]
        pltpu.async_copy(table_hbm.at[ich_v], rows_v, sem).wait()
        pltpu.sync_copy(rows_v, out_hbm.at[pl.ds(pl.multiple_of(out_row0 + c * chunk, chunk), chunk)])


def _sc_invert_gather(table, pos_flat, aff_flat, npairs, ne, t, cap):
    nc, nw, lanes, mesh, params = _sc_setup()
    per_w = npairs // nw
    width = table.shape[1]
    chunk = SC_GATHER_ROWS

    @functools.partial(
        pl.kernel, mesh=mesh, compiler_params=params,
        out_type=[jax.ShapeDtypeStruct((npairs * cap, width), table.dtype),
                  jax.ShapeDtypeStruct((npairs * cap,), F32)],
        scratch_types=[pltpu.VMEM((t,), I32), pltpu.VMEM((t,), F32), pltpu.VMEM((cap,), I32), pltpu.VMEM((cap,), F32),
                       pltpu.VMEM((chunk,), I32), pltpu.VMEM((chunk, width), table.dtype), pltpu.SemaphoreType.DMA],
    )
    def body(table_hbm, pos_hbm, aff_hbm, out_hbm, gate_hbm, pos_v, aff_v, idx_v, gate_v, ich_v, rows_v, sem):
        wid = _sc_worker_id(nc)

        @pl.loop(0, per_w)
        def _(kk):
            p = wid * per_w + kk
            tok0 = (p // ne) * t
            pltpu.sync_copy(pos_hbm.at[pl.ds(pl.multiple_of(p * t, t), t)], pos_v)
            pltpu.sync_copy(aff_hbm.at[pl.ds(pl.multiple_of(p * t, t), t)], aff_v)

            @pl.loop(0, t // lanes)
            def _(i):
                v = pos_v[pl.ds(pl.multiple_of(i * lanes, lanes), lanes)]
                tok = lax.iota(I32, lanes) + i * lanes
                plsc.store_scatter(idx_v, [v], tok, mask=v >= 0)

            @pl.loop(0, cap // lanes)
            def _(j):
                sl = pl.ds(pl.multiple_of(j * lanes, lanes), lanes)
                ii = idx_v[sl]
                gate_v[sl] = plsc.load_gather(aff_v, [ii])
                idx_v[sl] = ii + tok0

            pltpu.sync_copy(gate_v, gate_hbm.at[pl.ds(pl.multiple_of(p * cap, cap), cap)])
            _sc_gather_chunks(table_hbm, out_hbm, idx_v, ich_v, rows_v, sem, p * cap, cap, lanes)

    return body(table, pos_flat, aff_flat)


def _sc_assignment_order(pos_flat, base_flat, bsz, ne, t, cap):
    nc, nw, lanes, mesh, params = _sc_setup()
    na = ne * cap
    per_b = nw // bsz
    rng = na // per_b

    @functools.partial(
        pl.kernel, mesh=mesh, compiler_params=params,
        out_type=[jax.ShapeDtypeStruct((bsz * na,), I32), jax.ShapeDtypeStruct((bsz * na,), I32)],
        scratch_types=[pltpu.VMEM((t,), I32), pltpu.VMEM((t,), I32), pltpu.VMEM((t,), I32),
                       pltpu.VMEM((rng,), I32), pltpu.VMEM((rng,), I32)],
    )
    def body(pos_hbm, base_hbm, perm_hbm, tok_hbm, pos_v, base_v, rank_v, perm_v, tok_v):
        wid = _sc_worker_id(nc)
        b = wid // per_b
        a0 = (wid % per_b) * rng
        pltpu.sync_copy(base_hbm.at[pl.ds(pl.multiple_of(b * t, t), t)], base_v)

        @pl.loop(0, t // lanes)
        def _(i):
            rank_v[pl.ds(pl.multiple_of(i * lanes, lanes), lanes)] = jnp.zeros((lanes,), I32)

        @pl.loop(0, ne)
        def _(e):
            p = b * ne + e
            pltpu.sync_copy(pos_hbm.at[pl.ds(pl.multiple_of(p * t, t), t)], pos_v)

            @pl.loop(0, t // lanes)
            def _(i):
                sl = pl.ds(pl.multiple_of(i * lanes, lanes), lanes)
                v = pos_v[sl]
                r = rank_v[sl]
                a = base_v[sl] + r - a0
                sel = v >= 0
                mine = sel & (a >= 0) & (a < rng)
                plsc.store_scatter(perm_v, [a], v + p * cap, mask=mine)
                plsc.store_scatter(tok_v, [a], lax.iota(I32, lanes) + i * lanes, mask=mine)
                rank_v[sl] = r + jnp.where(sel, 1, 0)

        dst = pl.ds(pl.multiple_of(b * na + a0, rng), rng)
        pltpu.sync_copy(perm_v, perm_hbm.at[dst])
        pltpu.sync_copy(tok_v, tok_hbm.at[dst])

    return body(pos_flat, base_flat)


def _sc_gather_rows(table, idx):
    nc, nw, lanes, mesh, params = _sc_setup()
    n = idx.shape[0]
    per_w = n // nw
    width = table.shape[1]
    chunk = SC_GATHER_ROWS

    @functools.partial(
        pl.kernel, mesh=mesh, compiler_params=params,
        out_type=jax.ShapeDtypeStruct((n, width), table.dtype),
        scratch_types=[pltpu.VMEM((per_w,), I32), pltpu.VMEM((chunk,), I32),
                       pltpu.VMEM((chunk, width), table.dtype), pltpu.SemaphoreType.DMA],
    )
    def body(table_hbm, idx_hbm, out_hbm, idx_v, ich_v, rows_v, sem):
        row0 = _sc_worker_id(nc) * per_w
        pltpu.sync_copy(idx_hbm.at[pl.ds(pl.multiple_of(row0, per_w), per_w)], idx_v)
        _sc_gather_chunks(table_hbm, out_hbm, idx_v, ich_v, rows_v, sem, row0, per_w, lanes)

    return body(table, idx)


def _moe1_body(xs_ref, gate_ref, wg_ref, wu_ref, wd_ref, y_ref):
    xs = _unpack_halves(xs_ref[0])
    hg = _dot(xs, wg_ref[0].astype(BF16))
    hu = _dot(xs, wu_ref[0].astype(BF16))
    h = (_silu(hg) * hu).astype(BF16)
    y = _dot(h, wd_ref[0].astype(BF16))
    gt = gate_ref[0].T
    y = jnp.concatenate([y[k * 128:(k + 1) * 128] * gt[:, k:k + 1] for k in range(gt.shape[1])], axis=0)
    y_ref[0] = _pack_halves(y)


def _moe_experts(xs, gate, wg, wu, wd, bsz):
    npairs, cap, half = xs.shape
    ne, d, f = wg.shape
    pair = lambda b, e: (b * ne + e, 0, 0)
    expert = lambda b, e: (e, 0, 0)
    return pl.pallas_call(
        _moe1_body,
        grid=(bsz, ne),
        in_specs=[pl.BlockSpec((1, cap, half), pair),
                  pl.BlockSpec((1, cap // 128, 128), pair),
                  pl.BlockSpec((1, d, f), expert),
                  pl.BlockSpec((1, d, f), expert),
                  pl.BlockSpec((1, f, d), expert)],
        out_specs=pl.BlockSpec((1, cap, half), pair),
        out_shape=jax.ShapeDtypeStruct((npairs, cap, half), I32),
        compiler_params=_params(("arbitrary", "arbitrary")),
        name="moe_experts",
    )(xs, gate, wg, wu, wd)


def _combine_body(abase_ref, ys_hbm, tok_ref, x1_ref, g2_ref, fw_ref, o_ref, buf_ref, acc_ref, sem_ref, cnt_ref, *, na):
    n = ONEHOT_BLK
    ring = buf_ref.shape[0]
    nch = na // n
    b, i = pl.program_id(0), pl.program_id(1)

    @pl.when(i == 0)
    def _():
        cnt_ref[0] = 0
        cnt_ref[1] = 0

    lo = abase_ref[b, i]
    hi = abase_ref[b, i + 1]
    c0 = lo // n
    c1 = jnp.where(hi > lo, (hi + n - 1) // n, c0)

    def copy(c):
        slot = c % ring
        row0 = pl.multiple_of(b * na + c * n, n)
        return pltpu.make_async_copy(ys_hbm.at[pl.ds(row0, n)], buf_ref.at[slot], sem_ref.at[slot])

    acc_ref[...] = jnp.zeros(acc_ref.shape, F32)
    tok_ids = lax.broadcasted_iota(I32, (n, n), 0) + i * n

    def step(c, carry):
        started = cnt_ref[0]
        ahead = jnp.minimum(c + ring, nch)

        def start(k, _):
            copy(k).start()
            return 0

        lax.fori_loop(started, ahead, start, 0)
        cnt_ref[0] = jnp.maximum(started, ahead)

        @pl.when(cnt_ref[1] <= c)
        def _():
            copy(c).wait()
            cnt_ref[1] = c + 1

        seg = jnp.where(tok_ids == tok_ref[0, pl.ds(c, 1), :], 1.0, 0.0).astype(BF16)
        acc_ref[...] += _dot(seg, _unpack_halves(buf_ref[c % ring]))
        return carry

    lax.fori_loop(c0, c1, step, 0)
    x2 = x1_ref[0] + g2_ref[0] * acc_ref[...]
    o_ref[0] = x2 * lax.rsqrt(jnp.mean(x2 * x2, axis=-1, keepdims=True) + NORM_EPS) * fw_ref[...]


def _combine_final(abase, ys, tok, x1, g2, fw, na):
    bsz, t, d = x1.shape
    n = ONEHOT_BLK
    grid_spec = pltpu.PrefetchScalarGridSpec(
        num_scalar_prefetch=1,
        grid=(bsz, t // n),
        in_specs=[pl.BlockSpec(memory_space=pl.ANY),
                  pl.BlockSpec((1, na // n, n), lambda b, i, a: (b, 0, 0)),
                  pl.BlockSpec((1, n, d), lambda b, i, a: (b, i, 0)),
                  pl.BlockSpec((1, 1, d), lambda b, i, a: (b, 0, 0)),
                  pl.BlockSpec((1, d), lambda b, i, a: (0, 0))],
        out_specs=pl.BlockSpec((1, n, d), lambda b, i, a: (b, i, 0)),
        scratch_shapes=[pltpu.VMEM((COMBINE_RING, n, d // 2), I32), pltpu.VMEM((n, d), F32),
                        pltpu.SemaphoreType.DMA((COMBINE_RING,)), pltpu.SMEM((2,), I32)],
    )
    return pl.pallas_call(
        functools.partial(_combine_body, na=na),
        grid_spec=grid_spec,
        out_shape=jax.ShapeDtypeStruct((bsz, t, d), F32),
        compiler_params=_params(("arbitrary", "arbitrary")),
        name="combine_final",
    )(abase, ys, tok, x1, g2, fw)


def kernel(x, c, ctx, c_ctx, ada_w, ada_b, norm1_w, w_in, conv_w, conv_b, hg_lb_logits, ml_gate_b,
           hg_norm_w, ml_norm_w, w_out, norm2_w, router_w, exp_w_gate, exp_w_up, exp_w_down, final_norm_w):
    bsz, t, d = x.shape
    nctx = ctx.shape[1]
    assert ada_w.shape[0] == 1, "single-layer block"
    assert nctx == SCAN_STEP and t % SCAN_STEP == 0 and bsz + 1 <= 8
    tall = t + nctx
    nx = t // SCAN_STEP
    kw = HEADS_W
    ne = router_w.shape[-1]
    cap = EC_CAPACITY * t // ne
    tm = min(512, t)

    rows = jnp.concatenate([c, c_ctx[None], jnp.zeros((7 - bsz, d), F32)], axis=0)
    mod = _modulation(rows, ada_w[0], ada_b[0][None])
    mx = [m[:, None, :] for m in jnp.split(mod[:bsz], 6, axis=-1)]
    mc = [m[:, None, :] for m in jnp.split(mod[bsz:bsz + 1], 6, axis=-1)]
    sh1, sc1, g1, sh2, sc2, g2 = mx
    csh1, csc1 = mc[0], mc[1]

    main_w = 9 * kw
    w_main = w_in[0][:, :main_w].astype(BF16)
    w_gt = w_in[0][:, main_w:].T
    gate_b = ml_gate_b[0][:, None]
    nw1 = norm1_w[0][None]
    outs = _inproj(x, sc1, sh1, nw1, w_main, w_gt, gate_b, hg_lb_logits, tall, tm, 0)
    outs = _inproj(ctx, csc1, csh1, nw1, w_main, w_gt, gate_b, hg_lb_logits, tall, nctx, t // nctx, prev=outs)
    hgq, hgv, hgg, hgk, hglf, mlqk_pre, mlv, mlo, gates = outs

    mlqk = _conv(mlqk_pre, conv_w[0], conv_b[0][None], t, nctx)
    hg_f, hg_b = _hgrn2(hgq, hgk, hgv, hglf, nx)
    ml_f, ml_b = _mlstm(mlqk, mlv, gates, nx)

    x1, vpk, aff = _mixer_out(hg_f, hg_b, ml_f, ml_b, hgg, mlo, x, g1, sc2, sh2,
                             hg_norm_w[0][None], ml_norm_w[0][None], w_out[0].astype(BF16),
                             norm2_w[0][None], router_w[0].T, tm)

    pos4, base = _topk(aff.reshape(bsz, ne, t // 128, 128), cap)
    pos_flat = pos4.reshape(-1)
    xs, gate = _sc_invert_gather(vpk.reshape(bsz * t, d // 2), pos_flat, aff.reshape(-1), bsz * ne, ne, t, cap)
    perm, tok = _sc_assignment_order(pos_flat, base.reshape(-1), bsz, ne, t, cap)
    y = _moe_experts(xs.reshape(bsz * ne, cap, d // 2), gate.reshape(bsz * ne, cap // 128, 128),
                     exp_w_gate[0], exp_w_up[0], exp_w_down[0], bsz)
    na = ne * cap
    ys = _sc_gather_rows(y.reshape(bsz * na, d // 2), perm)
    abase = jnp.concatenate([base.reshape(bsz, t)[:, ::ONEHOT_BLK], jnp.full((bsz, 1), na, I32)], axis=1)
    return _combine_final(abase, ys, tok.reshape(bsz, na // ONEHOT_BLK, ONEHOT_BLK), x1, g2, final_norm_w[None], na)
```

```python
import dataclasses
import functools

import numpy as np
import jax
import jax.numpy as jnp
from jax import lax
from jax.experimental import pallas as pl
from jax.experimental.pallas import tpu as pltpu
from jax.experimental.pallas import tpu_sc as plsc

F32 = jnp.float32
BF16 = jnp.bfloat16
I32 = jnp.int32
HIGHEST = lax.Precision.HIGHEST
NORM_EPS = 1e-6

HEAD_DIM = 128
N_HEADS = 4
HEADS_W = N_HEADS * HEAD_DIM
GRID_W = 64
N_EXPERTS = 16
EC_CAPACITY = 2
HG_CHUNK = 64
ML_CHUNK = 128
SCAN_STEP = 256
ONEHOT_BLK = 256
CONV_HALO = 72
SC_GATHER_ROWS = 64
COMBINE_RING = 6
TOPK_BISECTIONS = 64
VMEM_LIMIT = 56 * 1024 * 1024

_NT = (((1,), (1,)), ((), ()))
_TN = (((0,), (0,)), ((), ()))


def _dot(a, b, dims=None, precision=None):
    if dims is None:
        return jnp.dot(a, b, preferred_element_type=F32, precision=precision)
    return lax.dot_general(a, b, dims, preferred_element_type=F32, precision=precision)


def _sigmoid(x):
    return jax.nn.sigmoid(x)


def _pack_halves(x):
    w = x.shape[-1] // 2
    bits = lax.bitcast_convert_type(x.astype(BF16).astype(F32), I32)
    return lax.shift_right_logical(bits[:, :w], 16) | bits[:, w:]


def _unpack_halves(p):
    lo = lax.bitcast_convert_type(lax.shift_left(p, 16), F32)
    hi = lax.bitcast_convert_type(p & jnp.int32(-65536), F32)
    return jnp.concatenate([lo, hi], axis=-1).astype(BF16)


def _silu(x):
    return x * jax.nn.sigmoid(x)


def _params(sem, flags=None):
    return pltpu.CompilerParams(dimension_semantics=sem, vmem_limit_bytes=VMEM_LIMIT, flags=flags)


def _mod_body(r_ref, w_ref, b_ref, o_ref):
    r = r_ref[...]
    o_ref[...] = _dot(_silu(r), w_ref[...], precision=HIGHEST) + b_ref[...]


def _modulation(rows, w, b):
    d, n = w.shape
    tn = n // 4
    return pl.pallas_call(
        _mod_body,
        grid=(n // tn,),
        in_specs=[pl.BlockSpec((8, d), lambda j: (0, 0)),
                  pl.BlockSpec((d, tn), lambda j: (0, j)),
                  pl.BlockSpec((1, tn), lambda j: (0, j))],
        out_specs=pl.BlockSpec((8, tn), lambda j: (0, j)),
        out_shape=jax.ShapeDtypeStruct((8, n), F32),
        compiler_params=_params(("arbitrary",)),
        name="modulation",
    )(rows, w, b)


def _log_sigmoid(x):
    return jnp.minimum(x, 0.0) - jnp.log(1.0 + jnp.exp(-jnp.abs(x)))


def _inproj_body(x_ref, sc_ref, sh_ref, nw_ref, w_ref, wg_ref, gb_ref, lbl_ref, *refs):
    hgq_ref, hgv_ref, hgg_ref, hgk_ref, hglf_ref, mlqk_ref, mlv_ref, mlo_ref, gates_ref = refs[-9:]
    kw = HEADS_W
    x = x_ref[0]
    y = x * lax.rsqrt(jnp.mean(x * x, axis=-1, keepdims=True) + NORM_EPS) * nw_ref[...]
    u = y * (1.0 + sc_ref[0]) + sh_ref[0]
    ub = u.astype(BF16)

    def proj(c0, c1):
        return _dot(ub, w_ref[:, c0:c1])

    hgq_ref[0] = _silu(proj(0, kw)).astype(BF16)
    hgv_ref[0] = proj(kw, 2 * kw).astype(BF16)
    hgg_ref[0] = _silu(proj(2 * kw, 3 * kw)).astype(BF16)

    lbl = lbl_ref[...]
    mx = jnp.max(lbl, axis=0)
    ex = jnp.exp(lbl - mx[None])
    lb = ex[0] / jnp.sum(ex, axis=0)
    for d in range(2):
        p = proj((3 + d) * kw, (4 + d) * kw)
        lbd = lb[d:d + 1]
        f = lbd + (1.0 - lbd) * _sigmoid(p)
        hgk_ref[0, :, d * kw:(d + 1) * kw] = (1.0 - f).astype(BF16)
        hglf_ref[0, :, d * kw:(d + 1) * kw] = jnp.log(f)

    mlqk_ref[0, :, 0:kw] = proj(5 * kw, 6 * kw)
    mlqk_ref[0, :, kw:2 * kw] = proj(6 * kw, 7 * kw)
    mlv_ref[0] = proj(7 * kw, 8 * kw).astype(BF16)
    mlo_ref[0] = _sigmoid(proj(8 * kw, 9 * kw)).astype(BF16)

    g = _dot(wg_ref[...], u, _NT, precision=HIGHEST) + gb_ref[...]
    row = lax.broadcasted_iota(I32, g.shape, 0)
    gates_ref[0] = jnp.where((row % 8) >= N_HEADS, _log_sigmoid(g), g)


def _inproj_shapes(bsz, tall):
    kw = HEADS_W
    return [
        jax.ShapeDtypeStruct((bsz, tall, kw), BF16),
        jax.ShapeDtypeStruct((bsz, tall, kw), BF16),
        jax.ShapeDtypeStruct((bsz, tall, kw), BF16),
        jax.ShapeDtypeStruct((bsz, tall, 2 * kw), BF16),
        jax.ShapeDtypeStruct((bsz, tall, 2 * kw), F32),
        jax.ShapeDtypeStruct((bsz, tall, 2 * kw), F32),
        jax.ShapeDtypeStruct((bsz, tall, kw), BF16),
        jax.ShapeDtypeStruct((bsz, tall, kw), BF16),
        jax.ShapeDtypeStruct((bsz, 4 * N_HEADS, tall), F32),
    ]


def _inproj(tokens, scale, shift, nw, w_main, w_gt, gate_b, lb_logits, tall, tm, blk0, prev=None):
    bsz, n, d = tokens.shape
    kw = HEADS_W
    nt = n // tm
    per_sample = scale.shape[0] == bsz
    mod_map = (lambda b, i: (b, 0, 0)) if per_sample else (lambda b, i: (0, 0, 0))
    const2 = lambda b, i: (0, 0)
    in_specs = [
        pl.BlockSpec((1, tm, d), lambda b, i: (b, i, 0)),
        pl.BlockSpec((1, 1, d), mod_map),
        pl.BlockSpec((1, 1, d), mod_map),
        pl.BlockSpec((1, d), const2),
        pl.BlockSpec(w_main.shape, const2),
        pl.BlockSpec(w_gt.shape, const2),
        pl.BlockSpec(gate_b.shape, const2),
        pl.BlockSpec(lb_logits.shape, lambda b, i: (0, 0, 0)),
    ]
    args = [tokens, scale, shift, nw, w_main, w_gt, gate_b, lb_logits]
    aliases = {}
    if prev is not None:
        for k, a in enumerate(prev):
            in_specs.append(pl.BlockSpec(memory_space=pl.ANY))
            aliases[len(args)] = k
            args.append(a)
    row_map = lambda b, i: (b, blk0 + i, 0)
    widths = [kw, kw, kw, 2 * kw, 2 * kw, 2 * kw, kw, kw]
    out_specs = [pl.BlockSpec((1, tm, w), row_map) for w in widths]
    out_specs.append(pl.BlockSpec((1, 4 * N_HEADS, tm), lambda b, i: (b, 0, blk0 + i)))
    return pl.pallas_call(
        _inproj_body,
        grid=(bsz, nt),
        in_specs=in_specs,
        out_specs=out_specs,
        out_shape=_inproj_shapes(bsz, tall),
        input_output_aliases=aliases,
        compiler_params=_params(("arbitrary", "arbitrary")),
        name="inproj_ctx" if prev is not None else "inproj_x",
    )(*args)


def _conv_body(x_ref, w_ref, b_ref, o_ref, pad_ref, cpad_ref, *, t, nctx, scale_from):
    halo = CONV_HALO
    rows = 512
    win = rows + 2 * halo
    ch = x_ref.shape[-1]
    scale = jnp.where(pl.program_id(1) >= scale_from, HEAD_DIM ** -0.5, 1.0).astype(F32)
    w = w_ref[...]
    bias = b_ref[...]

    pad_ref[0:halo, :] = jnp.zeros((halo, ch), F32)
    pad_ref[halo + t:halo + t + halo, :] = jnp.zeros((halo, ch), F32)
    pad_ref[halo:halo + t, :] = x_ref[0, 0:t, :]
    col = (lax.broadcasted_iota(I32, (win, ch), 0) + (GRID_W - halo % GRID_W)) % GRID_W
    left_ok = col > 0
    right_ok = col < GRID_W - 1

    def chunk(c, carry):
        o = pl.multiple_of(c * rows, rows)
        xw = pad_ref[pl.ds(o, win), :]
        xm = jnp.where(left_ok, pltpu.roll(xw, 1, 0), 0.0)
        xp = jnp.where(right_ok, pltpu.roll(xw, win - 1, 0), 0.0)
        z = [xm * w[dr, 0:1] + xw * w[dr, 1:2] + xp * w[dr, 2:3] for dr in range(3)]
        y = (z[1][halo:halo + rows]
             + z[0][halo - GRID_W:halo - GRID_W + rows]
             + z[2][halo + GRID_W:halo + GRID_W + rows])
        o_ref[0, pl.ds(o, rows), :] = (_silu(y + bias) * scale).astype(o_ref.dtype)
        return carry

    lax.fori_loop(0, t // rows, chunk, 0)

    cpad_ref[0:8, :] = jnp.zeros((8, ch), F32)
    cpad_ref[8 + nctx:16 + nctx, :] = jnp.zeros((8, ch), F32)
    cpad_ref[8:8 + nctx, :] = x_ref[0, t:t + nctx, :]
    xw = cpad_ref[...]
    n = nctx + 16
    y = (pltpu.roll(xw, 1, 0) * w[1, 0:1] + xw * w[1, 1:2] + pltpu.roll(xw, n - 1, 0) * w[1, 2:3])[8:8 + nctx]
    o_ref[0, t:t + nctx, :] = (_silu(y + bias) * scale).astype(o_ref.dtype)


def _conv(qk_pre, conv_w, conv_b, t, nctx):
    bsz, tall, c = qk_pre.shape
    ch = 128
    body = functools.partial(_conv_body, t=t, nctx=nctx, scale_from=(c // 2) // ch)
    return pl.pallas_call(
        body,
        grid=(bsz, c // ch),
        in_specs=[pl.BlockSpec((1, tall, ch), lambda b, j: (b, 0, j)),
                  pl.BlockSpec((3, 3, ch), lambda b, j: (0, 0, j)),
                  pl.BlockSpec((1, ch), lambda b, j: (0, j))],
        out_specs=pl.BlockSpec((1, tall, ch), lambda b, j: (b, 0, j)),
        out_shape=jax.ShapeDtypeStruct((bsz, tall, c), BF16),
        scratch_shapes=[pltpu.VMEM((t + 2 * CONV_HALO, ch), F32),
                        pltpu.VMEM((nctx + 16, ch), F32)],
        compiler_params=_params(("arbitrary", "arbitrary")),
        name="qk_conv",
    )(qk_pre, conv_w, conv_b)


def _fwd_blk(s, nx):
    return jnp.where(s == 0, nx, s - 1)


def _bwd_blk(s, nx):
    return jnp.where(s == 0, nx, nx - s)


def _hg_constants(rev):
    c = HG_CHUNK
    i = np.arange(c)[:, None]
    j = np.arange(c)[None, :]
    blocks = [(j >= i) if rev else (j <= i), (j < i) if rev else (j > i)]
    masks = [i == j]
    m = c // 2
    while m >= 1:
        b0 = (i // (2 * m)) * (2 * m)
        same = (i // (2 * m)) == (j // (2 * m))
        if rev:
            beta = b0 + m
            qrow = (i % (2 * m)) < m
            g = np.where(qrow, (j >= i) & (j < beta), (j >= beta) & (j < i))
            mask = same & qrow & ((j % (2 * m)) >= m)
        else:
            beta = b0 + m - 1
            qrow = (i % (2 * m)) >= m
            g = np.where(qrow, (j > beta) & (j <= i), (j > i) & (j <= beta))
            mask = same & qrow & ((j % (2 * m)) < m)
        blocks.append(g)
        masks.append(mask)
        m //= 2
    g = np.concatenate(blocks, axis=0).astype(np.float32)
    g3 = np.concatenate([g, g, g], axis=1)
    m2 = np.concatenate([np.stack(masks), np.stack(masks)], axis=2)
    return (jnp.asarray(g3, BF16), jnp.asarray(m2, F32))


def _block_diag(x, zero):
    w = x.shape[1] // 2
    return jnp.concatenate([jnp.concatenate([x[:, :w], zero], axis=1),
                            jnp.concatenate([zero, x[:, w:]], axis=1)], axis=0)


def _hg_chunk(dirs):
    c = HG_CHUNK
    w = 2 * HEAD_DIM
    zero = jnp.zeros((c, HEAD_DIM), BF16)
    units = []
    for rev, r0, q_ref, k_ref, v_ref, lf_ref, g_ref, msk_ref, o_ref, st_ref in dirs:
        rows = pl.ds(r0, c)
        lf = lf_ref[0, rows, :]
        p1 = lf.astype(BF16)
        r1 = lf - p1.astype(F32)
        p2 = r1.astype(BF16)
        p3 = (r1 - p2.astype(F32)).astype(BF16)
        dall = _dot(g_ref[...], jnp.concatenate([p1, p2, p3], axis=0))
        for hp in range(N_HEADS // 2):
            cs = slice(hp * w, (hp + 1) * w)
            units.append(dict(rev=rev, rows=rows, cs=cs, hp=hp, msk_ref=msk_ref, o_ref=o_ref, st_ref=st_ref,
                              q=q_ref[0, rows, cs], k=k_ref[0, rows, cs], v=v_ref[0, rows, cs], dall=dall[:, cs]))
    for u in units:
        msk_ref = u["msk_ref"]
        nlev = msk_ref.shape[0] - 1
        att = _dot(u["q"], _block_diag(u["k"], zero), _NT) * msk_ref[0]
        for l in range(nlev):
            e = jnp.exp(u["dall"][(l + 2) * c:(l + 3) * c]).astype(BF16)
            att = att + _dot(u["q"] * e, _block_diag(u["k"] * e, zero), _NT) * msk_ref[l + 1]
        u["att"] = att.astype(BF16)
    for u in units:
        a = u["dall"][0:c]
        u["a_tot"] = a[0:1] if u["rev"] else a[c - 1:c]
        u["st"] = [u["st_ref"][2 * u["hp"] + i] for i in range(2)]
        zf = jnp.zeros((HEAD_DIM, HEAD_DIM), BF16)
        st2 = jnp.concatenate([jnp.concatenate([u["st"][0].astype(BF16), zf], axis=1),
                               jnp.concatenate([zf, u["st"][1].astype(BF16)], axis=1)], axis=0)
        qbar = (u["q"].astype(F32) * jnp.exp(a)).astype(BF16)
        u["o"] = _dot(u["att"], _block_diag(u["v"], zero)) + _dot(qbar, st2, _NT)
        u["khat"] = (u["k"].astype(F32) * jnp.exp(u["dall"][c:2 * c])).astype(BF16)
    for u in units:
        u["o_ref"][0, u["rows"], u["cs"]] = u["o"]
        for i in range(2):
            hs = slice(i * HEAD_DIM, (i + 1) * HEAD_DIM)
            upd = _dot(u["v"][:, hs], u["khat"][:, hs], _TN)
            u["st_ref"][2 * u["hp"] + i] = u["st"][i] * jnp.exp(u["a_tot"][:, hs]) + upd


def _hg_body(qf_ref, kf_ref, vf_ref, lff_ref, qb_ref, kb_ref, vb_ref, lfb_ref,
             gf_ref, mf_ref, gb_ref, mb_ref, of_ref, ob_ref, st_ref):
    @pl.when(pl.program_id(1) == 0)
    def _():
        st_ref[...] = jnp.zeros(st_ref.shape, F32)

    nsub = SCAN_STEP // HG_CHUNK
    for c in range(nsub):
        _hg_chunk([
            (False, c * HG_CHUNK, qf_ref, kf_ref, vf_ref, lff_ref, gf_ref, mf_ref, of_ref, st_ref.at[0]),
            (True, (nsub - 1 - c) * HG_CHUNK, qb_ref, kb_ref, vb_ref, lfb_ref, gb_ref, mb_ref, ob_ref, st_ref.at[1]),
        ])


def _hgrn2(hgq, hgk, hgv, hglf, nx):
    bsz, tall, kw = hgq.shape
    steps = tall // SCAN_STEP
    cf = _hg_constants(False)
    cb = _hg_constants(True)
    blk = (1, SCAN_STEP, kw)
    fwd = lambda col: (lambda b, s: (b, _fwd_blk(s, nx), col))
    bwd = lambda col: (lambda b, s: (b, _bwd_blk(s, nx), col))
    const = lambda a: pl.BlockSpec(a.shape, lambda b, s: (0,) * a.ndim)
    in_specs = [pl.BlockSpec(blk, fwd(0)), pl.BlockSpec(blk, fwd(0)), pl.BlockSpec(blk, fwd(0)), pl.BlockSpec(blk, fwd(0)),
                pl.BlockSpec(blk, bwd(0)), pl.BlockSpec(blk, bwd(1)), pl.BlockSpec(blk, bwd(0)), pl.BlockSpec(blk, bwd(1))]
    in_specs += [const(a) for a in cf + cb]
    out_sds = jax.ShapeDtypeStruct((bsz, tall, kw), F32)
    return pl.pallas_call(
        _hg_body,
        grid=(bsz, steps),
        in_specs=in_specs,
        out_specs=[pl.BlockSpec(blk, fwd(0)), pl.BlockSpec(blk, bwd(0))],
        out_shape=[out_sds, out_sds],
        scratch_shapes=[pltpu.VMEM((2, N_HEADS, HEAD_DIM, HEAD_DIM), F32)],
        compiler_params=_params(("arbitrary", "arbitrary")),
        name="hgrn2_scan",
    )(hgq, hgk, hgv, hglf, hgq, hgk, hgv, hglf, *cf, *cb)


def _ml_constants(rev):
    k = np.arange(ML_CHUNK)
    tri = (k[:, None] >= k[None, :]) if rev else (k[:, None] <= k[None, :])
    return jnp.asarray(np.concatenate([tri, tri, tri], axis=0), BF16)


def _ml_chunk(dirs):
    c = ML_CHUNK
    ii = lax.broadcasted_iota(I32, (c, c), 0)
    jj = lax.broadcasted_iota(I32, (c, c), 1)
    ones = jnp.ones((c, HEAD_DIM), BF16)
    units = []
    for rev, r0, d, q_ref, k_ref, v_ref, g_ref, tri3_ref, o_ref, st_ref, m_ref in dirs:
        gates = g_ref[0, :, pl.ds(r0, c)]
        p1 = gates.astype(BF16)
        r1 = gates - p1.astype(F32)
        p2 = r1.astype(BF16)
        p3 = (r1 - p2.astype(F32)).astype(BF16)
        csum = _dot(jnp.concatenate([p1, p2, p3], axis=1), tri3_ref[...])
        for h in range(N_HEADS):
            cs = slice(h * HEAD_DIM, (h + 1) * HEAD_DIM)
            u = dict(rev=rev, o_ref=o_ref, rows=pl.ds(r0, c), cs=cs, st_ref=st_ref, m_ref=m_ref, h=h)
            u["qb"] = q_ref[0, pl.ds(r0, c), cs]
            u["kb"] = k_ref[0, pl.ds(r0, c), cs]
            u["v1"] = jnp.concatenate([v_ref[0, pl.ds(r0, c), cs], ones], axis=1)
            irow = gates[d * 8 + h:d * 8 + h + 1]
            u["brow"] = csum[d * 8 + N_HEADS + h:d * 8 + N_HEADS + h + 1]
            u["rrow"] = irow - u["brow"]
            units.append(u)
    for u in units:
        u["st"] = u["st_ref"][u["h"]]
        a = _dot(jnp.concatenate([u["kb"], u["st"].astype(BF16)], axis=0), u["qb"], _NT)
        u["s"] = a[:c]
        u["sq"] = a[c:]
    for u in units:
        last = 0 if u["rev"] else c - 1
        tri_t = (ii >= jj) if u["rev"] else (ii <= jj)
        rcol = jnp.concatenate([u["rrow"], jnp.zeros((7, c), F32)], axis=0).T[:, 0:1]
        u["mprev"] = u["m_ref"][u["h"]][:, 0:1]
        rmat = jnp.where(tri_t, rcol, -jnp.inf)
        u["grow"] = jnp.maximum(jnp.max(rmat, axis=0, keepdims=True), u["mprev"])
        qk = (u["s"] * jnp.exp(rmat - u["grow"])).astype(BF16)
        blast = u["brow"][:, last:last + 1]
        u["mnew"] = blast + u["grow"][:, last:last + 1]
        kh = (u["kb"].astype(F32) * jnp.exp(blast + rcol - u["mnew"])).astype(BF16)
        u["ws"] = jnp.exp(blast + u["mprev"] - u["mnew"])
        u["qkh"] = jnp.concatenate([qk, kh], axis=1)
    for u in units:
        u["nu"] = _dot(u["v1"], u["qkh"], _TN)
    for u in units:
        both = u["nu"][:, :c] + jnp.exp(u["mprev"] - u["grow"]) * u["sq"]
        den = both[HEAD_DIM:HEAD_DIM + 1]
        inv = 1.0 / jnp.maximum(jnp.abs(den), jnp.exp(-(u["brow"] + u["grow"])))
        u["o_ref"][0, u["rows"], u["cs"]] = (both[:HEAD_DIM] * inv).T
        u["st_ref"][u["h"]] = u["ws"] * u["st"] + u["nu"][:, c:]
        u["m_ref"][u["h"]] = jnp.broadcast_to(u["mnew"], (1, HEAD_DIM))


def _ml_body(qf_ref, kf_ref, vf_ref, gf_ref, qb_ref, kb_ref, vb_ref, gb_ref, tf_ref, tb_ref,
             of_ref, ob_ref, st_ref, m_ref):
    @pl.when(pl.program_id(1) == 0)
    def _():
        st_ref[...] = jnp.zeros(st_ref.shape, F32)
        m_ref[...] = jnp.zeros(m_ref.shape, F32)

    nsub = SCAN_STEP // ML_CHUNK
    for c in range(nsub):
        _ml_chunk([
            (False, c * ML_CHUNK, 0, qf_ref, kf_ref, vf_ref, gf_ref, tf_ref, of_ref, st_ref.at[0], m_ref.at[0]),
            (True, (nsub - 1 - c) * ML_CHUNK, 1, qb_ref, kb_ref, vb_ref, gb_ref, tb_ref, ob_ref, st_ref.at[1], m_ref.at[1]),
        ])


def _mlstm(mlqk, mlv, gates, nx):
    bsz, tall, kw = mlv.shape
    steps = tall // SCAN_STEP
    blk = (1, SCAN_STEP, kw)
    gblk = (1, 4 * N_HEADS, SCAN_STEP)
    tf, tb = _ml_constants(False), _ml_constants(True)
    fwd = lambda col: (lambda b, s: (b, _fwd_blk(s, nx), col))
    bwd = lambda col: (lambda b, s: (b, _bwd_blk(s, nx), col))
    const = pl.BlockSpec(tf.shape, lambda b, s: (0, 0))
    in_specs = [pl.BlockSpec(blk, fwd(0)), pl.BlockSpec(blk, fwd(1)), pl.BlockSpec(blk, fwd(0)),
                pl.BlockSpec(gblk, lambda b, s: (b, 0, _fwd_blk(s, nx))),
                pl.BlockSpec(blk, bwd(0)), pl.BlockSpec(blk, bwd(1)), pl.BlockSpec(blk, bwd(0)),
                pl.BlockSpec(gblk, lambda b, s: (b, 0, _bwd_blk(s, nx))), const, const]
    out_sds = jax.ShapeDtypeStruct((bsz, tall, kw), F32)
    return pl.pallas_call(
        _ml_body,
        grid=(bsz, steps),
        in_specs=in_specs,
        out_specs=[pl.BlockSpec(blk, fwd(0)), pl.BlockSpec(blk, bwd(0))],
        out_shape=[out_sds, out_sds],
        scratch_shapes=[pltpu.VMEM((2, N_HEADS, 2 * HEAD_DIM, HEAD_DIM), F32),
                        pltpu.VMEM((2, N_HEADS, 1, HEAD_DIM), F32)],
        compiler_params=_params(("arbitrary", "arbitrary")),
        name="mlstm_scan",
    )(mlqk, mlqk, mlv, gates, mlqk, mlqk, mlv, gates, tf, tb)


def _out_body(hof_ref, hob_ref, mhf_ref, mhb_ref, hgg_ref, mlo_ref, x_ref, g1_ref, sc2_ref, sh2_ref,
              hnw_ref, mnw_ref, wout_ref, n2w_ref, rwt_ref, x1_ref, vt_ref, aff_ref):
    hg = hof_ref[0] + hob_ref[0]
    ml = mhf_ref[0] + mhb_ref[0]
    hparts, mparts = [], []
    for h in range(N_HEADS):
        cs = slice(h * HEAD_DIM, (h + 1) * HEAD_DIM)
        t = hg[:, cs]
        hparts.append(t * lax.rsqrt(jnp.mean(t * t, axis=-1, keepdims=True) + NORM_EPS))
        t = ml[:, cs]
        t = t - jnp.mean(t, axis=-1, keepdims=True)
        mparts.append(t * lax.rsqrt(jnp.mean(t * t, axis=-1, keepdims=True) + NORM_EPS))
    hgn = jnp.concatenate(hparts, axis=-1) * hnw_ref[...] * hgg_ref[0].astype(F32)
    mln = jnp.concatenate(mparts, axis=-1) * mnw_ref[...] * mlo_ref[0].astype(F32)
    mix = jnp.concatenate([hgn, mln], axis=-1).astype(BF16)
    x1 = x_ref[0] + g1_ref[0] * _dot(mix, wout_ref[...])
    x1_ref[0] = x1
    v = x1 * lax.rsqrt(jnp.mean(x1 * x1, axis=-1, keepdims=True) + NORM_EPS) * n2w_ref[...]
    v = v * (1.0 + sc2_ref[0]) + sh2_ref[0]
    vt_ref[0] = _pack_halves(v)
    logits = _dot(rwt_ref[...], v, _NT, precision=HIGHEST)
    ex = jnp.exp(logits - jnp.max(logits, axis=0, keepdims=True))
    aff_ref[0] = ex / jnp.sum(ex, axis=0, keepdims=True)


def _mixer_out(hg_f, hg_b, ml_f, ml_b, hgg, mlo, x, g1, sc2, sh2, hnw, mnw, w_out, n2w, rwt, tm):
    bsz, t, d = x.shape
    kw = HEADS_W
    ne = rwt.shape[0]
    row = lambda b, i: (b, i, 0)
    mod = lambda b, i: (b, 0, 0)
    const2 = lambda b, i: (0, 0)
    act = pl.BlockSpec((1, tm, kw), row)
    in_specs = [act, act, act, act, act, act,
                pl.BlockSpec((1, tm, d), row),
                pl.BlockSpec((1, 1, d), mod), pl.BlockSpec((1, 1, d), mod), pl.BlockSpec((1, 1, d), mod),
                pl.BlockSpec((1, kw), const2), pl.BlockSpec((1, kw), const2),
                pl.BlockSpec(w_out.shape, const2), pl.BlockSpec((1, d), const2), pl.BlockSpec(rwt.shape, const2)]
    return pl.pallas_call(
        _out_body,
        grid=(bsz, t // tm),
        in_specs=in_specs,
        out_specs=[pl.BlockSpec((1, tm, d), row),
                   pl.BlockSpec((1, tm, d // 2), row),
                   pl.BlockSpec((1, ne, tm), lambda b, i: (b, 0, i))],
        out_shape=[jax.ShapeDtypeStruct((bsz, t, d), F32),
                   jax.ShapeDtypeStruct((bsz, t, d // 2), I32),
                   jax.ShapeDtypeStruct((bsz, ne, t), F32)],
        compiler_params=_params(("arbitrary", "arbitrary")),
        name="mixer_out",
    )(hg_f, hg_b, ml_f, ml_b, hgg, mlo, x, g1, sc2, sh2, hnw, mnw, w_out, n2w, rwt)


def _prefix_count(maskf, u_ref, ones_ref, bl_ref):
    e, nb, ln = maskf.shape
    x = maskf.reshape(e * nb, ln)
    xb = x.astype(BF16)
    incl = _dot(xb, u_ref[...])
    tot = _dot(xb, ones_ref[...])
    off = _dot(bl_ref[...], tot.astype(BF16))
    return (incl - x + off).reshape(e, nb, ln), off.reshape(e, nb, ln)


def _topk_body(aff_ref, u_ref, ones_ref, bl_ref, bl1_ref, pos_ref, base_ref, *, cap):
    x = aff_ref[0]
    ne = x.shape[0]

    def count(m):
        return jnp.sum(jnp.sum(jnp.where(m, 1.0, 0.0), axis=1, keepdims=True), axis=2, keepdims=True)

    def halve(_, carry):
        lo, hi = carry
        mid = 0.5 * (lo + hi)
        up = count(x > mid) >= cap
        return jnp.where(up, mid, lo), jnp.where(up, hi, mid)

    lo, hi = lax.fori_loop(0, TOPK_BISECTIONS, halve,
                           (jnp.full((ne, 1, 1), -1.0, F32), jnp.full((ne, 1, 1), 1.0, F32)))
    gt = jnp.where(x > hi, 1.0, 0.0)
    eq = jnp.where(x > lo, 1.0, 0.0) - gt
    need = cap - count(x > hi)
    eq_rank, _ = _prefix_count(eq, u_ref, ones_ref, bl_ref)
    sel = gt + eq * jnp.where(eq_rank < need, 1.0, 0.0)
    pos, _ = _prefix_count(sel, u_ref, ones_ref, bl_ref)
    pos_ref[0] = jnp.where(sel > 0, pos, -1.0).astype(I32)
    nsel = jnp.sum(sel, axis=0)
    nb16 = nsel.astype(BF16)
    incl = _dot(nb16, u_ref[...])
    tot = _dot(nb16, ones_ref[...])
    off = _dot(bl1_ref[...], tot, precision=HIGHEST)
    base_ref[0] = (incl - nsel + off).astype(I32)


def _topk(aff4, cap):
    bsz, ne, nb, ln = aff4.shape
    k = np.arange(ln)
    u = jnp.asarray(k[:, None] <= k[None, :], BF16)
    ones = jnp.ones((ln, ln), BF16)
    r = np.arange(ne * nb)
    bl = jnp.asarray(((r[:, None] // nb) == (r[None, :] // nb)) & (r[None, :] < r[:, None]), BF16)
    r1 = np.arange(nb)
    bl1 = jnp.asarray(r1[None, :] < r1[:, None], F32)
    blk = pl.BlockSpec((1, ne, nb, ln), lambda b: (b, 0, 0, 0))
    const = lambda a: pl.BlockSpec(a.shape, lambda b: (0, 0))
    return pl.pallas_call(
        functools.partial(_topk_body, cap=cap),
        grid=(bsz,),
        in_specs=[blk, const(u), const(ones), const(bl), const(bl1)],
        out_specs=[blk, pl.BlockSpec((1, nb, ln), lambda b: (b, 0, 0))],
        out_shape=[jax.ShapeDtypeStruct((bsz, ne, nb, ln), I32), jax.ShapeDtypeStruct((bsz, nb, ln), I32)],
        compiler_params=_params(("arbitrary",)),
        name="expert_topk",
    )(aff4, u, ones, bl, bl1)


def _sc_setup():
    info = plsc.get_sparse_core_info()
    mesh = plsc.VectorSubcoreMesh(core_axis_name="c", subcore_axis_name="s")
    params = dataclasses.replace(pltpu.CompilerParams(), needs_layout_passes=False)
    return info.num_cores, info.num_cores * info.num_subcores, info.num_lanes, mesh, params


def _sc_worker_id(nc):
    return lax.axis_index("s") * nc + lax.axis_index("c")


def _sc_gather_chunks(table_hbm, out_hbm, idx_v, ich_v, rows_v, sem, out_row0, n, lanes):
    chunk = SC_GATHER_ROWS

    @pl.loop(0, n // chunk)
    def _(c):
        for q in range(chunk // lanes):
            src = pl.ds(pl.multiple_of(c * chunk + q * lanes, lanes), lanes)
            ich_v[pl.ds(q * lanes, lanes)] = idx_v[src]
        pltpu.async_copy(table_hbm.at[ich_v], rows_v, sem).wait()
        pltpu.sync_copy(rows_v, out_hbm.at[pl.ds(pl.multiple_of(out_row0 + c * chunk, chunk), chunk)])


def _sc_invert_gather(table, pos_flat, aff_flat, npairs, ne, t, cap):
    nc, nw, lanes, mesh, params = _sc_setup()
    per_w = npairs // nw
    width = table.shape[1]
    chunk = SC_GATHER_ROWS

    @functools.partial(
        pl.kernel, mesh=mesh, compiler_params=params,
        out_type=[jax.ShapeDtypeStruct((npairs * cap, width), table.dtype),
                  jax.ShapeDtypeStruct((npairs * cap,), F32)],
        scratch_types=[pltpu.VMEM((t,), I32), pltpu.VMEM((t,), F32), pltpu.VMEM((cap,), I32), pltpu.VMEM((cap,), F32),
                       pltpu.VMEM((chunk,), I32), pltpu.VMEM((chunk, width), table.dtype), pltpu.SemaphoreType.DMA],
    )
    def body(table_hbm, pos_hbm, aff_hbm, out_hbm, gate_hbm, pos_v, aff_v, idx_v, gate_v, ich_v, rows_v, sem):
        wid = _sc_worker_id(nc)

        @pl.loop(0, per_w)
        def _(kk):
            p = wid * per_w + kk
            tok0 = (p // ne) * t
            pltpu.sync_copy(pos_hbm.at[pl.ds(pl.multiple_of(p * t, t), t)], pos_v)
            pltpu.sync_copy(aff_hbm.at[pl.ds(pl.multiple_of(p * t, t), t)], aff_v)

            @pl.loop(0, t // lanes)
            def _(i):
                v = pos_v[pl.ds(pl.multiple_of(i * lanes, lanes), lanes)]
                tok = lax.iota(I32, lanes) + i * lanes
                plsc.store_scatter(idx_v, [v], tok, mask=v >= 0)

            @pl.loop(0, cap // lanes)
            def _(j):
                sl = pl.ds(pl.multiple_of(j * lanes, lanes), lanes)
                ii = idx_v[sl]
                gate_v[sl] = plsc.load_gather(aff_v, [ii])
                idx_v[sl] = ii + tok0

            pltpu.sync_copy(gate_v, gate_hbm.at[pl.ds(pl.multiple_of(p * cap, cap), cap)])
            _sc_gather_chunks(table_hbm, out_hbm, idx_v, ich_v, rows_v, sem, p * cap, cap, lanes)

    return body(table, pos_flat, aff_flat)


def _sc_assignment_order(pos_flat, base_flat, bsz, ne, t, cap):
    nc, nw, lanes, mesh, params = _sc_setup()
    na = ne * cap
    per_b = nw // bsz
    rng = na // per_b

    @functools.partial(
        pl.kernel, mesh=mesh, compiler_params=params,
        out_type=[jax.ShapeDtypeStruct((bsz * na,), I32), jax.ShapeDtypeStruct((bsz * na,), I32)],
        scratch_types=[pltpu.VMEM((t,), I32), pltpu.VMEM((t,), I32), pltpu.VMEM((t,), I32),
                       pltpu.VMEM((rng,), I32), pltpu.VMEM((rng,), I32)],
    )
    def body(pos_hbm, base_hbm, perm_hbm, tok_hbm, pos_v, base_v, rank_v, perm_v, tok_v):
        wid = _sc_worker_id(nc)
        b = wid // per_b
        a0 = (wid % per_b) * rng
        pltpu.sync_copy(base_hbm.at[pl.ds(pl.multiple_of(b * t, t), t)], base_v)

        @pl.loop(0, t // lanes)
        def _(i):
            rank_v[pl.ds(pl.multiple_of(i * lanes, lanes), lanes)] = jnp.zeros((lanes,), I32)

        @pl.loop(0, ne)
        def _(e):
            p = b * ne + e
            pltpu.sync_copy(pos_hbm.at[pl.ds(pl.multiple_of(p * t, t), t)], pos_v)

            @pl.loop(0, t // lanes)
            def _(i):
                sl = pl.ds(pl.multiple_of(i * lanes, lanes), lanes)
                v = pos_v[sl]
                r = rank_v[sl]
                a = base_v[sl] + r - a0
                sel = v >= 0
                mine = sel & (a >= 0) & (a < rng)
                plsc.store_scatter(perm_v, [a], v + p * cap, mask=mine)
                plsc.store_scatter(tok_v, [a], lax.iota(I32, lanes) + i * lanes, mask=mine)
                rank_v[sl] = r + jnp.where(sel, 1, 0)

        dst = pl.ds(pl.multiple_of(b * na + a0, rng), rng)
        pltpu.sync_copy(perm_v, perm_hbm.at[dst])
        pltpu.sync_copy(tok_v, tok_hbm.at[dst])

    return body(pos_flat, base_flat)


def _sc_gather_rows(table, idx):
    nc, nw, lanes, mesh, params = _sc_setup()
    n = idx.shape[0]
    per_w = n // nw
    width = table.shape[1]
    chunk = SC_GATHER_ROWS

    @functools.partial(
        pl.kernel, mesh=mesh, compiler_params=params,
        out_type=jax.ShapeDtypeStruct((n, width), table.dtype),
        scratch_types=[pltpu.VMEM((per_w,), I32), pltpu.VMEM((chunk,), I32),
                       pltpu.VMEM((chunk, width), table.dtype), pltpu.SemaphoreType.DMA],
    )
    def body(table_hbm, idx_hbm, out_hbm, idx_v, ich_v, rows_v, sem):
        row0 = _sc_worker_id(nc) * per_w
        pltpu.sync_copy(idx_hbm.at[pl.ds(pl.multiple_of(row0, per_w), per_w)], idx_v)
        _sc_gather_chunks(table_hbm, out_hbm, idx_v, ich_v, rows_v, sem, row0, per_w, lanes)

    return body(table, idx)


def _moe1_body(xs_ref, gate_ref, wg_ref, wu_ref, wd_ref, y_ref):
    xs = _unpack_halves(xs_ref[0])
    hg = _dot(xs, wg_ref[0].astype(BF16))
    hu = _dot(xs, wu_ref[0].astype(BF16))
    h = (_silu(hg) * hu).astype(BF16)
    y = _dot(h, wd_ref[0].astype(BF16))
    gt = gate_ref[0].T
    y = jnp.concatenate([y[k * 128:(k + 1) * 128] * gt[:, k:k + 1] for k in range(gt.shape[1])], axis=0)
    y_ref[0] = _pack_halves(y)


def _moe_experts(xs, gate, wg, wu, wd, bsz):
    npairs, cap, half = xs.shape
    ne, d, f = wg.shape
    pair = lambda b, e: (b * ne + e, 0, 0)
    expert = lambda b, e: (e, 0, 0)
    return pl.pallas_call(
        _moe1_body,
        grid=(bsz, ne),
        in_specs=[pl.BlockSpec((1, cap, half), pair),
                  pl.BlockSpec((1, cap // 128, 128), pair),
                  pl.BlockSpec((1, d, f), expert),
                  pl.BlockSpec((1, d, f), expert),
                  pl.BlockSpec((1, f, d), expert)],
        out_specs=pl.BlockSpec((1, cap, half), pair),
        out_shape=jax.ShapeDtypeStruct((npairs, cap, half), I32),
        compiler_params=_params(("arbitrary", "arbitrary")),
        name="moe_experts",
    )(xs, gate, wg, wu, wd)


def _combine_body(abase_ref, ys_hbm, tok_ref, x1_ref, g2_ref, fw_ref, o_ref, buf_ref, acc_ref, sem_ref, cnt_ref, *, na):
    n = ONEHOT_BLK
    ring = buf_ref.shape[0]
    nch = na // n
    b, i = pl.program_id(0), pl.program_id(1)

    @pl.when(i == 0)
    def _():
        cnt_ref[0] = 0
        cnt_ref[1] = 0

    lo = abase_ref[b, i]
    hi = abase_ref[b, i + 1]
    c0 = lo // n
    c1 = jnp.where(hi > lo, (hi + n - 1) // n, c0)

    def copy(c):
        slot = c % ring
        row0 = pl.multiple_of(b * na + c * n, n)
        return pltpu.make_async_copy(ys_hbm.at[pl.ds(row0, n)], buf_ref.at[slot], sem_ref.at[slot])

    acc_ref[...] = jnp.zeros(acc_ref.shape, F32)
    tok_ids = lax.broadcasted_iota(I32, (n, n), 0) + i * n

    def step(c, carry):
        started = cnt_ref[0]
        ahead = jnp.minimum(c + ring, nch)

        def start(k, _):
            copy(k).start()
            return 0

        lax.fori_loop(started, ahead, start, 0)
        cnt_ref[0] = jnp.maximum(started, ahead)

        @pl.when(cnt_ref[1] <= c)
        def _():
            copy(c).wait()
            cnt_ref[1] = c + 1

        seg = jnp.where(tok_ids == tok_ref[0, pl.ds(c, 1), :], 1.0, 0.0).astype(BF16)
        acc_ref[...] += _dot(seg, _unpack_halves(buf_ref[c % ring]))
        return carry

    lax.fori_loop(c0, c1, step, 0)
    x2 = x1_ref[0] + g2_ref[0] * acc_ref[...]
    o_ref[0] = x2 * lax.rsqrt(jnp.mean(x2 * x2, axis=-1, keepdims=True) + NORM_EPS) * fw_ref[...]


def _combine_final(abase, ys, tok, x1, g2, fw, na):
    bsz, t, d = x1.shape
    n = ONEHOT_BLK
    grid_spec = pltpu.PrefetchScalarGridSpec(
        num_scalar_prefetch=1,
        grid=(bsz, t // n),
        in_specs=[pl.BlockSpec(memory_space=pl.ANY),
                  pl.BlockSpec((1, na // n, n), lambda b, i, a: (b, 0, 0)),
                  pl.BlockSpec((1, n, d), lambda b, i, a: (b, i, 0)),
                  pl.BlockSpec((1, 1, d), lambda b, i, a: (b, 0, 0)),
                  pl.BlockSpec((1, d), lambda b, i, a: (0, 0))],
        out_specs=pl.BlockSpec((1, n, d), lambda b, i, a: (b, i, 0)),
        scratch_shapes=[pltpu.VMEM((COMBINE_RING, n, d // 2), I32), pltpu.VMEM((n, d), F32),
                        pltpu.SemaphoreType.DMA((COMBINE_RING,)), pltpu.SMEM((2,), I32)],
    )
    return pl.pallas_call(
        functools.partial(_combine_body, na=na),
        grid_spec=grid_spec,
        out_shape=jax.ShapeDtypeStruct((bsz, t, d), F32),
        compiler_params=_params(("arbitrary", "arbitrary")),
        name="combine_final",
    )(abase, ys, tok, x1, g2, fw)


def kernel(x, c, ctx, c_ctx, ada_w, ada_b, norm1_w, w_in, conv_w, conv_b, hg_lb_logits, ml_gate_b,
           hg_norm_w, ml_norm_w, w_out, norm2_w, router_w, exp_w_gate, exp_w_up, exp_w_down, final_norm_w):
    bsz, t, d = x.shape
    nctx = ctx.shape[1]
    assert ada_w.shape[0] == 1, "single-layer block"
    assert nctx == SCAN_STEP and t % SCAN_STEP == 0 and bsz + 1 <= 8
    tall = t + nctx
    nx = t // SCAN_STEP
    kw = HEADS_W
    ne = router_w.shape[-1]
    cap = EC_CAPACITY * t // ne
    tm = min(512, t)

    rows = jnp.concatenate([c, c_ctx[None], jnp.zeros((7 - bsz, d), F32)], axis=0)
    mod = _modulation(rows, ada_w[0], ada_b[0][None])
    mx = [m[:, None, :] for m in jnp.split(mod[:bsz], 6, axis=-1)]
    mc = [m[:, None, :] for m in jnp.split(mod[bsz:bsz + 1], 6, axis=-1)]
    sh1, sc1, g1, sh2, sc2, g2 = mx
    csh1, csc1 = mc[0], mc[1]

    main_w = 9 * kw
    w_main = w_in[0][:, :main_w].astype(BF16)
    w_gt = w_in[0][:, main_w:].T
    gate_b = ml_gate_b[0][:, None]
    nw1 = norm1_w[0][None]
    outs = _inproj(x, sc1, sh1, nw1, w_main, w_gt, gate_b, hg_lb_logits, tall, tm, 0)
    outs = _inproj(ctx, csc1, csh1, nw1, w_main, w_gt, gate_b, hg_lb_logits, tall, nctx, t // nctx, prev=outs)
    hgq, hgv, hgg, hgk, hglf, mlqk_pre, mlv, mlo, gates = outs

    mlqk = _conv(mlqk_pre, conv_w[0], conv_b[0][None], t, nctx)
    hg_f, hg_b = _hgrn2(hgq, hgk, hgv, hglf, nx)
    ml_f, ml_b = _mlstm(mlqk, mlv, gates, nx)

    x1, vpk, aff = _mixer_out(hg_f, hg_b, ml_f, ml_b, hgg, mlo, x, g1, sc2, sh2,
                             hg_norm_w[0][None], ml_norm_w[0][None], w_out[0].astype(BF16),
                             norm2_w[0][None], router_w[0].T, tm)

    pos4, base = _topk(aff.reshape(bsz, ne, t // 128, 128), cap)
    pos_flat = pos4.reshape(-1)
    xs, gate = _sc_invert_gather(vpk.reshape(bsz * t, d // 2), pos_flat, aff.reshape(-1), bsz * ne, ne, t, cap)
    perm, tok = _sc_assignment_order(pos_flat, base.reshape(-1), bsz, ne, t, cap)
    y = _moe_experts(xs.reshape(bsz * ne, cap, d // 2), gate.reshape(bsz * ne, cap // 128, 128),
                     exp_w_gate[0], exp_w_up[0], exp_w_down[0], bsz)
    na = ne * cap
    ys = _sc_gather_rows(y.reshape(bsz * na, d // 2), perm)
    abase = jnp.concatenate([base.reshape(bsz, t)[:, ::ONEHOT_BLK], jnp.full((bsz, 1), na, I32)], axis=1)
    return _combine_final(abase, ys, tok.reshape(bsz, na // ONEHOT_BLK, ONEHOT_BLK), x1, g2, final_norm_w[None], na)
```

```python
import dataclasses
import functools

import numpy as np
import jax
import jax.numpy as jnp
from jax import lax
from jax.experimental import pallas as pl
from jax.experimental.pallas import tpu as pltpu
from jax.experimental.pallas import tpu_sc as plsc

F32 = jnp.float32
BF16 = jnp.bfloat16
I32 = jnp.int32
HIGHEST = lax.Precision.HIGHEST
NORM_EPS = 1e-6

HEAD_DIM = 128
N_HEADS = 4
HEADS_W = N_HEADS * HEAD_DIM
GRID_W = 64
N_EXPERTS = 16
EC_CAPACITY = 2
HG_CHUNK = 64
ML_CHUNK = 128
SCAN_STEP = 256
ONEHOT_BLK = 256
CONV_HALO = 72
SC_GATHER_ROWS = 64
COMBINE_RING = 6
TOPK_BISECTIONS = 64
VMEM_LIMIT = 56 * 1024 * 1024

_NT = (((1,), (1,)), ((), ()))
_TN = (((0,), (0,)), ((), ()))


def _dot(a, b, dims=None, precision=None):
    if dims is None:
        return jnp.dot(a, b, preferred_element_type=F32, precision=precision)
    return lax.dot_general(a, b, dims, preferred_element_type=F32, precision=precision)


def _sigmoid(x):
    return jax.nn.sigmoid(x)


def _pack_halves(x):
    w = x.shape[-1] // 2
    bits = lax.bitcast_convert_type(x.astype(BF16).astype(F32), I32)
    return lax.shift_right_logical(bits[:, :w], 16) | bits[:, w:]


def _unpack_halves(p):
    lo = lax.bitcast_convert_type(lax.shift_left(p, 16), F32)
    hi = lax.bitcast_convert_type(p & jnp.int32(-65536), F32)
    return jnp.concatenate([lo, hi], axis=-1).astype(BF16)


def _silu(x):
    return x * jax.nn.sigmoid(x)


def _params(sem, flags=None):
    return pltpu.CompilerParams(dimension_semantics=sem, vmem_limit_bytes=VMEM_LIMIT, flags=flags)


def _mod_body(r_ref, w_ref, b_ref, o_ref):
    r = r_ref[...]
    o_ref[...] = _dot(_silu(r), w_ref[...], precision=HIGHEST) + b_ref[...]


def _modulation(rows, w, b):
    d, n = w.shape
    tn = n // 4
    return pl.pallas_call(
        _mod_body,
        grid=(n // tn,),
        in_specs=[pl.BlockSpec((8, d), lambda j: (0, 0)),
                  pl.BlockSpec((d, tn), lambda j: (0, j)),
                  pl.BlockSpec((1, tn), lambda j: (0, j))],
        out_specs=pl.BlockSpec((8, tn), lambda j: (0, j)),
        out_shape=jax.ShapeDtypeStruct((8, n), F32),
        compiler_params=_params(("arbitrary",)),
        name="modulation",
    )(rows, w, b)


def _log_sigmoid(x):
    return jnp.minimum(x, 0.0) - jnp.log(1.0 + jnp.exp(-jnp.abs(x)))


def _inproj_body(x_ref, sc_ref, sh_ref, nw_ref, w_ref, wg_ref, gb_ref, lbl_ref, *refs):
    hgq_ref, hgv_ref, hgg_ref, hgk_ref, hglf_ref, mlqk_ref, mlv_ref, mlo_ref, gates_ref = refs[-9:]
    kw = HEADS_W
    x = x_ref[0]
    y = x * lax.rsqrt(jnp.mean(x * x, axis=-1, keepdims=True) + NORM_EPS) * nw_ref[...]
    u = y * (1.0 + sc_ref[0]) + sh_ref[0]
    ub = u.astype(BF16)

    def proj(c0, c1):
        return _dot(ub, w_ref[:, c0:c1])

    hgq_ref[0] = _silu(proj(0, kw)).astype(BF16)
    hgv_ref[0] = proj(kw, 2 * kw).astype(BF16)
    hgg_ref[0] = _silu(proj(2 * kw, 3 * kw)).astype(BF16)

    lbl = lbl_ref[...]
    mx = jnp.max(lbl, axis=0)
    ex = jnp.exp(lbl - mx[None])
    lb = ex[0] / jnp.sum(ex, axis=0)
    for d in range(2):
        p = proj((3 + d) * kw, (4 + d) * kw)
        lbd = lb[d:d + 1]
        f = lbd + (1.0 - lbd) * _sigmoid(p)
        hgk_ref[0, :, d * kw:(d + 1) * kw] = (1.0 - f).astype(BF16)
        hglf_ref[0, :, d * kw:(d + 1) * kw] = jnp.log(f)

    mlqk_ref[0, :, 0:kw] = proj(5 * kw, 6 * kw)
    mlqk_ref[0, :, kw:2 * kw] = proj(6 * kw, 7 * kw)
    mlv_ref[0] = proj(7 * kw, 8 * kw).astype(BF16)
    mlo_ref[0] = _sigmoid(proj(8 * kw, 9 * kw)).astype(BF16)

    g = _dot(wg_ref[...], u, _NT, precision=HIGHEST) + gb_ref[...]
    row = lax.broadcasted_iota(I32, g.shape, 0)
    gates_ref[0] = jnp.where((row % 8) >= N_HEADS, _log_sigmoid(g), g)


def _inproj_shapes(bsz, tall):
    kw = HEADS_W
    return [
        jax.ShapeDtypeStruct((bsz, tall, kw), BF16),
        jax.ShapeDtypeStruct((bsz, tall, kw), BF16),
        jax.ShapeDtypeStruct((bsz, tall, kw), BF16),
        jax.ShapeDtypeStruct((bsz, tall, 2 * kw), BF16),
        jax.ShapeDtypeStruct((bsz, tall, 2 * kw), F32),
        jax.ShapeDtypeStruct((bsz, tall, 2 * kw), F32),
        jax.ShapeDtypeStruct((bsz, tall, kw), BF16),
        jax.ShapeDtypeStruct((bsz, tall, kw), BF16),
        jax.ShapeDtypeStruct((bsz, 4 * N_HEADS, tall), F32),
    ]


def _inproj(tokens, scale, shift, nw, w_main, w_gt, gate_b, lb_logits, tall, tm, blk0, prev=None):
    bsz, n, d = tokens.shape
    kw = HEADS_W
    nt = n // tm
    per_sample = scale.shape[0] == bsz
    mod_map = (lambda b, i: (b, 0, 0)) if per_sample else (lambda b, i: (0, 0, 0))
    const2 = lambda b, i: (0, 0)
    in_specs = [
        pl.BlockSpec((1, tm, d), lambda b, i: (b, i, 0)),
        pl.BlockSpec((1, 1, d), mod_map),
        pl.BlockSpec((1, 1, d), mod_map),
        pl.BlockSpec((1, d), const2),
        pl.BlockSpec(w_main.shape, const2),
        pl.BlockSpec(w_gt.shape, const2),
        pl.BlockSpec(gate_b.shape, const2),
        pl.BlockSpec(lb_logits.shape, lambda b, i: (0, 0, 0)),
    ]
    args = [tokens, scale, shift, nw, w_main, w_gt, gate_b, lb_logits]
    aliases = {}
    if prev is not None:
        for k, a in enumerate(prev):
            in_specs.append(pl.BlockSpec(memory_space=pl.ANY))
            aliases[len(args)] = k
            args.append(a)
    row_map = lambda b, i: (b, blk0 + i, 0)
    widths = [kw, kw, kw, 2 * kw, 2 * kw, 2 * kw, kw, kw]
    out_specs = [pl.BlockSpec((1, tm, w), row_map) for w in widths]
    out_specs.append(pl.BlockSpec((1, 4 * N_HEADS, tm), lambda b, i: (b, 0, blk0 + i)))
    return pl.pallas_call(
        _inproj_body,
        grid=(bsz, nt),
        in_specs=in_specs,
        out_specs=out_specs,
        out_shape=_inproj_shapes(bsz, tall),
        input_output_aliases=aliases,
        compiler_params=_params(("arbitrary", "arbitrary")),
        name="inproj_ctx" if prev is not None else "inproj_x",
    )(*args)


def _conv_body(x_ref, w_ref, b_ref, o_ref, pad_ref, cpad_ref, *, t, nctx, scale_from):
    halo = CONV_HALO
    rows = 512
    win = rows + 2 * halo
    ch = x_ref.shape[-1]
    scale = jnp.where(pl.program_id(1) >= scale_from, HEAD_DIM ** -0.5, 1.0).astype(F32)
    w = w_ref[...]
    bias = b_ref[...]

    pad_ref[0:halo, :] = jnp.zeros((halo, ch), F32)
    pad_ref[halo + t:halo + t + halo, :] = jnp.zeros((halo, ch), F32)
    pad_ref[halo:halo + t, :] = x_ref[0, 0:t, :]
    col = (lax.broadcasted_iota(I32, (win, ch), 0) + (GRID_W - halo % GRID_W)) % GRID_W
    left_ok = col > 0
    right_ok = col < GRID_W - 1

    def chunk(c, carry):
        o = pl.multiple_of(c * rows, rows)
        xw = pad_ref[pl.ds(o, win), :]
        xm = jnp.where(left_ok, pltpu.roll(xw, 1, 0), 0.0)
        xp = jnp.where(right_ok, pltpu.roll(xw, win - 1, 0), 0.0)
        z = [xm * w[dr, 0:1] + xw * w[dr, 1:2] + xp * w[dr, 2:3] for dr in range(3)]
        y = (z[1][halo:halo + rows]
             + z[0][halo - GRID_W:halo - GRID_W + rows]
             + z[2][halo + GRID_W:halo + GRID_W + rows])
        o_ref[0, pl.ds(o, rows), :] = (_silu(y + bias) * scale).astype(o_ref.dtype)
        return carry

    lax.fori_loop(0, t // rows, chunk, 0)

    cpad_ref[0:8, :] = jnp.zeros((8, ch), F32)
    cpad_ref[8 + nctx:16 + nctx, :] = jnp.zeros((8, ch), F32)
    cpad_ref[8:8 + nctx, :] = x_ref[0, t:t + nctx, :]
    xw = cpad_ref[...]
    n = nctx + 16
    y = (pltpu.roll(xw, 1, 0) * w[1, 0:1] + xw * w[1, 1:2] + pltpu.roll(xw, n - 1, 0) * w[1, 2:3])[8:8 + nctx]
    o_ref[0, t:t + nctx, :] = (_silu(y + bias) * scale).astype(o_ref.dtype)


def _conv(qk_pre, conv_w, conv_b, t, nctx):
    bsz, tall, c = qk_pre.shape
    ch = 128
    body = functools.partial(_conv_body, t=t, nctx=nctx, scale_from=(c // 2) // ch)
    return pl.pallas_call(
        body,
        grid=(bsz, c // ch),
        in_specs=[pl.BlockSpec((1, tall, ch), lambda b, j: (b, 0, j)),
                  pl.BlockSpec((3, 3, ch), lambda b, j: (0, 0, j)),
                  pl.BlockSpec((1, ch), lambda b, j: (0, j))],
        out_specs=pl.BlockSpec((1, tall, ch), lambda b, j: (b, 0, j)),
        out_shape=jax.ShapeDtypeStruct((bsz, tall, c), BF16),
        scratch_shapes=[pltpu.VMEM((t + 2 * CONV_HALO, ch), F32),
                        pltpu.VMEM((nctx + 16, ch), F32)],
        compiler_params=_params(("arbitrary", "arbitrary")),
        name="qk_conv",
    )(qk_pre, conv_w, conv_b)


def _fwd_blk(s, nx):
    return jnp.where(s == 0, nx, s - 1)


def _bwd_blk(s, nx):
    return jnp.where(s == 0, nx, nx - s)


def _hg_constants(rev):
    c = HG_CHUNK
    i = np.arange(c)[:, None]
    j = np.arange(c)[None, :]
    blocks = [(j >= i) if rev else (j <= i), (j < i) if rev else (j > i)]
    masks = [i == j]
    m = c // 2
    while m >= 1:
        b0 = (i // (2 * m)) * (2 * m)
        same = (i // (2 * m)) == (j // (2 * m))
        if rev:
            beta = b0 + m
            qrow = (i % (2 * m)) < m
            g = np.where(qrow, (j >= i) & (j < beta), (j >= beta) & (j < i))
            mask = same & qrow & ((j % (2 * m)) >= m)
        else:
            beta = b0 + m - 1
            qrow = (i % (2 * m)) >= m
            g = np.where(qrow, (j > beta) & (j <= i), (j > i) & (j <= beta))
            mask = same & qrow & ((j % (2 * m)) < m)
        blocks.append(g)
        masks.append(mask)
        m //= 2
    g = np.concatenate(blocks, axis=0).astype(np.float32)
    g3 = np.concatenate([g, g, g], axis=1)
    m2 = np.concatenate([np.stack(masks), np.stack(masks)], axis=2)
    return (jnp.asarray(g3, BF16), jnp.asarray(m2, F32))


def _block_diag(x, zero):
    w = x.shape[1] // 2
    return jnp.concatenate([jnp.concatenate([x[:, :w], zero], axis=1),
                            jnp.concatenate([zero, x[:, w:]], axis=1)], axis=0)


def _hg_chunk(dirs):
    c = HG_CHUNK
    w = 2 * HEAD_DIM
    zero = jnp.zeros((c, HEAD_DIM), BF16)
    units = []
    for rev, r0, q_ref, k_ref, v_ref, lf_ref, g_ref, msk_ref, o_ref, st_ref in dirs:
        rows = pl.ds(r0, c)
        lf = lf_ref[0, rows, :]
        p1 = lf.astype(BF16)
        r1 = lf - p1.astype(F32)
        p2 = r1.astype(BF16)
        p3 = (r1 - p2.astype(F32)).astype(BF16)
        dall = _dot(g_ref[...], jnp.concatenate([p1, p2, p3], axis=0))
        for hp in range(N_HEADS // 2):
            cs = slice(hp * w, (hp + 1) * w)
            units.append(dict(rev=rev, rows=rows, cs=cs, hp=hp, msk_ref=msk_ref, o_ref=o_ref, st_ref=st_ref,
                              q=q_ref[0, rows, cs], k=k_ref[0, rows, cs], v=v_ref[0, rows, cs], dall=dall[:, cs]))
    for u in units:
        msk_ref = u["msk_ref"]
        nlev = msk_ref.shape[0] - 1
        att = _dot(u["q"], _block_diag(u["k"], zero), _NT) * msk_ref[0]
        for l in range(nlev):
            e = jnp.exp(u["dall"][(l + 2) * c:(l + 3) * c]).astype(BF16)
            att = att + _dot(u["q"] * e, _block_diag(u["k"] * e, zero), _NT) * msk_ref[l + 1]
        u["att"] = att.astype(BF16)
    for u in units:
        a = u["dall"][0:c]
        u["a_tot"] = a[0:1] if u["rev"] else a[c - 1:c]
        u["st"] = [u["st_ref"][2 * u["hp"] + i] for i in range(2)]
        zf = jnp.zeros((HEAD_DIM, HEAD_DIM), BF16)
        st2 = jnp.concatenate([jnp.concatenate([u["st"][0].astype(BF16), zf], axis=1),
                               jnp.concatenate([zf, u["st"][1].astype(BF16)], axis=1)], axis=0)
        qbar = (u["q"].astype(F32) * jnp.exp(a)).astype(BF16)
        u["o"] = _dot(u["att"], _block_diag(u["v"], zero)) + _dot(qbar, st2, _NT)
        u["khat"] = (u["k"].astype(F32) * jnp.exp(u["dall"][c:2 * c])).astype(BF16)
    for u in units:
        u["o_ref"][0, u["rows"], u["cs"]] = u["o"]
        for i in range(2):
            hs = slice(i * HEAD_DIM, (i + 1) * HEAD_DIM)
            upd = _dot(u["v"][:, hs], u["khat"][:, hs], _TN)
            u["st_ref"][2 * u["hp"] + i] = u["st"][i] * jnp.exp(u["a_tot"][:, hs]) + upd


def _hg_body(qf_ref, kf_ref, vf_ref, lff_ref, qb_ref, kb_ref, vb_ref, lfb_ref,
             gf_ref, mf_ref, gb_ref, mb_ref, of_ref, ob_ref, st_ref):
    @pl.when(pl.program_id(1) == 0)
    def _():
        st_ref[...] = jnp.zeros(st_ref.shape, F32)

    nsub = SCAN_STEP // HG_CHUNK
    for c in range(nsub):
        _hg_chunk([
            (False, c * HG_CHUNK, qf_ref, kf_ref, vf_ref, lff_ref, gf_ref, mf_ref, of_ref, st_ref.at[0]),
            (True, (nsub - 1 - c) * HG_CHUNK, qb_ref, kb_ref, vb_ref, lfb_ref, gb_ref, mb_ref, ob_ref, st_ref.at[1]),
        ])


def _hgrn2(hgq, hgk, hgv, hglf, nx):
    bsz, tall, kw = hgq.shape
    steps = tall // SCAN_STEP
    cf = _hg_constants(False)
    cb = _hg_constants(True)
    blk = (1, SCAN_STEP, kw)
    fwd = lambda col: (lambda b, s: (b, _fwd_blk(s, nx), col))
    bwd = lambda col: (lambda b, s: (b, _bwd_blk(s, nx), col))
    const = lambda a: pl.BlockSpec(a.shape, lambda b, s: (0,) * a.ndim)
    in_specs = [pl.BlockSpec(blk, fwd(0)), pl.BlockSpec(blk, fwd(0)), pl.BlockSpec(blk, fwd(0)), pl.BlockSpec(blk, fwd(0)),
                pl.BlockSpec(blk, bwd(0)), pl.BlockSpec(blk, bwd(1)), pl.BlockSpec(blk, bwd(0)), pl.BlockSpec(blk, bwd(1))]
    in_specs += [const(a) for a in cf + cb]
    out_sds = jax.ShapeDtypeStruct((bsz, tall, kw), F32)
    return pl.pallas_call(
        _hg_body,
        grid=(bsz, steps),
        in_specs=in_specs,
        out_specs=[pl.BlockSpec(blk, fwd(0)), pl.BlockSpec(blk, bwd(0))],
        out_shape=[out_sds, out_sds],
        scratch_shapes=[pltpu.VMEM((2, N_HEADS, HEAD_DIM, HEAD_DIM), F32)],
        compiler_params=_params(("arbitrary", "arbitrary")),
        name="hgrn2_scan",
    )(hgq, hgk, hgv, hglf, hgq, hgk, hgv, hglf, *cf, *cb)


def _ml_constants(rev):
    k = np.arange(ML_CHUNK)
    tri = (k[:, None] >= k[None, :]) if rev else (k[:, None] <= k[None, :])
    return jnp.asarray(np.concatenate([tri, tri, tri], axis=0), BF16)


def _ml_chunk(dirs):
    c = ML_CHUNK
    ii = lax.broadcasted_iota(I32, (c, c), 0)
    jj = lax.broadcasted_iota(I32, (c, c), 1)
    ones = jnp.ones((c, HEAD_DIM), BF16)
    units = []
    for rev, r0, d, q_ref, k_ref, v_ref, g_ref, tri3_ref, o_ref, st_ref, m_ref in dirs:
        gates = g_ref[0, :, pl.ds(r0, c)]
        p1 = gates.astype(BF16)
        r1 = gates - p1.astype(F32)
        p2 = r1.astype(BF16)
        p3 = (r1 - p2.astype(F32)).astype(BF16)
        csum = _dot(jnp.concatenate([p1, p2, p3], axis=1), tri3_ref[...])
        for h in range(N_HEADS):
            cs = slice(h * HEAD_DIM, (h + 1) * HEAD_DIM)
            u = dict(rev=rev, o_ref=o_ref, rows=pl.ds(r0, c), cs=cs, st_ref=st_ref, m_ref=m_ref, h=h)
            u["qb"] = q_ref[0, pl.ds(r0, c), cs]
            u["kb"] = k_ref[0, pl.ds(r0, c), cs]
            u["v1"] = jnp.concatenate([v_ref[0, pl.ds(r0, c), cs], ones], axis=1)
            irow = gates[d * 8 + h:d * 8 + h + 1]
            u["brow"] = csum[d * 8 + N_HEADS + h:d * 8 + N_HEADS + h + 1]
            u["rrow"] = irow - u["brow"]
            units.append(u)
    for u in units:
        u["st"] = u["st_ref"][u["h"]]
        a = _dot(jnp.concatenate([u["kb"], u["st"].astype(BF16)], axis=0), u["qb"], _NT)
        u["s"] = a[:c]
        u["sq"] = a[c:]
    for u in units:
        last = 0 if u["rev"] else c - 1
        tri_t = (ii >= jj) if u["rev"] else (ii <= jj)
        rcol = jnp.concatenate([u["rrow"], jnp.zeros((7, c), F32)], axis=0).T[:, 0:1]
        u["mprev"] = u["m_ref"][u["h"]][:, 0:1]
        rmat = jnp.where(tri_t, rcol, -jnp.inf)
        u["grow"] = jnp.maximum(jnp.max(rmat, axis=0, keepdims=True), u["mprev"])
        qk = (u["s"] * jnp.exp(rmat - u["grow"])).astype(BF16)
        blast = u["brow"][:, last:last + 1]
        u["mnew"] = blast + u["grow"][:, last:last + 1]
        kh = (u["kb"].astype(F32) * jnp.exp(blast + rcol - u["mnew"])).astype(BF16)
        u["ws"] = jnp.exp(blast + u["mprev"] - u["mnew"])
        u["qkh"] = jnp.concatenate([qk, kh], axis=1)
    for u in units:
        u["nu"] = _dot(u["v1"], u["qkh"], _TN)
    for u in units:
        both = u["nu"][:, :c] + jnp.exp(u["mprev"] - u["grow"]) * u["sq"]
        den = both[HEAD_DIM:HEAD_DIM + 1]
        inv = 1.0 / jnp.maximum(jnp.abs(den), jnp.exp(-(u["brow"] + u["grow"])))
        u["o_ref"][0, u["rows"], u["cs"]] = (both[:HEAD_DIM] * inv).T
        u["st_ref"][u["h"]] = u["ws"] * u["st"] + u["nu"][:, c:]
        u["m_ref"][u["h"]] = jnp.broadcast_to(u["mnew"], (1, HEAD_DIM))


def _ml_body(qf_ref, kf_ref, vf_ref, gf_ref, qb_ref, kb_ref, vb_ref, gb_ref, tf_ref, tb_ref,
             of_ref, ob_ref, st_ref, m_ref):
    @pl.when(pl.program_id(1) == 0)
    def _():
        st_ref[...] = jnp.zeros(st_ref.shape, F32)
        m_ref[...] = jnp.zeros(m_ref.shape, F32)

    nsub = SCAN_STEP // ML_CHUNK
    for c in range(nsub):
        _ml_chunk([
            (False, c * ML_CHUNK, 0, qf_ref, kf_ref, vf_ref, gf_ref, tf_ref, of_ref, st_ref.at[0], m_ref.at[0]),
            (True, (nsub - 1 - c) * ML_CHUNK, 1, qb_ref, kb_ref, vb_ref, gb_ref, tb_ref, ob_ref, st_ref.at[1], m_ref.at[1]),
        ])


def _mlstm(mlqk, mlv, gates, nx):
    bsz, tall, kw = mlv.shape
    steps = tall // SCAN_STEP
    blk = (1, SCAN_STEP, kw)
    gblk = (1, 4 * N_HEADS, SCAN_STEP)
    tf, tb = _ml_constants(False), _ml_constants(True)
    fwd = lambda col: (lambda b, s: (b, _fwd_blk(s, nx), col))
    bwd = lambda col: (lambda b, s: (b, _bwd_blk(s, nx), col))
    const = pl.BlockSpec(tf.shape, lambda b, s: (0, 0))
    in_specs = [pl.BlockSpec(blk, fwd(0)), pl.BlockSpec(blk, fwd(1)), pl.BlockSpec(blk, fwd(0)),
                pl.BlockSpec(gblk, lambda b, s: (b, 0, _fwd_blk(s, nx))),
                pl.BlockSpec(blk, bwd(0)), pl.BlockSpec(blk, bwd(1)), pl.BlockSpec(blk, bwd(0)),
                pl.BlockSpec(gblk, lambda b, s: (b, 0, _bwd_blk(s, nx))), const, const]
    out_sds = jax.ShapeDtypeStruct((bsz, tall, kw), F32)
    return pl.pallas_call(
        _ml_body,
        grid=(bsz, steps),
        in_specs=in_specs,
        out_specs=[pl.BlockSpec(blk, fwd(0)), pl.BlockSpec(blk, bwd(0))],
        out_shape=[out_sds, out_sds],
        scratch_shapes=[pltpu.VMEM((2, N_HEADS, 2 * HEAD_DIM, HEAD_DIM), F32),
                        pltpu.VMEM((2, N_HEADS, 1, HEAD_DIM), F32)],
        compiler_params=_params(("arbitrary", "arbitrary")),
        name="mlstm_scan",
    )(mlqk, mlqk, mlv, gates, mlqk, mlqk, mlv, gates, tf, tb)


def _out_body(hof_ref, hob_ref, mhf_ref, mhb_ref, hgg_ref, mlo_ref, x_ref, g1_ref, sc2_ref, sh2_ref,
              hnw_ref, mnw_ref, wout_ref, n2w_ref, rwt_ref, x1_ref, vt_ref, aff_ref):
    hg = hof_ref[0] + hob_ref[0]
    ml = mhf_ref[0] + mhb_ref[0]
    hparts, mparts = [], []
    for h in range(N_HEADS):
        cs = slice(h * HEAD_DIM, (h + 1) * HEAD_DIM)
        t = hg[:, cs]
        hparts.append(t * lax.rsqrt(jnp.mean(t * t, axis=-1, keepdims=True) + NORM_EPS))
        t = ml[:, cs]
        t = t - jnp.mean(t, axis=-1, keepdims=True)
        mparts.append(t * lax.rsqrt(jnp.mean(t * t, axis=-1, keepdims=True) + NORM_EPS))
    hgn = jnp.concatenate(hparts, axis=-1) * hnw_ref[...] * hgg_ref[0].astype(F32)
    mln = jnp.concatenate(mparts, axis=-1) * mnw_ref[...] * mlo_ref[0].astype(F32)
    mix = jnp.concatenate([hgn, mln], axis=-1).astype(BF16)
    x1 = x_ref[0] + g1_ref[0] * _dot(mix, wout_ref[...])
    x1_ref[0] = x1
    v = x1 * lax.rsqrt(jnp.mean(x1 * x1, axis=-1, keepdims=True) + NORM_EPS) * n2w_ref[...]
    v = v * (1.0 + sc2_ref[0]) + sh2_ref[0]
    vt_ref[0] = _pack_halves(v)
    logits = _dot(rwt_ref[...], v, _NT, precision=HIGHEST)
    ex = jnp.exp(logits - jnp.max(logits, axis=0, keepdims=True))
    aff_ref[0] = ex / jnp.sum(ex, axis=0, keepdims=True)


def _mixer_out(hg_f, hg_b, ml_f, ml_b, hgg, mlo, x, g1, sc2, sh2, hnw, mnw, w_out, n2w, rwt, tm):
    bsz, t, d = x.shape
    kw = HEADS_W
    ne = rwt.shape[0]
    row = lambda b, i: (b, i, 0)
    mod = lambda b, i: (b, 0, 0)
    const2 = lambda b, i: (0, 0)
    act = pl.BlockSpec((1, tm, kw), row)
    in_specs = [act, act, act, act, act, act,
                pl.BlockSpec((1, tm, d), row),
                pl.BlockSpec((1, 1, d), mod), pl.BlockSpec((1, 1, d), mod), pl.BlockSpec((1, 1, d), mod),
                pl.BlockSpec((1, kw), const2), pl.BlockSpec((1, kw), const2),
                pl.BlockSpec(w_out.shape, const2), pl.BlockSpec((1, d), const2), pl.BlockSpec(rwt.shape, const2)]
    return pl.pallas_call(
        _out_body,
        grid=(bsz, t // tm),
        in_specs=in_specs,
        out_specs=[pl.BlockSpec((1, tm, d), row),
                   pl.BlockSpec((1, tm, d // 2), row),
                   pl.BlockSpec((1, ne, tm), lambda b, i: (b, 0, i))],
        out_shape=[jax.ShapeDtypeStruct((bsz, t, d), F32),
                   jax.ShapeDtypeStruct((bsz, t, d // 2), I32),
                   jax.ShapeDtypeStruct((bsz, ne, t), F32)],
        compiler_params=_params(("arbitrary", "arbitrary")),
        name="mixer_out",
    )(hg_f, hg_b, ml_f, ml_b, hgg, mlo, x, g1, sc2, sh2, hnw, mnw, w_out, n2w, rwt)


def _prefix_count(maskf, u_ref, ones_ref, bl_ref):
    e, nb, ln = maskf.shape
    x = maskf.reshape(e * nb, ln)
    xb = x.astype(BF16)
    incl = _dot(xb, u_ref[...])
    tot = _dot(xb, ones_ref[...])
    off = _dot(bl_ref[...], tot.astype(BF16))
    return (incl - x + off).reshape(e, nb, ln), off.reshape(e, nb, ln)


def _topk_body(aff_ref, u_ref, ones_ref, bl_ref, bl1_ref, pos_ref, base_ref, *, cap):
    x = aff_ref[0]
    ne = x.shape[0]

    def count(m):
        return jnp.sum(jnp.sum(jnp.where(m, 1.0, 0.0), axis=1, keepdims=True), axis=2, keepdims=True)

    def halve(_, carry):
        lo, hi = carry
        mid = 0.5 * (lo + hi)
        up = count(x > mid) >= cap
        return jnp.where(up, mid, lo), jnp.where(up, hi, mid)

    lo, hi = lax.fori_loop(0, TOPK_BISECTIONS, halve,
                           (jnp.full((ne, 1, 1), -1.0, F32), jnp.full((ne, 1, 1), 1.0, F32)))
    gt = jnp.where(x > hi, 1.0, 0.0)
    eq = jnp.where(x > lo, 1.0, 0.0) - gt
    need = cap - count(x > hi)
    eq_rank, _ = _prefix_count(eq, u_ref, ones_ref, bl_ref)
    sel = gt + eq * jnp.where(eq_rank < need, 1.0, 0.0)
    pos, _ = _prefix_count(sel, u_ref, ones_ref, bl_ref)
    pos_ref[0] = jnp.where(sel > 0, pos, -1.0).astype(I32)
    nsel = jnp.sum(sel, axis=0)
    nb16 = nsel.astype(BF16)
    incl = _dot(nb16, u_ref[...])
    tot = _dot(nb16, ones_ref[...])
    off = _dot(bl1_ref[...], tot, precision=HIGHEST)
    base_ref[0] = (incl - nsel + off).astype(I32)


def _topk(aff4, cap):
    bsz, ne, nb, ln = aff4.shape
    k = np.arange(ln)
    u = jnp.asarray(k[:, None] <= k[None, :], BF16)
    ones = jnp.ones((ln, ln), BF16)
    r = np.arange(ne * nb)
    bl = jnp.asarray(((r[:, None] // nb) == (r[None, :] // nb)) & (r[None, :] < r[:, None]), BF16)
    r1 = np.arange(nb)
    bl1 = jnp.asarray(r1[None, :] < r1[:, None], F32)
    blk = pl.BlockSpec((1, ne, nb, ln), lambda b: (b, 0, 0, 0))
    const = lambda a: pl.BlockSpec(a.shape, lambda b: (0, 0))
    return pl.pallas_call(
        functools.partial(_topk_body, cap=cap),
        grid=(bsz,),
        in_specs=[blk, const(u), const(ones), const(bl), const(bl1)],
        out_specs=[blk, pl.BlockSpec((1, nb, ln), lambda b: (b, 0, 0))],
        out_shape=[jax.ShapeDtypeStruct((bsz, ne, nb, ln), I32), jax.ShapeDtypeStruct((bsz, nb, ln), I32)],
        compiler_params=_params(("arbitrary",)),
        name="expert_topk",
    )(aff4, u, ones, bl, bl1)


def _sc_setup():
    info = plsc.get_sparse_core_info()
    mesh = plsc.VectorSubcoreMesh(core_axis_name="c", subcore_axis_name="s")
    params = dataclasses.replace(pltpu.CompilerParams(), needs_layout_passes=False)
    return info.num_cores, info.num_cores * info.num_subcores, info.num_lanes, mesh, params


def _sc_worker_id(nc):
    return lax.axis_index("s") * nc + lax.axis_index("c")


def _sc_gather_scratch(width, dtype):
    one = [pltpu.VMEM((SC_GATHER_ROWS,), I32), pltpu.VMEM((SC_GATHER_ROWS, width), dtype), pltpu.SemaphoreType.DMA]
    return one + one


def _sc_gather_chunks(table_hbm, out_hbm, idx_v, bufs, out_row0, n, lanes):
    chunk = SC_GATHER_ROWS

    def gather(c, buf):
        ich_v, rows_v, sem = buf
        for q in range(chunk // lanes):
            src = pl.ds(pl.multiple_of(c * chunk + q * lanes, lanes), lanes)
            ich_v[pl.ds(q * lanes, lanes)] = idx_v[src]
        return pltpu.make_async_copy(table_hbm.at[ich_v], rows_v, sem)

    def finish(c, buf):
        ich_v, rows_v, sem = buf
        pltpu.make_async_copy(table_hbm.at[ich_v], rows_v, sem).wait()
        pltpu.sync_copy(rows_v, out_hbm.at[pl.ds(pl.multiple_of(out_row0 + c * chunk, chunk), chunk)])

    npair = n // (2 * chunk)
    gather(0, bufs[0]).start()

    @pl.loop(0, npair)
    def _(i):
        gather(2 * i + 1, bufs[1]).start()
        finish(2 * i, bufs[0])

        @pl.when(i + 1 < npair)
        def _():
            gather(2 * i + 2, bufs[0]).start()

        finish(2 * i + 1, bufs[1])


def _sc_invert_gather(table, pos_flat, aff_flat, npairs, ne, t, cap):
    nc, nw, lanes, mesh, params = _sc_setup()
    split = max(1, nw // npairs)
    seg = cap // split
    per_w = npairs * split // nw
    width = table.shape[1]

    @functools.partial(
        pl.kernel, mesh=mesh, compiler_params=params,
        out_type=[jax.ShapeDtypeStruct((npairs * cap, width), table.dtype),
                  jax.ShapeDtypeStruct((npairs * cap,), F32)],
        scratch_types=[pltpu.VMEM((t,), I32), pltpu.VMEM((t,), F32), pltpu.VMEM((seg,), I32), pltpu.VMEM((seg,), F32)]
        + _sc_gather_scratch(width, table.dtype),
    )
    def body(table_hbm, pos_hbm, aff_hbm, out_hbm, gate_hbm, pos_v, aff_v, idx_v, gate_v, *g):
        wid = _sc_worker_id(nc)

        @pl.loop(0, per_w)
        def _(kk):
            item = wid * per_w + kk
            p = item // split
            lo = (item % split) * seg
            tok0 = (p // ne) * t
            pltpu.sync_copy(pos_hbm.at[pl.ds(pl.multiple_of(p * t, t), t)], pos_v)
            pltpu.sync_copy(aff_hbm.at[pl.ds(pl.multiple_of(p * t, t), t)], aff_v)

            @pl.loop(0, t // lanes)
            def _(i):
                v = pos_v[pl.ds(pl.multiple_of(i * lanes, lanes), lanes)] - lo
                tok = lax.iota(I32, lanes) + i * lanes
                plsc.store_scatter(idx_v, [v], tok, mask=(v >= 0) & (v < seg))

            @pl.loop(0, seg // lanes)
            def _(j):
                sl = pl.ds(pl.multiple_of(j * lanes, lanes), lanes)
                ii = idx_v[sl]
                gate_v[sl] = plsc.load_gather(aff_v, [ii])
                idx_v[sl] = ii + tok0

            row0 = p * cap + lo
            pltpu.sync_copy(gate_v, gate_hbm.at[pl.ds(pl.multiple_of(row0, seg), seg)])
            _sc_gather_chunks(table_hbm, out_hbm, idx_v, (g[0:3], g[3:6]), row0, seg, lanes)

    return body(table, pos_flat, aff_flat)


def _sc_assignment_order(pos_flat, base_flat, bsz, ne, t, cap):
    nc, nw, lanes, mesh, params = _sc_setup()
    na = ne * cap
    per_b = nw // bsz
    rng = na // per_b

    @functools.partial(
        pl.kernel, mesh=mesh, compiler_params=params,
        out_type=[jax.ShapeDtypeStruct((bsz * na,), I32), jax.ShapeDtypeStruct((bsz * na,), I32)],
        scratch_types=[pltpu.VMEM((t,), I32), pltpu.VMEM((t,), I32), pltpu.VMEM((t,), I32),
                       pltpu.VMEM((rng,), I32), pltpu.VMEM((rng,), I32)],
    )
    def body(pos_hbm, base_hbm, perm_hbm, tok_hbm, pos_v, base_v, rank_v, perm_v, tok_v):
        wid = _sc_worker_id(nc)
        b = wid // per_b
        a0 = (wid % per_b) * rng
        pltpu.sync_copy(base_hbm.at[pl.ds(pl.multiple_of(b * t, t), t)], base_v)

        @pl.loop(0, t // lanes)
        def _(i):
            rank_v[pl.ds(pl.multiple_of(i * lanes, lanes), lanes)] = jnp.zeros((lanes,), I32)

        @pl.loop(0, ne)
        def _(e):
            p = b * ne + e
            pltpu.sync_copy(pos_hbm.at[pl.ds(pl.multiple_of(p * t, t), t)], pos_v)

            @pl.loop(0, t // lanes)
            def _(i):
                sl = pl.ds(pl.multiple_of(i * lanes, lanes), lanes)
                v = pos_v[sl]
                r = rank_v[sl]
                a = base_v[sl] + r - a0
                sel = v >= 0
                mine = sel & (a >= 0) & (a < rng)
                plsc.store_scatter(perm_v, [a], v + p * cap, mask=mine)
                plsc.store_scatter(tok_v, [a], lax.iota(I32, lanes) + i * lanes, mask=mine)
                rank_v[sl] = r + jnp.where(sel, 1, 0)

        dst = pl.ds(pl.multiple_of(b * na + a0, rng), rng)
        pltpu.sync_copy(perm_v, perm_hbm.at[dst])
        pltpu.sync_copy(tok_v, tok_hbm.at[dst])

    return body(pos_flat, base_flat)


def _sc_gather_rows(table, idx):
    nc, nw, lanes, mesh, params = _sc_setup()
    n = idx.shape[0]
    per_w = n // nw
    width = table.shape[1]
    chunk = SC_GATHER_ROWS

    @functools.partial(
        pl.kernel, mesh=mesh, compiler_params=params,
        out_type=jax.ShapeDtypeStruct((n, width), table.dtype),
        scratch_types=[pltpu.VMEM((per_w,), I32)] + _sc_gather_scratch(width, table.dtype),
    )
    def body(table_hbm, idx_hbm, out_hbm, idx_v, *g):
        row0 = _sc_worker_id(nc) * per_w
        pltpu.sync_copy(idx_hbm.at[pl.ds(pl.multiple_of(row0, per_w), per_w)], idx_v)
        _sc_gather_chunks(table_hbm, out_hbm, idx_v, (g[0:3], g[3:6]), row0, per_w, lanes)

    return body(table, idx)


def _moe1_body(xs_ref, gate_ref, wg_ref, wu_ref, wd_ref, y_ref):
    xs = _unpack_halves(xs_ref[0])
    hg = _dot(xs, wg_ref[0].astype(BF16))
    hu = _dot(xs, wu_ref[0].astype(BF16))
    h = (_silu(hg) * hu).astype(BF16)
    y = _dot(h, wd_ref[0].astype(BF16))
    gt = gate_ref[0].T
    y = jnp.concatenate([y[k * 128:(k + 1) * 128] * gt[:, k:k + 1] for k in range(gt.shape[1])], axis=0)
    y_ref[0] = _pack_halves(y)


def _moe_experts(xs, gate, wg, wu, wd, bsz):
    npairs, cap, half = xs.shape
    ne, d, f = wg.shape
    pair = lambda b, e: (b * ne + e, 0, 0)
    expert = lambda b, e: (e, 0, 0)
    return pl.pallas_call(
        _moe1_body,
        grid=(bsz, ne),
        in_specs=[pl.BlockSpec((1, cap, half), pair),
                  pl.BlockSpec((1, cap // 128, 128), pair),
                  pl.BlockSpec((1, d, f), expert),
                  pl.BlockSpec((1, d, f), expert),
                  pl.BlockSpec((1, f, d), expert)],
        out_specs=pl.BlockSpec((1, cap, half), pair),
        out_shape=jax.ShapeDtypeStruct((npairs, cap, half), I32),
        compiler_params=_params(("arbitrary", "arbitrary")),
        name="moe_experts",
    )(xs, gate, wg, wu, wd)


def _combine_body(abase_ref, *refs, na, bsz):
    ys_hbm = refs[:bsz]
    tok_ref, x1_ref, g2_ref, fw_ref, o_ref, buf_ref, acc_ref, sem_ref, cnt_ref = refs[bsz:]
    n = ONEHOT_BLK
    ring = buf_ref.shape[0]
    nch = na // n
    b, i = pl.program_id(0), pl.program_id(1)

    @pl.when(i == 0)
    def _():
        cnt_ref[0] = 0
        cnt_ref[1] = 0

    lo = abase_ref[b, i]
    hi = abase_ref[b, i + 1]
    c0 = lo // n
    c1 = jnp.where(hi > lo, (hi + n - 1) // n, c0)

    def copy(c, src):
        slot = c % ring
        row0 = pl.multiple_of(c * n, n)
        return pltpu.make_async_copy(src.at[pl.ds(row0, n)], buf_ref.at[slot], sem_ref.at[slot])

    acc_ref[...] = jnp.zeros(acc_ref.shape, F32)
    tok_ids = lax.broadcasted_iota(I32, (n, n), 0) + i * n

    def step(c, carry):
        started = cnt_ref[0]
        ahead = jnp.minimum(c + ring, nch)

        def start(k, _):
            for bb in range(bsz):
                @pl.when(b == bb)
                def _(bb=bb):
                    copy(k, ys_hbm[bb]).start()
            return 0

        lax.fori_loop(started, ahead, start, 0)
        cnt_ref[0] = jnp.maximum(started, ahead)

        @pl.when(cnt_ref[1] <= c)
        def _():
            copy(c, ys_hbm[0]).wait()
            cnt_ref[1] = c + 1

        seg = jnp.where(tok_ids == tok_ref[0, pl.ds(c, 1), :], 1.0, 0.0).astype(BF16)
        acc_ref[...] += _dot(seg, _unpack_halves(buf_ref[c % ring]))
        return carry

    lax.fori_loop(c0, c1, step, 0)
    x2 = x1_ref[0] + g2_ref[0] * acc_ref[...]
    o_ref[0] = x2 * lax.rsqrt(jnp.mean(x2 * x2, axis=-1, keepdims=True) + NORM_EPS) * fw_ref[...]


def _combine_final(abase, ys, tok, x1, g2, fw, na):
    bsz, t, d = x1.shape
    n = ONEHOT_BLK
    grid_spec = pltpu.PrefetchScalarGridSpec(
        num_scalar_prefetch=1,
        grid=(bsz, t // n),
        in_specs=[pl.BlockSpec(memory_space=pl.ANY)] * bsz + [
                  pl.BlockSpec((1, na // n, n), lambda b, i, a: (b, 0, 0)),
                  pl.BlockSpec((1, n, d), lambda b, i, a: (b, i, 0)),
                  pl.BlockSpec((1, 1, d), lambda b, i, a: (b, 0, 0)),
                  pl.BlockSpec((1, d), lambda b, i, a: (0, 0))],
        out_specs=pl.BlockSpec((1, n, d), lambda b, i, a: (b, i, 0)),
        scratch_shapes=[pltpu.VMEM((COMBINE_RING, n, d // 2), I32), pltpu.VMEM((n, d), F32),
                        pltpu.SemaphoreType.DMA((COMBINE_RING,)), pltpu.SMEM((2,), I32)],
    )
    return pl.pallas_call(
        functools.partial(_combine_body, na=na, bsz=bsz),
        grid_spec=grid_spec,
        out_shape=jax.ShapeDtypeStruct((bsz, t, d), F32),
        compiler_params=_params(("arbitrary", "arbitrary")),
        name="combine_final",
    )(abase, *ys, tok, x1, g2, fw)


def kernel(x, c, ctx, c_ctx, ada_w, ada_b, norm1_w, w_in, conv_w, conv_b, hg_lb_logits, ml_gate_b,
           hg_norm_w, ml_norm_w, w_out, norm2_w, router_w, exp_w_gate, exp_w_up, exp_w_down, final_norm_w):
    bsz, t, d = x.shape
    nctx = ctx.shape[1]
    assert ada_w.shape[0] == 1, "single-layer block"
    assert nctx == SCAN_STEP and t % SCAN_STEP == 0 and bsz + 1 <= 8
    tall = t + nctx
    nx = t // SCAN_STEP
    kw = HEADS_W
    ne = router_w.shape[-1]
    cap = EC_CAPACITY * t // ne
    tm = min(512, t)

    rows = jnp.concatenate([c, c_ctx[None], jnp.zeros((7 - bsz, d), F32)], axis=0)
    mod = _modulation(rows, ada_w[0], ada_b[0][None])
    mx = [m[:, None, :] for m in jnp.split(mod[:bsz], 6, axis=-1)]
    mc = [m[:, None, :] for m in jnp.split(mod[bsz:bsz + 1], 6, axis=-1)]
    sh1, sc1, g1, sh2, sc2, g2 = mx
    csh1, csc1 = mc[0], mc[1]

    main_w = 9 * kw
    w_main = w_in[0][:, :main_w].astype(BF16)
    w_gt = w_in[0][:, main_w:].T
    gate_b = ml_gate_b[0][:, None]
    nw1 = norm1_w[0][None]
    outs = _inproj(x, sc1, sh1, nw1, w_main, w_gt, gate_b, hg_lb_logits, tall, tm, 0)
    outs = _inproj(ctx, csc1, csh1, nw1, w_main, w_gt, gate_b, hg_lb_logits, tall, nctx, t // nctx, prev=outs)
    hgq, hgv, hgg, hgk, hglf, mlqk_pre, mlv, mlo, gates = outs

    mlqk = _conv(mlqk_pre, conv_w[0], conv_b[0][None], t, nctx)
    hg_f, hg_b = _hgrn2(hgq, hgk, hgv, hglf, nx)
    ml_f, ml_b = _mlstm(mlqk, mlv, gates, nx)

    x1, vpk, aff = _mixer_out(hg_f, hg_b, ml_f, ml_b, hgg, mlo, x, g1, sc2, sh2,
                             hg_norm_w[0][None], ml_norm_w[0][None], w_out[0].astype(BF16),
                             norm2_w[0][None], router_w[0].T, tm)

    pos4, base = _topk(aff.reshape(bsz, ne, t // 128, 128), cap)
    na = ne * cap
    ys, toks = [], []
    for b in range(bsz):
        pos_b = pos4[b].reshape(-1)
        xs, gate = _sc_invert_gather(vpk[b], pos_b, aff[b].reshape(-1), ne, ne, t, cap)
        perm, tok = _sc_assignment_order(pos_b, base[b].reshape(-1), 1, ne, t, cap)
        y = _moe_experts(xs.reshape(ne, cap, d // 2), gate.reshape(ne, cap // 128, 128),
                         exp_w_gate[0], exp_w_up[0], exp_w_down[0], 1)
        ys.append(_sc_gather_rows(y.reshape(na, d // 2), perm))
        toks.append(tok.reshape(na // ONEHOT_BLK, ONEHOT_BLK))
    abase = jnp.concatenate([base.reshape(bsz, t)[:, ::ONEHOT_BLK], jnp.full((bsz, 1), na, I32)], axis=1)
    return _combine_final(abase, ys, jnp.stack(toks), x1, g2, final_norm_w[None], na)
```

```python
import dataclasses
import functools

import numpy as np
import jax
import jax.numpy as jnp
from jax import lax
from jax.experimental import pallas as pl
from jax.experimental.pallas import tpu as pltpu
from jax.experimental.pallas import tpu_sc as plsc

F32 = jnp.float32
BF16 = jnp.bfloat16
I32 = jnp.int32
HIGHEST = lax.Precision.HIGHEST
NORM_EPS = 1e-6

HEAD_DIM = 128
N_HEADS = 4
HEADS_W = N_HEADS * HEAD_DIM
GRID_W = 64
N_EXPERTS = 16
EC_CAPACITY = 2
HG_CHUNK = 64
ML_CHUNK = 128
SCAN_STEP = 256
ML_SAMPLES = 2
ONEHOT_BLK = 256
CONV_HALO = 72
SC_GATHER_ROWS = 64
COMBINE_RING = 6
TOPK_BISECTIONS = 64
VMEM_LIMIT = 56 * 1024 * 1024

_NT = (((1,), (1,)), ((), ()))
_TN = (((0,), (0,)), ((), ()))


def _dot(a, b, dims=None, precision=None):
    if dims is None:
        return jnp.dot(a, b, preferred_element_type=F32, precision=precision)
    return lax.dot_general(a, b, dims, preferred_element_type=F32, precision=precision)


def _sigmoid(x):
    return jax.nn.sigmoid(x)


def _pack_halves(x):
    w = x.shape[-1] // 2
    bits = lax.bitcast_convert_type(x.astype(BF16).astype(F32), I32)
    return lax.shift_right_logical(bits[:, :w], 16) | bits[:, w:]


def _unpack_halves(p):
    lo = lax.bitcast_convert_type(lax.shift_left(p, 16), F32)
    hi = lax.bitcast_convert_type(p & jnp.int32(-65536), F32)
    return jnp.concatenate([lo, hi], axis=-1).astype(BF16)


def _silu(x):
    return x * jax.nn.sigmoid(x)


def _params(sem, flags=None):
    return pltpu.CompilerParams(dimension_semantics=sem, vmem_limit_bytes=VMEM_LIMIT, flags=flags)


def _mod_body(r_ref, w_ref, b_ref, o_ref):
    r = r_ref[...]
    o_ref[...] = _dot(_silu(r), w_ref[...], precision=HIGHEST) + b_ref[...]


def _modulation(rows, w, b):
    d, n = w.shape
    tn = n // 4
    return pl.pallas_call(
        _mod_body,
        grid=(n // tn,),
        in_specs=[pl.BlockSpec((8, d), lambda j: (0, 0)),
                  pl.BlockSpec((d, tn), lambda j: (0, j)),
                  pl.BlockSpec((1, tn), lambda j: (0, j))],
        out_specs=pl.BlockSpec((8, tn), lambda j: (0, j)),
        out_shape=jax.ShapeDtypeStruct((8, n), F32),
        compiler_params=_params(("arbitrary",)),
        name="modulation",
    )(rows, w, b)


def _log_sigmoid(x):
    return jnp.minimum(x, 0.0) - jnp.log(1.0 + jnp.exp(-jnp.abs(x)))


def _inproj_body(x_ref, sc_ref, sh_ref, nw_ref, w_ref, wg_ref, gb_ref, lbl_ref, *refs):
    hgq_ref, hgv_ref, hgg_ref, hgk_ref, hglf_ref, mlqk_ref, mlv_ref, mlo_ref, gates_ref = refs[-9:]
    kw = HEADS_W
    x = x_ref[0]
    y = x * lax.rsqrt(jnp.mean(x * x, axis=-1, keepdims=True) + NORM_EPS) * nw_ref[...]
    u = y * (1.0 + sc_ref[0]) + sh_ref[0]
    ub = u.astype(BF16)

    def proj(c0, c1):
        return _dot(ub, w_ref[:, c0:c1])

    hgq_ref[0] = _silu(proj(0, kw)).astype(BF16)
    hgv_ref[0] = proj(kw, 2 * kw).astype(BF16)
    hgg_ref[0] = _silu(proj(2 * kw, 3 * kw)).astype(BF16)

    lbl = lbl_ref[...]
    mx = jnp.max(lbl, axis=0)
    ex = jnp.exp(lbl - mx[None])
    lb = ex[0] / jnp.sum(ex, axis=0)
    for d in range(2):
        p = proj((3 + d) * kw, (4 + d) * kw)
        lbd = lb[d:d + 1]
        f = lbd + (1.0 - lbd) * _sigmoid(p)
        hgk_ref[0, :, d * kw:(d + 1) * kw] = (1.0 - f).astype(BF16)
        hglf_ref[0, :, d * kw:(d + 1) * kw] = jnp.log(f)

    mlqk_ref[0, :, 0:kw] = proj(5 * kw, 6 * kw)
    mlqk_ref[0, :, kw:2 * kw] = proj(6 * kw, 7 * kw)
    mlv_ref[0] = proj(7 * kw, 8 * kw).astype(BF16)
    mlo_ref[0] = _sigmoid(proj(8 * kw, 9 * kw)).astype(BF16)

    g = _dot(wg_ref[...], u, _NT, precision=HIGHEST) + gb_ref[...]
    row = lax.broadcasted_iota(I32, g.shape, 0)
    gates_ref[0] = jnp.where((row % 8) >= N_HEADS, _log_sigmoid(g), g)


def _inproj_shapes(bsz, tall):
    kw = HEADS_W
    return [
        jax.ShapeDtypeStruct((bsz, tall, kw), BF16),
        jax.ShapeDtypeStruct((bsz, tall, kw), BF16),
        jax.ShapeDtypeStruct((bsz, tall, kw), BF16),
        jax.ShapeDtypeStruct((bsz, tall, 2 * kw), BF16),
        jax.ShapeDtypeStruct((bsz, tall, 2 * kw), F32),
        jax.ShapeDtypeStruct((bsz, tall, 2 * kw), F32),
        jax.ShapeDtypeStruct((bsz, tall, kw), BF16),
        jax.ShapeDtypeStruct((bsz, tall, kw), BF16),
        jax.ShapeDtypeStruct((bsz, 4 * N_HEADS, tall), F32),
    ]


def _inproj(tokens, scale, shift, nw, w_main, w_gt, gate_b, lb_logits, tall, tm, blk0, prev=None):
    bsz, n, d = tokens.shape
    kw = HEADS_W
    nt = n // tm
    per_sample = scale.shape[0] == bsz
    mod_map = (lambda b, i: (b, 0, 0)) if per_sample else (lambda b, i: (0, 0, 0))
    const2 = lambda b, i: (0, 0)
    in_specs = [
        pl.BlockSpec((1, tm, d), lambda b, i: (b, i, 0)),
        pl.BlockSpec((1, 1, d), mod_map),
        pl.BlockSpec((1, 1, d), mod_map),
        pl.BlockSpec((1, d), const2),
        pl.BlockSpec(w_main.shape, const2),
        pl.BlockSpec(w_gt.shape, const2),
        pl.BlockSpec(gate_b.shape, const2),
        pl.BlockSpec(lb_logits.shape, lambda b, i: (0, 0, 0)),
    ]
    args = [tokens, scale, shift, nw, w_main, w_gt, gate_b, lb_logits]
    aliases = {}
    if prev is not None:
        for k, a in enumerate(prev):
            in_specs.append(pl.BlockSpec(memory_space=pl.ANY))
            aliases[len(args)] = k
            args.append(a)
    row_map = lambda b, i: (b, blk0 + i, 0)
    widths = [kw, kw, kw, 2 * kw, 2 * kw, 2 * kw, kw, kw]
    out_specs = [pl.BlockSpec((1, tm, w), row_map) for w in widths]
    out_specs.append(pl.BlockSpec((1, 4 * N_HEADS, tm), lambda b, i: (b, 0, blk0 + i)))
    return pl.pallas_call(
        _inproj_body,
        grid=(bsz, nt),
        in_specs=in_specs,
        out_specs=out_specs,
        out_shape=_inproj_shapes(bsz, tall),
        input_output_aliases=aliases,
        compiler_params=_params(("arbitrary", "arbitrary")),
        name="inproj_ctx" if prev is not None else "inproj_x",
    )(*args)


def _conv_body(x_ref, w_ref, b_ref, o_ref, pad_ref, cpad_ref, *, t, nctx, scale_from):
    halo = CONV_HALO
    rows = 512
    win = rows + 2 * halo
    ch = x_ref.shape[-1]
    scale = jnp.where(pl.program_id(1) >= scale_from, HEAD_DIM ** -0.5, 1.0).astype(F32)
    w = w_ref[...]
    bias = b_ref[...]

    pad_ref[0:halo, :] = jnp.zeros((halo, ch), F32)
    pad_ref[halo + t:halo + t + halo, :] = jnp.zeros((halo, ch), F32)
    pad_ref[halo:halo + t, :] = x_ref[0, 0:t, :]
    col = (lax.broadcasted_iota(I32, (win, ch), 0) + (GRID_W - halo % GRID_W)) % GRID_W
    left_ok = col > 0
    right_ok = col < GRID_W - 1

    def chunk(c, carry):
        o = pl.multiple_of(c * rows, rows)
        xw = pad_ref[pl.ds(o, win), :]
        xm = jnp.where(left_ok, pltpu.roll(xw, 1, 0), 0.0)
        xp = jnp.where(right_ok, pltpu.roll(xw, win - 1, 0), 0.0)
        z = [xm * w[dr, 0:1] + xw * w[dr, 1:2] + xp * w[dr, 2:3] for dr in range(3)]
        y = (z[1][halo:halo + rows]
             + z[0][halo - GRID_W:halo - GRID_W + rows]
             + z[2][halo + GRID_W:halo + GRID_W + rows])
        o_ref[0, pl.ds(o, rows), :] = (_silu(y + bias) * scale).astype(o_ref.dtype)
        return carry

    lax.fori_loop(0, t // rows, chunk, 0)

    cpad_ref[0:8, :] = jnp.zeros((8, ch), F32)
    cpad_ref[8 + nctx:16 + nctx, :] = jnp.zeros((8, ch), F32)
    cpad_ref[8:8 + nctx, :] = x_ref[0, t:t + nctx, :]
    xw = cpad_ref[...]
    n = nctx + 16
    y = (pltpu.roll(xw, 1, 0) * w[1, 0:1] + xw * w[1, 1:2] + pltpu.roll(xw, n - 1, 0) * w[1, 2:3])[8:8 + nctx]
    o_ref[0, t:t + nctx, :] = (_silu(y + bias) * scale).astype(o_ref.dtype)


def _conv(qk_pre, conv_w, conv_b, t, nctx):
    bsz, tall, c = qk_pre.shape
    ch = 128
    body = functools.partial(_conv_body, t=t, nctx=nctx, scale_from=(c // 2) // ch)
    return pl.pallas_call(
        body,
        grid=(bsz, c // ch),
        in_specs=[pl.BlockSpec((1, tall, ch), lambda b, j: (b, 0, j)),
                  pl.BlockSpec((3, 3, ch), lambda b, j: (0, 0, j)),
                  pl.BlockSpec((1, ch), lambda b, j: (0, j))],
        out_specs=pl.BlockSpec((1, tall, ch), lambda b, j: (b, 0, j)),
        out_shape=jax.ShapeDtypeStruct((bsz, tall, c), BF16),
        scratch_shapes=[pltpu.VMEM((t + 2 * CONV_HALO, ch), F32),
                        pltpu.VMEM((nctx + 16, ch), F32)],
        compiler_params=_params(("arbitrary", "arbitrary")),
        name="qk_conv",
    )(qk_pre, conv_w, conv_b)


def _fwd_blk(s, nx):
    return jnp.where(s == 0, nx, s - 1)


def _bwd_blk(s, nx):
    return jnp.where(s == 0, nx, nx - s)


def _hg_constants(rev):
    c = HG_CHUNK
    i = np.arange(c)[:, None]
    j = np.arange(c)[None, :]
    blocks = [(j >= i) if rev else (j <= i), (j < i) if rev else (j > i)]
    masks = [i == j]
    m = c // 2
    while m >= 1:
        b0 = (i // (2 * m)) * (2 * m)
        same = (i // (2 * m)) == (j // (2 * m))
        if rev:
            beta = b0 + m
            qrow = (i % (2 * m)) < m
            g = np.where(qrow, (j >= i) & (j < beta), (j >= beta) & (j < i))
            mask = same & qrow & ((j % (2 * m)) >= m)
        else:
            beta = b0 + m - 1
            qrow = (i % (2 * m)) >= m
            g = np.where(qrow, (j > beta) & (j <= i), (j > i) & (j <= beta))
            mask = same & qrow & ((j % (2 * m)) < m)
        blocks.append(g)
        masks.append(mask)
        m //= 2
    g = np.concatenate(blocks, axis=0).astype(np.float32)
    g3 = np.concatenate([g, g, g], axis=1)
    m2 = np.concatenate([np.stack(masks), np.stack(masks)], axis=2)
    return (jnp.asarray(g3, BF16), jnp.asarray(m2, F32))


def _block_diag(x, zero):
    w = x.shape[1] // 2
    return jnp.concatenate([jnp.concatenate([x[:, :w], zero], axis=1),
                            jnp.concatenate([zero, x[:, w:]], axis=1)], axis=0)


def _hg_chunk(dirs):
    c = HG_CHUNK
    w = 2 * HEAD_DIM
    zero = jnp.zeros((c, HEAD_DIM), BF16)
    units = []
    for rev, r0, q_ref, k_ref, v_ref, lf_ref, g_ref, msk_ref, o_ref, st_ref in dirs:
        rows = pl.ds(r0, c)
        lf = lf_ref[0, rows, :]
        p1 = lf.astype(BF16)
        r1 = lf - p1.astype(F32)
        p2 = r1.astype(BF16)
        p3 = (r1 - p2.astype(F32)).astype(BF16)
        dall = _dot(g_ref[...], jnp.concatenate([p1, p2, p3], axis=0))
        for hp in range(N_HEADS // 2):
            cs = slice(hp * w, (hp + 1) * w)
            units.append(dict(rev=rev, rows=rows, cs=cs, hp=hp, msk_ref=msk_ref, o_ref=o_ref, st_ref=st_ref,
                              q=q_ref[0, rows, cs], k=k_ref[0, rows, cs], v=v_ref[0, rows, cs], dall=dall[:, cs]))
    for u in units:
        msk_ref = u["msk_ref"]
        nlev = msk_ref.shape[0] - 1
        att = _dot(u["q"], _block_diag(u["k"], zero), _NT) * msk_ref[0]
        for l in range(nlev):
            e = jnp.exp(u["dall"][(l + 2) * c:(l + 3) * c]).astype(BF16)
            att = att + _dot(u["q"] * e, _block_diag(u["k"] * e, zero), _NT) * msk_ref[l + 1]
        u["att"] = att.astype(BF16)
    for u in units:
        a = u["dall"][0:c]
        u["a_tot"] = a[0:1] if u["rev"] else a[c - 1:c]
        u["st"] = [u["st_ref"][2 * u["hp"] + i] for i in range(2)]
        zf = jnp.zeros((HEAD_DIM, HEAD_DIM), BF16)
        st2 = jnp.concatenate([jnp.concatenate([u["st"][0].astype(BF16), zf], axis=1),
                               jnp.concatenate([zf, u["st"][1].astype(BF16)], axis=1)], axis=0)
        qbar = (u["q"].astype(F32) * jnp.exp(a)).astype(BF16)
        u["o"] = _dot(u["att"], _block_diag(u["v"], zero)) + _dot(qbar, st2, _NT)
        u["khat"] = (u["k"].astype(F32) * jnp.exp(u["dall"][c:2 * c])).astype(BF16)
    for u in units:
        u["o_ref"][0, u["rows"], u["cs"]] = u["o"].astype(BF16)
        for i in range(2):
            hs = slice(i * HEAD_DIM, (i + 1) * HEAD_DIM)
            upd = _dot(u["v"][:, hs], u["khat"][:, hs], _TN)
            u["st_ref"][2 * u["hp"] + i] = u["st"][i] * jnp.exp(u["a_tot"][:, hs]) + upd


def _hg_body(qf_ref, kf_ref, vf_ref, lff_ref, qb_ref, kb_ref, vb_ref, lfb_ref,
             gf_ref, mf_ref, gb_ref, mb_ref, of_ref, ob_ref, st_ref):
    @pl.when(pl.program_id(1) == 0)
    def _():
        st_ref[...] = jnp.zeros(st_ref.shape, F32)

    nsub = SCAN_STEP // HG_CHUNK
    for c in range(nsub):
        _hg_chunk([
            (False, c * HG_CHUNK, qf_ref, kf_ref, vf_ref, lff_ref, gf_ref, mf_ref, of_ref, st_ref.at[0]),
            (True, (nsub - 1 - c) * HG_CHUNK, qb_ref, kb_ref, vb_ref, lfb_ref, gb_ref, mb_ref, ob_ref, st_ref.at[1]),
        ])


def _hgrn2(hgq, hgk, hgv, hglf, nx):
    bsz, tall, kw = hgq.shape
    steps = tall // SCAN_STEP
    cf = _hg_constants(False)
    cb = _hg_constants(True)
    blk = (1, SCAN_STEP, kw)
    fwd = lambda col: (lambda b, s: (b, _fwd_blk(s, nx), col))
    bwd = lambda col: (lambda b, s: (b, _bwd_blk(s, nx), col))
    const = lambda a: pl.BlockSpec(a.shape, lambda b, s: (0,) * a.ndim)
    in_specs = [pl.BlockSpec(blk, fwd(0)), pl.BlockSpec(blk, fwd(0)), pl.BlockSpec(blk, fwd(0)), pl.BlockSpec(blk, fwd(0)),
                pl.BlockSpec(blk, bwd(0)), pl.BlockSpec(blk, bwd(1)), pl.BlockSpec(blk, bwd(0)), pl.BlockSpec(blk, bwd(1))]
    in_specs += [const(a) for a in cf + cb]
    out_sds = jax.ShapeDtypeStruct((bsz, tall, kw), BF16)
    return pl.pallas_call(
        _hg_body,
        grid=(bsz, steps),
        in_specs=in_specs,
        out_specs=[pl.BlockSpec(blk, fwd(0)), pl.BlockSpec(blk, bwd(0))],
        out_shape=[out_sds, out_sds],
        scratch_shapes=[pltpu.VMEM((2, N_HEADS, HEAD_DIM, HEAD_DIM), F32)],
        compiler_params=_params(("arbitrary", "arbitrary")),
        name="hgrn2_scan",
    )(hgq, hgk, hgv, hglf, hgq, hgk, hgv, hglf, *cf, *cb)


def _ml_constants(rev):
    k = np.arange(ML_CHUNK)
    tri = (k[:, None] >= k[None, :]) if rev else (k[:, None] <= k[None, :])
    return jnp.asarray(np.concatenate([tri, tri, tri], axis=0), BF16)


def _ml_chunk(dirs):
    c = ML_CHUNK
    ii = lax.broadcasted_iota(I32, (c, c), 0)
    jj = lax.broadcasted_iota(I32, (c, c), 1)
    ones = jnp.ones((c, HEAD_DIM), BF16)
    units = []
    for rev, r0, d, bb, q_ref, k_ref, v_ref, g_ref, tri3_ref, o_ref, st_ref, m_ref in dirs:
        gates = g_ref[bb, :, pl.ds(r0, c)]
        p1 = gates.astype(BF16)
        r1 = gates - p1.astype(F32)
        p2 = r1.astype(BF16)
        p3 = (r1 - p2.astype(F32)).astype(BF16)
        csum = _dot(jnp.concatenate([p1, p2, p3], axis=1), tri3_ref[...])
        for h in range(N_HEADS):
            cs = slice(h * HEAD_DIM, (h + 1) * HEAD_DIM)
            u = dict(rev=rev, o_ref=o_ref, bb=bb, rows=pl.ds(r0, c), cs=cs, st_ref=st_ref, m_ref=m_ref, h=h)
            u["qb"] = q_ref[bb, pl.ds(r0, c), cs]
            u["kb"] = k_ref[bb, pl.ds(r0, c), cs]
            u["v1"] = jnp.concatenate([v_ref[bb, pl.ds(r0, c), cs], ones], axis=1)
            irow = gates[d * 8 + h:d * 8 + h + 1]
            u["brow"] = csum[d * 8 + N_HEADS + h:d * 8 + N_HEADS + h + 1]
            u["rrow"] = irow - u["brow"]
            units.append(u)
    for u in units:
        u["st"] = u["st_ref"][u["h"]]
        a = _dot(jnp.concatenate([u["kb"], u["st"].astype(BF16)], axis=0), u["qb"], _NT)
        u["s"] = a[:c]
        u["sq"] = a[c:]
    for u in units:
        last = 0 if u["rev"] else c - 1
        tri_t = (ii >= jj) if u["rev"] else (ii <= jj)
        rcol = jnp.concatenate([u["rrow"], jnp.zeros((7, c), F32)], axis=0).T[:, 0:1]
        u["mprev"] = u["m_ref"][u["h"]][:, 0:1]
        rmat = jnp.where(tri_t, rcol, -jnp.inf)
        u["grow"] = jnp.maximum(jnp.max(rmat, axis=0, keepdims=True), u["mprev"])
        qk = (u["s"] * jnp.exp(rmat - u["grow"])).astype(BF16)
        blast = u["brow"][:, last:last + 1]
        u["mnew"] = blast + u["grow"][:, last:last + 1]
        kh = (u["kb"].astype(F32) * jnp.exp(blast + rcol - u["mnew"])).astype(BF16)
        u["ws"] = jnp.exp(blast + u["mprev"] - u["mnew"])
        u["qkh"] = jnp.concatenate([qk, kh], axis=1)
    for u in units:
        u["nu"] = _dot(u["v1"], u["qkh"], _TN)
    for u in units:
        both = u["nu"][:, :c] + jnp.exp(u["mprev"] - u["grow"]) * u["sq"]
        den = both[HEAD_DIM:HEAD_DIM + 1]
        inv = 1.0 / jnp.maximum(jnp.abs(den), jnp.exp(-(u["brow"] + u["grow"])))
        u["o_ref"][u["bb"], u["rows"], u["cs"]] = (both[:HEAD_DIM] * inv).T.astype(BF16)
        u["st_ref"][u["h"]] = u["ws"] * u["st"] + u["nu"][:, c:]
        u["m_ref"][u["h"]] = jnp.broadcast_to(u["mnew"], (1, HEAD_DIM))


def _ml_body(qf_ref, kf_ref, vf_ref, gf_ref, qb_ref, kb_ref, vb_ref, gb_ref, tf_ref, tb_ref,
             of_ref, ob_ref, st_ref, m_ref):
    @pl.when(pl.program_id(1) == 0)
    def _():
        st_ref[...] = jnp.zeros(st_ref.shape, F32)
        m_ref[...] = jnp.zeros(m_ref.shape, F32)

    nsub = SCAN_STEP // ML_CHUNK
    for c in range(nsub):
        dirs = []
        for bb in range(qf_ref.shape[0]):
            dirs.append((False, c * ML_CHUNK, 0, bb, qf_ref, kf_ref, vf_ref, gf_ref, tf_ref, of_ref,
                         st_ref.at[0, bb], m_ref.at[0, bb]))
            dirs.append((True, (nsub - 1 - c) * ML_CHUNK, 1, bb, qb_ref, kb_ref, vb_ref, gb_ref, tb_ref, ob_ref,
                         st_ref.at[1, bb], m_ref.at[1, bb]))
        _ml_chunk(dirs)


def _mlstm(mlqk, mlv, gates, nx):
    bsz, tall, kw = mlv.shape
    steps = tall // SCAN_STEP
    nb = ML_SAMPLES if bsz % ML_SAMPLES == 0 else 1
    blk = (nb, SCAN_STEP, kw)
    gblk = (nb, 4 * N_HEADS, SCAN_STEP)
    tf, tb = _ml_constants(False), _ml_constants(True)
    fwd = lambda col: (lambda b, s: (b, _fwd_blk(s, nx), col))
    bwd = lambda col: (lambda b, s: (b, _bwd_blk(s, nx), col))
    const = pl.BlockSpec(tf.shape, lambda b, s: (0, 0))
    in_specs = [pl.BlockSpec(blk, fwd(0)), pl.BlockSpec(blk, fwd(1)), pl.BlockSpec(blk, fwd(0)),
                pl.BlockSpec(gblk, lambda b, s: (b, 0, _fwd_blk(s, nx))),
                pl.BlockSpec(blk, bwd(0)), pl.BlockSpec(blk, bwd(1)), pl.BlockSpec(blk, bwd(0)),
                pl.BlockSpec(gblk, lambda b, s: (b, 0, _bwd_blk(s, nx))), const, const]
    out_sds = jax.ShapeDtypeStruct((bsz, tall, kw), BF16)
    return pl.pallas_call(
        _ml_body,
        grid=(bsz // nb, steps),
        in_specs=in_specs,
        out_specs=[pl.BlockSpec(blk, fwd(0)), pl.BlockSpec(blk, bwd(0))],
        out_shape=[out_sds, out_sds],
        scratch_shapes=[pltpu.VMEM((2, nb, N_HEADS, 2 * HEAD_DIM, HEAD_DIM), F32),
                        pltpu.VMEM((2, nb, N_HEADS, 1, HEAD_DIM), F32)],
        compiler_params=_params(("arbitrary", "arbitrary")),
        name="mlstm_scan",
    )(mlqk, mlqk, mlv, gates, mlqk, mlqk, mlv, gates, tf, tb)


def _out_body(hof_ref, hob_ref, mhf_ref, mhb_ref, hgg_ref, mlo_ref, x_ref, g1_ref, sc2_ref, sh2_ref,
              hnw_ref, mnw_ref, wout_ref, n2w_ref, rwt_ref, x1_ref, vt_ref, aff_ref):
    hg = hof_ref[0].astype(F32) + hob_ref[0].astype(F32)
    ml = mhf_ref[0].astype(F32) + mhb_ref[0].astype(F32)
    hparts, mparts = [], []
    for h in range(N_HEADS):
        cs = slice(h * HEAD_DIM, (h + 1) * HEAD_DIM)
        t = hg[:, cs]
        hparts.append(t * lax.rsqrt(jnp.mean(t * t, axis=-1, keepdims=True) + NORM_EPS))
        t = ml[:, cs]
        t = t - jnp.mean(t, axis=-1, keepdims=True)
        mparts.append(t * lax.rsqrt(jnp.mean(t * t, axis=-1, keepdims=True) + NORM_EPS))
    hgn = jnp.concatenate(hparts, axis=-1) * hnw_ref[...] * hgg_ref[0].astype(F32)
    mln = jnp.concatenate(mparts, axis=-1) * mnw_ref[...] * mlo_ref[0].astype(F32)
    mix = jnp.concatenate([hgn, mln], axis=-1).astype(BF16)
    x1 = x_ref[0] + g1_ref[0] * _dot(mix, wout_ref[...])
    x1_ref[0] = x1
    v = x1 * lax.rsqrt(jnp.mean(x1 * x1, axis=-1, keepdims=True) + NORM_EPS) * n2w_ref[...]
    v = v * (1.0 + sc2_ref[0]) + sh2_ref[0]
    vt_ref[0] = _pack_halves(v)
    logits = _dot(rwt_ref[...], v, _NT, precision=HIGHEST)
    ex = jnp.exp(logits - jnp.max(logits, axis=0, keepdims=True))
    aff_ref[0] = ex / jnp.sum(ex, axis=0, keepdims=True)


def _mixer_out(hg_f, hg_b, ml_f, ml_b, hgg, mlo, x, g1, sc2, sh2, hnw, mnw, w_out, n2w, rwt, tm):
    bsz, t, d = x.shape
    kw = HEADS_W
    ne = rwt.shape[0]
    row = lambda b, i: (b, i, 0)
    mod = lambda b, i: (b, 0, 0)
    const2 = lambda b, i: (0, 0)
    act = pl.BlockSpec((1, tm, kw), row)
    in_specs = [act, act, act, act, act, act,
                pl.BlockSpec((1, tm, d), row),
                pl.BlockSpec((1, 1, d), mod), pl.BlockSpec((1, 1, d), mod), pl.BlockSpec((1, 1, d), mod),
                pl.BlockSpec((1, kw), const2), pl.BlockSpec((1, kw), const2),
                pl.BlockSpec(w_out.shape, const2), pl.BlockSpec((1, d), const2), pl.BlockSpec(rwt.shape, const2)]
    return pl.pallas_call(
        _out_body,
        grid=(bsz, t // tm),
        in_specs=in_specs,
        out_specs=[pl.BlockSpec((1, tm, d), row),
                   pl.BlockSpec((1, tm, d // 2), row),
                   pl.BlockSpec((1, ne, tm), lambda b, i: (b, 0, i))],
        out_shape=[jax.ShapeDtypeStruct((bsz, t, d), F32),
                   jax.ShapeDtypeStruct((bsz, t, d // 2), I32),
                   jax.ShapeDtypeStruct((bsz, ne, t), F32)],
        compiler_params=_params(("arbitrary", "arbitrary")),
        name="mixer_out",
    )(hg_f, hg_b, ml_f, ml_b, hgg, mlo, x, g1, sc2, sh2, hnw, mnw, w_out, n2w, rwt)


def _prefix_count(maskf, u_ref, ones_ref, bl_ref):
    e, nb, ln = maskf.shape
    x = maskf.reshape(e * nb, ln)
    xb = x.astype(BF16)
    incl = _dot(xb, u_ref[...])
    tot = _dot(xb, ones_ref[...])
    off = _dot(bl_ref[...], tot.astype(BF16))
    return (incl - x + off).reshape(e, nb, ln), off.reshape(e, nb, ln)


def _topk_body(aff_ref, u_ref, ones_ref, bl_ref, bl1_ref, pos_ref, base_ref, *, cap):
    x = aff_ref[0]
    ne = x.shape[0]

    def count(m):
        return jnp.sum(jnp.sum(jnp.where(m, 1.0, 0.0), axis=1, keepdims=True), axis=2, keepdims=True)

    def halve(_, carry):
        lo, hi = carry
        mid = 0.5 * (lo + hi)
        up = count(x > mid) >= cap
        return jnp.where(up, mid, lo), jnp.where(up, hi, mid)

    lo, hi = lax.fori_loop(0, TOPK_BISECTIONS, halve,
                           (jnp.full((ne, 1, 1), -1.0, F32), jnp.full((ne, 1, 1), 1.0, F32)))
    gt = jnp.where(x > hi, 1.0, 0.0)
    eq = jnp.where(x > lo, 1.0, 0.0) - gt
    need = cap - count(x > hi)
    eq_rank, _ = _prefix_count(eq, u_ref, ones_ref, bl_ref)
    sel = gt + eq * jnp.where(eq_rank < need, 1.0, 0.0)
    pos, _ = _prefix_count(sel, u_ref, ones_ref, bl_ref)
    pos_ref[0] = jnp.where(sel > 0, pos, -1.0).astype(I32)
    nsel = jnp.sum(sel, axis=0)
    nb16 = nsel.astype(BF16)
    incl = _dot(nb16, u_ref[...])
    tot = _dot(nb16, ones_ref[...])
    off = _dot(bl1_ref[...], tot, precision=HIGHEST)
    base_ref[0] = (incl - nsel + off).astype(I32)


def _topk(aff4, cap):
    bsz, ne, nb, ln = aff4.shape
    k = np.arange(ln)
    u = jnp.asarray(k[:, None] <= k[None, :], BF16)
    ones = jnp.ones((ln, ln), BF16)
    r = np.arange(ne * nb)
    bl = jnp.asarray(((r[:, None] // nb) == (r[None, :] // nb)) & (r[None, :] < r[:, None]), BF16)
    r1 = np.arange(nb)
    bl1 = jnp.asarray(r1[None, :] < r1[:, None], F32)
    blk = pl.BlockSpec((1, ne, nb, ln), lambda b: (b, 0, 0, 0))
    const = lambda a: pl.BlockSpec(a.shape, lambda b: (0, 0))
    return pl.pallas_call(
        functools.partial(_topk_body, cap=cap),
        grid=(bsz,),
        in_specs=[blk, const(u), const(ones), const(bl), const(bl1)],
        out_specs=[blk, pl.BlockSpec((1, nb, ln), lambda b: (b, 0, 0))],
        out_shape=[jax.ShapeDtypeStruct((bsz, ne, nb, ln), I32), jax.ShapeDtypeStruct((bsz, nb, ln), I32)],
        compiler_params=_params(("arbitrary",)),
        name="expert_topk",
    )(aff4, u, ones, bl, bl1)


def _sc_setup():
    info = plsc.get_sparse_core_info()
    mesh = plsc.VectorSubcoreMesh(core_axis_name="c", subcore_axis_name="s")
    params = dataclasses.replace(pltpu.CompilerParams(), needs_layout_passes=False)
    return info.num_cores, info.num_cores * info.num_subcores, info.num_lanes, mesh, params


def _sc_worker_id(nc):
    return lax.axis_index("s") * nc + lax.axis_index("c")


def _sc_gather_scratch(width, dtype):
    one = [pltpu.VMEM((SC_GATHER_ROWS,), I32), pltpu.VMEM((SC_GATHER_ROWS, width), dtype), pltpu.SemaphoreType.DMA]
    return one + one


def _sc_gather_chunks(table_hbm, out_hbm, idx_v, bufs, out_row0, n, lanes):
    chunk = SC_GATHER_ROWS

    def gather(c, buf):
        ich_v, rows_v, sem = buf
        for q in range(chunk // lanes):
            src = pl.ds(pl.multiple_of(c * chunk + q * lanes, lanes), lanes)
            ich_v[pl.ds(q * lanes, lanes)] = idx_v[src]
        return pltpu.make_async_copy(table_hbm.at[ich_v], rows_v, sem)

    def finish(c, buf):
        ich_v, rows_v, sem = buf
        pltpu.make_async_copy(table_hbm.at[ich_v], rows_v, sem).wait()
        pltpu.sync_copy(rows_v, out_hbm.at[pl.ds(pl.multiple_of(out_row0 + c * chunk, chunk), chunk)])

    npair = n // (2 * chunk)
    gather(0, bufs[0]).start()

    @pl.loop(0, npair)
    def _(i):
        gather(2 * i + 1, bufs[1]).start()
        finish(2 * i, bufs[0])

        @pl.when(i + 1 < npair)
        def _():
            gather(2 * i + 2, bufs[0]).start()

        finish(2 * i + 1, bufs[1])


def _sc_invert_gather(table, pos_flat, aff_flat, npairs, ne, t, cap):
    nc, nw, lanes, mesh, params = _sc_setup()
    split = max(1, nw // npairs)
    seg = cap // split
    per_w = npairs * split // nw
    width = table.shape[1]

    @functools.partial(
        pl.kernel, mesh=mesh, compiler_params=params,
        out_type=[jax.ShapeDtypeStruct((npairs * cap, width), table.dtype),
                  jax.ShapeDtypeStruct((npairs * cap,), F32)],
        scratch_types=[pltpu.VMEM((t,), I32), pltpu.VMEM((t,), F32), pltpu.VMEM((seg,), I32), pltpu.VMEM((seg,), F32)]
        + _sc_gather_scratch(width, table.dtype),
    )
    def body(table_hbm, pos_hbm, aff_hbm, out_hbm, gate_hbm, pos_v, aff_v, idx_v, gate_v, *g):
        wid = _sc_worker_id(nc)

        @pl.loop(0, per_w)
        def _(kk):
            item = wid * per_w + kk
            p = item // split
            lo = (item % split) * seg
            tok0 = (p // ne) * t
            pltpu.sync_copy(pos_hbm.at[pl.ds(pl.multiple_of(p * t, t), t)], pos_v)
            pltpu.sync_copy(aff_hbm.at[pl.ds(pl.multiple_of(p * t, t), t)], aff_v)

            @pl.loop(0, t // lanes)
            def _(i):
                v = pos_v[pl.ds(pl.multiple_of(i * lanes, lanes), lanes)] - lo
                tok = lax.iota(I32, lanes) + i * lanes
                plsc.store_scatter(idx_v, [v], tok, mask=(v >= 0) & (v < seg))

            @pl.loop(0, seg // lanes)
            def _(j):
                sl = pl.ds(pl.multiple_of(j * lanes, lanes), lanes)
                ii = idx_v[sl]
                gate_v[sl] = plsc.load_gather(aff_v, [ii])
                idx_v[sl] = ii + tok0

            row0 = p * cap + lo
            pltpu.sync_copy(gate_v, gate_hbm.at[pl.ds(pl.multiple_of(row0, seg), seg)])
            _sc_gather_chunks(table_hbm, out_hbm, idx_v, (g[0:3], g[3:6]), row0, seg, lanes)

    return body(table, pos_flat, aff_flat)


def _sc_assignment_order(pos_flat, base_flat, bsz, ne, t, cap):
    nc, nw, lanes, mesh, params = _sc_setup()
    na = ne * cap
    per_b = nw // bsz
    rng = na // per_b

    @functools.partial(
        pl.kernel, mesh=mesh, compiler_params=params,
        out_type=[jax.ShapeDtypeStruct((bsz * na,), I32), jax.ShapeDtypeStruct((bsz * na,), I32)],
        scratch_types=[pltpu.VMEM((t,), I32), pltpu.VMEM((t,), I32), pltpu.VMEM((t,), I32),
                       pltpu.VMEM((rng,), I32), pltpu.VMEM((rng,), I32)],
    )
    def body(pos_hbm, base_hbm, perm_hbm, tok_hbm, pos_v, base_v, rank_v, perm_v, tok_v):
        wid = _sc_worker_id(nc)
        b = wid // per_b
        a0 = (wid % per_b) * rng
        pltpu.sync_copy(base_hbm.at[pl.ds(pl.multiple_of(b * t, t), t)], base_v)

        @pl.loop(0, t // lanes)
        def _(i):
            rank_v[pl.ds(pl.multiple_of(i * lanes, lanes), lanes)] = jnp.zeros((lanes,), I32)

        @pl.loop(0, ne)
        def _(e):
            p = b * ne + e
            pltpu.sync_copy(pos_hbm.at[pl.ds(pl.multiple_of(p * t, t), t)], pos_v)

            @pl.loop(0, t // lanes)
            def _(i):
                sl = pl.ds(pl.multiple_of(i * lanes, lanes), lanes)
                v = pos_v[sl]
                r = rank_v[sl]
                a = base_v[sl] + r - a0
                sel = v >= 0
                mine = sel & (a >= 0) & (a < rng)
                plsc.store_scatter(perm_v, [a], v + p * cap, mask=mine)
                plsc.store_scatter(tok_v, [a], lax.iota(I32, lanes) + i * lanes, mask=mine)
                rank_v[sl] = r + jnp.where(sel, 1, 0)

        dst = pl.ds(pl.multiple_of(b * na + a0, rng), rng)
        pltpu.sync_copy(perm_v, perm_hbm.at[dst])
        pltpu.sync_copy(tok_v, tok_hbm.at[dst])

    return body(pos_flat, base_flat)


def _sc_gather_rows(table, idx):
    nc, nw, lanes, mesh, params = _sc_setup()
    n = idx.shape[0]
    per_w = n // nw
    width = table.shape[1]
    chunk = SC_GATHER_ROWS

    @functools.partial(
        pl.kernel, mesh=mesh, compiler_params=params,
        out_type=jax.ShapeDtypeStruct((n, width), table.dtype),
        scratch_types=[pltpu.VMEM((per_w,), I32)] + _sc_gather_scratch(width, table.dtype),
    )
    def body(table_hbm, idx_hbm, out_hbm, idx_v, *g):
        row0 = _sc_worker_id(nc) * per_w
        pltpu.sync_copy(idx_hbm.at[pl.ds(pl.multiple_of(row0, per_w), per_w)], idx_v)
        _sc_gather_chunks(table_hbm, out_hbm, idx_v, (g[0:3], g[3:6]), row0, per_w, lanes)

    return body(table, idx)


def _moe1_body(xs_ref, gate_ref, wg_ref, wu_ref, wd_ref, y_ref):
    xs = _unpack_halves(xs_ref[0])
    hg = _dot(xs, wg_ref[0].astype(BF16))
    hu = _dot(xs, wu_ref[0].astype(BF16))
    h = (_silu(hg) * hu).astype(BF16)
    y = _dot(h, wd_ref[0].astype(BF16))
    gt = gate_ref[0].T
    y = jnp.concatenate([y[k * 128:(k + 1) * 128] * gt[:, k:k + 1] for k in range(gt.shape[1])], axis=0)
    y_ref[0] = _pack_halves(y)


def _moe_experts(xs, gate, wg, wu, wd, bsz):
    npairs, cap, half = xs.shape
    ne, d, f = wg.shape
    pair = lambda b, e: (b * ne + e, 0, 0)
    expert = lambda b, e: (e, 0, 0)
    return pl.pallas_call(
        _moe1_body,
        grid=(bsz, ne),
        in_specs=[pl.BlockSpec((1, cap, half), pair),
                  pl.BlockSpec((1, cap // 128, 128), pair),
                  pl.BlockSpec((1, d, f), expert),
                  pl.BlockSpec((1, d, f), expert),
                  pl.BlockSpec((1, f, d), expert)],
        out_specs=pl.BlockSpec((1, cap, half), pair),
        out_shape=jax.ShapeDtypeStruct((npairs, cap, half), I32),
        compiler_params=_params(("arbitrary", "arbitrary")),
        cost_estimate=pl.CostEstimate(flops=6 * npairs * cap * d * f, transcendentals=npairs * cap * f,
                                      bytes_accessed=12 * npairs * d * f + 16 * npairs * cap * half),
        name="moe_experts",
    )(xs, gate, wg, wu, wd)


def _combine_body(abase_ref, ys_hbm, tok_ref, x1_ref, g2_ref, fw_ref, o_ref, buf_ref, acc_ref, sem_ref, cnt_ref, *, na):
    n = ONEHOT_BLK
    ring = buf_ref.shape[0]
    nch = na // n
    b, i = pl.program_id(0), pl.program_id(1)

    @pl.when(i == 0)
    def _():
        cnt_ref[0] = 0
        cnt_ref[1] = 0

    lo = abase_ref[b, i]
    hi = abase_ref[b, i + 1]
    c0 = lo // n
    c1 = jnp.where(hi > lo, (hi + n - 1) // n, c0)

    def copy(c):
        slot = c % ring
        row0 = pl.multiple_of(b * na + c * n, n)
        return pltpu.make_async_copy(ys_hbm.at[pl.ds(row0, n)], buf_ref.at[slot], sem_ref.at[slot])

    acc_ref[...] = jnp.zeros(acc_ref.shape, F32)
    tok_ids = lax.broadcasted_iota(I32, (n, n), 0) + i * n

    def step(c, carry):
        started = cnt_ref[0]
        ahead = jnp.minimum(c + ring, nch)

        def start(k, _):
            copy(k).start()
            return 0

        lax.fori_loop(started, ahead, start, 0)
        cnt_ref[0] = jnp.maximum(started, ahead)

        @pl.when(cnt_ref[1] <= c)
        def _():
            copy(c).wait()
            cnt_ref[1] = c + 1

        seg = jnp.where(tok_ids == tok_ref[0, pl.ds(c, 1), :], 1.0, 0.0).astype(BF16)
        acc_ref[...] += _dot(seg, _unpack_halves(buf_ref[c % ring]))
        return carry

    lax.fori_loop(c0, c1, step, 0)
    x2 = x1_ref[0] + g2_ref[0] * acc_ref[...]
    o_ref[0] = x2 * lax.rsqrt(jnp.mean(x2 * x2, axis=-1, keepdims=True) + NORM_EPS) * fw_ref[...]


def _combine_final(abase, ys, tok, x1, g2, fw, na):
    bsz, t, d = x1.shape
    n = ONEHOT_BLK
    grid_spec = pltpu.PrefetchScalarGridSpec(
        num_scalar_prefetch=1,
        grid=(bsz, t // n),
        in_specs=[pl.BlockSpec(memory_space=pl.ANY),
                  pl.BlockSpec((1, na // n, n), lambda b, i, a: (b, 0, 0)),
                  pl.BlockSpec((1, n, d), lambda b, i, a: (b, i, 0)),
                  pl.BlockSpec((1, 1, d), lambda b, i, a: (b, 0, 0)),
                  pl.BlockSpec((1, d), lambda b, i, a: (0, 0))],
        out_specs=pl.BlockSpec((1, n, d), lambda b, i, a: (b, i, 0)),
        scratch_shapes=[pltpu.VMEM((COMBINE_RING, n, d // 2), I32), pltpu.VMEM((n, d), F32),
                        pltpu.SemaphoreType.DMA((COMBINE_RING,)), pltpu.SMEM((2,), I32)],
    )
    return pl.pallas_call(
        functools.partial(_combine_body, na=na),
        grid_spec=grid_spec,
        out_shape=jax.ShapeDtypeStruct((bsz, t, d), F32),
        compiler_params=_params(("arbitrary", "arbitrary")),
        name="combine_final",
    )(abase, ys, tok, x1, g2, fw)


def kernel(x, c, ctx, c_ctx, ada_w, ada_b, norm1_w, w_in, conv_w, conv_b, hg_lb_logits, ml_gate_b,
           hg_norm_w, ml_norm_w, w_out, norm2_w, router_w, exp_w_gate, exp_w_up, exp_w_down, final_norm_w):
    bsz, t, d = x.shape
    nctx = ctx.shape[1]
    assert ada_w.shape[0] == 1, "single-layer block"
    assert nctx == SCAN_STEP and t % SCAN_STEP == 0 and bsz + 1 <= 8
    tall = t + nctx
    nx = t // SCAN_STEP
    kw = HEADS_W
    ne = router_w.shape[-1]
    cap = EC_CAPACITY * t // ne
    tm = min(512, t)

    rows = jnp.concatenate([c, c_ctx[None], jnp.zeros((7 - bsz, d), F32)], axis=0)
    mod = _modulation(rows, ada_w[0], ada_b[0][None])
    mx = [m[:, None, :] for m in jnp.split(mod[:bsz], 6, axis=-1)]
    mc = [m[:, None, :] for m in jnp.split(mod[bsz:bsz + 1], 6, axis=-1)]
    sh1, sc1, g1, sh2, sc2, g2 = mx
    csh1, csc1 = mc[0], mc[1]

    main_w = 9 * kw
    w_main = w_in[0][:, :main_w].astype(BF16)
    w_gt = w_in[0][:, main_w:].T
    gate_b = ml_gate_b[0][:, None]
    nw1 = norm1_w[0][None]
    outs = _inproj(x, sc1, sh1, nw1, w_main, w_gt, gate_b, hg_lb_logits, tall, tm, 0)
    outs = _inproj(ctx, csc1, csh1, nw1, w_main, w_gt, gate_b, hg_lb_logits, tall, nctx, t // nctx, prev=outs)
    hgq, hgv, hgg, hgk, hglf, mlqk_pre, mlv, mlo, gates = outs

    mlqk = _conv(mlqk_pre, conv_w[0], conv_b[0][None], t, nctx)
    hg_f, hg_b = _hgrn2(hgq, hgk, hgv, hglf, nx)
    ml_f, ml_b = _mlstm(mlqk, mlv, gates, nx)

    x1, vpk, aff = _mixer_out(hg_f, hg_b, ml_f, ml_b, hgg, mlo, x, g1, sc2, sh2,
                             hg_norm_w[0][None], ml_norm_w[0][None], w_out[0].astype(BF16),
                             norm2_w[0][None], router_w[0].T, tm)

    pos4, base = _topk(aff.reshape(bsz, ne, t // 128, 128), cap)
    na = ne * cap
    pos_flat = pos4.reshape(-1)
    xs, gate = _sc_invert_gather(vpk.reshape(bsz * t, d // 2), pos_flat, aff.reshape(-1), bsz * ne, ne, t, cap)
    perm, tok = _sc_assignment_order(pos_flat, base.reshape(-1), bsz, ne, t, cap)
    y = _moe_experts(xs.reshape(bsz * ne, cap, d // 2), gate.reshape(bsz * ne, cap // 128, 128),
                     exp_w_gate[0], exp_w_up[0], exp_w_down[0], bsz)
    ys = _sc_gather_rows(y.reshape(bsz * na, d // 2), perm)
    abase = jnp.concatenate([base.reshape(bsz, t)[:, ::ONEHOT_BLK], jnp.full((bsz, 1), na, I32)], axis=1)
    return _combine_final(abase, ys, tok.reshape(bsz, na // ONEHOT_BLK, ONEHOT_BLK), x1, g2, final_norm_w[None], na)
```

```python
import dataclasses
import functools

import numpy as np
import jax
import jax.numpy as jnp
from jax import lax
from jax.experimental import pallas as pl
from jax.experimental.pallas import tpu as pltpu
from jax.experimental.pallas import tpu_sc as plsc

F32 = jnp.float32
BF16 = jnp.bfloat16
I32 = jnp.int32
HIGHEST = lax.Precision.HIGHEST
NORM_EPS = 1e-6

HEAD_DIM = 128
N_HEADS = 4
HEADS_W = N_HEADS * HEAD_DIM
GRID_W = 64
N_EXPERTS = 16
EC_CAPACITY = 2
HG_CHUNK = 64
HG_VPU_LEVEL_MIN = 4
ML_CHUNK = 128
SCAN_STEP = 256
ML_SAMPLES = 2
ONEHOT_BLK = 256
CONV_HALO = 72
SC_GATHER_ROWS = 64
COMBINE_RING = 6
TOPK_BISECTIONS = 64
VMEM_LIMIT = 56 * 1024 * 1024

_NT = (((1,), (1,)), ((), ()))
_TN = (((0,), (0,)), ((), ()))


def _dot(a, b, dims=None, precision=None):
    if dims is None:
        return jnp.dot(a, b, preferred_element_type=F32, precision=precision)
    return lax.dot_general(a, b, dims, preferred_element_type=F32, precision=precision)


def _sigmoid(x):
    return jax.nn.sigmoid(x)


def _pack_halves(x):
    w = x.shape[-1] // 2
    bits = lax.bitcast_convert_type(x.astype(BF16).astype(F32), I32)
    return lax.shift_right_logical(bits[:, :w], 16) | bits[:, w:]


def _unpack_halves(p):
    lo = lax.bitcast_convert_type(lax.shift_left(p, 16), F32)
    hi = lax.bitcast_convert_type(p & jnp.int32(-65536), F32)
    return jnp.concatenate([lo, hi], axis=-1).astype(BF16)


def _silu(x):
    return x * jax.nn.sigmoid(x)


def _params(sem, flags=None):
    return pltpu.CompilerParams(dimension_semantics=sem, vmem_limit_bytes=VMEM_LIMIT, flags=flags)


def _mod_body(r_ref, w_ref, b_ref, o_ref):
    r = r_ref[...]
    o_ref[...] = _dot(_silu(r), w_ref[...], precision=HIGHEST) + b_ref[...]


def _modulation(rows, w, b):
    d, n = w.shape
    tn = n // 4
    return pl.pallas_call(
        _mod_body,
        grid=(n // tn,),
        in_specs=[pl.BlockSpec((8, d), lambda j: (0, 0)),
                  pl.BlockSpec((d, tn), lambda j: (0, j)),
                  pl.BlockSpec((1, tn), lambda j: (0, j))],
        out_specs=pl.BlockSpec((8, tn), lambda j: (0, j)),
        out_shape=jax.ShapeDtypeStruct((8, n), F32),
        compiler_params=_params(("arbitrary",)),
        name="modulation",
    )(rows, w, b)


def _log_sigmoid(x):
    return jnp.minimum(x, 0.0) - jnp.log(1.0 + jnp.exp(-jnp.abs(x)))


def _inproj_body(x_ref, sc_ref, sh_ref, nw_ref, w_ref, wg_ref, gb_ref, lbl_ref, *refs):
    hgq_ref, hgv_ref, hgg_ref, hgk_ref, hglf_ref, mlqk_ref, mlv_ref, mlo_ref, gates_ref = refs[-9:]
    kw = HEADS_W
    x = x_ref[0]
    y = x * lax.rsqrt(jnp.mean(x * x, axis=-1, keepdims=True) + NORM_EPS) * nw_ref[...]
    u = y * (1.0 + sc_ref[0]) + sh_ref[0]
    ub = u.astype(BF16)

    def proj(c0, c1):
        return _dot(ub, w_ref[:, c0:c1])

    hgq_ref[0] = _silu(proj(0, kw)).astype(BF16)
    hgv_ref[0] = proj(kw, 2 * kw).astype(BF16)
    hgg_ref[0] = _silu(proj(2 * kw, 3 * kw)).astype(BF16)

    lbl = lbl_ref[...]
    mx = jnp.max(lbl, axis=0)
    ex = jnp.exp(lbl - mx[None])
    lb = ex[0] / jnp.sum(ex, axis=0)
    for d in range(2):
        p = proj((3 + d) * kw, (4 + d) * kw)
        lbd = lb[d:d + 1]
        f = lbd + (1.0 - lbd) * _sigmoid(p)
        hgk_ref[0, :, d * kw:(d + 1) * kw] = (1.0 - f).astype(BF16)
        hglf_ref[0, :, d * kw:(d + 1) * kw] = jnp.log(f)

    mlqk_ref[0, :, 0:kw] = proj(5 * kw, 6 * kw)
    mlqk_ref[0, :, kw:2 * kw] = proj(6 * kw, 7 * kw)
    mlv_ref[0] = proj(7 * kw, 8 * kw).astype(BF16)
    mlo_ref[0] = _sigmoid(proj(8 * kw, 9 * kw)).astype(BF16)

    g = _dot(wg_ref[...], u, _NT, precision=HIGHEST) + gb_ref[...]
    row = lax.broadcasted_iota(I32, g.shape, 0)
    gates_ref[0] = jnp.where((row % 8) >= N_HEADS, _log_sigmoid(g), g)


def _inproj_shapes(bsz, tall):
    kw = HEADS_W
    return [
        jax.ShapeDtypeStruct((bsz, tall, kw), BF16),
        jax.ShapeDtypeStruct((bsz, tall, kw), BF16),
        jax.ShapeDtypeStruct((bsz, tall, kw), BF16),
        jax.ShapeDtypeStruct((bsz, tall, 2 * kw), BF16),
        jax.ShapeDtypeStruct((bsz, tall, 2 * kw), F32),
        jax.ShapeDtypeStruct((bsz, tall, 2 * kw), F32),
        jax.ShapeDtypeStruct((bsz, tall, kw), BF16),
        jax.ShapeDtypeStruct((bsz, tall, kw), BF16),
        jax.ShapeDtypeStruct((bsz, 4 * N_HEADS, tall), F32),
    ]


def _inproj(tokens, scale, shift, nw, w_main, w_gt, gate_b, lb_logits, tall, tm, blk0, prev=None):
    bsz, n, d = tokens.shape
    kw = HEADS_W
    nt = n // tm
    per_sample = scale.shape[0] == bsz
    mod_map = (lambda b, i: (b, 0, 0)) if per_sample else (lambda b, i: (0, 0, 0))
    const2 = lambda b, i: (0, 0)
    in_specs = [
        pl.BlockSpec((1, tm, d), lambda b, i: (b, i, 0)),
        pl.BlockSpec((1, 1, d), mod_map),
        pl.BlockSpec((1, 1, d), mod_map),
        pl.BlockSpec((1, d), const2),
        pl.BlockSpec(w_main.shape, const2),
        pl.BlockSpec(w_gt.shape, const2),
        pl.BlockSpec(gate_b.shape, const2),
        pl.BlockSpec(lb_logits.shape, lambda b, i: (0, 0, 0)),
    ]
    args = [tokens, scale, shift, nw, w_main, w_gt, gate_b, lb_logits]
    aliases = {}
    if prev is not None:
        for k, a in enumerate(prev):
            in_specs.append(pl.BlockSpec(memory_space=pl.ANY))
            aliases[len(args)] = k
            args.append(a)
    row_map = lambda b, i: (b, blk0 + i, 0)
    widths = [kw, kw, kw, 2 * kw, 2 * kw, 2 * kw, kw, kw]
    out_specs = [pl.BlockSpec((1, tm, w), row_map) for w in widths]
    out_specs.append(pl.BlockSpec((1, 4 * N_HEADS, tm), lambda b, i: (b, 0, blk0 + i)))
    return pl.pallas_call(
        _inproj_body,
        grid=(bsz, nt),
        in_specs=in_specs,
        out_specs=out_specs,
        out_shape=_inproj_shapes(bsz, tall),
        input_output_aliases=aliases,
        compiler_params=_params(("arbitrary", "arbitrary")),
        name="inproj_ctx" if prev is not None else "inproj_x",
    )(*args)


def _conv_body(x_ref, w_ref, b_ref, o_ref, pad_ref, cpad_ref, *, t, nctx, scale_from):
    halo = CONV_HALO
    rows = 512
    win = rows + 2 * halo
    ch = x_ref.shape[-1]
    scale = jnp.where(pl.program_id(1) >= scale_from, HEAD_DIM ** -0.5, 1.0).astype(F32)
    w = w_ref[...]
    bias = b_ref[...]

    pad_ref[0:halo, :] = jnp.zeros((halo, ch), F32)
    pad_ref[halo + t:halo + t + halo, :] = jnp.zeros((halo, ch), F32)
    pad_ref[halo:halo + t, :] = x_ref[0, 0:t, :]
    col = (lax.broadcasted_iota(I32, (win, ch), 0) + (GRID_W - halo % GRID_W)) % GRID_W
    left_ok = col > 0
    right_ok = col < GRID_W - 1

    def chunk(c, carry):
        o = pl.multiple_of(c * rows, rows)
        xw = pad_ref[pl.ds(o, win), :]
        xm = jnp.where(left_ok, pltpu.roll(xw, 1, 0), 0.0)
        xp = jnp.where(right_ok, pltpu.roll(xw, win - 1, 0), 0.0)
        def taps(dr, lo):
            sl = slice(lo, lo + rows)
            return xm[sl] * w[dr, 0:1] + xw[sl] * w[dr, 1:2] + xp[sl] * w[dr, 2:3]

        y = taps(1, halo) + taps(0, halo - GRID_W) + taps(2, halo + GRID_W)
        o_ref[0, pl.ds(o, rows), :] = (_silu(y + bias) * scale).astype(o_ref.dtype)
        return carry

    lax.fori_loop(0, t // rows, chunk, 0)

    cpad_ref[0:8, :] = jnp.zeros((8, ch), F32)
    cpad_ref[8 + nctx:16 + nctx, :] = jnp.zeros((8, ch), F32)
    cpad_ref[8:8 + nctx, :] = x_ref[0, t:t + nctx, :]
    xw = cpad_ref[...]
    n = nctx + 16
    y = (pltpu.roll(xw, 1, 0) * w[1, 0:1] + xw * w[1, 1:2] + pltpu.roll(xw, n - 1, 0) * w[1, 2:3])[8:8 + nctx]
    o_ref[0, t:t + nctx, :] = (_silu(y + bias) * scale).astype(o_ref.dtype)


def _conv(qk_pre, conv_w, conv_b, t, nctx):
    bsz, tall, c = qk_pre.shape
    ch = 128
    body = functools.partial(_conv_body, t=t, nctx=nctx, scale_from=(c // 2) // ch)
    return pl.pallas_call(
        body,
        grid=(bsz, c // ch),
        in_specs=[pl.BlockSpec((1, tall, ch), lambda b, j: (b, 0, j)),
                  pl.BlockSpec((3, 3, ch), lambda b, j: (0, 0, j)),
                  pl.BlockSpec((1, ch), lambda b, j: (0, j))],
        out_specs=pl.BlockSpec((1, tall, ch), lambda b, j: (b, 0, j)),
        out_shape=jax.ShapeDtypeStruct((bsz, tall, c), BF16),
        scratch_shapes=[pltpu.VMEM((t + 2 * CONV_HALO, ch), F32),
                        pltpu.VMEM((nctx + 16, ch), F32)],
        compiler_params=_params(("arbitrary", "arbitrary")),
        name="qk_conv",
    )(qk_pre, conv_w, conv_b)


def _fwd_blk(s, nx):
    return jnp.where(s == 0, nx, s - 1)


def _bwd_blk(s, nx):
    return jnp.where(s == 0, nx, nx - s)


def _hg_constants(rev):
    c = HG_CHUNK
    i = np.arange(c)[:, None]
    j = np.arange(c)[None, :]
    blocks = [(j >= i) if rev else (j <= i)]
    masks = [i == j]
    m = c // 2
    while m >= 1:
        b0 = (i // (2 * m)) * (2 * m)
        same = (i // (2 * m)) == (j // (2 * m))
        if rev:
            beta = b0 + m
            qrow = (i % (2 * m)) < m
            g = np.where(qrow, (j >= i) & (j < beta), (j >= beta) & (j < i))
            mask = same & qrow & ((j % (2 * m)) >= m)
        else:
            beta = b0 + m - 1
            qrow = (i % (2 * m)) >= m
            g = np.where(qrow, (j > beta) & (j <= i), (j > i) & (j <= beta))
            mask = same & qrow & ((j % (2 * m)) < m)
        if m < HG_VPU_LEVEL_MIN:
            blocks.append(g)
        masks.append(mask)
        m //= 2
    g = np.concatenate(blocks, axis=0).astype(np.float32)
    g3 = np.concatenate([g, g, g], axis=1)
    m2 = np.concatenate([np.stack(masks), np.stack(masks)], axis=2)
    return (jnp.asarray(g3, BF16), jnp.asarray(m2, F32))


def _hg_level_decay(a, m, rev):
    c = a.shape[0]
    parts = []
    for b0 in range(0, c, 2 * m):
        beta = b0 + m if rev else b0 + m - 1
        d = a[b0:b0 + 2 * m] - a[beta:beta + 1]
        parts.append(jnp.minimum(d, -d))
    return jnp.concatenate(parts, axis=0) if len(parts) > 1 else parts[0]


def _block_diag(x, zero):
    w = x.shape[1] // 2
    return jnp.concatenate([jnp.concatenate([x[:, :w], zero], axis=1),
                            jnp.concatenate([zero, x[:, w:]], axis=1)], axis=0)


def _hg_chunk(dirs):
    c = HG_CHUNK
    w = 2 * HEAD_DIM
    zero = jnp.zeros((c, HEAD_DIM), BF16)
    units = []
    for rev, r0, q_ref, k_ref, v_ref, lf_ref, g_ref, msk_ref, o_ref, st_ref in dirs:
        rows = pl.ds(r0, c)
        lf = lf_ref[0, rows, :]
        p1 = lf.astype(BF16)
        r1 = lf - p1.astype(F32)
        p2 = r1.astype(BF16)
        p3 = (r1 - p2.astype(F32)).astype(BF16)
        dall = _dot(g_ref[...], jnp.concatenate([p1, p2, p3], axis=0))
        for hp in range(N_HEADS // 2):
            cs = slice(hp * w, (hp + 1) * w)
            units.append(dict(rev=rev, rows=rows, cs=cs, hp=hp, msk_ref=msk_ref, o_ref=o_ref, st_ref=st_ref,
                              q=q_ref[0, rows, cs], k=k_ref[0, rows, cs], v=v_ref[0, rows, cs], dall=dall[:, cs]))
    for u in units:
        msk_ref = u["msk_ref"]
        nlev = msk_ref.shape[0] - 1
        att = _dot(u["q"], _block_diag(u["k"], zero), _NT) * msk_ref[0]
        a = u["dall"][0:c]
        row = 1
        for l in range(nlev):
            m = c >> (l + 1)
            if m >= HG_VPU_LEVEL_MIN:
                dec = _hg_level_decay(a, m, u["rev"])
            else:
                dec = u["dall"][row * c:(row + 1) * c]
                row += 1
            e = jnp.exp(dec).astype(BF16)
            att = att + _dot(u["q"] * e, _block_diag(u["k"] * e, zero), _NT) * msk_ref[l + 1]
        u["att"] = att.astype(BF16)
    for u in units:
        a = u["dall"][0:c]
        u["a_tot"] = a[0:1] if u["rev"] else a[c - 1:c]
        u["st"] = [u["st_ref"][2 * u["hp"] + i] for i in range(2)]
        zf = jnp.zeros((HEAD_DIM, HEAD_DIM), BF16)
        st2 = jnp.concatenate([jnp.concatenate([u["st"][0].astype(BF16), zf], axis=1),
                               jnp.concatenate([zf, u["st"][1].astype(BF16)], axis=1)], axis=0)
        qbar = (u["q"].astype(F32) * jnp.exp(a)).astype(BF16)
        u["o"] = _dot(u["att"], _block_diag(u["v"], zero)) + _dot(qbar, st2, _NT)
        u["khat"] = (u["k"].astype(F32) * jnp.exp(u["a_tot"] - a)).astype(BF16)
    for u in units:
        u["o_ref"][0, u["rows"], u["cs"]] = u["o"].astype(BF16)
        for i in range(2):
            hs = slice(i * HEAD_DIM, (i + 1) * HEAD_DIM)
            upd = _dot(u["v"][:, hs], u["khat"][:, hs], _TN)
            u["st_ref"][2 * u["hp"] + i] = u["st"][i] * jnp.exp(u["a_tot"][:, hs]) + upd


def _hg_body(qf_ref, kf_ref, vf_ref, lff_ref, qb_ref, kb_ref, vb_ref, lfb_ref,
             gf_ref, mf_ref, gb_ref, mb_ref, of_ref, ob_ref, st_ref):
    @pl.when(pl.program_id(1) == 0)
    def _():
        st_ref[...] = jnp.zeros(st_ref.shape, F32)

    nsub = SCAN_STEP // HG_CHUNK
    for c in range(nsub):
        _hg_chunk([
            (False, c * HG_CHUNK, qf_ref, kf_ref, vf_ref, lff_ref, gf_ref, mf_ref, of_ref, st_ref.at[0]),
            (True, (nsub - 1 - c) * HG_CHUNK, qb_ref, kb_ref, vb_ref, lfb_ref, gb_ref, mb_ref, ob_ref, st_ref.at[1]),
        ])


def _hgrn2(hgq, hgk, hgv, hglf, nx):
    bsz, tall, kw = hgq.shape
    steps = tall // SCAN_STEP
    cf = _hg_constants(False)
    cb = _hg_constants(True)
    blk = (1, SCAN_STEP, kw)
    fwd = lambda col: (lambda b, s: (b, _fwd_blk(s, nx), col))
    bwd = lambda col: (lambda b, s: (b, _bwd_blk(s, nx), col))
    const = lambda a: pl.BlockSpec(a.shape, lambda b, s: (0,) * a.ndim)
    in_specs = [pl.BlockSpec(blk, fwd(0)), pl.BlockSpec(blk, fwd(0)), pl.BlockSpec(blk, fwd(0)), pl.BlockSpec(blk, fwd(0)),
                pl.BlockSpec(blk, bwd(0)), pl.BlockSpec(blk, bwd(1)), pl.BlockSpec(blk, bwd(0)), pl.BlockSpec(blk, bwd(1))]
    in_specs += [const(a) for a in cf + cb]
    out_sds = jax.ShapeDtypeStruct((bsz, tall, kw), BF16)
    return pl.pallas_call(
        _hg_body,
        grid=(bsz, steps),
        in_specs=in_specs,
        out_specs=[pl.BlockSpec(blk, fwd(0)), pl.BlockSpec(blk, bwd(0))],
        out_shape=[out_sds, out_sds],
        scratch_shapes=[pltpu.VMEM((2, N_HEADS, HEAD_DIM, HEAD_DIM), F32)],
        compiler_params=_params(("arbitrary", "arbitrary")),
        name="hgrn2_scan",
    )(hgq, hgk, hgv, hglf, hgq, hgk, hgv, hglf, *cf, *cb)


def _ml_constants(rev):
    k = np.arange(ML_CHUNK)
    tri = (k[:, None] >= k[None, :]) if rev else (k[:, None] <= k[None, :])
    return jnp.asarray(np.concatenate([tri, tri, tri], axis=0), BF16)


def _ml_chunk(dirs):
    c = ML_CHUNK
    ii = lax.broadcasted_iota(I32, (c, c), 0)
    jj = lax.broadcasted_iota(I32, (c, c), 1)
    ones = jnp.ones((c, HEAD_DIM), BF16)
    units = []
    for rev, r0, d, bb, q_ref, k_ref, v_ref, g_ref, tri3_ref, o_ref, st_ref, m_ref in dirs:
        gates = g_ref[bb, :, pl.ds(r0, c)]
        p1 = gates.astype(BF16)
        r1 = gates - p1.astype(F32)
        p2 = r1.astype(BF16)
        p3 = (r1 - p2.astype(F32)).astype(BF16)
        csum = _dot(jnp.concatenate([p1, p2, p3], axis=1), tri3_ref[...])
        for h in range(N_HEADS):
            cs = slice(h * HEAD_DIM, (h + 1) * HEAD_DIM)
            u = dict(rev=rev, o_ref=o_ref, bb=bb, rows=pl.ds(r0, c), cs=cs, st_ref=st_ref, m_ref=m_ref, h=h)
            u["qb"] = q_ref[bb, pl.ds(r0, c), cs]
            u["kb"] = k_ref[bb, pl.ds(r0, c), cs]
            u["v1"] = jnp.concatenate([v_ref[bb, pl.ds(r0, c), cs], ones], axis=1)
            irow = gates[d * 8 + h:d * 8 + h + 1]
            u["brow"] = csum[d * 8 + N_HEADS + h:d * 8 + N_HEADS + h + 1]
            u["rrow"] = irow - u["brow"]
            units.append(u)
    for u in units:
        u["st"] = u["st_ref"][u["h"]]
        a = _dot(jnp.concatenate([u["kb"], u["st"].astype(BF16)], axis=0), u["qb"], _NT)
        u["s"] = a[:c]
        u["sq"] = a[c:]
    for u in units:
        last = 0 if u["rev"] else c - 1
        tri_t = (ii >= jj) if u["rev"] else (ii <= jj)
        rcol = jnp.concatenate([u["rrow"], jnp.zeros((7, c), F32)], axis=0).T[:, 0:1]
        u["mprev"] = u["m_ref"][u["h"]][:, 0:1]
        rmat = jnp.where(tri_t, rcol, -jnp.inf)
        u["grow"] = jnp.maximum(jnp.max(rmat, axis=0, keepdims=True), u["mprev"])
        qk = (u["s"] * jnp.exp(rmat - u["grow"])).astype(BF16)
        blast = u["brow"][:, last:last + 1]
        u["mnew"] = blast + u["grow"][:, last:last + 1]
        kh = (u["kb"].astype(F32) * jnp.exp(blast + rcol - u["mnew"])).astype(BF16)
        u["ws"] = jnp.exp(blast + u["mprev"] - u["mnew"])
        u["qkh"] = jnp.concatenate([qk, kh], axis=1)
    for u in units:
        u["nu"] = _dot(u["v1"], u["qkh"], _TN)
    for u in units:
        both = u["nu"][:, :c] + jnp.exp(u["mprev"] - u["grow"]) * u["sq"]
        den = both[HEAD_DIM:HEAD_DIM + 1]
        inv = 1.0 / jnp.maximum(jnp.abs(den), jnp.exp(-(u["brow"] + u["grow"])))
        u["o_ref"][u["bb"], u["rows"], u["cs"]] = (both[:HEAD_DIM] * inv).T.astype(BF16)
        u["st_ref"][u["h"]] = u["ws"] * u["st"] + u["nu"][:, c:]
        u["m_ref"][u["h"]] = jnp.broadcast_to(u["mnew"], (1, HEAD_DIM))


def _ml_body(qf_ref, kf_ref, vf_ref, gf_ref, qb_ref, kb_ref, vb_ref, gb_ref, tf_ref, tb_ref,
             of_ref, ob_ref, st_ref, m_ref):
    @pl.when(pl.program_id(1) == 0)
    def _():
        st_ref[...] = jnp.zeros(st_ref.shape, F32)
        m_ref[...] = jnp.zeros(m_ref.shape, F32)

    nsub = SCAN_STEP // ML_CHUNK
    for c in range(nsub):
        dirs = []
        for bb in range(qf_ref.shape[0]):
            dirs.append((False, c * ML_CHUNK, 0, bb, qf_ref, kf_ref, vf_ref, gf_ref, tf_ref, of_ref,
                         st_ref.at[0, bb], m_ref.at[0, bb]))
            dirs.append((True, (nsub - 1 - c) * ML_CHUNK, 1, bb, qb_ref, kb_ref, vb_ref, gb_ref, tb_ref, ob_ref,
                         st_ref.at[1, bb], m_ref.at[1, bb]))
        _ml_chunk(dirs)


def _mlstm(mlqk, mlv, gates, nx):
    bsz, tall, kw = mlv.shape
    steps = tall // SCAN_STEP
    nb = ML_SAMPLES if bsz % ML_SAMPLES == 0 else 1
    blk = (nb, SCAN_STEP, kw)
    gblk = (nb, 4 * N_HEADS, SCAN_STEP)
    tf, tb = _ml_constants(False), _ml_constants(True)
    fwd = lambda col: (lambda b, s: (b, _fwd_blk(s, nx), col))
    bwd = lambda col: (lambda b, s: (b, _bwd_blk(s, nx), col))
    const = pl.BlockSpec(tf.shape, lambda b, s: (0, 0))
    in_specs = [pl.BlockSpec(blk, fwd(0)), pl.BlockSpec(blk, fwd(1)), pl.BlockSpec(blk, fwd(0)),
                pl.BlockSpec(gblk, lambda b, s: (b, 0, _fwd_blk(s, nx))),
                pl.BlockSpec(blk, bwd(0)), pl.BlockSpec(blk, bwd(1)), pl.BlockSpec(blk, bwd(0)),
                pl.BlockSpec(gblk, lambda b, s: (b, 0, _bwd_blk(s, nx))), const, const]
    out_sds = jax.ShapeDtypeStruct((bsz, tall, kw), BF16)
    return pl.pallas_call(
        _ml_body,
        grid=(bsz // nb, steps),
        in_specs=in_specs,
        out_specs=[pl.BlockSpec(blk, fwd(0)), pl.BlockSpec(blk, bwd(0))],
        out_shape=[out_sds, out_sds],
        scratch_shapes=[pltpu.VMEM((2, nb, N_HEADS, 2 * HEAD_DIM, HEAD_DIM), F32),
                        pltpu.VMEM((2, nb, N_HEADS, 1, HEAD_DIM), F32)],
        compiler_params=_params(("arbitrary", "arbitrary")),
        name="mlstm_scan",
    )(mlqk, mlqk, mlv, gates, mlqk, mlqk, mlv, gates, tf, tb)


def _out_body(hof_ref, hob_ref, mhf_ref, mhb_ref, hgg_ref, mlo_ref, x_ref, g1_ref, sc2_ref, sh2_ref,
              hnw_ref, mnw_ref, wout_ref, n2w_ref, rwt_ref, x1_ref, vt_ref, aff_ref):
    hg = hof_ref[0].astype(F32) + hob_ref[0].astype(F32)
    ml = mhf_ref[0].astype(F32) + mhb_ref[0].astype(F32)
    hparts, mparts = [], []
    for h in range(N_HEADS):
        cs = slice(h * HEAD_DIM, (h + 1) * HEAD_DIM)
        t = hg[:, cs]
        hparts.append(t * lax.rsqrt(jnp.mean(t * t, axis=-1, keepdims=True) + NORM_EPS))
        t = ml[:, cs]
        t = t - jnp.mean(t, axis=-1, keepdims=True)
        mparts.append(t * lax.rsqrt(jnp.mean(t * t, axis=-1, keepdims=True) + NORM_EPS))
    hgn = jnp.concatenate(hparts, axis=-1) * hnw_ref[...] * hgg_ref[0].astype(F32)
    mln = jnp.concatenate(mparts, axis=-1) * mnw_ref[...] * mlo_ref[0].astype(F32)
    mix = jnp.concatenate([hgn, mln], axis=-1).astype(BF16)
    x1 = x_ref[0] + g1_ref[0] * _dot(mix, wout_ref[...])
    x1_ref[0] = x1
    v = x1 * lax.rsqrt(jnp.mean(x1 * x1, axis=-1, keepdims=True) + NORM_EPS) * n2w_ref[...]
    v = v * (1.0 + sc2_ref[0]) + sh2_ref[0]
    vt_ref[0] = _pack_halves(v)
    v1 = v.astype(BF16)
    v2 = (v - v1.astype(F32)).astype(BF16)
    w = rwt_ref[...]
    ne = w.shape[0]
    w1 = w.astype(BF16)
    wr = w - w1.astype(F32)
    w2 = wr.astype(BF16)
    w3 = (wr - w2.astype(F32)).astype(BF16)
    pa = _dot(jnp.concatenate([w1, w2, w3], axis=0), v1, _NT)
    pb = _dot(jnp.concatenate([w1, w2], axis=0), v2, _NT)
    logits = pa[0:ne] + pa[ne:2 * ne] + pa[2 * ne:] + pb[0:ne] + pb[ne:]
    ex = jnp.exp(logits - jnp.max(logits, axis=0, keepdims=True))
    aff_ref[0] = ex / jnp.sum(ex, axis=0, keepdims=True)


def _mixer_out(hg_f, hg_b, ml_f, ml_b, hgg, mlo, x, g1, sc2, sh2, hnw, mnw, w_out, n2w, rwt, tm):
    bsz, t, d = x.shape
    kw = HEADS_W
    ne = rwt.shape[0]
    row = lambda b, i: (b, i, 0)
    mod = lambda b, i: (b, 0, 0)
    const2 = lambda b, i: (0, 0)
    act = pl.BlockSpec((1, tm, kw), row)
    in_specs = [act, act, act, act, act, act,
                pl.BlockSpec((1, tm, d), row),
                pl.BlockSpec((1, 1, d), mod), pl.BlockSpec((1, 1, d), mod), pl.BlockSpec((1, 1, d), mod),
                pl.BlockSpec((1, kw), const2), pl.BlockSpec((1, kw), const2),
                pl.BlockSpec(w_out.shape, const2), pl.BlockSpec((1, d), const2), pl.BlockSpec(rwt.shape, const2)]
    return pl.pallas_call(
        _out_body,
        grid=(bsz, t // tm),
        in_specs=in_specs,
        out_specs=[pl.BlockSpec((1, tm, d), row),
                   pl.BlockSpec((1, tm, d // 2), row),
                   pl.BlockSpec((1, ne, tm), lambda b, i: (b, 0, i))],
        out_shape=[jax.ShapeDtypeStruct((bsz, t, d), F32),
                   jax.ShapeDtypeStruct((bsz, t, d // 2), I32),
                   jax.ShapeDtypeStruct((bsz, ne, t), F32)],
        compiler_params=_params(("arbitrary", "arbitrary")),
        name="mixer_out",
    )(hg_f, hg_b, ml_f, ml_b, hgg, mlo, x, g1, sc2, sh2, hnw, mnw, w_out, n2w, rwt)


def _prefix_count(maskf, u_ref, ones_ref, bl_ref):
    e, nb, ln = maskf.shape
    x = maskf.reshape(e * nb, ln)
    xb = x.astype(BF16)
    incl = _dot(xb, u_ref[...])
    tot = _dot(xb, ones_ref[...])
    off = _dot(bl_ref[...], tot.astype(BF16))
    return (incl - x + off).reshape(e, nb, ln), off.reshape(e, nb, ln)


def _topk_body(aff_ref, u_ref, ones_ref, bl_ref, bl1_ref, pos_ref, base_ref, *, cap):
    x = aff_ref[0]
    ne = x.shape[0]

    def count(m):
        return jnp.sum(jnp.sum(jnp.where(m, 1.0, 0.0), axis=1, keepdims=True), axis=2, keepdims=True)

    def halve(_, carry):
        lo, hi = carry
        mid = 0.5 * (lo + hi)
        up = count(x > mid) >= cap
        return jnp.where(up, mid, lo), jnp.where(up, hi, mid)

    lo, hi = lax.fori_loop(0, TOPK_BISECTIONS, halve,
                           (jnp.full((ne, 1, 1), -1.0, F32), jnp.full((ne, 1, 1), 1.0, F32)))
    gt = jnp.where(x > hi, 1.0, 0.0)
    eq = jnp.where(x > lo, 1.0, 0.0) - gt
    need = cap - count(x > hi)
    eq_rank, _ = _prefix_count(eq, u_ref, ones_ref, bl_ref)
    sel = gt + eq * jnp.where(eq_rank < need, 1.0, 0.0)
    pos, _ = _prefix_count(sel, u_ref, ones_ref, bl_ref)
    pos_ref[0] = jnp.where(sel > 0, pos, -1.0).astype(I32)
    nsel = jnp.sum(sel, axis=0)
    nb16 = nsel.astype(BF16)
    incl = _dot(nb16, u_ref[...])
    tot = _dot(nb16, ones_ref[...])
    off = _dot(bl1_ref[...], tot, precision=HIGHEST)
    base_ref[0] = (incl - nsel + off).astype(I32)


def _topk(aff4, cap):
    bsz, ne, nb, ln = aff4.shape
    k = np.arange(ln)
    u = jnp.asarray(k[:, None] <= k[None, :], BF16)
    ones = jnp.ones((ln, ln), BF16)
    r = np.arange(ne * nb)
    bl = jnp.asarray(((r[:, None] // nb) == (r[None, :] // nb)) & (r[None, :] < r[:, None]), BF16)
    r1 = np.arange(nb)
    bl1 = jnp.asarray(r1[None, :] < r1[:, None], F32)
    blk = pl.BlockSpec((1, ne, nb, ln), lambda b: (b, 0, 0, 0))
    const = lambda a: pl.BlockSpec(a.shape, lambda b: (0, 0))
    return pl.pallas_call(
        functools.partial(_topk_body, cap=cap),
        grid=(bsz,),
        in_specs=[blk, const(u), const(ones), const(bl), const(bl1)],
        out_specs=[blk, pl.BlockSpec((1, nb, ln), lambda b: (b, 0, 0))],
        out_shape=[jax.ShapeDtypeStruct((bsz, ne, nb, ln), I32), jax.ShapeDtypeStruct((bsz, nb, ln), I32)],
        compiler_params=_params(("arbitrary",)),
        name="expert_topk",
    )(aff4, u, ones, bl, bl1)


def _sc_setup():
    info = plsc.get_sparse_core_info()
    mesh = plsc.VectorSubcoreMesh(core_axis_name="c", subcore_axis_name="s")
    params = dataclasses.replace(pltpu.CompilerParams(), needs_layout_passes=False)
    return info.num_cores, info.num_cores * info.num_subcores, info.num_lanes, mesh, params


def _sc_worker_id(nc):
    return lax.axis_index("s") * nc + lax.axis_index("c")


def _sc_gather_scratch(width, dtype):
    one = [pltpu.VMEM((SC_GATHER_ROWS,), I32), pltpu.VMEM((SC_GATHER_ROWS, width), dtype), pltpu.SemaphoreType.DMA]
    return one + one


def _sc_gather_chunks(table_hbm, out_hbm, idx_v, bufs, out_row0, n, lanes):
    chunk = SC_GATHER_ROWS

    def gather(c, buf):
        ich_v, rows_v, sem = buf
        for q in range(chunk // lanes):
            src = pl.ds(pl.multiple_of(c * chunk + q * lanes, lanes), lanes)
            ich_v[pl.ds(q * lanes, lanes)] = idx_v[src]
        return pltpu.make_async_copy(table_hbm.at[ich_v], rows_v, sem)

    def finish(c, buf):
        ich_v, rows_v, sem = buf
        pltpu.make_async_copy(table_hbm.at[ich_v], rows_v, sem).wait()
        pltpu.sync_copy(rows_v, out_hbm.at[pl.ds(pl.multiple_of(out_row0 + c * chunk, chunk), chunk)])

    npair = n // (2 * chunk)
    gather(0, bufs[0]).start()

    @pl.loop(0, npair)
    def _(i):
        gather(2 * i + 1, bufs[1]).start()
        finish(2 * i, bufs[0])

        @pl.when(i + 1 < npair)
        def _():
            gather(2 * i + 2, bufs[0]).start()

        finish(2 * i + 1, bufs[1])


def _sc_invert_gather(table, pos_flat, aff_flat, npairs, ne, t, cap):
    nc, nw, lanes, mesh, params = _sc_setup()
    split = max(1, nw // npairs)
    seg = cap // split
    per_w = npairs * split // nw
    width = table.shape[1]

    @functools.partial(
        pl.kernel, mesh=mesh, compiler_params=params,
        out_type=[jax.ShapeDtypeStruct((npairs * cap, width), table.dtype),
                  jax.ShapeDtypeStruct((npairs * cap,), F32)],
        scratch_types=[pltpu.VMEM((t,), I32), pltpu.VMEM((t,), F32), pltpu.VMEM((seg,), I32), pltpu.VMEM((seg,), F32)]
        + _sc_gather_scratch(width, table.dtype),
    )
    def body(table_hbm, pos_hbm, aff_hbm, out_hbm, gate_hbm, pos_v, aff_v, idx_v, gate_v, *g):
        wid = _sc_worker_id(nc)

        @pl.loop(0, per_w)
        def _(kk):
            item = wid * per_w + kk
            p = item // split
            lo = (item % split) * seg
            tok0 = (p // ne) * t
            pltpu.sync_copy(pos_hbm.at[pl.ds(pl.multiple_of(p * t, t), t)], pos_v)
            pltpu.sync_copy(aff_hbm.at[pl.ds(pl.multiple_of(p * t, t), t)], aff_v)

            @pl.loop(0, t // lanes)
            def _(i):
                v = pos_v[pl.ds(pl.multiple_of(i * lanes, lanes), lanes)] - lo
                tok = lax.iota(I32, lanes) + i * lanes
                plsc.store_scatter(idx_v, [v], tok, mask=(v >= 0) & (v < seg))

            @pl.loop(0, seg // lanes)
            def _(j):
                sl = pl.ds(pl.multiple_of(j * lanes, lanes), lanes)
                ii = idx_v[sl]
                gate_v[sl] = plsc.load_gather(aff_v, [ii])
                idx_v[sl] = ii + tok0

            row0 = p * cap + lo
            pltpu.sync_copy(gate_v, gate_hbm.at[pl.ds(pl.multiple_of(row0, seg), seg)])
            _sc_gather_chunks(table_hbm, out_hbm, idx_v, (g[0:3], g[3:6]), row0, seg, lanes)

    return body(table, pos_flat, aff_flat)


def _sc_assignment_order(pos_flat, base_flat, bsz, ne, t, cap):
    nc, nw, lanes, mesh, params = _sc_setup()
    na = ne * cap
    per_b = nw // bsz
    rng = na // per_b

    @functools.partial(
        pl.kernel, mesh=mesh, compiler_params=params,
        out_type=[jax.ShapeDtypeStruct((bsz * na,), I32), jax.ShapeDtypeStruct((bsz * na,), I32)],
        scratch_types=[pltpu.VMEM((t,), I32), pltpu.VMEM((t,), I32), pltpu.VMEM((t,), I32),
                       pltpu.VMEM((rng,), I32), pltpu.VMEM((rng,), I32)],
    )
    def body(pos_hbm, base_hbm, perm_hbm, tok_hbm, pos_v, base_v, rank_v, perm_v, tok_v):
        wid = _sc_worker_id(nc)
        b = wid // per_b
        a0 = (wid % per_b) * rng
        pltpu.sync_copy(base_hbm.at[pl.ds(pl.multiple_of(b * t, t), t)], base_v)

        @pl.loop(0, t // lanes)
        def _(i):
            rank_v[pl.ds(pl.multiple_of(i * lanes, lanes), lanes)] = jnp.zeros((lanes,), I32)

        @pl.loop(0, ne)
        def _(e):
            p = b * ne + e
            pltpu.sync_copy(pos_hbm.at[pl.ds(pl.multiple_of(p * t, t), t)], pos_v)

            @pl.loop(0, t // lanes)
            def _(i):
                sl = pl.ds(pl.multiple_of(i * lanes, lanes), lanes)
                v = pos_v[sl]
                r = rank_v[sl]
                a = base_v[sl] + r - a0
                sel = v >= 0
                mine = sel & (a >= 0) & (a < rng)
                plsc.store_scatter(perm_v, [a], v + p * cap, mask=mine)
                plsc.store_scatter(tok_v, [a], lax.iota(I32, lanes) + i * lanes, mask=mine)
                rank_v[sl] = r + jnp.where(sel, 1, 0)

        dst = pl.ds(pl.multiple_of(b * na + a0, rng), rng)
        pltpu.sync_copy(perm_v, perm_hbm.at[dst])
        pltpu.sync_copy(tok_v, tok_hbm.at[dst])

    return body(pos_flat, base_flat)


def _sc_gather_rows(table, idx):
    nc, nw, lanes, mesh, params = _sc_setup()
    n = idx.shape[0]
    per_w = n // nw
    width = table.shape[1]
    chunk = SC_GATHER_ROWS

    @functools.partial(
        pl.kernel, mesh=mesh, compiler_params=params,
        out_type=jax.ShapeDtypeStruct((n, width), table.dtype),
        scratch_types=[pltpu.VMEM((per_w,), I32)] + _sc_gather_scratch(width, table.dtype),
    )
    def body(table_hbm, idx_hbm, out_hbm, idx_v, *g):
        row0 = _sc_worker_id(nc) * per_w
        pltpu.sync_copy(idx_hbm.at[pl.ds(pl.multiple_of(row0, per_w), per_w)], idx_v)
        _sc_gather_chunks(table_hbm, out_hbm, idx_v, (g[0:3], g[3:6]), row0, per_w, lanes)

    return body(table, idx)


def _moe1_body(xs_ref, gate_ref, wg_ref, wu_ref, wd_ref, y_ref):
    xs = _unpack_halves(xs_ref[0])
    hg = _dot(xs, wg_ref[0].astype(BF16))
    hu = _dot(xs, wu_ref[0].astype(BF16))
    h = (_silu(hg) * hu).astype(BF16)
    y = _dot(h, wd_ref[0].astype(BF16))
    gt = gate_ref[0].T
    y = jnp.concatenate([y[k * 128:(k + 1) * 128] * gt[:, k:k + 1] for k in range(gt.shape[1])], axis=0)
    y_ref[0] = _pack_halves(y)


def _moe_experts(xs, gate, wg, wu, wd, bsz):
    npairs, cap, half = xs.shape
    ne, d, f = wg.shape
    pair = lambda b, e: (b * ne + e, 0, 0)
    expert = lambda b, e: (e, 0, 0)
    return pl.pallas_call(
        _moe1_body,
        grid=(bsz, ne),
        in_specs=[pl.BlockSpec((1, cap, half), pair),
                  pl.BlockSpec((1, cap // 128, 128), pair),
                  pl.BlockSpec((1, d, f), expert),
                  pl.BlockSpec((1, d, f), expert),
                  pl.BlockSpec((1, f, d), expert)],
        out_specs=pl.BlockSpec((1, cap, half), pair),
        out_shape=jax.ShapeDtypeStruct((npairs, cap, half), I32),
        compiler_params=_params(("arbitrary", "arbitrary")),
        cost_estimate=pl.CostEstimate(flops=6 * npairs * cap * d * f, transcendentals=npairs * cap * f,
                                      bytes_accessed=12 * npairs * d * f + 16 * npairs * cap * half),
        name="moe_experts",
    )(xs, gate, wg, wu, wd)


def _combine_body(abase_ref, ys_hbm, tok_ref, x1_ref, g2_ref, fw_ref, o_ref, buf_ref, acc_ref, sem_ref, cnt_ref, *, na):
    n = ONEHOT_BLK
    ring = buf_ref.shape[0]
    nch = na // n
    b, i = pl.program_id(0), pl.program_id(1)

    @pl.when(i == 0)
    def _():
        cnt_ref[0] = 0
        cnt_ref[1] = 0

    lo = abase_ref[b, i]
    hi = abase_ref[b, i + 1]
    c0 = lo // n
    c1 = jnp.where(hi > lo, (hi + n - 1) // n, c0)

    def copy(c):
        slot = c % ring
        row0 = pl.multiple_of(b * na + c * n, n)
        return pltpu.make_async_copy(ys_hbm.at[pl.ds(row0, n)], buf_ref.at[slot], sem_ref.at[slot])

    acc_ref[...] = jnp.zeros(acc_ref.shape, F32)
    tok_ids = lax.broadcasted_iota(I32, (n, n), 0) + i * n

    def step(c, carry):
        started = cnt_ref[0]
        ahead = jnp.minimum(c + ring, nch)

        def start(k, _):
            copy(k).start()
            return 0

        lax.fori_loop(started, ahead, start, 0)
        cnt_ref[0] = jnp.maximum(started, ahead)

        @pl.when(cnt_ref[1] <= c)
        def _():
            copy(c).wait()
            cnt_ref[1] = c + 1

        seg = jnp.where(tok_ids == tok_ref[0, pl.ds(c, 1), :], 1.0, 0.0).astype(BF16)
        acc_ref[...] += _dot(seg, _unpack_halves(buf_ref[c % ring]))
        return carry

    lax.fori_loop(c0, c1, step, 0)
    x2 = x1_ref[0] + g2_ref[0] * acc_ref[...]
    o_ref[0] = x2 * lax.rsqrt(jnp.mean(x2 * x2, axis=-1, keepdims=True) + NORM_EPS) * fw_ref[...]


def _combine_final(abase, ys, tok, x1, g2, fw, na):
    bsz, t, d = x1.shape
    n = ONEHOT_BLK
    grid_spec = pltpu.PrefetchScalarGridSpec(
        num_scalar_prefetch=1,
        grid=(bsz, t // n),
        in_specs=[pl.BlockSpec(memory_space=pl.ANY),
                  pl.BlockSpec((1, na // n, n), lambda b, i, a: (b, 0, 0)),
                  pl.BlockSpec((1, n, d), lambda b, i, a: (b, i, 0)),
                  pl.BlockSpec((1, 1, d), lambda b, i, a: (b, 0, 0)),
                  pl.BlockSpec((1, d), lambda b, i, a: (0, 0))],
        out_specs=pl.BlockSpec((1, n, d), lambda b, i, a: (b, i, 0)),
        scratch_shapes=[pltpu.VMEM((COMBINE_RING, n, d // 2), I32), pltpu.VMEM((n, d), F32),
                        pltpu.SemaphoreType.DMA((COMBINE_RING,)), pltpu.SMEM((2,), I32)],
    )
    return pl.pallas_call(
        functools.partial(_combine_body, na=na),
        grid_spec=grid_spec,
        out_shape=jax.ShapeDtypeStruct((bsz, t, d), F32),
        compiler_params=_params(("arbitrary", "arbitrary")),
        name="combine_final",
    )(abase, ys, tok, x1, g2, fw)


def kernel(x, c, ctx, c_ctx, ada_w, ada_b, norm1_w, w_in, conv_w, conv_b, hg_lb_logits, ml_gate_b,
           hg_norm_w, ml_norm_w, w_out, norm2_w, router_w, exp_w_gate, exp_w_up, exp_w_down, final_norm_w):
    bsz, t, d = x.shape
    nctx = ctx.shape[1]
    assert ada_w.shape[0] == 1, "single-layer block"
    assert nctx == SCAN_STEP and t % SCAN_STEP == 0 and bsz + 1 <= 8
    tall = t + nctx
    nx = t // SCAN_STEP
    kw = HEADS_W
    ne = router_w.shape[-1]
    cap = EC_CAPACITY * t // ne
    tm = min(512, t)

    rows = jnp.concatenate([c, c_ctx[None], jnp.zeros((7 - bsz, d), F32)], axis=0)
    mod = _modulation(rows, ada_w[0], ada_b[0][None])
    mx = [m[:, None, :] for m in jnp.split(mod[:bsz], 6, axis=-1)]
    mc = [m[:, None, :] for m in jnp.split(mod[bsz:bsz + 1], 6, axis=-1)]
    sh1, sc1, g1, sh2, sc2, g2 = mx
    csh1, csc1 = mc[0], mc[1]

    main_w = 9 * kw
    w_main = w_in[0][:, :main_w].astype(BF16)
    w_gt = w_in[0][:, main_w:].T
    gate_b = ml_gate_b[0][:, None]
    nw1 = norm1_w[0][None]
    outs = _inproj(x, sc1, sh1, nw1, w_main, w_gt, gate_b, hg_lb_logits, tall, tm, 0)
    outs = _inproj(ctx, csc1, csh1, nw1, w_main, w_gt, gate_b, hg_lb_logits, tall, nctx, t // nctx, prev=outs)
    hgq, hgv, hgg, hgk, hglf, mlqk_pre, mlv, mlo, gates = outs

    mlqk = _conv(mlqk_pre, conv_w[0], conv_b[0][None], t, nctx)
    hg_f, hg_b = _hgrn2(hgq, hgk, hgv, hglf, nx)
    ml_f, ml_b = _mlstm(mlqk, mlv, gates, nx)

    x1, vpk, aff = _mixer_out(hg_f, hg_b, ml_f, ml_b, hgg, mlo, x, g1, sc2, sh2,
                             hg_norm_w[0][None], ml_norm_w[0][None], w_out[0].astype(BF16),
                             norm2_w[0][None], router_w[0].T, tm)

    pos4, base = _topk(aff.reshape(bsz, ne, t // 128, 128), cap)
    na = ne * cap
    pos_flat = pos4.reshape(-1)
    xs, gate = _sc_invert_gather(vpk.reshape(bsz * t, d // 2), pos_flat, aff.reshape(-1), bsz * ne, ne, t, cap)
    perm, tok = _sc_assignment_order(pos_flat, base.reshape(-1), bsz, ne, t, cap)
    y = _moe_experts(xs.reshape(bsz * ne, cap, d // 2), gate.reshape(bsz * ne, cap // 128, 128),
                     exp_w_gate[0], exp_w_up[0], exp_w_down[0], bsz)
    ys = _sc_gather_rows(y.reshape(bsz * na, d // 2), perm)
    abase = jnp.concatenate([base.reshape(bsz, t)[:, ::ONEHOT_BLK], jnp.full((bsz, 1), na, I32)], axis=1)
    return _combine_final(abase, ys, tok.reshape(bsz, na // ONEHOT_BLK, ONEHOT_BLK), x1, g2, final_norm_w[None], na)
```

```python
import dataclasses
import functools

import numpy as np
import jax
import jax.numpy as jnp
from jax import lax
from jax.experimental import pallas as pl
from jax.experimental.pallas import tpu as pltpu
from jax.experimental.pallas import tpu_sc as plsc

F32 = jnp.float32
BF16 = jnp.bfloat16
I32 = jnp.int32
HIGHEST = lax.Precision.HIGHEST
NORM_EPS = 1e-6

HEAD_DIM = 128
N_HEADS = 4
HEADS_W = N_HEADS * HEAD_DIM
GRID_W = 64
N_EXPERTS = 16
EC_CAPACITY = 2
HG_CHUNK = 64
HG_VPU_LEVEL_MIN = 4
ML_CHUNK = 128
SCAN_STEP = 256
ML_SAMPLES = 2
ONEHOT_BLK = 256
CONV_HALO = 72
SC_GATHER_ROWS = 64
COMBINE_RING = 6
TOPK_BISECTIONS = 64
VMEM_LIMIT = 56 * 1024 * 1024

_NT = (((1,), (1,)), ((), ()))
_TN = (((0,), (0,)), ((), ()))


def _dot(a, b, dims=None, precision=None):
    if dims is None:
        return jnp.dot(a, b, preferred_element_type=F32, precision=precision)
    return lax.dot_general(a, b, dims, preferred_element_type=F32, precision=precision)


def _dot_split(w, x, x1):
    n = w.shape[0]
    x2 = (x - x1.astype(F32)).astype(BF16)
    w1 = w.astype(BF16)
    wr = w - w1.astype(F32)
    w2 = wr.astype(BF16)
    w3 = (wr - w2.astype(F32)).astype(BF16)
    pa = _dot(jnp.concatenate([w1, w2, w3], axis=0), x1, _NT)
    pb = _dot(jnp.concatenate([w1, w2], axis=0), x2, _NT)
    return pa[0:n] + pa[n:2 * n] + pa[2 * n:] + pb[0:n] + pb[n:]


def _sigmoid(x):
    return jax.nn.sigmoid(x)


def _pack_halves(x):
    w = x.shape[-1] // 2
    bits = lax.bitcast_convert_type(x.astype(BF16).astype(F32), I32)
    return lax.shift_right_logical(bits[:, :w], 16) | bits[:, w:]


def _unpack_halves(p):
    lo = lax.bitcast_convert_type(lax.shift_left(p, 16), F32)
    hi = lax.bitcast_convert_type(p & jnp.int32(-65536), F32)
    return jnp.concatenate([lo, hi], axis=-1).astype(BF16)


def _silu(x):
    return x * jax.nn.sigmoid(x)


def _params(sem, flags=None):
    return pltpu.CompilerParams(dimension_semantics=sem, vmem_limit_bytes=VMEM_LIMIT, flags=flags)


def _mod_body(r_ref, w_ref, b_ref, o_ref):
    r = r_ref[...]
    o_ref[...] = _dot(_silu(r), w_ref[...], precision=HIGHEST) + b_ref[...]


def _modulation(rows, w, b):
    d, n = w.shape
    tn = n // 4
    return pl.pallas_call(
        _mod_body,
        grid=(n // tn,),
        in_specs=[pl.BlockSpec((8, d), lambda j: (0, 0)),
                  pl.BlockSpec((d, tn), lambda j: (0, j)),
                  pl.BlockSpec((1, tn), lambda j: (0, j))],
        out_specs=pl.BlockSpec((8, tn), lambda j: (0, j)),
        out_shape=jax.ShapeDtypeStruct((8, n), F32),
        compiler_params=_params(("arbitrary",)),
        name="modulation",
    )(rows, w, b)


def _log_sigmoid(x):
    return jnp.minimum(x, 0.0) - jnp.log(1.0 + jnp.exp(-jnp.abs(x)))


def _inproj_body(x_ref, sc_ref, sh_ref, nw_ref, w_ref, wg_ref, gb_ref, lbl_ref, *refs):
    hgq_ref, hgv_ref, hgg_ref, hgk_ref, hglf_ref, mlqk_ref, mlv_ref, mlo_ref, gates_ref = refs[-9:]
    kw = HEADS_W
    x = x_ref[0]
    y = x * lax.rsqrt(jnp.mean(x * x, axis=-1, keepdims=True) + NORM_EPS) * nw_ref[...]
    u = y * (1.0 + sc_ref[0]) + sh_ref[0]
    ub = u.astype(BF16)

    def proj(c0, c1):
        return _dot(ub, w_ref[:, c0:c1])

    hgq_ref[0] = _silu(proj(0, kw)).astype(BF16)
    hgv_ref[0] = proj(kw, 2 * kw).astype(BF16)
    hgg_ref[0] = _silu(proj(2 * kw, 3 * kw)).astype(BF16)

    lbl = lbl_ref[...]
    mx = jnp.max(lbl, axis=0)
    ex = jnp.exp(lbl - mx[None])
    lb = ex[0] / jnp.sum(ex, axis=0)
    for d in range(2):
        p = proj((3 + d) * kw, (4 + d) * kw)
        lbd = lb[d:d + 1]
        f = lbd + (1.0 - lbd) * _sigmoid(p)
        hgk_ref[0, :, d * kw:(d + 1) * kw] = (1.0 - f).astype(BF16)
        hglf_ref[0, :, d * kw:(d + 1) * kw] = jnp.log(f)

    mlqk_ref[0, :, 0:kw] = proj(5 * kw, 6 * kw)
    mlqk_ref[0, :, kw:2 * kw] = proj(6 * kw, 7 * kw)
    mlv_ref[0] = proj(7 * kw, 8 * kw).astype(BF16)
    mlo_ref[0] = _sigmoid(proj(8 * kw, 9 * kw)).astype(BF16)

    g = _dot_split(wg_ref[...], u, ub) + gb_ref[...]
    row = lax.broadcasted_iota(I32, g.shape, 0)
    gates_ref[0] = jnp.where((row % 8) >= N_HEADS, _log_sigmoid(g), g)


def _inproj_shapes(bsz, tall):
    kw = HEADS_W
    return [
        jax.ShapeDtypeStruct((bsz, tall, kw), BF16),
        jax.ShapeDtypeStruct((bsz, tall, kw), BF16),
        jax.ShapeDtypeStruct((bsz, tall, kw), BF16),
        jax.ShapeDtypeStruct((bsz, tall, 2 * kw), BF16),
        jax.ShapeDtypeStruct((bsz, tall, 2 * kw), F32),
        jax.ShapeDtypeStruct((bsz, tall, 2 * kw), F32),
        jax.ShapeDtypeStruct((bsz, tall, kw), BF16),
        jax.ShapeDtypeStruct((bsz, tall, kw), BF16),
        jax.ShapeDtypeStruct((bsz, 4 * N_HEADS, tall), F32),
    ]


def _inproj(tokens, scale, shift, nw, w_main, w_gt, gate_b, lb_logits, tall, tm, blk0, prev=None):
    bsz, n, d = tokens.shape
    kw = HEADS_W
    nt = n // tm
    per_sample = scale.shape[0] == bsz
    mod_map = (lambda b, i: (b, 0, 0)) if per_sample else (lambda b, i: (0, 0, 0))
    const2 = lambda b, i: (0, 0)
    in_specs = [
        pl.BlockSpec((1, tm, d), lambda b, i: (b, i, 0)),
        pl.BlockSpec((1, 1, d), mod_map),
        pl.BlockSpec((1, 1, d), mod_map),
        pl.BlockSpec((1, d), const2),
        pl.BlockSpec(w_main.shape, const2),
        pl.BlockSpec(w_gt.shape, const2),
        pl.BlockSpec(gate_b.shape, const2),
        pl.BlockSpec(lb_logits.shape, lambda b, i: (0, 0, 0)),
    ]
    args = [tokens, scale, shift, nw, w_main, w_gt, gate_b, lb_logits]
    aliases = {}
    if prev is not None:
        for k, a in enumerate(prev):
            in_specs.append(pl.BlockSpec(memory_space=pl.ANY))
            aliases[len(args)] = k
            args.append(a)
    row_map = lambda b, i: (b, blk0 + i, 0)
    widths = [kw, kw, kw, 2 * kw, 2 * kw, 2 * kw, kw, kw]
    out_specs = [pl.BlockSpec((1, tm, w), row_map) for w in widths]
    out_specs.append(pl.BlockSpec((1, 4 * N_HEADS, tm), lambda b, i: (b, 0, blk0 + i)))
    return pl.pallas_call(
        _inproj_body,
        grid=(bsz, nt),
        in_specs=in_specs,
        out_specs=out_specs,
        out_shape=_inproj_shapes(bsz, tall),
        input_output_aliases=aliases,
        compiler_params=_params(("arbitrary", "arbitrary")),
        name="inproj_ctx" if prev is not None else "inproj_x",
    )(*args)


def _conv_body(x_ref, w_ref, b_ref, o_ref, pad_ref, cpad_ref, *, t, nctx, scale_from):
    halo = CONV_HALO
    rows = 512
    win = rows + 2 * halo
    ch = x_ref.shape[-1]
    scale = jnp.where(pl.program_id(1) >= scale_from, HEAD_DIM ** -0.5, 1.0).astype(F32)
    w = w_ref[...]
    bias = b_ref[...]

    pad_ref[0:halo, :] = jnp.zeros((halo, ch), F32)
    pad_ref[halo + t:halo + t + halo, :] = jnp.zeros((halo, ch), F32)
    pad_ref[halo:halo + t, :] = x_ref[0, 0:t, :]
    col = (lax.broadcasted_iota(I32, (win, ch), 0) + (GRID_W - halo % GRID_W)) % GRID_W
    left_ok = col > 0
    right_ok = col < GRID_W - 1

    def chunk(c, carry):
        o = pl.multiple_of(c * rows, rows)
        xw = pad_ref[pl.ds(o, win), :]
        xm = jnp.where(left_ok, pltpu.roll(xw, 1, 0), 0.0)
        xp = jnp.where(right_ok, pltpu.roll(xw, win - 1, 0), 0.0)
        def taps(dr, lo):
            sl = slice(lo, lo + rows)
            return xm[sl] * w[dr, 0:1] + xw[sl] * w[dr, 1:2] + xp[sl] * w[dr, 2:3]

        y = taps(1, halo) + taps(0, halo - GRID_W) + taps(2, halo + GRID_W)
        o_ref[0, pl.ds(o, rows), :] = (_silu(y + bias) * scale).astype(o_ref.dtype)
        return carry

    lax.fori_loop(0, t // rows, chunk, 0)

    cpad_ref[0:8, :] = jnp.zeros((8, ch), F32)
    cpad_ref[8 + nctx:16 + nctx, :] = jnp.zeros((8, ch), F32)
    cpad_ref[8:8 + nctx, :] = x_ref[0, t:t + nctx, :]
    xw = cpad_ref[...]
    n = nctx + 16
    y = (pltpu.roll(xw, 1, 0) * w[1, 0:1] + xw * w[1, 1:2] + pltpu.roll(xw, n - 1, 0) * w[1, 2:3])[8:8 + nctx]
    o_ref[0, t:t + nctx, :] = (_silu(y + bias) * scale).astype(o_ref.dtype)


def _conv(qk_pre, conv_w, conv_b, t, nctx):
    bsz, tall, c = qk_pre.shape
    ch = 128
    body = functools.partial(_conv_body, t=t, nctx=nctx, scale_from=(c // 2) // ch)
    return pl.pallas_call(
        body,
        grid=(bsz, c // ch),
        in_specs=[pl.BlockSpec((1, tall, ch), lambda b, j: (b, 0, j)),
                  pl.BlockSpec((3, 3, ch), lambda b, j: (0, 0, j)),
                  pl.BlockSpec((1, ch), lambda b, j: (0, j))],
        out_specs=pl.BlockSpec((1, tall, ch), lambda b, j: (b, 0, j)),
        out_shape=jax.ShapeDtypeStruct((bsz, tall, c), BF16),
        scratch_shapes=[pltpu.VMEM((t + 2 * CONV_HALO, ch), F32),
                        pltpu.VMEM((nctx + 16, ch), F32)],
        compiler_params=_params(("arbitrary", "arbitrary")),
        name="qk_conv",
    )(qk_pre, conv_w, conv_b)


def _fwd_blk(s, nx):
    return jnp.where(s == 0, nx, s - 1)


def _bwd_blk(s, nx):
    return jnp.where(s == 0, nx, nx - s)


def _hg_constants(rev):
    c = HG_CHUNK
    i = np.arange(c)[:, None]
    j = np.arange(c)[None, :]
    blocks = [(j >= i) if rev else (j <= i)]
    masks = [i == j]
    m = c // 2
    while m >= 1:
        b0 = (i // (2 * m)) * (2 * m)
        same = (i // (2 * m)) == (j // (2 * m))
        if rev:
            beta = b0 + m
            qrow = (i % (2 * m)) < m
            g = np.where(qrow, (j >= i) & (j < beta), (j >= beta) & (j < i))
            mask = same & qrow & ((j % (2 * m)) >= m)
        else:
            beta = b0 + m - 1
            qrow = (i % (2 * m)) >= m
            g = np.where(qrow, (j > beta) & (j <= i), (j > i) & (j <= beta))
            mask = same & qrow & ((j % (2 * m)) < m)
        if m < HG_VPU_LEVEL_MIN:
            blocks.append(g)
        masks.append(mask)
        m //= 2
    g = np.concatenate(blocks, axis=0).astype(np.float32)
    g3 = np.concatenate([g, g, g], axis=1)
    m2 = np.concatenate([np.stack(masks), np.stack(masks)], axis=2)
    return (jnp.asarray(g3, BF16), jnp.asarray(m2, F32))


def _hg_level_decay(a, m, rev):
    c = a.shape[0]
    parts = []
    for b0 in range(0, c, 2 * m):
        beta = b0 + m if rev else b0 + m - 1
        d = a[b0:b0 + 2 * m] - a[beta:beta + 1]
        parts.append(jnp.minimum(d, -d))
    return jnp.concatenate(parts, axis=0) if len(parts) > 1 else parts[0]


def _block_diag(x, zero):
    w = x.shape[1] // 2
    return jnp.concatenate([jnp.concatenate([x[:, :w], zero], axis=1),
                            jnp.concatenate([zero, x[:, w:]], axis=1)], axis=0)


def _hg_chunk(dirs):
    c = HG_CHUNK
    w = 2 * HEAD_DIM
    zero = jnp.zeros((c, HEAD_DIM), BF16)
    units = []
    for rev, r0, q_ref, k_ref, v_ref, lf_ref, g_ref, msk_ref, o_ref, st_ref in dirs:
        rows = pl.ds(r0, c)
        lf = lf_ref[0, rows, :]
        p1 = lf.astype(BF16)
        r1 = lf - p1.astype(F32)
        p2 = r1.astype(BF16)
        p3 = (r1 - p2.astype(F32)).astype(BF16)
        dall = _dot(g_ref[...], jnp.concatenate([p1, p2, p3], axis=0))
        for hp in range(N_HEADS // 2):
            cs = slice(hp * w, (hp + 1) * w)
            units.append(dict(rev=rev, rows=rows, cs=cs, hp=hp, msk_ref=msk_ref, o_ref=o_ref, st_ref=st_ref,
                              q=q_ref[0, rows, cs], k=k_ref[0, rows, cs], v=v_ref[0, rows, cs], dall=dall[:, cs]))
    for u in units:
        msk_ref = u["msk_ref"]
        nlev = msk_ref.shape[0] - 1
        att = _dot(u["q"], _block_diag(u["k"], zero), _NT) * msk_ref[0]
        a = u["dall"][0:c]
        row = 1
        for l in range(nlev):
            m = c >> (l + 1)
            if m >= HG_VPU_LEVEL_MIN:
                dec = _hg_level_decay(a, m, u["rev"])
            else:
                dec = u["dall"][row * c:(row + 1) * c]
                row += 1
            e = jnp.exp(dec).astype(BF16)
            att = att + _dot(u["q"] * e, _block_diag(u["k"] * e, zero), _NT) * msk_ref[l + 1]
        u["att"] = att.astype(BF16)
    for u in units:
        a = u["dall"][0:c]
        u["a_tot"] = a[0:1] if u["rev"] else a[c - 1:c]
        u["st"] = [u["st_ref"][2 * u["hp"] + i] for i in range(2)]
        zf = jnp.zeros((HEAD_DIM, HEAD_DIM), BF16)
        st2 = jnp.concatenate([jnp.concatenate([u["st"][0].astype(BF16), zf], axis=1),
                               jnp.concatenate([zf, u["st"][1].astype(BF16)], axis=1)], axis=0)
        qbar = (u["q"].astype(F32) * jnp.exp(a)).astype(BF16)
        u["o"] = _dot(u["att"], _block_diag(u["v"], zero)) + _dot(qbar, st2, _NT)
        u["khat"] = (u["k"].astype(F32) * jnp.exp(u["a_tot"] - a)).astype(BF16)
    for u in units:
        u["o_ref"][0, u["rows"], u["cs"]] = u["o"].astype(BF16)
        for i in range(2):
            hs = slice(i * HEAD_DIM, (i + 1) * HEAD_DIM)
            upd = _dot(u["v"][:, hs], u["khat"][:, hs], _TN)
            u["st_ref"][2 * u["hp"] + i] = u["st"][i] * jnp.exp(u["a_tot"][:, hs]) + upd


def _hg_body(qf_ref, kf_ref, vf_ref, lff_ref, qb_ref, kb_ref, vb_ref, lfb_ref,
             gf_ref, mf_ref, gb_ref, mb_ref, of_ref, ob_ref, st_ref):
    @pl.when(pl.program_id(1) == 0)
    def _():
        st_ref[...] = jnp.zeros(st_ref.shape, F32)

    nsub = SCAN_STEP // HG_CHUNK
    for c in range(nsub):
        _hg_chunk([
            (False, c * HG_CHUNK, qf_ref, kf_ref, vf_ref, lff_ref, gf_ref, mf_ref, of_ref, st_ref.at[0]),
            (True, (nsub - 1 - c) * HG_CHUNK, qb_ref, kb_ref, vb_ref, lfb_ref, gb_ref, mb_ref, ob_ref, st_ref.at[1]),
        ])


def _hgrn2(hgq, hgk, hgv, hglf, nx):
    bsz, tall, kw = hgq.shape
    steps = tall // SCAN_STEP
    cf = _hg_constants(False)
    cb = _hg_constants(True)
    blk = (1, SCAN_STEP, kw)
    fwd = lambda col: (lambda b, s: (b, _fwd_blk(s, nx), col))
    bwd = lambda col: (lambda b, s: (b, _bwd_blk(s, nx), col))
    const = lambda a: pl.BlockSpec(a.shape, lambda b, s: (0,) * a.ndim)
    in_specs = [pl.BlockSpec(blk, fwd(0)), pl.BlockSpec(blk, fwd(0)), pl.BlockSpec(blk, fwd(0)), pl.BlockSpec(blk, fwd(0)),
                pl.BlockSpec(blk, bwd(0)), pl.BlockSpec(blk, bwd(1)), pl.BlockSpec(blk, bwd(0)), pl.BlockSpec(blk, bwd(1))]
    in_specs += [const(a) for a in cf + cb]
    out_sds = jax.ShapeDtypeStruct((bsz, tall, kw), BF16)
    return pl.pallas_call(
        _hg_body,
        grid=(bsz, steps),
        in_specs=in_specs,
        out_specs=[pl.BlockSpec(blk, fwd(0)), pl.BlockSpec(blk, bwd(0))],
        out_shape=[out_sds, out_sds],
        scratch_shapes=[pltpu.VMEM((2, N_HEADS, HEAD_DIM, HEAD_DIM), F32)],
        compiler_params=_params(("arbitrary", "arbitrary")),
        name="hgrn2_scan",
    )(hgq, hgk, hgv, hglf, hgq, hgk, hgv, hglf, *cf, *cb)


def _ml_constants(rev):
    k = np.arange(ML_CHUNK)
    tri = (k[:, None] >= k[None, :]) if rev else (k[:, None] <= k[None, :])
    return jnp.asarray(np.concatenate([tri, tri, tri], axis=0), BF16)


def _ml_chunk(dirs):
    c = ML_CHUNK
    ii = lax.broadcasted_iota(I32, (c, c), 0)
    jj = lax.broadcasted_iota(I32, (c, c), 1)
    ones = jnp.ones((c, HEAD_DIM), BF16)
    units = []
    for rev, r0, d, bb, q_ref, k_ref, v_ref, g_ref, tri3_ref, o_ref, st_ref, m_ref in dirs:
        gates = g_ref[bb, :, pl.ds(r0, c)]
        p1 = gates.astype(BF16)
        r1 = gates - p1.astype(F32)
        p2 = r1.astype(BF16)
        p3 = (r1 - p2.astype(F32)).astype(BF16)
        csum = _dot(jnp.concatenate([p1, p2, p3], axis=1), tri3_ref[...])
        for h in range(N_HEADS):
            cs = slice(h * HEAD_DIM, (h + 1) * HEAD_DIM)
            u = dict(rev=rev, o_ref=o_ref, bb=bb, rows=pl.ds(r0, c), cs=cs, st_ref=st_ref, m_ref=m_ref, h=h)
            u["qb"] = q_ref[bb, pl.ds(r0, c), cs]
            u["kb"] = k_ref[bb, pl.ds(r0, c), cs]
            u["v1"] = jnp.concatenate([v_ref[bb, pl.ds(r0, c), cs], ones], axis=1)
            irow = gates[d * 8 + h:d * 8 + h + 1]
            u["brow"] = csum[d * 8 + N_HEADS + h:d * 8 + N_HEADS + h + 1]
            u["rrow"] = irow - u["brow"]
            units.append(u)
    for u in units:
        u["st"] = u["st_ref"][u["h"]]
        a = _dot(jnp.concatenate([u["kb"], u["st"].astype(BF16)], axis=0), u["qb"], _NT)
        u["s"] = a[:c]
        u["sq"] = a[c:]
    for u in units:
        last = 0 if u["rev"] else c - 1
        tri_t = (ii >= jj) if u["rev"] else (ii <= jj)
        rcol = jnp.concatenate([u["rrow"], jnp.zeros((7, c), F32)], axis=0).T[:, 0:1]
        u["mprev"] = u["m_ref"][u["h"]][:, 0:1]
        rmat = jnp.where(tri_t, rcol, -jnp.inf)
        u["grow"] = jnp.maximum(jnp.max(rmat, axis=0, keepdims=True), u["mprev"])
        qk = (u["s"] * jnp.exp(rmat - u["grow"])).astype(BF16)
        blast = u["brow"][:, last:last + 1]
        u["mnew"] = blast + u["grow"][:, last:last + 1]
        kh = (u["kb"].astype(F32) * jnp.exp(blast + rcol - u["mnew"])).astype(BF16)
        u["ws"] = jnp.exp(blast + u["mprev"] - u["mnew"])
        u["qkh"] = jnp.concatenate([qk, kh], axis=1)
    for u in units:
        u["nu"] = _dot(u["v1"], u["qkh"], _TN)
    for u in units:
        both = u["nu"][:, :c] + jnp.exp(u["mprev"] - u["grow"]) * u["sq"]
        den = both[HEAD_DIM:HEAD_DIM + 1]
        inv = 1.0 / jnp.maximum(jnp.abs(den), jnp.exp(-(u["brow"] + u["grow"])))
        u["o_ref"][u["bb"], u["rows"], u["cs"]] = (both[:HEAD_DIM] * inv).T.astype(BF16)
        u["st_ref"][u["h"]] = u["ws"] * u["st"] + u["nu"][:, c:]
        u["m_ref"][u["h"]] = jnp.broadcast_to(u["mnew"], (1, HEAD_DIM))


def _ml_body(qf_ref, kf_ref, vf_ref, gf_ref, qb_ref, kb_ref, vb_ref, gb_ref, tf_ref, tb_ref,
             of_ref, ob_ref, st_ref, m_ref):
    @pl.when(pl.program_id(1) == 0)
    def _():
        st_ref[...] = jnp.zeros(st_ref.shape, F32)
        m_ref[...] = jnp.zeros(m_ref.shape, F32)

    nsub = SCAN_STEP // ML_CHUNK
    for c in range(nsub):
        dirs = []
        for bb in range(qf_ref.shape[0]):
            dirs.append((False, c * ML_CHUNK, 0, bb, qf_ref, kf_ref, vf_ref, gf_ref, tf_ref, of_ref,
                         st_ref.at[0, bb], m_ref.at[0, bb]))
            dirs.append((True, (nsub - 1 - c) * ML_CHUNK, 1, bb, qb_ref, kb_ref, vb_ref, gb_ref, tb_ref, ob_ref,
                         st_ref.at[1, bb], m_ref.at[1, bb]))
        _ml_chunk(dirs)


def _mlstm(mlqk, mlv, gates, nx):
    bsz, tall, kw = mlv.shape
    steps = tall // SCAN_STEP
    nb = ML_SAMPLES if bsz % ML_SAMPLES == 0 else 1
    blk = (nb, SCAN_STEP, kw)
    gblk = (nb, 4 * N_HEADS, SCAN_STEP)
    tf, tb = _ml_constants(False), _ml_constants(True)
    fwd = lambda col: (lambda b, s: (b, _fwd_blk(s, nx), col))
    bwd = lambda col: (lambda b, s: (b, _bwd_blk(s, nx), col))
    const = pl.BlockSpec(tf.shape, lambda b, s: (0, 0))
    in_specs = [pl.BlockSpec(blk, fwd(0)), pl.BlockSpec(blk, fwd(1)), pl.BlockSpec(blk, fwd(0)),
                pl.BlockSpec(gblk, lambda b, s: (b, 0, _fwd_blk(s, nx))),
                pl.BlockSpec(blk, bwd(0)), pl.BlockSpec(blk, bwd(1)), pl.BlockSpec(blk, bwd(0)),
                pl.BlockSpec(gblk, lambda b, s: (b, 0, _bwd_blk(s, nx))), const, const]
    out_sds = jax.ShapeDtypeStruct((bsz, tall, kw), BF16)
    return pl.pallas_call(
        _ml_body,
        grid=(bsz // nb, steps),
        in_specs=in_specs,
        out_specs=[pl.BlockSpec(blk, fwd(0)), pl.BlockSpec(blk, bwd(0))],
        out_shape=[out_sds, out_sds],
        scratch_shapes=[pltpu.VMEM((2, nb, N_HEADS, 2 * HEAD_DIM, HEAD_DIM), F32),
                        pltpu.VMEM((2, nb, N_HEADS, 1, HEAD_DIM), F32)],
        compiler_params=_params(("arbitrary", "arbitrary")),
        name="mlstm_scan",
    )(mlqk, mlqk, mlv, gates, mlqk, mlqk, mlv, gates, tf, tb)


def _out_body(hof_ref, hob_ref, mhf_ref, mhb_ref, hgg_ref, mlo_ref, x_ref, g1_ref, sc2_ref, sh2_ref,
              hnw_ref, mnw_ref, wout_ref, n2w_ref, rwt_ref, x1_ref, vt_ref, aff_ref):
    hg = hof_ref[0].astype(F32) + hob_ref[0].astype(F32)
    ml = mhf_ref[0].astype(F32) + mhb_ref[0].astype(F32)
    hparts, mparts = [], []
    for h in range(N_HEADS):
        cs = slice(h * HEAD_DIM, (h + 1) * HEAD_DIM)
        t = hg[:, cs]
        hparts.append(t * lax.rsqrt(jnp.mean(t * t, axis=-1, keepdims=True) + NORM_EPS))
        t = ml[:, cs]
        t = t - jnp.mean(t, axis=-1, keepdims=True)
        mparts.append(t * lax.rsqrt(jnp.mean(t * t, axis=-1, keepdims=True) + NORM_EPS))
    hgn = jnp.concatenate(hparts, axis=-1) * hnw_ref[...] * hgg_ref[0].astype(F32)
    mln = jnp.concatenate(mparts, axis=-1) * mnw_ref[...] * mlo_ref[0].astype(F32)
    mix = jnp.concatenate([hgn, mln], axis=-1).astype(BF16)
    x1 = x_ref[0] + g1_ref[0] * _dot(mix, wout_ref[...])
    x1_ref[0] = x1
    v = x1 * lax.rsqrt(jnp.mean(x1 * x1, axis=-1, keepdims=True) + NORM_EPS) * n2w_ref[...]
    v = v * (1.0 + sc2_ref[0]) + sh2_ref[0]
    vt_ref[0] = _pack_halves(v)
    logits = _dot_split(rwt_ref[...], v, v.astype(BF16))
    ex = jnp.exp(logits - jnp.max(logits, axis=0, keepdims=True))
    aff_ref[0] = ex / jnp.sum(ex, axis=0, keepdims=True)


def _mixer_out(hg_f, hg_b, ml_f, ml_b, hgg, mlo, x, g1, sc2, sh2, hnw, mnw, w_out, n2w, rwt, tm):
    bsz, t, d = x.shape
    kw = HEADS_W
    ne = rwt.shape[0]
    row = lambda b, i: (b, i, 0)
    mod = lambda b, i: (b, 0, 0)
    const2 = lambda b, i: (0, 0)
    act = pl.BlockSpec((1, tm, kw), row)
    in_specs = [act, act, act, act, act, act,
                pl.BlockSpec((1, tm, d), row),
                pl.BlockSpec((1, 1, d), mod), pl.BlockSpec((1, 1, d), mod), pl.BlockSpec((1, 1, d), mod),
                pl.BlockSpec((1, kw), const2), pl.BlockSpec((1, kw), const2),
                pl.BlockSpec(w_out.shape, const2), pl.BlockSpec((1, d), const2), pl.BlockSpec(rwt.shape, const2)]
    return pl.pallas_call(
        _out_body,
        grid=(bsz, t // tm),
        in_specs=in_specs,
        out_specs=[pl.BlockSpec((1, tm, d), row),
                   pl.BlockSpec((1, tm, d // 2), row),
                   pl.BlockSpec((1, ne, tm), lambda b, i: (b, 0, i))],
        out_shape=[jax.ShapeDtypeStruct((bsz, t, d), F32),
                   jax.ShapeDtypeStruct((bsz, t, d // 2), I32),
                   jax.ShapeDtypeStruct((bsz, ne, t), F32)],
        compiler_params=_params(("arbitrary", "arbitrary")),
        name="mixer_out",
    )(hg_f, hg_b, ml_f, ml_b, hgg, mlo, x, g1, sc2, sh2, hnw, mnw, w_out, n2w, rwt)


def _prefix_count(maskf, u_ref, ones_ref, bl_ref):
    e, nb, ln = maskf.shape
    x = maskf.reshape(e * nb, ln)
    xb = x.astype(BF16)
    incl = _dot(xb, u_ref[...])
    tot = _dot(xb, ones_ref[...])
    off = _dot(bl_ref[...], tot.astype(BF16))
    return (incl - x + off).reshape(e, nb, ln), off.reshape(e, nb, ln)


def _topk_body(aff_ref, u_ref, ones_ref, bl_ref, bl1_ref, pos_ref, base_ref, *, cap):
    x = aff_ref[0]
    ne = x.shape[0]

    def count(m):
        return jnp.sum(jnp.sum(jnp.where(m, 1.0, 0.0), axis=1, keepdims=True), axis=2, keepdims=True)

    def halve(_, carry):
        lo, hi = carry
        mid = 0.5 * (lo + hi)
        up = count(x > mid) >= cap
        return jnp.where(up, mid, lo), jnp.where(up, hi, mid)

    lo, hi = lax.fori_loop(0, TOPK_BISECTIONS, halve,
                           (jnp.full((ne, 1, 1), -1.0, F32), jnp.full((ne, 1, 1), 1.0, F32)))
    gt = jnp.where(x > hi, 1.0, 0.0)
    eq = jnp.where(x > lo, 1.0, 0.0) - gt
    need = cap - count(x > hi)
    eq_rank, _ = _prefix_count(eq, u_ref, ones_ref, bl_ref)
    sel = gt + eq * jnp.where(eq_rank < need, 1.0, 0.0)
    pos, _ = _prefix_count(sel, u_ref, ones_ref, bl_ref)
    pos_ref[0] = jnp.where(sel > 0, pos, -1.0).astype(I32)
    nsel = jnp.sum(sel, axis=0)
    nb16 = nsel.astype(BF16)
    incl = _dot(nb16, u_ref[...])
    tot = _dot(nb16, ones_ref[...])
    off = _dot(bl1_ref[...], tot, precision=HIGHEST)
    base_ref[0] = (incl - nsel + off).astype(I32)


def _topk(aff4, cap):
    bsz, ne, nb, ln = aff4.shape
    k = np.arange(ln)
    u = jnp.asarray(k[:, None] <= k[None, :], BF16)
    ones = jnp.ones((ln, ln), BF16)
    r = np.arange(ne * nb)
    bl = jnp.asarray(((r[:, None] // nb) == (r[None, :] // nb)) & (r[None, :] < r[:, None]), BF16)
    r1 = np.arange(nb)
    bl1 = jnp.asarray(r1[None, :] < r1[:, None], F32)
    blk = pl.BlockSpec((1, ne, nb, ln), lambda b: (b, 0, 0, 0))
    const = lambda a: pl.BlockSpec(a.shape, lambda b: (0, 0))
    return pl.pallas_call(
        functools.partial(_topk_body, cap=cap),
        grid=(bsz,),
        in_specs=[blk, const(u), const(ones), const(bl), const(bl1)],
        out_specs=[blk, pl.BlockSpec((1, nb, ln), lambda b: (b, 0, 0))],
        out_shape=[jax.ShapeDtypeStruct((bsz, ne, nb, ln), I32), jax.ShapeDtypeStruct((bsz, nb, ln), I32)],
        compiler_params=_params(("arbitrary",)),
        name="expert_topk",
    )(aff4, u, ones, bl, bl1)


def _sc_setup():
    info = plsc.get_sparse_core_info()
    mesh = plsc.VectorSubcoreMesh(core_axis_name="c", subcore_axis_name="s")
    params = dataclasses.replace(pltpu.CompilerParams(), needs_layout_passes=False)
    return info.num_cores, info.num_cores * info.num_subcores, info.num_lanes, mesh, params


def _sc_worker_id(nc):
    return lax.axis_index("s") * nc + lax.axis_index("c")


def _sc_gather_scratch(width, dtype):
    one = [pltpu.VMEM((SC_GATHER_ROWS,), I32), pltpu.VMEM((SC_GATHER_ROWS, width), dtype), pltpu.SemaphoreType.DMA]
    return one + one


def _sc_gather_chunks(table_hbm, out_hbm, idx_v, bufs, out_row0, n, lanes):
    chunk = SC_GATHER_ROWS

    def gather(c, buf):
        ich_v, rows_v, sem = buf
        for q in range(chunk // lanes):
            src = pl.ds(pl.multiple_of(c * chunk + q * lanes, lanes), lanes)
            ich_v[pl.ds(q * lanes, lanes)] = idx_v[src]
        return pltpu.make_async_copy(table_hbm.at[ich_v], rows_v, sem)

    def finish(c, buf):
        ich_v, rows_v, sem = buf
        pltpu.make_async_copy(table_hbm.at[ich_v], rows_v, sem).wait()
        pltpu.sync_copy(rows_v, out_hbm.at[pl.ds(pl.multiple_of(out_row0 + c * chunk, chunk), chunk)])

    npair = n // (2 * chunk)
    gather(0, bufs[0]).start()

    @pl.loop(0, npair)
    def _(i):
        gather(2 * i + 1, bufs[1]).start()
        finish(2 * i, bufs[0])

        @pl.when(i + 1 < npair)
        def _():
            gather(2 * i + 2, bufs[0]).start()

        finish(2 * i + 1, bufs[1])


def _sc_invert_gather(table, pos_flat, aff_flat, npairs, ne, t, cap):
    nc, nw, lanes, mesh, params = _sc_setup()
    split = max(1, nw // npairs)
    seg = cap // split
    per_w = npairs * split // nw
    width = table.shape[1]

    @functools.partial(
        pl.kernel, mesh=mesh, compiler_params=params,
        out_type=[jax.ShapeDtypeStruct((npairs * cap, width), table.dtype),
                  jax.ShapeDtypeStruct((npairs * cap,), F32)],
        scratch_types=[pltpu.VMEM((t,), I32), pltpu.VMEM((t,), F32), pltpu.VMEM((seg,), I32), pltpu.VMEM((seg,), F32)]
        + _sc_gather_scratch(width, table.dtype),
    )
    def body(table_hbm, pos_hbm, aff_hbm, out_hbm, gate_hbm, pos_v, aff_v, idx_v, gate_v, *g):
        wid = _sc_worker_id(nc)

        @pl.loop(0, per_w)
        def _(kk):
            item = wid * per_w + kk
            p = item // split
            lo = (item % split) * seg
            tok0 = (p // ne) * t
            pltpu.sync_copy(pos_hbm.at[pl.ds(pl.multiple_of(p * t, t), t)], pos_v)
            pltpu.sync_copy(aff_hbm.at[pl.ds(pl.multiple_of(p * t, t), t)], aff_v)

            @pl.loop(0, t // lanes)
            def _(i):
                v = pos_v[pl.ds(pl.multiple_of(i * lanes, lanes), lanes)] - lo
                tok = lax.iota(I32, lanes) + i * lanes
                plsc.store_scatter(idx_v, [v], tok, mask=(v >= 0) & (v < seg))

            @pl.loop(0, seg // lanes)
            def _(j):
                sl = pl.ds(pl.multiple_of(j * lanes, lanes), lanes)
                ii = idx_v[sl]
                gate_v[sl] = plsc.load_gather(aff_v, [ii])
                idx_v[sl] = ii + tok0

            row0 = p * cap + lo
            pltpu.sync_copy(gate_v, gate_hbm.at[pl.ds(pl.multiple_of(row0, seg), seg)])
            _sc_gather_chunks(table_hbm, out_hbm, idx_v, (g[0:3], g[3:6]), row0, seg, lanes)

    return body(table, pos_flat, aff_flat)


def _sc_assignment_order(pos_flat, base_flat, bsz, ne, t, cap):
    nc, nw, lanes, mesh, params = _sc_setup()
    na = ne * cap
    per_b = nw // bsz
    rng = na // per_b

    @functools.partial(
        pl.kernel, mesh=mesh, compiler_params=params,
        out_type=[jax.ShapeDtypeStruct((bsz * na,), I32), jax.ShapeDtypeStruct((bsz * na,), I32)],
        scratch_types=[pltpu.VMEM((t,), I32), pltpu.VMEM((t,), I32), pltpu.VMEM((t,), I32),
                       pltpu.VMEM((rng,), I32), pltpu.VMEM((rng,), I32)],
    )
    def body(pos_hbm, base_hbm, perm_hbm, tok_hbm, pos_v, base_v, rank_v, perm_v, tok_v):
        wid = _sc_worker_id(nc)
        b = wid // per_b
        a0 = (wid % per_b) * rng
        pltpu.sync_copy(base_hbm.at[pl.ds(pl.multiple_of(b * t, t), t)], base_v)

        @pl.loop(0, t // lanes)
        def _(i):
            rank_v[pl.ds(pl.multiple_of(i * lanes, lanes), lanes)] = jnp.zeros((lanes,), I32)

        @pl.loop(0, ne)
        def _(e):
            p = b * ne + e
            pltpu.sync_copy(pos_hbm.at[pl.ds(pl.multiple_of(p * t, t), t)], pos_v)

            @pl.loop(0, t // lanes)
            def _(i):
                sl = pl.ds(pl.multiple_of(i * lanes, lanes), lanes)
                v = pos_v[sl]
                r = rank_v[sl]
                a = base_v[sl] + r - a0
                sel = v >= 0
                mine = sel & (a >= 0) & (a < rng)
                plsc.store_scatter(perm_v, [a], v + p * cap, mask=mine)
                plsc.store_scatter(tok_v, [a], lax.iota(I32, lanes) + i * lanes, mask=mine)
                rank_v[sl] = r + jnp.where(sel, 1, 0)

        dst = pl.ds(pl.multiple_of(b * na + a0, rng), rng)
        pltpu.sync_copy(perm_v, perm_hbm.at[dst])
        pltpu.sync_copy(tok_v, tok_hbm.at[dst])

    return body(pos_flat, base_flat)


def _sc_gather_rows(table, idx):
    nc, nw, lanes, mesh, params = _sc_setup()
    n = idx.shape[0]
    per_w = n // nw
    width = table.shape[1]
    chunk = SC_GATHER_ROWS

    @functools.partial(
        pl.kernel, mesh=mesh, compiler_params=params,
        out_type=jax.ShapeDtypeStruct((n, width), table.dtype),
        scratch_types=[pltpu.VMEM((per_w,), I32)] + _sc_gather_scratch(width, table.dtype),
    )
    def body(table_hbm, idx_hbm, out_hbm, idx_v, *g):
        row0 = _sc_worker_id(nc) * per_w
        pltpu.sync_copy(idx_hbm.at[pl.ds(pl.multiple_of(row0, per_w), per_w)], idx_v)
        _sc_gather_chunks(table_hbm, out_hbm, idx_v, (g[0:3], g[3:6]), row0, per_w, lanes)

    return body(table, idx)


def _moe1_body(xs_ref, gate_ref, wg_ref, wu_ref, wd_ref, y_ref):
    xs = _unpack_halves(xs_ref[0])
    hg = _dot(xs, wg_ref[0].astype(BF16))
    hu = _dot(xs, wu_ref[0].astype(BF16))
    h = (_silu(hg) * hu).astype(BF16)
    y = _dot(h, wd_ref[0].astype(BF16))
    gt = gate_ref[0].T
    y = jnp.concatenate([y[k * 128:(k + 1) * 128] * gt[:, k:k + 1] for k in range(gt.shape[1])], axis=0)
    y_ref[0] = _pack_halves(y)


def _moe_experts(xs, gate, wg, wu, wd, bsz):
    npairs, cap, half = xs.shape
    ne, d, f = wg.shape
    pair = lambda b, e: (b * ne + e, 0, 0)
    expert = lambda b, e: (e, 0, 0)
    return pl.pallas_call(
        _moe1_body,
        grid=(bsz, ne),
        in_specs=[pl.BlockSpec((1, cap, half), pair),
                  pl.BlockSpec((1, cap // 128, 128), pair),
                  pl.BlockSpec((1, d, f), expert),
                  pl.BlockSpec((1, d, f), expert),
                  pl.BlockSpec((1, f, d), expert)],
        out_specs=pl.BlockSpec((1, cap, half), pair),
        out_shape=jax.ShapeDtypeStruct((npairs, cap, half), I32),
        compiler_params=_params(("arbitrary", "arbitrary")),
        cost_estimate=pl.CostEstimate(flops=6 * npairs * cap * d * f, transcendentals=npairs * cap * f,
                                      bytes_accessed=12 * npairs * d * f + 16 * npairs * cap * half),
        name="moe_experts",
    )(xs, gate, wg, wu, wd)


def _combine_body(abase_ref, ys_hbm, tok_ref, x1_ref, g2_ref, fw_ref, o_ref, buf_ref, acc_ref, sem_ref, cnt_ref, *, na):
    n = ONEHOT_BLK
    ring = buf_ref.shape[0]
    nch = na // n
    b, i = pl.program_id(0), pl.program_id(1)

    @pl.when(i == 0)
    def _():
        cnt_ref[0] = 0
        cnt_ref[1] = 0

    lo = abase_ref[b, i]
    hi = abase_ref[b, i + 1]
    c0 = lo // n
    c1 = jnp.where(hi > lo, (hi + n - 1) // n, c0)

    def copy(c):
        slot = c % ring
        row0 = pl.multiple_of(b * na + c * n, n)
        return pltpu.make_async_copy(ys_hbm.at[pl.ds(row0, n)], buf_ref.at[slot], sem_ref.at[slot])

    acc_ref[...] = jnp.zeros(acc_ref.shape, F32)
    tok_ids = lax.broadcasted_iota(I32, (n, n), 0) + i * n

    def step(c, carry):
        started = cnt_ref[0]
        ahead = jnp.minimum(c + ring, nch)

        def start(k, _):
            copy(k).start()
            return 0

        lax.fori_loop(started, ahead, start, 0)
        cnt_ref[0] = jnp.maximum(started, ahead)

        @pl.when(cnt_ref[1] <= c)
        def _():
            copy(c).wait()
            cnt_ref[1] = c + 1

        seg = jnp.where(tok_ids == tok_ref[0, pl.ds(c, 1), :], 1.0, 0.0).astype(BF16)
        acc_ref[...] += _dot(seg, _unpack_halves(buf_ref[c % ring]))
        return carry

    lax.fori_loop(c0, c1, step, 0)
    x2 = x1_ref[0] + g2_ref[0] * acc_ref[...]
    o_ref[0] = x2 * lax.rsqrt(jnp.mean(x2 * x2, axis=-1, keepdims=True) + NORM_EPS) * fw_ref[...]


def _combine_final(abase, ys, tok, x1, g2, fw, na):
    bsz, t, d = x1.shape
    n = ONEHOT_BLK
    grid_spec = pltpu.PrefetchScalarGridSpec(
        num_scalar_prefetch=1,
        grid=(bsz, t // n),
        in_specs=[pl.BlockSpec(memory_space=pl.ANY),
                  pl.BlockSpec((1, na // n, n), lambda b, i, a: (b, 0, 0)),
                  pl.BlockSpec((1, n, d), lambda b, i, a: (b, i, 0)),
                  pl.BlockSpec((1, 1, d), lambda b, i, a: (b, 0, 0)),
                  pl.BlockSpec((1, d), lambda b, i, a: (0, 0))],
        out_specs=pl.BlockSpec((1, n, d), lambda b, i, a: (b, i, 0)),
        scratch_shapes=[pltpu.VMEM((COMBINE_RING, n, d // 2), I32), pltpu.VMEM((n, d), F32),
                        pltpu.SemaphoreType.DMA((COMBINE_RING,)), pltpu.SMEM((2,), I32)],
    )
    return pl.pallas_call(
        functools.partial(_combine_body, na=na),
        grid_spec=grid_spec,
        out_shape=jax.ShapeDtypeStruct((bsz, t, d), F32),
        compiler_params=_params(("arbitrary", "arbitrary")),
        name="combine_final",
    )(abase, ys, tok, x1, g2, fw)


def kernel(x, c, ctx, c_ctx, ada_w, ada_b, norm1_w, w_in, conv_w, conv_b, hg_lb_logits, ml_gate_b,
           hg_norm_w, ml_norm_w, w_out, norm2_w, router_w, exp_w_gate, exp_w_up, exp_w_down, final_norm_w):
    bsz, t, d = x.shape
    nctx = ctx.shape[1]
    assert ada_w.shape[0] == 1, "single-layer block"
    assert nctx == SCAN_STEP and t % SCAN_STEP == 0 and bsz + 1 <= 8
    tall = t + nctx
    nx = t // SCAN_STEP
    kw = HEADS_W
    ne = router_w.shape[-1]
    cap = EC_CAPACITY * t // ne
    tm = min(512, t)

    rows = jnp.concatenate([c, c_ctx[None], jnp.zeros((7 - bsz, d), F32)], axis=0)
    mod = _modulation(rows, ada_w[0], ada_b[0][None])
    mx = [m[:, None, :] for m in jnp.split(mod[:bsz], 6, axis=-1)]
    mc = [m[:, None, :] for m in jnp.split(mod[bsz:bsz + 1], 6, axis=-1)]
    sh1, sc1, g1, sh2, sc2, g2 = mx
    csh1, csc1 = mc[0], mc[1]

    main_w = 9 * kw
    w_main = w_in[0][:, :main_w].astype(BF16)
    w_gt = w_in[0][:, main_w:].T
    gate_b = ml_gate_b[0][:, None]
    nw1 = norm1_w[0][None]
    outs = _inproj(x, sc1, sh1, nw1, w_main, w_gt, gate_b, hg_lb_logits, tall, tm, 0)
    outs = _inproj(ctx, csc1, csh1, nw1, w_main, w_gt, gate_b, hg_lb_logits, tall, nctx, t // nctx, prev=outs)
    hgq, hgv, hgg, hgk, hglf, mlqk_pre, mlv, mlo, gates = outs

    mlqk = _conv(mlqk_pre, conv_w[0], conv_b[0][None], t, nctx)
    hg_f, hg_b = _hgrn2(hgq, hgk, hgv, hglf, nx)
    ml_f, ml_b = _mlstm(mlqk, mlv, gates, nx)

    x1, vpk, aff = _mixer_out(hg_f, hg_b, ml_f, ml_b, hgg, mlo, x, g1, sc2, sh2,
                             hg_norm_w[0][None], ml_norm_w[0][None], w_out[0].astype(BF16),
                             norm2_w[0][None], router_w[0].T, tm)

    pos4, base = _topk(aff.reshape(bsz, ne, t // 128, 128), cap)
    na = ne * cap
    pos_flat = pos4.reshape(-1)
    xs, gate = _sc_invert_gather(vpk.reshape(bsz * t, d // 2), pos_flat, aff.reshape(-1), bsz * ne, ne, t, cap)
    perm, tok = _sc_assignment_order(pos_flat, base.reshape(-1), bsz, ne, t, cap)
    y = _moe_experts(xs.reshape(bsz * ne, cap, d // 2), gate.reshape(bsz * ne, cap // 128, 128),
                     exp_w_gate[0], exp_w_up[0], exp_w_down[0], bsz)
    ys = _sc_gather_rows(y.reshape(bsz * na, d // 2), perm)
    abase = jnp.concatenate([base.reshape(bsz, t)[:, ::ONEHOT_BLK], jnp.full((bsz, 1), na, I32)], axis=1)
    return _combine_final(abase, ys, tok.reshape(bsz, na // ONEHOT_BLK, ONEHOT_BLK), x1, g2, final_norm_w[None], na)
```

```python
import dataclasses
import functools

import numpy as np
import jax
import jax.numpy as jnp
from jax import lax
from jax.experimental import pallas as pl
from jax.experimental.pallas import tpu as pltpu
from jax.experimental.pallas import tpu_sc as plsc

F32 = jnp.float32
BF16 = jnp.bfloat16
I32 = jnp.int32
HIGHEST = lax.Precision.HIGHEST
NORM_EPS = 1e-6
LOG2_E = 1.4426950408889634

HEAD_DIM = 128
N_HEADS = 4
HEADS_W = N_HEADS * HEAD_DIM
GRID_W = 64
N_EXPERTS = 16
EC_CAPACITY = 2
HG_CHUNK = 64
HG_VPU_LEVEL_MIN = 4
ML_CHUNK = 128
SCAN_STEP = 256
ML_SAMPLES = 2
ONEHOT_BLK = 256
CONV_HALO = 72
SC_GATHER_ROWS = 64
COMBINE_RING = 6
TOPK_BISECTIONS = 64
VMEM_LIMIT = 56 * 1024 * 1024

_NT = (((1,), (1,)), ((), ()))
_TN = (((0,), (0,)), ((), ()))


def _dot(a, b, dims=None, precision=None):
    if dims is None:
        return jnp.dot(a, b, preferred_element_type=F32, precision=precision)
    return lax.dot_general(a, b, dims, preferred_element_type=F32, precision=precision)


def _dot_split(w, x, x1):
    n = w.shape[0]
    x2 = (x - x1.astype(F32)).astype(BF16)
    w1 = w.astype(BF16)
    wr = w - w1.astype(F32)
    w2 = wr.astype(BF16)
    w3 = (wr - w2.astype(F32)).astype(BF16)
    pa = _dot(jnp.concatenate([w1, w2, w3], axis=0), x1, _NT)
    pb = _dot(jnp.concatenate([w1, w2], axis=0), x2, _NT)
    return pa[0:n] + pa[n:2 * n] + pa[2 * n:] + pb[0:n] + pb[n:]


def _sigmoid(x):
    return jax.nn.sigmoid(x)


def _pack_halves(x):
    w = x.shape[-1] // 2
    bits = lax.bitcast_convert_type(x.astype(BF16).astype(F32), I32)
    return lax.shift_right_logical(bits[:, :w], 16) | bits[:, w:]


def _unpack_halves(p):
    lo = lax.bitcast_convert_type(lax.shift_left(p, 16), F32)
    hi = lax.bitcast_convert_type(p & jnp.int32(-65536), F32)
    return jnp.concatenate([lo, hi], axis=-1).astype(BF16)


def _silu(x):
    return x * jax.nn.sigmoid(x)


def _params(sem, flags=None):
    return pltpu.CompilerParams(dimension_semantics=sem, vmem_limit_bytes=VMEM_LIMIT, flags=flags)


def _mod_body(r_ref, w_ref, b_ref, o_ref):
    r = r_ref[...]
    o_ref[...] = _dot(_silu(r), w_ref[...], precision=HIGHEST) + b_ref[...]


def _modulation(rows, w, b):
    d, n = w.shape
    tn = n // 4
    return pl.pallas_call(
        _mod_body,
        grid=(n // tn,),
        in_specs=[pl.BlockSpec((8, d), lambda j: (0, 0)),
                  pl.BlockSpec((d, tn), lambda j: (0, j)),
                  pl.BlockSpec((1, tn), lambda j: (0, j))],
        out_specs=pl.BlockSpec((8, tn), lambda j: (0, j)),
        out_shape=jax.ShapeDtypeStruct((8, n), F32),
        compiler_params=_params(("arbitrary",)),
        name="modulation",
    )(rows, w, b)


def _log_sigmoid(x):
    return jnp.minimum(x, 0.0) - jnp.log(1.0 + jnp.exp(-jnp.abs(x)))


def _inproj_body(x_ref, sc_ref, sh_ref, nw_ref, w_ref, wg_ref, gb_ref, lbl_ref, *refs):
    hgq_ref, hgv_ref, hgg_ref, hgk_ref, hglf_ref, mlqk_ref, mlv_ref, mlo_ref, gates_ref = refs[-9:]
    kw = HEADS_W
    x = x_ref[0]
    y = x * lax.rsqrt(jnp.mean(x * x, axis=-1, keepdims=True) + NORM_EPS) * nw_ref[...]
    u = y * (1.0 + sc_ref[0]) + sh_ref[0]
    ub = u.astype(BF16)

    def proj(c0, c1):
        return _dot(ub, w_ref[:, c0:c1])

    hgq_ref[0] = _silu(proj(0, kw)).astype(BF16)
    hgv_ref[0] = proj(kw, 2 * kw).astype(BF16)
    hgg_ref[0] = _silu(proj(2 * kw, 3 * kw)).astype(BF16)

    lbl = lbl_ref[...]
    mx = jnp.max(lbl, axis=0)
    ex = jnp.exp(lbl - mx[None])
    lb = ex[0] / jnp.sum(ex, axis=0)
    for d in range(2):
        p = proj((3 + d) * kw, (4 + d) * kw)
        lbd = lb[d:d + 1]
        f = lbd + (1.0 - lbd) * _sigmoid(p)
        hgk_ref[0, :, d * kw:(d + 1) * kw] = (1.0 - f).astype(BF16)
        hglf_ref[0, :, d * kw:(d + 1) * kw] = jnp.log(f)

    mlqk_ref[0, :, 0:kw] = proj(5 * kw, 6 * kw)
    mlqk_ref[0, :, kw:2 * kw] = proj(6 * kw, 7 * kw)
    mlv_ref[0] = proj(7 * kw, 8 * kw).astype(BF16)
    mlo_ref[0] = _sigmoid(proj(8 * kw, 9 * kw)).astype(BF16)

    g = _dot_split(wg_ref[...], u, ub) + gb_ref[...]
    row = lax.broadcasted_iota(I32, g.shape, 0)
    gates_ref[0] = jnp.where((row % 8) >= N_HEADS, _log_sigmoid(g), g)


def _inproj_shapes(bsz, tall):
    kw = HEADS_W
    return [
        jax.ShapeDtypeStruct((bsz, tall, kw), BF16),
        jax.ShapeDtypeStruct((bsz, tall, kw), BF16),
        jax.ShapeDtypeStruct((bsz, tall, kw), BF16),
        jax.ShapeDtypeStruct((bsz, tall, 2 * kw), BF16),
        jax.ShapeDtypeStruct((bsz, tall, 2 * kw), F32),
        jax.ShapeDtypeStruct((bsz, tall, 2 * kw), F32),
        jax.ShapeDtypeStruct((bsz, tall, kw), BF16),
        jax.ShapeDtypeStruct((bsz, tall, kw), BF16),
        jax.ShapeDtypeStruct((bsz, 4 * N_HEADS, tall), F32),
    ]


def _inproj(tokens, scale, shift, nw, w_main, w_gt, gate_b, lb_logits, tall, tm, blk0, prev=None):
    bsz, n, d = tokens.shape
    kw = HEADS_W
    nt = n // tm
    per_sample = scale.shape[0] == bsz
    mod_map = (lambda b, i: (b, 0, 0)) if per_sample else (lambda b, i: (0, 0, 0))
    const2 = lambda b, i: (0, 0)
    in_specs = [
        pl.BlockSpec((1, tm, d), lambda b, i: (b, i, 0)),
        pl.BlockSpec((1, 1, d), mod_map),
        pl.BlockSpec((1, 1, d), mod_map),
        pl.BlockSpec((1, d), const2),
        pl.BlockSpec(w_main.shape, const2),
        pl.BlockSpec(w_gt.shape, const2),
        pl.BlockSpec(gate_b.shape, const2),
        pl.BlockSpec(lb_logits.shape, lambda b, i: (0, 0, 0)),
    ]
    args = [tokens, scale, shift, nw, w_main, w_gt, gate_b, lb_logits]
    aliases = {}
    if prev is not None:
        for k, a in enumerate(prev):
            in_specs.append(pl.BlockSpec(memory_space=pl.ANY))
            aliases[len(args)] = k
            args.append(a)
    row_map = lambda b, i: (b, blk0 + i, 0)
    widths = [kw, kw, kw, 2 * kw, 2 * kw, 2 * kw, kw, kw]
    out_specs = [pl.BlockSpec((1, tm, w), row_map) for w in widths]
    out_specs.append(pl.BlockSpec((1, 4 * N_HEADS, tm), lambda b, i: (b, 0, blk0 + i)))
    return pl.pallas_call(
        _inproj_body,
        grid=(bsz, nt),
        in_specs=in_specs,
        out_specs=out_specs,
        out_shape=_inproj_shapes(bsz, tall),
        input_output_aliases=aliases,
        compiler_params=_params(("arbitrary", "arbitrary")),
        name="inproj_ctx" if prev is not None else "inproj_x",
    )(*args)


def _conv_body(x_ref, w_ref, b_ref, o_ref, pad_ref, cpad_ref, *, t, nctx, scale_from):
    halo = CONV_HALO
    rows = 512
    win = rows + 2 * halo
    ch = x_ref.shape[-1]
    scale = jnp.where(pl.program_id(1) >= scale_from, HEAD_DIM ** -0.5, 1.0).astype(F32)
    w = w_ref[...]
    bias = b_ref[...]

    pad_ref[0:halo, :] = jnp.zeros((halo, ch), F32)
    pad_ref[halo + t:halo + t + halo, :] = jnp.zeros((halo, ch), F32)
    pad_ref[halo:halo + t, :] = x_ref[0, 0:t, :]
    col = (lax.broadcasted_iota(I32, (win, ch), 0) + (GRID_W - halo % GRID_W)) % GRID_W
    left_ok = col > 0
    right_ok = col < GRID_W - 1

    def chunk(c, carry):
        o = pl.multiple_of(c * rows, rows)
        xw = pad_ref[pl.ds(o, win), :]
        xm = jnp.where(left_ok, pltpu.roll(xw, 1, 0), 0.0)
        xp = jnp.where(right_ok, pltpu.roll(xw, win - 1, 0), 0.0)
        def taps(dr, lo):
            sl = slice(lo, lo + rows)
            return xm[sl] * w[dr, 0:1] + xw[sl] * w[dr, 1:2] + xp[sl] * w[dr, 2:3]

        y = taps(1, halo) + taps(0, halo - GRID_W) + taps(2, halo + GRID_W)
        o_ref[0, pl.ds(o, rows), :] = (_silu(y + bias) * scale).astype(o_ref.dtype)
        return carry

    lax.fori_loop(0, t // rows, chunk, 0)

    cpad_ref[0:8, :] = jnp.zeros((8, ch), F32)
    cpad_ref[8 + nctx:16 + nctx, :] = jnp.zeros((8, ch), F32)
    cpad_ref[8:8 + nctx, :] = x_ref[0, t:t + nctx, :]
    xw = cpad_ref[...]
    n = nctx + 16
    y = (pltpu.roll(xw, 1, 0) * w[1, 0:1] + xw * w[1, 1:2] + pltpu.roll(xw, n - 1, 0) * w[1, 2:3])[8:8 + nctx]
    o_ref[0, t:t + nctx, :] = (_silu(y + bias) * scale).astype(o_ref.dtype)


def _conv(qk_pre, conv_w, conv_b, t, nctx):
    bsz, tall, c = qk_pre.shape
    ch = 128
    body = functools.partial(_conv_body, t=t, nctx=nctx, scale_from=(c // 2) // ch)
    return pl.pallas_call(
        body,
        grid=(bsz, c // ch),
        in_specs=[pl.BlockSpec((1, tall, ch), lambda b, j: (b, 0, j)),
                  pl.BlockSpec((3, 3, ch), lambda b, j: (0, 0, j)),
                  pl.BlockSpec((1, ch), lambda b, j: (0, j))],
        out_specs=pl.BlockSpec((1, tall, ch), lambda b, j: (b, 0, j)),
        out_shape=jax.ShapeDtypeStruct((bsz, tall, c), BF16),
        scratch_shapes=[pltpu.VMEM((t + 2 * CONV_HALO, ch), F32),
                        pltpu.VMEM((nctx + 16, ch), F32)],
        compiler_params=_params(("arbitrary", "arbitrary")),
        name="qk_conv",
    )(qk_pre, conv_w, conv_b)


def _fwd_blk(s, nx):
    return jnp.where(s == 0, nx, s - 1)


def _bwd_blk(s, nx):
    return jnp.where(s == 0, nx, nx - s)


def _hg_constants(rev):
    c = HG_CHUNK
    i = np.arange(c)[:, None]
    j = np.arange(c)[None, :]
    blocks = [(j >= i) if rev else (j <= i)]
    masks = [i == j]
    m = c // 2
    while m >= 1:
        b0 = (i // (2 * m)) * (2 * m)
        same = (i // (2 * m)) == (j // (2 * m))
        if rev:
            beta = b0 + m
            qrow = (i % (2 * m)) < m
            g = np.where(qrow, (j >= i) & (j < beta), (j >= beta) & (j < i))
            mask = same & qrow & ((j % (2 * m)) >= m)
        else:
            beta = b0 + m - 1
            qrow = (i % (2 * m)) >= m
            g = np.where(qrow, (j > beta) & (j <= i), (j > i) & (j <= beta))
            mask = same & qrow & ((j % (2 * m)) < m)
        if m < HG_VPU_LEVEL_MIN:
            blocks.append(g)
        masks.append(mask)
        m //= 2
    g = np.concatenate(blocks, axis=0).astype(np.float32)
    g3 = np.concatenate([g, g, g], axis=1)
    m2 = np.concatenate([np.stack(masks), np.stack(masks)], axis=2)
    return (jnp.asarray(g3, BF16), jnp.asarray(m2, F32))


def _hg_level_decay(a, m, rev):
    c = a.shape[0]
    parts = []
    for b0 in range(0, c, 2 * m):
        beta = b0 + m if rev else b0 + m - 1
        ref = a[beta:beta + 1]
        if m % 8 == 0:
            first, second = a[b0:b0 + m], a[b0 + m:b0 + 2 * m]
            parts += [first - ref, ref - second] if rev else [ref - first, second - ref]
        else:
            d = a[b0:b0 + 2 * m] - ref
            parts.append(jnp.minimum(d, -d))
    return jnp.concatenate(parts, axis=0)


def _block_diag(x, zero):
    w = x.shape[1] // 2
    return jnp.concatenate([jnp.concatenate([x[:, :w], zero], axis=1),
                            jnp.concatenate([zero, x[:, w:]], axis=1)], axis=0)


def _hg_chunk(dirs):
    c = HG_CHUNK
    w = 2 * HEAD_DIM
    zero = jnp.zeros((c, HEAD_DIM), BF16)
    units = []
    for rev, r0, q_ref, k_ref, v_ref, lf_ref, g_ref, msk_ref, o_ref, st_ref in dirs:
        rows = pl.ds(r0, c)
        lf = lf_ref[0, rows, :] * LOG2_E
        p1 = lf.astype(BF16)
        r1 = lf - p1.astype(F32)
        p2 = r1.astype(BF16)
        p3 = (r1 - p2.astype(F32)).astype(BF16)
        dall = _dot(g_ref[...], jnp.concatenate([p1, p2, p3], axis=0))
        for hp in range(N_HEADS // 2):
            cs = slice(hp * w, (hp + 1) * w)
            units.append(dict(rev=rev, rows=rows, cs=cs, hp=hp, msk_ref=msk_ref, o_ref=o_ref, st_ref=st_ref,
                              q=q_ref[0, rows, cs], k=k_ref[0, rows, cs], v=v_ref[0, rows, cs], dall=dall[:, cs]))
    for u in units:
        msk_ref = u["msk_ref"]
        nlev = msk_ref.shape[0] - 1
        att = _dot(u["q"], _block_diag(u["k"], zero), _NT) * msk_ref[0]
        a = u["dall"][0:c]
        row = 1
        for l in range(nlev):
            m = c >> (l + 1)
            if m >= HG_VPU_LEVEL_MIN:
                dec = _hg_level_decay(a, m, u["rev"])
            else:
                dec = u["dall"][row * c:(row + 1) * c]
                row += 1
            e = jnp.exp2(dec).astype(BF16)
            att = att + _dot(u["q"] * e, _block_diag(u["k"] * e, zero), _NT) * msk_ref[l + 1]
        u["att"] = att.astype(BF16)
    for u in units:
        a = u["dall"][0:c]
        u["a_tot"] = a[0:1] if u["rev"] else a[c - 1:c]
        u["st"] = [u["st_ref"][2 * u["hp"] + i] for i in range(2)]
        zf = jnp.zeros((HEAD_DIM, HEAD_DIM), BF16)
        st2 = jnp.concatenate([jnp.concatenate([u["st"][0].astype(BF16), zf], axis=1),
                               jnp.concatenate([zf, u["st"][1].astype(BF16)], axis=1)], axis=0)
        qbar = (u["q"].astype(F32) * jnp.exp2(a)).astype(BF16)
        u["o"] = _dot(u["att"], _block_diag(u["v"], zero)) + _dot(qbar, st2, _NT)
        u["khat"] = (u["k"].astype(F32) * jnp.exp2(u["a_tot"] - a)).astype(BF16)
    for u in units:
        u["o_ref"][0, u["rows"], u["cs"]] = u["o"].astype(BF16)
        for i in range(2):
            hs = slice(i * HEAD_DIM, (i + 1) * HEAD_DIM)
            upd = _dot(u["v"][:, hs], u["khat"][:, hs], _TN)
            u["st_ref"][2 * u["hp"] + i] = u["st"][i] * jnp.exp2(u["a_tot"][:, hs]) + upd


def _hg_body(qf_ref, kf_ref, vf_ref, lff_ref, qb_ref, kb_ref, vb_ref, lfb_ref,
             gf_ref, mf_ref, gb_ref, mb_ref, of_ref, ob_ref, st_ref):
    @pl.when(pl.program_id(1) == 0)
    def _():
        st_ref[...] = jnp.zeros(st_ref.shape, F32)

    nsub = SCAN_STEP // HG_CHUNK
    for c in range(nsub):
        _hg_chunk([
            (False, c * HG_CHUNK, qf_ref, kf_ref, vf_ref, lff_ref, gf_ref, mf_ref, of_ref, st_ref.at[0]),
            (True, (nsub - 1 - c) * HG_CHUNK, qb_ref, kb_ref, vb_ref, lfb_ref, gb_ref, mb_ref, ob_ref, st_ref.at[1]),
        ])


def _hgrn2(hgq, hgk, hgv, hglf, nx):
    bsz, tall, kw = hgq.shape
    steps = tall // SCAN_STEP
    cf = _hg_constants(False)
    cb = _hg_constants(True)
    blk = (1, SCAN_STEP, kw)
    fwd = lambda col: (lambda b, s: (b, _fwd_blk(s, nx), col))
    bwd = lambda col: (lambda b, s: (b, _bwd_blk(s, nx), col))
    const = lambda a: pl.BlockSpec(a.shape, lambda b, s: (0,) * a.ndim)
    in_specs = [pl.BlockSpec(blk, fwd(0)), pl.BlockSpec(blk, fwd(0)), pl.BlockSpec(blk, fwd(0)), pl.BlockSpec(blk, fwd(0)),
                pl.BlockSpec(blk, bwd(0)), pl.BlockSpec(blk, bwd(1)), pl.BlockSpec(blk, bwd(0)), pl.BlockSpec(blk, bwd(1))]
    in_specs += [const(a) for a in cf + cb]
    out_sds = jax.ShapeDtypeStruct((bsz, tall, kw), BF16)
    return pl.pallas_call(
        _hg_body,
        grid=(bsz, steps),
        in_specs=in_specs,
        out_specs=[pl.BlockSpec(blk, fwd(0)), pl.BlockSpec(blk, bwd(0))],
        out_shape=[out_sds, out_sds],
        scratch_shapes=[pltpu.VMEM((2, N_HEADS, HEAD_DIM, HEAD_DIM), F32)],
        compiler_params=_params(("arbitrary", "arbitrary")),
        name="hgrn2_scan",
    )(hgq, hgk, hgv, hglf, hgq, hgk, hgv, hglf, *cf, *cb)


def _ml_constants(rev):
    k = np.arange(ML_CHUNK)
    tri = (k[:, None] >= k[None, :]) if rev else (k[:, None] <= k[None, :])
    return jnp.asarray(np.concatenate([tri, tri, tri], axis=0), BF16)


def _ml_chunk(dirs):
    c = ML_CHUNK
    ii = lax.broadcasted_iota(I32, (c, c), 0)
    jj = lax.broadcasted_iota(I32, (c, c), 1)
    ones = jnp.ones((c, HEAD_DIM), BF16)
    units = []
    for rev, r0, d, bb, q_ref, k_ref, v_ref, g_ref, tri3_ref, o_ref, st_ref, m_ref in dirs:
        gates = g_ref[bb, :, pl.ds(r0, c)] * LOG2_E
        p1 = gates.astype(BF16)
        r1 = gates - p1.astype(F32)
        p2 = r1.astype(BF16)
        p3 = (r1 - p2.astype(F32)).astype(BF16)
        csum = _dot(jnp.concatenate([p1, p2, p3], axis=1), tri3_ref[...])
        for h in range(N_HEADS):
            cs = slice(h * HEAD_DIM, (h + 1) * HEAD_DIM)
            u = dict(rev=rev, o_ref=o_ref, bb=bb, rows=pl.ds(r0, c), cs=cs, st_ref=st_ref, m_ref=m_ref, h=h)
            u["qb"] = q_ref[bb, pl.ds(r0, c), cs]
            u["kb"] = k_ref[bb, pl.ds(r0, c), cs]
            u["v1"] = jnp.concatenate([v_ref[bb, pl.ds(r0, c), cs], ones], axis=1)
            irow = gates[d * 8 + h:d * 8 + h + 1]
            u["brow"] = csum[d * 8 + N_HEADS + h:d * 8 + N_HEADS + h + 1]
            u["rrow"] = irow - u["brow"]
            units.append(u)
    for u in units:
        u["st"] = u["st_ref"][u["h"]]
        a = _dot(jnp.concatenate([u["kb"], u["st"].astype(BF16)], axis=0), u["qb"], _NT)
        u["s"] = a[:c]
        u["sq"] = a[c:]
    for u in units:
        last = 0 if u["rev"] else c - 1
        tri_t = (ii >= jj) if u["rev"] else (ii <= jj)
        rcol = jnp.concatenate([u["rrow"], jnp.zeros((7, c), F32)], axis=0).T[:, 0:1]
        u["mprev"] = u["m_ref"][u["h"]][:, 0:1]
        rmat = jnp.where(tri_t, rcol, -jnp.inf)
        u["grow"] = jnp.maximum(jnp.max(rmat, axis=0, keepdims=True), u["mprev"])
        qk = (u["s"] * jnp.exp2(rmat - u["grow"])).astype(BF16)
        blast = u["brow"][:, last:last + 1]
        u["mnew"] = blast + u["grow"][:, last:last + 1]
        kh = (u["kb"].astype(F32) * jnp.exp2(blast + rcol - u["mnew"])).astype(BF16)
        u["ws"] = jnp.exp2(blast + u["mprev"] - u["mnew"])
        u["qkh"] = jnp.concatenate([qk, kh], axis=1)
    for u in units:
        u["nu"] = _dot(u["v1"], u["qkh"], _TN)
    for u in units:
        both = u["nu"][:, :c] + jnp.exp2(u["mprev"] - u["grow"]) * u["sq"]
        den = both[HEAD_DIM:HEAD_DIM + 1]
        inv = 1.0 / jnp.maximum(jnp.abs(den), jnp.exp2(-(u["brow"] + u["grow"])))
        u["o_ref"][u["bb"], u["rows"], u["cs"]] = (both[:HEAD_DIM] * inv).T.astype(BF16)
        u["st_ref"][u["h"]] = u["ws"] * u["st"] + u["nu"][:, c:]
        u["m_ref"][u["h"]] = jnp.broadcast_to(u["mnew"], (1, HEAD_DIM))


def _ml_body(qf_ref, kf_ref, vf_ref, gf_ref, qb_ref, kb_ref, vb_ref, gb_ref, tf_ref, tb_ref,
             of_ref, ob_ref, st_ref, m_ref):
    @pl.when(pl.program_id(1) == 0)
    def _():
        st_ref[...] = jnp.zeros(st_ref.shape, F32)
        m_ref[...] = jnp.zeros(m_ref.shape, F32)

    nsub = SCAN_STEP // ML_CHUNK
    for c in range(nsub):
        dirs = []
        for bb in range(qf_ref.shape[0]):
            dirs.append((False, c * ML_CHUNK, 0, bb, qf_ref, kf_ref, vf_ref, gf_ref, tf_ref, of_ref,
                         st_ref.at[0, bb], m_ref.at[0, bb]))
            dirs.append((True, (nsub - 1 - c) * ML_CHUNK, 1, bb, qb_ref, kb_ref, vb_ref, gb_ref, tb_ref, ob_ref,
                         st_ref.at[1, bb], m_ref.at[1, bb]))
        _ml_chunk(dirs)


def _mlstm(mlqk, mlv, gates, nx):
    bsz, tall, kw = mlv.shape
    steps = tall // SCAN_STEP
    nb = ML_SAMPLES if bsz % ML_SAMPLES == 0 else 1
    blk = (nb, SCAN_STEP, kw)
    gblk = (nb, 4 * N_HEADS, SCAN_STEP)
    tf, tb = _ml_constants(False), _ml_constants(True)
    fwd = lambda col: (lambda b, s: (b, _fwd_blk(s, nx), col))
    bwd = lambda col: (lambda b, s: (b, _bwd_blk(s, nx), col))
    const = pl.BlockSpec(tf.shape, lambda b, s: (0, 0))
    in_specs = [pl.BlockSpec(blk, fwd(0)), pl.BlockSpec(blk, fwd(1)), pl.BlockSpec(blk, fwd(0)),
                pl.BlockSpec(gblk, lambda b, s: (b, 0, _fwd_blk(s, nx))),
                pl.BlockSpec(blk, bwd(0)), pl.BlockSpec(blk, bwd(1)), pl.BlockSpec(blk, bwd(0)),
                pl.BlockSpec(gblk, lambda b, s: (b, 0, _bwd_blk(s, nx))), const, const]
    out_sds = jax.ShapeDtypeStruct((bsz, tall, kw), BF16)
    return pl.pallas_call(
        _ml_body,
        grid=(bsz // nb, steps),
        in_specs=in_specs,
        out_specs=[pl.BlockSpec(blk, fwd(0)), pl.BlockSpec(blk, bwd(0))],
        out_shape=[out_sds, out_sds],
        scratch_shapes=[pltpu.VMEM((2, nb, N_HEADS, 2 * HEAD_DIM, HEAD_DIM), F32),
                        pltpu.VMEM((2, nb, N_HEADS, 1, HEAD_DIM), F32)],
        compiler_params=_params(("arbitrary", "arbitrary")),
        name="mlstm_scan",
    )(mlqk, mlqk, mlv, gates, mlqk, mlqk, mlv, gates, tf, tb)


def _out_body(hof_ref, hob_ref, mhf_ref, mhb_ref, hgg_ref, mlo_ref, x_ref, g1_ref, sc2_ref, sh2_ref,
              hnw_ref, mnw_ref, wout_ref, n2w_ref, rwt_ref, x1_ref, vt_ref, aff_ref):
    hg = hof_ref[0].astype(F32) + hob_ref[0].astype(F32)
    ml = mhf_ref[0].astype(F32) + mhb_ref[0].astype(F32)
    hparts, mparts = [], []
    for h in range(N_HEADS):
        cs = slice(h * HEAD_DIM, (h + 1) * HEAD_DIM)
        t = hg[:, cs]
        hparts.append(t * lax.rsqrt(jnp.mean(t * t, axis=-1, keepdims=True) + NORM_EPS))
        t = ml[:, cs]
        t = t - jnp.mean(t, axis=-1, keepdims=True)
        mparts.append(t * lax.rsqrt(jnp.mean(t * t, axis=-1, keepdims=True) + NORM_EPS))
    hgn = jnp.concatenate(hparts, axis=-1) * hnw_ref[...] * hgg_ref[0].astype(F32)
    mln = jnp.concatenate(mparts, axis=-1) * mnw_ref[...] * mlo_ref[0].astype(F32)
    mix = jnp.concatenate([hgn, mln], axis=-1).astype(BF16)
    x1 = x_ref[0] + g1_ref[0] * _dot(mix, wout_ref[...])
    x1_ref[0] = x1
    v = x1 * lax.rsqrt(jnp.mean(x1 * x1, axis=-1, keepdims=True) + NORM_EPS) * n2w_ref[...]
    v = v * (1.0 + sc2_ref[0]) + sh2_ref[0]
    vt_ref[0] = _pack_halves(v)
    logits = _dot_split(rwt_ref[...], v, v.astype(BF16))
    ex = jnp.exp(logits - jnp.max(logits, axis=0, keepdims=True))
    aff_ref[0] = ex / jnp.sum(ex, axis=0, keepdims=True)


def _mixer_out(hg_f, hg_b, ml_f, ml_b, hgg, mlo, x, g1, sc2, sh2, hnw, mnw, w_out, n2w, rwt, tm):
    bsz, t, d = x.shape
    kw = HEADS_W
    ne = rwt.shape[0]
    row = lambda b, i: (b, i, 0)
    mod = lambda b, i: (b, 0, 0)
    const2 = lambda b, i: (0, 0)
    act = pl.BlockSpec((1, tm, kw), row)
    in_specs = [act, act, act, act, act, act,
                pl.BlockSpec((1, tm, d), row),
                pl.BlockSpec((1, 1, d), mod), pl.BlockSpec((1, 1, d), mod), pl.BlockSpec((1, 1, d), mod),
                pl.BlockSpec((1, kw), const2), pl.BlockSpec((1, kw), const2),
                pl.BlockSpec(w_out.shape, const2), pl.BlockSpec((1, d), const2), pl.BlockSpec(rwt.shape, const2)]
    return pl.pallas_call(
        _out_body,
        grid=(bsz, t // tm),
        in_specs=in_specs,
        out_specs=[pl.BlockSpec((1, tm, d), row),
                   pl.BlockSpec((1, tm, d // 2), row),
                   pl.BlockSpec((1, ne, tm), lambda b, i: (b, 0, i))],
        out_shape=[jax.ShapeDtypeStruct((bsz, t, d), F32),
                   jax.ShapeDtypeStruct((bsz, t, d // 2), I32),
                   jax.ShapeDtypeStruct((bsz, ne, t), F32)],
        compiler_params=_params(("arbitrary", "arbitrary")),
        name="mixer_out",
    )(hg_f, hg_b, ml_f, ml_b, hgg, mlo, x, g1, sc2, sh2, hnw, mnw, w_out, n2w, rwt)


def _prefix_count(maskf, u_ref, ones_ref, bl_ref):
    e, nb, ln = maskf.shape
    x = maskf.reshape(e * nb, ln)
    xb = x.astype(BF16)
    incl = _dot(xb, u_ref[...])
    tot = _dot(xb, ones_ref[...])
    off = _dot(bl_ref[...], tot.astype(BF16))
    return (incl - x + off).reshape(e, nb, ln), off.reshape(e, nb, ln)


def _topk_body(aff_ref, u_ref, ones_ref, bl_ref, bl1_ref, pos_ref, base_ref, *, cap):
    x = aff_ref[0]
    ne = x.shape[0]

    def count(m):
        return jnp.sum(jnp.sum(jnp.where(m, 1.0, 0.0), axis=1, keepdims=True), axis=2, keepdims=True)

    def halve(_, carry):
        lo, hi = carry
        mid = 0.5 * (lo + hi)
        up = count(x > mid) >= cap
        return jnp.where(up, mid, lo), jnp.where(up, hi, mid)

    lo, hi = lax.fori_loop(0, TOPK_BISECTIONS, halve,
                           (jnp.full((ne, 1, 1), -1.0, F32), jnp.full((ne, 1, 1), 1.0, F32)))
    gt = jnp.where(x > hi, 1.0, 0.0)
    eq = jnp.where(x > lo, 1.0, 0.0) - gt
    need = cap - count(x > hi)
    eq_rank, _ = _prefix_count(eq, u_ref, ones_ref, bl_ref)
    sel = gt + eq * jnp.where(eq_rank < need, 1.0, 0.0)
    pos, _ = _prefix_count(sel, u_ref, ones_ref, bl_ref)
    pos_ref[0] = jnp.where(sel > 0, pos, -1.0).astype(I32)
    nsel = jnp.sum(sel, axis=0)
    nb16 = nsel.astype(BF16)
    incl = _dot(nb16, u_ref[...])
    tot = _dot(nb16, ones_ref[...])
    off = _dot(bl1_ref[...], tot, precision=HIGHEST)
    base_ref[0] = (incl - nsel + off).astype(I32)


def _topk(aff4, cap):
    bsz, ne, nb, ln = aff4.shape
    k = np.arange(ln)
    u = jnp.asarray(k[:, None] <= k[None, :], BF16)
    ones = jnp.ones((ln, ln), BF16)
    r = np.arange(ne * nb)
    bl = jnp.asarray(((r[:, None] // nb) == (r[None, :] // nb)) & (r[None, :] < r[:, None]), BF16)
    r1 = np.arange(nb)
    bl1 = jnp.asarray(r1[None, :] < r1[:, None], F32)
    blk = pl.BlockSpec((1, ne, nb, ln), lambda b: (b, 0, 0, 0))
    const = lambda a: pl.BlockSpec(a.shape, lambda b: (0, 0))
    return pl.pallas_call(
        functools.partial(_topk_body, cap=cap),
        grid=(bsz,),
        in_specs=[blk, const(u), const(ones), const(bl), const(bl1)],
        out_specs=[blk, pl.BlockSpec((1, nb, ln), lambda b: (b, 0, 0))],
        out_shape=[jax.ShapeDtypeStruct((bsz, ne, nb, ln), I32), jax.ShapeDtypeStruct((bsz, nb, ln), I32)],
        compiler_params=_params(("arbitrary",)),
        name="expert_topk",
    )(aff4, u, ones, bl, bl1)


def _sc_setup():
    info = plsc.get_sparse_core_info()
    mesh = plsc.VectorSubcoreMesh(core_axis_name="c", subcore_axis_name="s")
    params = dataclasses.replace(pltpu.CompilerParams(), needs_layout_passes=False)
    return info.num_cores, info.num_cores * info.num_subcores, info.num_lanes, mesh, params


def _sc_worker_id(nc):
    return lax.axis_index("s") * nc + lax.axis_index("c")


def _sc_gather_scratch(width, dtype):
    one = [pltpu.VMEM((SC_GATHER_ROWS,), I32), pltpu.VMEM((SC_GATHER_ROWS, width), dtype), pltpu.SemaphoreType.DMA]
    return one + one


def _sc_gather_chunks(table_hbm, out_hbm, idx_v, bufs, out_row0, n, lanes):
    chunk = SC_GATHER_ROWS

    def gather(c, buf):
        ich_v, rows_v, sem = buf
        for q in range(chunk // lanes):
            src = pl.ds(pl.multiple_of(c * chunk + q * lanes, lanes), lanes)
            ich_v[pl.ds(q * lanes, lanes)] = idx_v[src]
        return pltpu.make_async_copy(table_hbm.at[ich_v], rows_v, sem)

    def finish(c, buf):
        ich_v, rows_v, sem = buf
        pltpu.make_async_copy(table_hbm.at[ich_v], rows_v, sem).wait()
        pltpu.sync_copy(rows_v, out_hbm.at[pl.ds(pl.multiple_of(out_row0 + c * chunk, chunk), chunk)])

    npair = n // (2 * chunk)
    gather(0, bufs[0]).start()

    @pl.loop(0, npair)
    def _(i):
        gather(2 * i + 1, bufs[1]).start()
        finish(2 * i, bufs[0])

        @pl.when(i + 1 < npair)
        def _():
            gather(2 * i + 2, bufs[0]).start()

        finish(2 * i + 1, bufs[1])


def _sc_invert_gather(table, pos_flat, aff_flat, npairs, ne, t, cap):
    nc, nw, lanes, mesh, params = _sc_setup()
    split = max(1, nw // npairs)
    seg = cap // split
    per_w = npairs * split // nw
    width = table.shape[1]

    @functools.partial(
        pl.kernel, mesh=mesh, compiler_params=params,
        out_type=[jax.ShapeDtypeStruct((npairs * cap, width), table.dtype),
                  jax.ShapeDtypeStruct((npairs * cap,), F32)],
        scratch_types=[pltpu.VMEM((t,), I32), pltpu.VMEM((t,), F32), pltpu.VMEM((seg,), I32), pltpu.VMEM((seg,), F32)]
        + _sc_gather_scratch(width, table.dtype),
    )
    def body(table_hbm, pos_hbm, aff_hbm, out_hbm, gate_hbm, pos_v, aff_v, idx_v, gate_v, *g):
        wid = _sc_worker_id(nc)

        @pl.loop(0, per_w)
        def _(kk):
            item = wid * per_w + kk
            p = item // split
            lo = (item % split) * seg
            tok0 = (p // ne) * t
            pltpu.sync_copy(pos_hbm.at[pl.ds(pl.multiple_of(p * t, t), t)], pos_v)
            pltpu.sync_copy(aff_hbm.at[pl.ds(pl.multiple_of(p * t, t), t)], aff_v)

            @pl.loop(0, t // lanes)
            def _(i):
                v = pos_v[pl.ds(pl.multiple_of(i * lanes, lanes), lanes)] - lo
                tok = lax.iota(I32, lanes) + i * lanes
                plsc.store_scatter(idx_v, [v], tok, mask=(v >= 0) & (v < seg))

            @pl.loop(0, seg // lanes)
            def _(j):
                sl = pl.ds(pl.multiple_of(j * lanes, lanes), lanes)
                ii = idx_v[sl]
                gate_v[sl] = plsc.load_gather(aff_v, [ii])
                idx_v[sl] = ii + tok0

            row0 = p * cap + lo
            pltpu.sync_copy(gate_v, gate_hbm.at[pl.ds(pl.multiple_of(row0, seg), seg)])
            _sc_gather_chunks(table_hbm, out_hbm, idx_v, (g[0:3], g[3:6]), row0, seg, lanes)

    return body(table, pos_flat, aff_flat)


def _sc_assignment_order(pos_flat, base_flat, bsz, ne, t, cap):
    nc, nw, lanes, mesh, params = _sc_setup()
    na = ne * cap
    per_b = nw // bsz
    rng = na // per_b

    @functools.partial(
        pl.kernel, mesh=mesh, compiler_params=params,
        out_type=[jax.ShapeDtypeStruct((bsz * na,), I32), jax.ShapeDtypeStruct((bsz * na,), I32)],
        scratch_types=[pltpu.VMEM((t,), I32), pltpu.VMEM((t,), I32), pltpu.VMEM((t,), I32),
                       pltpu.VMEM((rng,), I32), pltpu.VMEM((rng,), I32)],
    )
    def body(pos_hbm, base_hbm, perm_hbm, tok_hbm, pos_v, base_v, rank_v, perm_v, tok_v):
        wid = _sc_worker_id(nc)
        b = wid // per_b
        a0 = (wid % per_b) * rng
        pltpu.sync_copy(base_hbm.at[pl.ds(pl.multiple_of(b * t, t), t)], base_v)

        @pl.loop(0, t // lanes)
        def _(i):
            rank_v[pl.ds(pl.multiple_of(i * lanes, lanes), lanes)] = jnp.zeros((lanes,), I32)

        @pl.loop(0, ne)
        def _(e):
            p = b * ne + e
            pltpu.sync_copy(pos_hbm.at[pl.ds(pl.multiple_of(p * t, t), t)], pos_v)

            @pl.loop(0, t // lanes)
            def _(i):
                sl = pl.ds(pl.multiple_of(i * lanes, lanes), lanes)
                v = pos_v[sl]
                r = rank_v[sl]
                a = base_v[sl] + r - a0
                sel = v >= 0
                mine = sel & (a >= 0) & (a < rng)
                plsc.store_scatter(perm_v, [a], v + p * cap, mask=mine)
                plsc.store_scatter(tok_v, [a], lax.iota(I32, lanes) + i * lanes, mask=mine)
                rank_v[sl] = r + jnp.where(sel, 1, 0)

        dst = pl.ds(pl.multiple_of(b * na + a0, rng), rng)
        pltpu.sync_copy(perm_v, perm_hbm.at[dst])
        pltpu.sync_copy(tok_v, tok_hbm.at[dst])

    return body(pos_flat, base_flat)


def _sc_gather_rows(table, idx):
    nc, nw, lanes, mesh, params = _sc_setup()
    n = idx.shape[0]
    per_w = n // nw
    width = table.shape[1]
    chunk = SC_GATHER_ROWS

    @functools.partial(
        pl.kernel, mesh=mesh, compiler_params=params,
        out_type=jax.ShapeDtypeStruct((n, width), table.dtype),
        scratch_types=[pltpu.VMEM((per_w,), I32)] + _sc_gather_scratch(width, table.dtype),
    )
    def body(table_hbm, idx_hbm, out_hbm, idx_v, *g):
        row0 = _sc_worker_id(nc) * per_w
        pltpu.sync_copy(idx_hbm.at[pl.ds(pl.multiple_of(row0, per_w), per_w)], idx_v)
        _sc_gather_chunks(table_hbm, out_hbm, idx_v, (g[0:3], g[3:6]), row0, per_w, lanes)

    return body(table, idx)


def _moe1_body(xs_ref, gate_ref, wg_ref, wu_ref, wd_ref, y_ref):
    xs = _unpack_halves(xs_ref[0])
    hg = _dot(xs, wg_ref[0].astype(BF16))
    hu = _dot(xs, wu_ref[0].astype(BF16))
    h = (_silu(hg) * hu).astype(BF16)
    y = _dot(h, wd_ref[0].astype(BF16))
    gt = gate_ref[0].T
    y = jnp.concatenate([y[k * 128:(k + 1) * 128] * gt[:, k:k + 1] for k in range(gt.shape[1])], axis=0)
    y_ref[0] = _pack_halves(y)


def _moe_experts(xs, gate, wg, wu, wd, bsz):
    npairs, cap, half = xs.shape
    ne, d, f = wg.shape
    pair = lambda b, e: (b * ne + e, 0, 0)
    expert = lambda b, e: (e, 0, 0)
    return pl.pallas_call(
        _moe1_body,
        grid=(bsz, ne),
        in_specs=[pl.BlockSpec((1, cap, half), pair),
                  pl.BlockSpec((1, cap // 128, 128), pair),
                  pl.BlockSpec((1, d, f), expert),
                  pl.BlockSpec((1, d, f), expert),
                  pl.BlockSpec((1, f, d), expert)],
        out_specs=pl.BlockSpec((1, cap, half), pair),
        out_shape=jax.ShapeDtypeStruct((npairs, cap, half), I32),
        compiler_params=_params(("arbitrary", "arbitrary")),
        cost_estimate=pl.CostEstimate(flops=6 * npairs * cap * d * f, transcendentals=npairs * cap * f,
                                      bytes_accessed=12 * npairs * d * f + 16 * npairs * cap * half),
        name="moe_experts",
    )(xs, gate, wg, wu, wd)


def _combine_body(abase_ref, ys_hbm, tok_ref, x1_ref, g2_ref, fw_ref, o_ref, buf_ref, acc_ref, sem_ref, cnt_ref, *, na):
    n = ONEHOT_BLK
    ring = buf_ref.shape[0]
    nch = na // n
    b, i = pl.program_id(0), pl.program_id(1)

    @pl.when(i == 0)
    def _():
        cnt_ref[0] = 0
        cnt_ref[1] = 0

    lo = abase_ref[b, i]
    hi = abase_ref[b, i + 1]
    c0 = lo // n
    c1 = jnp.where(hi > lo, (hi + n - 1) // n, c0)

    def copy(c):
        slot = c % ring
        row0 = pl.multiple_of(b * na + c * n, n)
        return pltpu.make_async_copy(ys_hbm.at[pl.ds(row0, n)], buf_ref.at[slot], sem_ref.at[slot])

    acc_ref[...] = jnp.zeros(acc_ref.shape, F32)
    tok_ids = lax.broadcasted_iota(I32, (n, n), 0) + i * n

    def step(c, carry):
        started = cnt_ref[0]
        ahead = jnp.minimum(c + ring, nch)

        def start(k, _):
            copy(k).start()
            return 0

        lax.fori_loop(started, ahead, start, 0)
        cnt_ref[0] = jnp.maximum(started, ahead)

        @pl.when(cnt_ref[1] <= c)
        def _():
            copy(c).wait()
            cnt_ref[1] = c + 1

        seg = jnp.where(tok_ids == tok_ref[0, pl.ds(c, 1), :], 1.0, 0.0).astype(BF16)
        acc_ref[...] += _dot(seg, _unpack_halves(buf_ref[c % ring]))
        return carry

    lax.fori_loop(c0, c1, step, 0)
    x2 = x1_ref[0] + g2_ref[0] * acc_ref[...]
    o_ref[0] = x2 * lax.rsqrt(jnp.mean(x2 * x2, axis=-1, keepdims=True) + NORM_EPS) * fw_ref[...]


def _combine_final(abase, ys, tok, x1, g2, fw, na):
    bsz, t, d = x1.shape
    n = ONEHOT_BLK
    grid_spec = pltpu.PrefetchScalarGridSpec(
        num_scalar_prefetch=1,
        grid=(bsz, t // n),
        in_specs=[pl.BlockSpec(memory_space=pl.ANY),
                  pl.BlockSpec((1, na // n, n), lambda b, i, a: (b, 0, 0)),
                  pl.BlockSpec((1, n, d), lambda b, i, a: (b, i, 0)),
                  pl.BlockSpec((1, 1, d), lambda b, i, a: (b, 0, 0)),
                  pl.BlockSpec((1, d), lambda b, i, a: (0, 0))],
        out_specs=pl.BlockSpec((1, n, d), lambda b, i, a: (b, i, 0)),
        scratch_shapes=[pltpu.VMEM((COMBINE_RING, n, d // 2), I32), pltpu.VMEM((n, d), F32),
                        pltpu.SemaphoreType.DMA((COMBINE_RING,)), pltpu.SMEM((2,), I32)],
    )
    return pl.pallas_call(
        functools.partial(_combine_body, na=na),
        grid_spec=grid_spec,
        out_shape=jax.ShapeDtypeStruct((bsz, t, d), F32),
        compiler_params=_params(("arbitrary", "arbitrary")),
        name="combine_final",
    )(abase, ys, tok, x1, g2, fw)


def kernel(x, c, ctx, c_ctx, ada_w, ada_b, norm1_w, w_in, conv_w, conv_b, hg_lb_logits, ml_gate_b,
           hg_norm_w, ml_norm_w, w_out, norm2_w, router_w, exp_w_gate, exp_w_up, exp_w_down, final_norm_w):
    bsz, t, d = x.shape
    nctx = ctx.shape[1]
    assert ada_w.shape[0] == 1, "single-layer block"
    assert nctx == SCAN_STEP and t % SCAN_STEP == 0 and bsz + 1 <= 8
    tall = t + nctx
    nx = t // SCAN_STEP
    kw = HEADS_W
    ne = router_w.shape[-1]
    cap = EC_CAPACITY * t // ne
    tm = min(512, t)

    rows = jnp.concatenate([c, c_ctx[None], jnp.zeros((7 - bsz, d), F32)], axis=0)
    mod = _modulation(rows, ada_w[0], ada_b[0][None])
    mx = [m[:, None, :] for m in jnp.split(mod[:bsz], 6, axis=-1)]
    mc = [m[:, None, :] for m in jnp.split(mod[bsz:bsz + 1], 6, axis=-1)]
    sh1, sc1, g1, sh2, sc2, g2 = mx
    csh1, csc1 = mc[0], mc[1]

    main_w = 9 * kw
    w_main = w_in[0][:, :main_w].astype(BF16)
    w_gt = w_in[0][:, main_w:].T
    gate_b = ml_gate_b[0][:, None]
    nw1 = norm1_w[0][None]
    outs = _inproj(x, sc1, sh1, nw1, w_main, w_gt, gate_b, hg_lb_logits, tall, tm, 0)
    outs = _inproj(ctx, csc1, csh1, nw1, w_main, w_gt, gate_b, hg_lb_logits, tall, nctx, t // nctx, prev=outs)
    hgq, hgv, hgg, hgk, hglf, mlqk_pre, mlv, mlo, gates = outs

    mlqk = _conv(mlqk_pre, conv_w[0], conv_b[0][None], t, nctx)
    hg_f, hg_b = _hgrn2(hgq, hgk, hgv, hglf, nx)
    ml_f, ml_b = _mlstm(mlqk, mlv, gates, nx)

    x1, vpk, aff = _mixer_out(hg_f, hg_b, ml_f, ml_b, hgg, mlo, x, g1, sc2, sh2,
                             hg_norm_w[0][None], ml_norm_w[0][None], w_out[0].astype(BF16),
                             norm2_w[0][None], router_w[0].T, min(2 * tm, t))

    pos4, base = _topk(aff.reshape(bsz, ne, t // 128, 128), cap)
    na = ne * cap
    pos_flat = pos4.reshape(-1)
    xs, gate = _sc_invert_gather(vpk.reshape(bsz * t, d // 2), pos_flat, aff.reshape(-1), bsz * ne, ne, t, cap)
    perm, tok = _sc_assignment_order(pos_flat, base.reshape(-1), bsz, ne, t, cap)
    y = _moe_experts(xs.reshape(bsz * ne, cap, d // 2), gate.reshape(bsz * ne, cap // 128, 128),
                     exp_w_gate[0], exp_w_up[0], exp_w_down[0], bsz)
    ys = _sc_gather_rows(y.reshape(bsz * na, d // 2), perm)
    abase = jnp.concatenate([base.reshape(bsz, t)[:, ::ONEHOT_BLK], jnp.full((bsz, 1), na, I32)], axis=1)
    return _combine_final(abase, ys, tok.reshape(bsz, na // ONEHOT_BLK, ONEHOT_BLK), x1, g2, final_norm_w[None], na)
```

```python
import dataclasses
import functools

import numpy as np
import jax
import jax.numpy as jnp
from jax import lax
from jax.experimental import pallas as pl
from jax.experimental.pallas import tpu as pltpu
from jax.experimental.pallas import tpu_sc as plsc

F32 = jnp.float32
BF16 = jnp.bfloat16
I32 = jnp.int32
HIGHEST = lax.Precision.HIGHEST
NORM_EPS = 1e-6
LOG2_E = 1.4426950408889634

HEAD_DIM = 128
N_HEADS = 4
HEADS_W = N_HEADS * HEAD_DIM
GRID_W = 64
N_EXPERTS = 16
EC_CAPACITY = 2
HG_CHUNK = 64
HG_VPU_LEVEL_MIN = 4
ML_CHUNK = 256
SCAN_STEP = 256
HG_SAMPLES = 2
ML_SAMPLES = 4
ONEHOT_BLK = 256
CONV_HALO = 72
SC_GATHER_ROWS = 64
COMBINE_RING = 6
TOPK_BISECTIONS = 64
VMEM_LIMIT = 56 * 1024 * 1024

_NT = (((1,), (1,)), ((), ()))
_TN = (((0,), (0,)), ((), ()))


def _dot(a, b, dims=None, precision=None):
    if dims is None:
        return jnp.dot(a, b, preferred_element_type=F32, precision=precision)
    return lax.dot_general(a, b, dims, preferred_element_type=F32, precision=precision)


def _dot_split(w, x, x1):
    n = w.shape[0]
    x2 = (x - x1.astype(F32)).astype(BF16)
    w1 = w.astype(BF16)
    wr = w - w1.astype(F32)
    w2 = wr.astype(BF16)
    w3 = (wr - w2.astype(F32)).astype(BF16)
    pa = _dot(jnp.concatenate([w1, w2, w3], axis=0), x1, _NT)
    pb = _dot(jnp.concatenate([w1, w2], axis=0), x2, _NT)
    return pa[0:n] + pa[n:2 * n] + pa[2 * n:] + pb[0:n] + pb[n:]


def _sigmoid(x):
    return jax.nn.sigmoid(x)


def _pack_halves(x):
    w = x.shape[-1] // 2
    bits = lax.bitcast_convert_type(x.astype(BF16).astype(F32), I32)
    return lax.shift_right_logical(bits[:, :w], 16) | bits[:, w:]


def _unpack_halves(p):
    lo = lax.bitcast_convert_type(lax.shift_left(p, 16), F32)
    hi = lax.bitcast_convert_type(p & jnp.int32(-65536), F32)
    return jnp.concatenate([lo, hi], axis=-1).astype(BF16)


def _silu(x):
    return x * jax.nn.sigmoid(x)


def _params(sem, flags=None):
    return pltpu.CompilerParams(dimension_semantics=sem, vmem_limit_bytes=VMEM_LIMIT, flags=flags)


def _mod_body(r_ref, w_ref, b_ref, o_ref):
    r = r_ref[...]
    o_ref[...] = _dot(_silu(r), w_ref[...], precision=HIGHEST) + b_ref[...]


def _modulation(rows, w, b):
    d, n = w.shape
    tn = n // 4
    return pl.pallas_call(
        _mod_body,
        grid=(n // tn,),
        in_specs=[pl.BlockSpec((8, d), lambda j: (0, 0)),
                  pl.BlockSpec((d, tn), lambda j: (0, j)),
                  pl.BlockSpec((1, tn), lambda j: (0, j))],
        out_specs=pl.BlockSpec((8, tn), lambda j: (0, j)),
        out_shape=jax.ShapeDtypeStruct((8, n), F32),
        compiler_params=_params(("arbitrary",)),
        name="modulation",
    )(rows, w, b)


def _log_sigmoid(x):
    return jnp.minimum(x, 0.0) - jnp.log(1.0 + jnp.exp(-jnp.abs(x)))


def _inproj_body(x_ref, sc_ref, sh_ref, nw_ref, w_ref, wg_ref, gb_ref, lbl_ref, *refs):
    hgq_ref, hgv_ref, hgg_ref, hgk_ref, hglf_ref, mlqk_ref, mlv_ref, mlo_ref, gates_ref = refs[-9:]
    kw = HEADS_W
    x = x_ref[0]
    y = x * lax.rsqrt(jnp.mean(x * x, axis=-1, keepdims=True) + NORM_EPS) * nw_ref[...]
    u = y * (1.0 + sc_ref[0]) + sh_ref[0]
    ub = u.astype(BF16)

    def proj(c0, c1):
        return _dot(ub, w_ref[:, c0:c1])

    hgq_ref[0] = _silu(proj(0, kw)).astype(BF16)
    hgv_ref[0] = proj(kw, 2 * kw).astype(BF16)
    hgg_ref[0] = _silu(proj(2 * kw, 3 * kw)).astype(BF16)

    lbl = lbl_ref[...]
    mx = jnp.max(lbl, axis=0)
    ex = jnp.exp(lbl - mx[None])
    lb = ex[0] / jnp.sum(ex, axis=0)
    for d in range(2):
        p = proj((3 + d) * kw, (4 + d) * kw)
        lbd = lb[d:d + 1]
        f = lbd + (1.0 - lbd) * _sigmoid(p)
        hgk_ref[0, :, d * kw:(d + 1) * kw] = (1.0 - f).astype(BF16)
        hglf_ref[0, :, d * kw:(d + 1) * kw] = jnp.log(f)

    mlqk_ref[0, :, 0:kw] = proj(5 * kw, 6 * kw)
    mlqk_ref[0, :, kw:2 * kw] = proj(6 * kw, 7 * kw)
    mlv_ref[0] = proj(7 * kw, 8 * kw).astype(BF16)
    mlo_ref[0] = _sigmoid(proj(8 * kw, 9 * kw)).astype(BF16)

    g = _dot_split(wg_ref[...], u, ub) + gb_ref[...]
    row = lax.broadcasted_iota(I32, g.shape, 0)
    gates_ref[0] = jnp.where((row % 8) >= N_HEADS, _log_sigmoid(g), g)


def _inproj_shapes(bsz, tall):
    kw = HEADS_W
    return [
        jax.ShapeDtypeStruct((bsz, tall, kw), BF16),
        jax.ShapeDtypeStruct((bsz, tall, kw), BF16),
        jax.ShapeDtypeStruct((bsz, tall, kw), BF16),
        jax.ShapeDtypeStruct((bsz, tall, 2 * kw), BF16),
        jax.ShapeDtypeStruct((bsz, tall, 2 * kw), F32),
        jax.ShapeDtypeStruct((bsz, tall, 2 * kw), F32),
        jax.ShapeDtypeStruct((bsz, tall, kw), BF16),
        jax.ShapeDtypeStruct((bsz, tall, kw), BF16),
        jax.ShapeDtypeStruct((bsz, 4 * N_HEADS, tall), F32),
    ]


def _inproj(tokens, scale, shift, nw, w_main, w_gt, gate_b, lb_logits, tall, tm, blk0, prev=None):
    bsz, n, d = tokens.shape
    kw = HEADS_W
    nt = n // tm
    per_sample = scale.shape[0] == bsz
    mod_map = (lambda b, i: (b, 0, 0)) if per_sample else (lambda b, i: (0, 0, 0))
    const2 = lambda b, i: (0, 0)
    in_specs = [
        pl.BlockSpec((1, tm, d), lambda b, i: (b, i, 0)),
        pl.BlockSpec((1, 1, d), mod_map),
        pl.BlockSpec((1, 1, d), mod_map),
        pl.BlockSpec((1, d), const2),
        pl.BlockSpec(w_main.shape, const2),
        pl.BlockSpec(w_gt.shape, const2),
        pl.BlockSpec(gate_b.shape, const2),
        pl.BlockSpec(lb_logits.shape, lambda b, i: (0, 0, 0)),
    ]
    args = [tokens, scale, shift, nw, w_main, w_gt, gate_b, lb_logits]
    aliases = {}
    if prev is not None:
        for k, a in enumerate(prev):
            in_specs.append(pl.BlockSpec(memory_space=pl.ANY))
            aliases[len(args)] = k
            args.append(a)
    row_map = lambda b, i: (b, blk0 + i, 0)
    widths = [kw, kw, kw, 2 * kw, 2 * kw, 2 * kw, kw, kw]
    out_specs = [pl.BlockSpec((1, tm, w), row_map) for w in widths]
    out_specs.append(pl.BlockSpec((1, 4 * N_HEADS, tm), lambda b, i: (b, 0, blk0 + i)))
    return pl.pallas_call(
        _inproj_body,
        grid=(bsz, nt),
        in_specs=in_specs,
        out_specs=out_specs,
        out_shape=_inproj_shapes(bsz, tall),
        input_output_aliases=aliases,
        compiler_params=_params(("arbitrary", "arbitrary")),
        name="inproj_ctx" if prev is not None else "inproj_x",
    )(*args)


def _conv_body(x_ref, w_ref, b_ref, o_ref, pad_ref, cpad_ref, *, t, nctx, scale_from):
    halo = CONV_HALO
    rows = 512
    win = rows + 2 * halo
    ch = x_ref.shape[-1]
    scale = jnp.where(pl.program_id(1) >= scale_from, HEAD_DIM ** -0.5, 1.0).astype(F32)
    w = w_ref[...]
    bias = b_ref[...]

    pad_ref[0:halo, :] = jnp.zeros((halo, ch), F32)
    pad_ref[halo + t:halo + t + halo, :] = jnp.zeros((halo, ch), F32)
    pad_ref[halo:halo + t, :] = x_ref[0, 0:t, :]
    col = (lax.broadcasted_iota(I32, (win, ch), 0) + (GRID_W - halo % GRID_W)) % GRID_W
    left_ok = col > 0
    right_ok = col < GRID_W - 1

    def chunk(c, carry):
        o = pl.multiple_of(c * rows, rows)
        xw = pad_ref[pl.ds(o, win), :]
        xm = jnp.where(left_ok, pltpu.roll(xw, 1, 0), 0.0)
        xp = jnp.where(right_ok, pltpu.roll(xw, win - 1, 0), 0.0)
        def taps(dr, lo):
            sl = slice(lo, lo + rows)
            return xm[sl] * w[dr, 0:1] + xw[sl] * w[dr, 1:2] + xp[sl] * w[dr, 2:3]

        y = taps(1, halo) + taps(0, halo - GRID_W) + taps(2, halo + GRID_W)
        o_ref[0, pl.ds(o, rows), :] = (_silu(y + bias) * scale).astype(o_ref.dtype)
        return carry

    lax.fori_loop(0, t // rows, chunk, 0)

    cpad_ref[0:8, :] = jnp.zeros((8, ch), F32)
    cpad_ref[8 + nctx:16 + nctx, :] = jnp.zeros((8, ch), F32)
    cpad_ref[8:8 + nctx, :] = x_ref[0, t:t + nctx, :]
    xw = cpad_ref[...]
    n = nctx + 16
    y = (pltpu.roll(xw, 1, 0) * w[1, 0:1] + xw * w[1, 1:2] + pltpu.roll(xw, n - 1, 0) * w[1, 2:3])[8:8 + nctx]
    o_ref[0, t:t + nctx, :] = (_silu(y + bias) * scale).astype(o_ref.dtype)


def _conv(qk_pre, conv_w, conv_b, t, nctx):
    bsz, tall, c = qk_pre.shape
    ch = 128
    body = functools.partial(_conv_body, t=t, nctx=nctx, scale_from=(c // 2) // ch)
    return pl.pallas_call(
        body,
        grid=(bsz, c // ch),
        in_specs=[pl.BlockSpec((1, tall, ch), lambda b, j: (b, 0, j)),
                  pl.BlockSpec((3, 3, ch), lambda b, j: (0, 0, j)),
                  pl.BlockSpec((1, ch), lambda b, j: (0, j))],
        out_specs=pl.BlockSpec((1, tall, ch), lambda b, j: (b, 0, j)),
        out_shape=jax.ShapeDtypeStruct((bsz, tall, c), BF16),
        scratch_shapes=[pltpu.VMEM((t + 2 * CONV_HALO, ch), F32),
                        pltpu.VMEM((nctx + 16, ch), F32)],
        compiler_params=_params(("arbitrary", "arbitrary")),
        name="qk_conv",
    )(qk_pre, conv_w, conv_b)


def _fwd_blk(s, nx):
    return jnp.where(s == 0, nx, s - 1)


def _bwd_blk(s, nx):
    return jnp.where(s == 0, nx, nx - s)


def _hg_constants(rev):
    c = HG_CHUNK
    i = np.arange(c)[:, None]
    j = np.arange(c)[None, :]
    blocks = [(j >= i) if rev else (j <= i)]
    masks = [i == j]
    m = c // 2
    while m >= 1:
        b0 = (i // (2 * m)) * (2 * m)
        same = (i // (2 * m)) == (j // (2 * m))
        if rev:
            beta = b0 + m
            qrow = (i % (2 * m)) < m
            g = np.where(qrow, (j >= i) & (j < beta), (j >= beta) & (j < i))
            mask = same & qrow & ((j % (2 * m)) >= m)
        else:
            beta = b0 + m - 1
            qrow = (i % (2 * m)) >= m
            g = np.where(qrow, (j > beta) & (j <= i), (j > i) & (j <= beta))
            mask = same & qrow & ((j % (2 * m)) < m)
        if m < HG_VPU_LEVEL_MIN:
            blocks.append(g)
        masks.append(mask)
        m //= 2
    g = np.concatenate(blocks, axis=0).astype(np.float32)
    g3 = np.concatenate([g, g, g], axis=1)
    m2 = np.concatenate([np.stack(masks), np.stack(masks)], axis=2)
    return (jnp.asarray(g3, BF16), jnp.asarray(m2, F32))


def _hg_level_decay(a, m, rev):
    c = a.shape[0]
    parts = []
    for b0 in range(0, c, 2 * m):
        beta = b0 + m if rev else b0 + m - 1
        ref = a[beta:beta + 1]
        if m % 8 == 0:
            first, second = a[b0:b0 + m], a[b0 + m:b0 + 2 * m]
            parts += [first - ref, ref - second] if rev else [ref - first, second - ref]
        else:
            d = a[b0:b0 + 2 * m] - ref
            parts.append(jnp.minimum(d, -d))
    return jnp.concatenate(parts, axis=0)


def _block_diag(x, zero):
    w = x.shape[1] // 2
    return jnp.concatenate([jnp.concatenate([x[:, :w], zero], axis=1),
                            jnp.concatenate([zero, x[:, w:]], axis=1)], axis=0)


def _hg_chunk(dirs):
    c = HG_CHUNK
    w = 2 * HEAD_DIM
    zero = jnp.zeros((c, HEAD_DIM), BF16)
    units = []
    for rev, r0, bb, q_ref, k_ref, v_ref, lf_ref, g_ref, msk_ref, o_ref, st_ref in dirs:
        rows = pl.ds(r0, c)
        lf = lf_ref[bb, rows, :] * LOG2_E
        p1 = lf.astype(BF16)
        r1 = lf - p1.astype(F32)
        p2 = r1.astype(BF16)
        p3 = (r1 - p2.astype(F32)).astype(BF16)
        dall = _dot(g_ref[...], jnp.concatenate([p1, p2, p3], axis=0))
        for hp in range(N_HEADS // 2):
            cs = slice(hp * w, (hp + 1) * w)
            units.append(dict(rev=rev, rows=rows, cs=cs, hp=hp, bb=bb, msk_ref=msk_ref, o_ref=o_ref, st_ref=st_ref,
                              q=q_ref[bb, rows, cs], k=k_ref[bb, rows, cs], v=v_ref[bb, rows, cs], dall=dall[:, cs]))
    for u in units:
        msk_ref = u["msk_ref"]
        nlev = msk_ref.shape[0] - 1
        att = _dot(u["q"], _block_diag(u["k"], zero), _NT) * msk_ref[0]
        a = u["dall"][0:c]
        row = 1
        for l in range(nlev):
            m = c >> (l + 1)
            if m >= HG_VPU_LEVEL_MIN:
                dec = _hg_level_decay(a, m, u["rev"])
            else:
                dec = u["dall"][row * c:(row + 1) * c]
                row += 1
            e = jnp.exp2(dec).astype(BF16)
            att = att + _dot(u["q"] * e, _block_diag(u["k"] * e, zero), _NT) * msk_ref[l + 1]
        u["att"] = att.astype(BF16)
    for u in units:
        a = u["dall"][0:c]
        u["a_tot"] = a[0:1] if u["rev"] else a[c - 1:c]
        u["st"] = [u["st_ref"][2 * u["hp"] + i] for i in range(2)]
        zf = jnp.zeros((HEAD_DIM, HEAD_DIM), BF16)
        st2 = jnp.concatenate([jnp.concatenate([u["st"][0].astype(BF16), zf], axis=1),
                               jnp.concatenate([zf, u["st"][1].astype(BF16)], axis=1)], axis=0)
        qbar = (u["q"].astype(F32) * jnp.exp2(a)).astype(BF16)
        u["o"] = _dot(u["att"], _block_diag(u["v"], zero)) + _dot(qbar, st2, _NT)
        u["khat"] = (u["k"].astype(F32) * jnp.exp2(u["a_tot"] - a)).astype(BF16)
    for u in units:
        u["o_ref"][u["bb"], u["rows"], u["cs"]] = u["o"].astype(BF16)
        for i in range(2):
            hs = slice(i * HEAD_DIM, (i + 1) * HEAD_DIM)
            upd = _dot(u["v"][:, hs], u["khat"][:, hs], _TN)
            u["st_ref"][2 * u["hp"] + i] = u["st"][i] * jnp.exp2(u["a_tot"][:, hs]) + upd


def _hg_body(qf_ref, kf_ref, vf_ref, lff_ref, qb_ref, kb_ref, vb_ref, lfb_ref,
             gf_ref, mf_ref, gb_ref, mb_ref, of_ref, ob_ref, st_ref):
    @pl.when(pl.program_id(1) == 0)
    def _():
        st_ref[...] = jnp.zeros(st_ref.shape, F32)

    nsub = SCAN_STEP // HG_CHUNK
    for c in range(nsub):
        dirs = []
        for bb in range(qf_ref.shape[0]):
            dirs.append((False, c * HG_CHUNK, bb, qf_ref, kf_ref, vf_ref, lff_ref, gf_ref, mf_ref, of_ref,
                         st_ref.at[0, bb]))
            dirs.append((True, (nsub - 1 - c) * HG_CHUNK, bb, qb_ref, kb_ref, vb_ref, lfb_ref, gb_ref, mb_ref, ob_ref,
                         st_ref.at[1, bb]))
        _hg_chunk(dirs)


def _hgrn2(hgq, hgk, hgv, hglf, nx):
    bsz, tall, kw = hgq.shape
    steps = tall // SCAN_STEP
    cf = _hg_constants(False)
    cb = _hg_constants(True)
    nb = HG_SAMPLES if bsz % HG_SAMPLES == 0 else 1
    blk = (nb, SCAN_STEP, kw)
    fwd = lambda col: (lambda b, s: (b, _fwd_blk(s, nx), col))
    bwd = lambda col: (lambda b, s: (b, _bwd_blk(s, nx), col))
    const = lambda a: pl.BlockSpec(a.shape, lambda b, s: (0,) * a.ndim)
    in_specs = [pl.BlockSpec(blk, fwd(0)), pl.BlockSpec(blk, fwd(0)), pl.BlockSpec(blk, fwd(0)), pl.BlockSpec(blk, fwd(0)),
                pl.BlockSpec(blk, bwd(0)), pl.BlockSpec(blk, bwd(1)), pl.BlockSpec(blk, bwd(0)), pl.BlockSpec(blk, bwd(1))]
    in_specs += [const(a) for a in cf + cb]
    out_sds = jax.ShapeDtypeStruct((bsz, tall, kw), BF16)
    return pl.pallas_call(
        _hg_body,
        grid=(bsz // nb, steps),
        in_specs=in_specs,
        out_specs=[pl.BlockSpec(blk, fwd(0)), pl.BlockSpec(blk, bwd(0))],
        out_shape=[out_sds, out_sds],
        scratch_shapes=[pltpu.VMEM((2, nb, N_HEADS, HEAD_DIM, HEAD_DIM), F32)],
        compiler_params=_params(("arbitrary", "arbitrary")),
        name="hgrn2_scan",
    )(hgq, hgk, hgv, hglf, hgq, hgk, hgv, hglf, *cf, *cb)


def _ml_constants(rev):
    k = np.arange(ML_CHUNK)
    tri = (k[:, None] >= k[None, :]) if rev else (k[:, None] <= k[None, :])
    return jnp.asarray(np.concatenate([tri, tri, tri], axis=0), BF16)


def _ml_chunk(dirs):
    c = ML_CHUNK
    ii = lax.broadcasted_iota(I32, (c, c), 0)
    jj = lax.broadcasted_iota(I32, (c, c), 1)
    ones = jnp.ones((c, HEAD_DIM), BF16)
    units = []
    for rev, r0, d, bb, q_ref, k_ref, v_ref, g_ref, tri3_ref, o_ref, st_ref, m_ref in dirs:
        gates = g_ref[bb, :, pl.ds(r0, c)] * LOG2_E
        p1 = gates.astype(BF16)
        r1 = gates - p1.astype(F32)
        p2 = r1.astype(BF16)
        p3 = (r1 - p2.astype(F32)).astype(BF16)
        csum = _dot(jnp.concatenate([p1, p2, p3], axis=1), tri3_ref[...])
        for h in range(N_HEADS):
            cs = slice(h * HEAD_DIM, (h + 1) * HEAD_DIM)
            u = dict(rev=rev, o_ref=o_ref, bb=bb, rows=pl.ds(r0, c), cs=cs, st_ref=st_ref, m_ref=m_ref, h=h)
            u["qb"] = q_ref[bb, pl.ds(r0, c), cs]
            u["kb"] = k_ref[bb, pl.ds(r0, c), cs]
            u["v1"] = jnp.concatenate([v_ref[bb, pl.ds(r0, c), cs], ones], axis=1)
            irow = gates[d * 8 + h:d * 8 + h + 1]
            u["brow"] = csum[d * 8 + N_HEADS + h:d * 8 + N_HEADS + h + 1]
            u["rrow"] = irow - u["brow"]
            units.append(u)
    for u in units:
        u["st"] = u["st_ref"][u["h"]]
        a = _dot(jnp.concatenate([u["kb"], u["st"].astype(BF16)], axis=0), u["qb"], _NT)
        u["s"] = a[:c]
        u["sq"] = a[c:]
    for u in units:
        last = 0 if u["rev"] else c - 1
        tri_t = (ii >= jj) if u["rev"] else (ii <= jj)
        rcol = jnp.concatenate([u["rrow"], jnp.zeros((7, c), F32)], axis=0).T[:, 0:1]
        u["mprev"] = u["m_ref"][u["h"]][:, 0:1]
        rmat = jnp.where(tri_t, rcol, -jnp.inf)
        u["grow"] = jnp.maximum(jnp.max(rmat, axis=0, keepdims=True), u["mprev"])
        qk = (u["s"] * jnp.exp2(rmat - u["grow"])).astype(BF16)
        blast = u["brow"][:, last:last + 1]
        u["mnew"] = blast + u["grow"][:, last:last + 1]
        kh = (u["kb"].astype(F32) * jnp.exp2(blast + rcol - u["mnew"])).astype(BF16)
        u["ws"] = jnp.exp2(blast + u["mprev"] - u["mnew"])
        u["qkh"] = jnp.concatenate([qk, kh], axis=1)
    for u in units:
        u["nu"] = _dot(u["v1"], u["qkh"], _TN)
    for u in units:
        both = u["nu"][:, :c] + jnp.exp2(u["mprev"] - u["grow"]) * u["sq"]
        den = both[HEAD_DIM:HEAD_DIM + 1]
        inv = 1.0 / jnp.maximum(jnp.abs(den), jnp.exp2(-(u["brow"] + u["grow"])))
        u["o_ref"][u["bb"], u["rows"], u["cs"]] = (both[:HEAD_DIM] * inv).T.astype(BF16)
        u["st_ref"][u["h"]] = u["ws"] * u["st"] + u["nu"][:, c:]
        u["m_ref"][u["h"]] = jnp.broadcast_to(u["mnew"], (1, HEAD_DIM))


def _ml_body(qf_ref, kf_ref, vf_ref, gf_ref, qb_ref, kb_ref, vb_ref, gb_ref, tf_ref, tb_ref,
             of_ref, ob_ref, st_ref, m_ref):
    @pl.when(pl.program_id(1) == 0)
    def _():
        st_ref[...] = jnp.zeros(st_ref.shape, F32)
        m_ref[...] = jnp.zeros(m_ref.shape, F32)

    nsub = SCAN_STEP // ML_CHUNK
    for c in range(nsub):
        dirs = []
        for bb in range(qf_ref.shape[0]):
            dirs.append((False, c * ML_CHUNK, 0, bb, qf_ref, kf_ref, vf_ref, gf_ref, tf_ref, of_ref,
                         st_ref.at[0, bb], m_ref.at[0, bb]))
            dirs.append((True, (nsub - 1 - c) * ML_CHUNK, 1, bb, qb_ref, kb_ref, vb_ref, gb_ref, tb_ref, ob_ref,
                         st_ref.at[1, bb], m_ref.at[1, bb]))
        _ml_chunk(dirs)


def _mlstm(mlqk, mlv, gates, nx):
    bsz, tall, kw = mlv.shape
    steps = tall // SCAN_STEP
    nb = ML_SAMPLES if bsz % ML_SAMPLES == 0 else 1
    blk = (nb, SCAN_STEP, kw)
    gblk = (nb, 4 * N_HEADS, SCAN_STEP)
    tf, tb = _ml_constants(False), _ml_constants(True)
    fwd = lambda col: (lambda b, s: (b, _fwd_blk(s, nx), col))
    bwd = lambda col: (lambda b, s: (b, _bwd_blk(s, nx), col))
    const = pl.BlockSpec(tf.shape, lambda b, s: (0, 0))
    in_specs = [pl.BlockSpec(blk, fwd(0)), pl.BlockSpec(blk, fwd(1)), pl.BlockSpec(blk, fwd(0)),
                pl.BlockSpec(gblk, lambda b, s: (b, 0, _fwd_blk(s, nx))),
                pl.BlockSpec(blk, bwd(0)), pl.BlockSpec(blk, bwd(1)), pl.BlockSpec(blk, bwd(0)),
                pl.BlockSpec(gblk, lambda b, s: (b, 0, _bwd_blk(s, nx))), const, const]
    out_sds = jax.ShapeDtypeStruct((bsz, tall, kw), BF16)
    return pl.pallas_call(
        _ml_body,
        grid=(bsz // nb, steps),
        in_specs=in_specs,
        out_specs=[pl.BlockSpec(blk, fwd(0)), pl.BlockSpec(blk, bwd(0))],
        out_shape=[out_sds, out_sds],
        scratch_shapes=[pltpu.VMEM((2, nb, N_HEADS, 2 * HEAD_DIM, HEAD_DIM), F32),
                        pltpu.VMEM((2, nb, N_HEADS, 1, HEAD_DIM), F32)],
        compiler_params=_params(("arbitrary", "arbitrary")),
        name="mlstm_scan",
    )(mlqk, mlqk, mlv, gates, mlqk, mlqk, mlv, gates, tf, tb)


def _out_body(hof_ref, hob_ref, mhf_ref, mhb_ref, hgg_ref, mlo_ref, x_ref, g1_ref, sc2_ref, sh2_ref,
              hnw_ref, mnw_ref, wout_ref, n2w_ref, rwt_ref, x1_ref, vt_ref, aff_ref):
    hg = hof_ref[0].astype(F32) + hob_ref[0].astype(F32)
    ml = mhf_ref[0].astype(F32) + mhb_ref[0].astype(F32)
    hparts, mparts = [], []
    for h in range(N_HEADS):
        cs = slice(h * HEAD_DIM, (h + 1) * HEAD_DIM)
        t = hg[:, cs]
        hparts.append(t * lax.rsqrt(jnp.mean(t * t, axis=-1, keepdims=True) + NORM_EPS))
        t = ml[:, cs]
        t = t - jnp.mean(t, axis=-1, keepdims=True)
        mparts.append(t * lax.rsqrt(jnp.mean(t * t, axis=-1, keepdims=True) + NORM_EPS))
    hgn = jnp.concatenate(hparts, axis=-1) * hnw_ref[...] * hgg_ref[0].astype(F32)
    mln = jnp.concatenate(mparts, axis=-1) * mnw_ref[...] * mlo_ref[0].astype(F32)
    mix = jnp.concatenate([hgn, mln], axis=-1).astype(BF16)
    x1 = x_ref[0] + g1_ref[0] * _dot(mix, wout_ref[...])
    x1_ref[0] = x1
    v = x1 * lax.rsqrt(jnp.mean(x1 * x1, axis=-1, keepdims=True) + NORM_EPS) * n2w_ref[...]
    v = v * (1.0 + sc2_ref[0]) + sh2_ref[0]
    vt_ref[0] = _pack_halves(v)
    logits = _dot_split(rwt_ref[...], v, v.astype(BF16))
    ex = jnp.exp(logits - jnp.max(logits, axis=0, keepdims=True))
    aff_ref[0] = ex / jnp.sum(ex, axis=0, keepdims=True)


def _mixer_out(hg_f, hg_b, ml_f, ml_b, hgg, mlo, x, g1, sc2, sh2, hnw, mnw, w_out, n2w, rwt, tm):
    bsz, t, d = x.shape
    kw = HEADS_W
    ne = rwt.shape[0]
    row = lambda b, i: (b, i, 0)
    mod = lambda b, i: (b, 0, 0)
    const2 = lambda b, i: (0, 0)
    act = pl.BlockSpec((1, tm, kw), row)
    in_specs = [act, act, act, act, act, act,
                pl.BlockSpec((1, tm, d), row),
                pl.BlockSpec((1, 1, d), mod), pl.BlockSpec((1, 1, d), mod), pl.BlockSpec((1, 1, d), mod),
                pl.BlockSpec((1, kw), const2), pl.BlockSpec((1, kw), const2),
                pl.BlockSpec(w_out.shape, const2), pl.BlockSpec((1, d), const2), pl.BlockSpec(rwt.shape, const2)]
    return pl.pallas_call(
        _out_body,
        grid=(bsz, t // tm),
        in_specs=in_specs,
        out_specs=[pl.BlockSpec((1, tm, d), row),
                   pl.BlockSpec((1, tm, d // 2), row),
                   pl.BlockSpec((1, ne, tm), lambda b, i: (b, 0, i))],
        out_shape=[jax.ShapeDtypeStruct((bsz, t, d), F32),
                   jax.ShapeDtypeStruct((bsz, t, d // 2), I32),
                   jax.ShapeDtypeStruct((bsz, ne, t), F32)],
        compiler_params=_params(("arbitrary", "arbitrary")),
        name="mixer_out",
    )(hg_f, hg_b, ml_f, ml_b, hgg, mlo, x, g1, sc2, sh2, hnw, mnw, w_out, n2w, rwt)


def _prefix_count(maskf, u_ref, ones_ref, bl_ref):
    e, nb, ln = maskf.shape
    x = maskf.reshape(e * nb, ln)
    xb = x.astype(BF16)
    incl = _dot(xb, u_ref[...])
    tot = _dot(xb, ones_ref[...])
    off = _dot(bl_ref[...], tot.astype(BF16))
    return (incl - x + off).reshape(e, nb, ln), off.reshape(e, nb, ln)


def _topk_body(aff_ref, u_ref, ones_ref, bl_ref, bl1_ref, pos_ref, base_ref, *, cap):
    x = aff_ref[0]
    ne = x.shape[0]

    def count(m):
        return jnp.sum(jnp.sum(jnp.where(m, 1.0, 0.0), axis=1, keepdims=True), axis=2, keepdims=True)

    def halve(_, carry):
        lo, hi = carry
        mid = 0.5 * (lo + hi)
        up = count(x > mid) >= cap
        return jnp.where(up, mid, lo), jnp.where(up, hi, mid)

    lo, hi = lax.fori_loop(0, TOPK_BISECTIONS, halve,
                           (jnp.full((ne, 1, 1), -1.0, F32), jnp.full((ne, 1, 1), 1.0, F32)))
    gt = jnp.where(x > hi, 1.0, 0.0)
    eq = jnp.where(x > lo, 1.0, 0.0) - gt
    need = cap - count(x > hi)
    eq_rank, _ = _prefix_count(eq, u_ref, ones_ref, bl_ref)
    sel = gt + eq * jnp.where(eq_rank < need, 1.0, 0.0)
    pos, _ = _prefix_count(sel, u_ref, ones_ref, bl_ref)
    pos_ref[0] = jnp.where(sel > 0, pos, -1.0).astype(I32)
    nsel = jnp.sum(sel, axis=0)
    nb16 = nsel.astype(BF16)
    incl = _dot(nb16, u_ref[...])
    tot = _dot(nb16, ones_ref[...])
    off = _dot(bl1_ref[...], tot, precision=HIGHEST)
    base_ref[0] = (incl - nsel + off).astype(I32)


def _topk(aff4, cap):
    bsz, ne, nb, ln = aff4.shape
    k = np.arange(ln)
    u = jnp.asarray(k[:, None] <= k[None, :], BF16)
    ones = jnp.ones((ln, ln), BF16)
    r = np.arange(ne * nb)
    bl = jnp.asarray(((r[:, None] // nb) == (r[None, :] // nb)) & (r[None, :] < r[:, None]), BF16)
    r1 = np.arange(nb)
    bl1 = jnp.asarray(r1[None, :] < r1[:, None], F32)
    blk = pl.BlockSpec((1, ne, nb, ln), lambda b: (b, 0, 0, 0))
    const = lambda a: pl.BlockSpec(a.shape, lambda b: (0, 0))
    return pl.pallas_call(
        functools.partial(_topk_body, cap=cap),
        grid=(bsz,),
        in_specs=[blk, const(u), const(ones), const(bl), const(bl1)],
        out_specs=[blk, pl.BlockSpec((1, nb, ln), lambda b: (b, 0, 0))],
        out_shape=[jax.ShapeDtypeStruct((bsz, ne, nb, ln), I32), jax.ShapeDtypeStruct((bsz, nb, ln), I32)],
        compiler_params=_params(("arbitrary",)),
        name="expert_topk",
    )(aff4, u, ones, bl, bl1)


def _sc_setup():
    info = plsc.get_sparse_core_info()
    mesh = plsc.VectorSubcoreMesh(core_axis_name="c", subcore_axis_name="s")
    params = dataclasses.replace(pltpu.CompilerParams(), needs_layout_passes=False)
    return info.num_cores, info.num_cores * info.num_subcores, info.num_lanes, mesh, params


def _sc_worker_id(nc):
    return lax.axis_index("s") * nc + lax.axis_index("c")


def _sc_gather_scratch(width, dtype):
    one = [pltpu.VMEM((SC_GATHER_ROWS,), I32), pltpu.VMEM((SC_GATHER_ROWS, width), dtype), pltpu.SemaphoreType.DMA]
    return one + one


def _sc_gather_chunks(table_hbm, out_hbm, idx_v, bufs, out_row0, n, lanes):
    chunk = SC_GATHER_ROWS

    def gather(c, buf):
        ich_v, rows_v, sem = buf
        for q in range(chunk // lanes):
            src = pl.ds(pl.multiple_of(c * chunk + q * lanes, lanes), lanes)
            ich_v[pl.ds(q * lanes, lanes)] = idx_v[src]
        return pltpu.make_async_copy(table_hbm.at[ich_v], rows_v, sem)

    def finish(c, buf):
        ich_v, rows_v, sem = buf
        pltpu.make_async_copy(table_hbm.at[ich_v], rows_v, sem).wait()
        pltpu.sync_copy(rows_v, out_hbm.at[pl.ds(pl.multiple_of(out_row0 + c * chunk, chunk), chunk)])

    npair = n // (2 * chunk)
    gather(0, bufs[0]).start()

    @pl.loop(0, npair)
    def _(i):
        gather(2 * i + 1, bufs[1]).start()
        finish(2 * i, bufs[0])

        @pl.when(i + 1 < npair)
        def _():
            gather(2 * i + 2, bufs[0]).start()

        finish(2 * i + 1, bufs[1])


def _sc_invert_gather(table, pos_flat, aff_flat, npairs, ne, t, cap):
    nc, nw, lanes, mesh, params = _sc_setup()
    split = max(1, nw // npairs)
    seg = cap // split
    per_w = npairs * split // nw
    width = table.shape[1]

    @functools.partial(
        pl.kernel, mesh=mesh, compiler_params=params,
        out_type=[jax.ShapeDtypeStruct((npairs * cap, width), table.dtype),
                  jax.ShapeDtypeStruct((npairs * cap,), F32)],
        scratch_types=[pltpu.VMEM((t,), I32), pltpu.VMEM((t,), F32), pltpu.VMEM((seg,), I32), pltpu.VMEM((seg,), F32)]
        + _sc_gather_scratch(width, table.dtype),
    )
    def body(table_hbm, pos_hbm, aff_hbm, out_hbm, gate_hbm, pos_v, aff_v, idx_v, gate_v, *g):
        wid = _sc_worker_id(nc)

        @pl.loop(0, per_w)
        def _(kk):
            item = wid * per_w + kk
            p = item // split
            lo = (item % split) * seg
            tok0 = (p // ne) * t
            pltpu.sync_copy(pos_hbm.at[pl.ds(pl.multiple_of(p * t, t), t)], pos_v)
            pltpu.sync_copy(aff_hbm.at[pl.ds(pl.multiple_of(p * t, t), t)], aff_v)

            @pl.loop(0, t // lanes)
            def _(i):
                v = pos_v[pl.ds(pl.multiple_of(i * lanes, lanes), lanes)] - lo
                tok = lax.iota(I32, lanes) + i * lanes
                plsc.store_scatter(idx_v, [v], tok, mask=(v >= 0) & (v < seg))

            @pl.loop(0, seg // lanes)
            def _(j):
                sl = pl.ds(pl.multiple_of(j * lanes, lanes), lanes)
                ii = idx_v[sl]
                gate_v[sl] = plsc.load_gather(aff_v, [ii])
                idx_v[sl] = ii + tok0

            row0 = p * cap + lo
            pltpu.sync_copy(gate_v, gate_hbm.at[pl.ds(pl.multiple_of(row0, seg), seg)])
            _sc_gather_chunks(table_hbm, out_hbm, idx_v, (g[0:3], g[3:6]), row0, seg, lanes)

    return body(table, pos_flat, aff_flat)


def _sc_assignment_order(pos_flat, base_flat, bsz, ne, t, cap):
    nc, nw, lanes, mesh, params = _sc_setup()
    na = ne * cap
    per_b = nw // bsz
    rng = na // per_b

    @functools.partial(
        pl.kernel, mesh=mesh, compiler_params=params,
        out_type=[jax.ShapeDtypeStruct((bsz * na,), I32), jax.ShapeDtypeStruct((bsz * na,), I32)],
        scratch_types=[pltpu.VMEM((t,), I32), pltpu.VMEM((t,), I32), pltpu.VMEM((t,), I32),
                       pltpu.VMEM((rng,), I32), pltpu.VMEM((rng,), I32)],
    )
    def body(pos_hbm, base_hbm, perm_hbm, tok_hbm, pos_v, base_v, rank_v, perm_v, tok_v):
        wid = _sc_worker_id(nc)
        b = wid // per_b
        a0 = (wid % per_b) * rng
        pltpu.sync_copy(base_hbm.at[pl.ds(pl.multiple_of(b * t, t), t)], base_v)

        @pl.loop(0, t // lanes)
        def _(i):
            rank_v[pl.ds(pl.multiple_of(i * lanes, lanes), lanes)] = jnp.zeros((lanes,), I32)

        @pl.loop(0, ne)
        def _(e):
            p = b * ne + e
            pltpu.sync_copy(pos_hbm.at[pl.ds(pl.multiple_of(p * t, t), t)], pos_v)

            @pl.loop(0, t // lanes)
            def _(i):
                sl = pl.ds(pl.multiple_of(i * lanes, lanes), lanes)
                v = pos_v[sl]
                r = rank_v[sl]
                a = base_v[sl] + r - a0
                sel = v >= 0
                mine = sel & (a >= 0) & (a < rng)
                plsc.store_scatter(perm_v, [a], v + p * cap, mask=mine)
                plsc.store_scatter(tok_v, [a], lax.iota(I32, lanes) + i * lanes, mask=mine)
                rank_v[sl] = r + jnp.where(sel, 1, 0)

        dst = pl.ds(pl.multiple_of(b * na + a0, rng), rng)
        pltpu.sync_copy(perm_v, perm_hbm.at[dst])
        pltpu.sync_copy(tok_v, tok_hbm.at[dst])

    return body(pos_flat, base_flat)


def _sc_gather_rows(table, idx):
    nc, nw, lanes, mesh, params = _sc_setup()
    n = idx.shape[0]
    per_w = n // nw
    width = table.shape[1]
    chunk = SC_GATHER_ROWS

    @functools.partial(
        pl.kernel, mesh=mesh, compiler_params=params,
        out_type=jax.ShapeDtypeStruct((n, width), table.dtype),
        scratch_types=[pltpu.VMEM((per_w,), I32)] + _sc_gather_scratch(width, table.dtype),
    )
    def body(table_hbm, idx_hbm, out_hbm, idx_v, *g):
        row0 = _sc_worker_id(nc) * per_w
        pltpu.sync_copy(idx_hbm.at[pl.ds(pl.multiple_of(row0, per_w), per_w)], idx_v)
        _sc_gather_chunks(table_hbm, out_hbm, idx_v, (g[0:3], g[3:6]), row0, per_w, lanes)

    return body(table, idx)


def _moe1_body(xs_ref, gate_ref, wg_ref, wu_ref, wd_ref, y_ref):
    xs = _unpack_halves(xs_ref[0])
    hg = _dot(xs, wg_ref[0].astype(BF16))
    hu = _dot(xs, wu_ref[0].astype(BF16))
    h = (_silu(hg) * hu).astype(BF16)
    y = _dot(h, wd_ref[0].astype(BF16))
    gt = gate_ref[0].T
    y = jnp.concatenate([y[k * 128:(k + 1) * 128] * gt[:, k:k + 1] for k in range(gt.shape[1])], axis=0)
    y_ref[0] = _pack_halves(y)


def _moe_experts(xs, gate, wg, wu, wd, bsz):
    npairs, cap, half = xs.shape
    ne, d, f = wg.shape
    pair = lambda b, e: (b * ne + e, 0, 0)
    expert = lambda b, e: (e, 0, 0)
    return pl.pallas_call(
        _moe1_body,
        grid=(bsz, ne),
        in_specs=[pl.BlockSpec((1, cap, half), pair),
                  pl.BlockSpec((1, cap // 128, 128), pair),
                  pl.BlockSpec((1, d, f), expert),
                  pl.BlockSpec((1, d, f), expert),
                  pl.BlockSpec((1, f, d), expert)],
        out_specs=pl.BlockSpec((1, cap, half), pair),
        out_shape=jax.ShapeDtypeStruct((npairs, cap, half), I32),
        compiler_params=_params(("arbitrary", "arbitrary")),
        cost_estimate=pl.CostEstimate(flops=6 * npairs * cap * d * f, transcendentals=npairs * cap * f,
                                      bytes_accessed=12 * npairs * d * f + 16 * npairs * cap * half),
        name="moe_experts",
    )(xs, gate, wg, wu, wd)


def _combine_body(abase_ref, ys_hbm, tok_ref, x1_ref, g2_ref, fw_ref, o_ref, buf_ref, acc_ref, sem_ref, cnt_ref, *, na):
    n = ONEHOT_BLK
    ring = buf_ref.shape[0]
    nch = na // n
    b, i = pl.program_id(0), pl.program_id(1)

    @pl.when(i == 0)
    def _():
        cnt_ref[0] = 0
        cnt_ref[1] = 0

    lo = abase_ref[b, i]
    hi = abase_ref[b, i + 1]
    c0 = lo // n
    c1 = jnp.where(hi > lo, (hi + n - 1) // n, c0)

    def copy(c):
        slot = c % ring
        row0 = pl.multiple_of(b * na + c * n, n)
        return pltpu.make_async_copy(ys_hbm.at[pl.ds(row0, n)], buf_ref.at[slot], sem_ref.at[slot])

    acc_ref[...] = jnp.zeros(acc_ref.shape, F32)
    tok_ids = lax.broadcasted_iota(I32, (n, n), 0) + i * n

    def step(c, carry):
        started = cnt_ref[0]
        ahead = jnp.minimum(c + ring, nch)

        def start(k, _):
            copy(k).start()
            return 0

        lax.fori_loop(started, ahead, start, 0)
        cnt_ref[0] = jnp.maximum(started, ahead)

        @pl.when(cnt_ref[1] <= c)
        def _():
            copy(c).wait()
            cnt_ref[1] = c + 1

        seg = jnp.where(tok_ids == tok_ref[0, pl.ds(c, 1), :], 1.0, 0.0).astype(BF16)
        acc_ref[...] += _dot(seg, _unpack_halves(buf_ref[c % ring]))
        return carry

    lax.fori_loop(c0, c1, step, 0)
    x2 = x1_ref[0] + g2_ref[0] * acc_ref[...]
    o_ref[0] = x2 * lax.rsqrt(jnp.mean(x2 * x2, axis=-1, keepdims=True) + NORM_EPS) * fw_ref[...]


def _combine_final(abase, ys, tok, x1, g2, fw, na):
    bsz, t, d = x1.shape
    n = ONEHOT_BLK
    grid_spec = pltpu.PrefetchScalarGridSpec(
        num_scalar_prefetch=1,
        grid=(bsz, t // n),
        in_specs=[pl.BlockSpec(memory_space=pl.ANY),
                  pl.BlockSpec((1, na // n, n), lambda b, i, a: (b, 0, 0)),
                  pl.BlockSpec((1, n, d), lambda b, i, a: (b, i, 0)),
                  pl.BlockSpec((1, 1, d), lambda b, i, a: (b, 0, 0)),
                  pl.BlockSpec((1, d), lambda b, i, a: (0, 0))],
        out_specs=pl.BlockSpec((1, n, d), lambda b, i, a: (b, i, 0)),
        scratch_shapes=[pltpu.VMEM((COMBINE_RING, n, d // 2), I32), pltpu.VMEM((n, d), F32),
                        pltpu.SemaphoreType.DMA((COMBINE_RING,)), pltpu.SMEM((2,), I32)],
    )
    return pl.pallas_call(
        functools.partial(_combine_body, na=na),
        grid_spec=grid_spec,
        out_shape=jax.ShapeDtypeStruct((bsz, t, d), F32),
        compiler_params=_params(("arbitrary", "arbitrary")),
        name="combine_final",
    )(abase, ys, tok, x1, g2, fw)


def kernel(x, c, ctx, c_ctx, ada_w, ada_b, norm1_w, w_in, conv_w, conv_b, hg_lb_logits, ml_gate_b,
           hg_norm_w, ml_norm_w, w_out, norm2_w, router_w, exp_w_gate, exp_w_up, exp_w_down, final_norm_w):
    bsz, t, d = x.shape
    nctx = ctx.shape[1]
    assert ada_w.shape[0] == 1, "single-layer block"
    assert nctx == SCAN_STEP and t % SCAN_STEP == 0 and bsz + 1 <= 8
    tall = t + nctx
    nx = t // SCAN_STEP
    kw = HEADS_W
    ne = router_w.shape[-1]
    cap = EC_CAPACITY * t // ne
    tm = min(512, t)

    rows = jnp.concatenate([c, c_ctx[None], jnp.zeros((7 - bsz, d), F32)], axis=0)
    mod = _modulation(rows, ada_w[0], ada_b[0][None])
    mx = [m[:, None, :] for m in jnp.split(mod[:bsz], 6, axis=-1)]
    mc = [m[:, None, :] for m in jnp.split(mod[bsz:bsz + 1], 6, axis=-1)]
    sh1, sc1, g1, sh2, sc2, g2 = mx
    csh1, csc1 = mc[0], mc[1]

    main_w = 9 * kw
    w_main = w_in[0][:, :main_w].astype(BF16)
    w_gt = w_in[0][:, main_w:].T
    gate_b = ml_gate_b[0][:, None]
    nw1 = norm1_w[0][None]
    outs = _inproj(x, sc1, sh1, nw1, w_main, w_gt, gate_b, hg_lb_logits, tall, tm, 0)
    outs = _inproj(ctx, csc1, csh1, nw1, w_main, w_gt, gate_b, hg_lb_logits, tall, nctx, t // nctx, prev=outs)
    hgq, hgv, hgg, hgk, hglf, mlqk_pre, mlv, mlo, gates = outs

    mlqk = _conv(mlqk_pre, conv_w[0], conv_b[0][None], t, nctx)
    hg_f, hg_b = _hgrn2(hgq, hgk, hgv, hglf, nx)
    ml_f, ml_b = _mlstm(mlqk, mlv, gates, nx)

    x1, vpk, aff = _mixer_out(hg_f, hg_b, ml_f, ml_b, hgg, mlo, x, g1, sc2, sh2,
                             hg_norm_w[0][None], ml_norm_w[0][None], w_out[0].astype(BF16),
                             norm2_w[0][None], router_w[0].T, min(2 * tm, t))

    pos4, base = _topk(aff.reshape(bsz, ne, t // 128, 128), cap)
    na = ne * cap
    pos_flat = pos4.reshape(-1)
    xs, gate = _sc_invert_gather(vpk.reshape(bsz * t, d // 2), pos_flat, aff.reshape(-1), bsz * ne, ne, t, cap)
    perm, tok = _sc_assignment_order(pos_flat, base.reshape(-1), bsz, ne, t, cap)
    y = _moe_experts(xs.reshape(bsz * ne, cap, d // 2), gate.reshape(bsz * ne, cap // 128, 128),
                     exp_w_gate[0], exp_w_up[0], exp_w_down[0], bsz)
    ys = _sc_gather_rows(y.reshape(bsz * na, d // 2), perm)
    abase = jnp.concatenate([base.reshape(bsz, t)[:, ::ONEHOT_BLK], jnp.full((bsz, 1), na, I32)], axis=1)
    return _combine_final(abase, ys, tok.reshape(bsz, na // ONEHOT_BLK, ONEHOT_BLK), x1, g2, final_norm_w[None], na)
```

```python
import dataclasses
import functools

import numpy as np
import jax
import jax.numpy as jnp
from jax import lax
from jax.experimental import pallas as pl
from jax.experimental.pallas import tpu as pltpu
from jax.experimental.pallas import tpu_sc as plsc

F32 = jnp.float32
BF16 = jnp.bfloat16
I32 = jnp.int32
HIGHEST = lax.Precision.HIGHEST
NORM_EPS = 1e-6
LOG2_E = 1.4426950408889634

HEAD_DIM = 128
N_HEADS = 4
HEADS_W = N_HEADS * HEAD_DIM
GRID_W = 64
N_EXPERTS = 16
EC_CAPACITY = 2
HG_CHUNK = 64
HG_VPU_LEVEL_MIN = 4
ML_CHUNK = 256
SCAN_STEP = 256
HG_SAMPLES = 2
ML_SAMPLES = 4
ONEHOT_BLK = 256
CONV_HALO = 72
SC_GATHER_ROWS = 64
COMBINE_RING = 6
TOPK_BISECTIONS = 64
VMEM_LIMIT = 56 * 1024 * 1024

_NT = (((1,), (1,)), ((), ()))
_TN = (((0,), (0,)), ((), ()))


def _dot(a, b, dims=None, precision=None):
    if dims is None:
        return jnp.dot(a, b, preferred_element_type=F32, precision=precision)
    return lax.dot_general(a, b, dims, preferred_element_type=F32, precision=precision)


def _dot_split(w, x, x1):
    n = w.shape[0]
    x2 = (x - x1.astype(F32)).astype(BF16)
    w1 = w.astype(BF16)
    wr = w - w1.astype(F32)
    w2 = wr.astype(BF16)
    w3 = (wr - w2.astype(F32)).astype(BF16)
    pa = _dot(jnp.concatenate([w1, w2, w3], axis=0), x1, _NT)
    pb = _dot(jnp.concatenate([w1, w2], axis=0), x2, _NT)
    return pa[0:n] + pa[n:2 * n] + pa[2 * n:] + pb[0:n] + pb[n:]


def _sigmoid(x):
    return jax.nn.sigmoid(x)


def _pack_halves(x):
    w = x.shape[-1] // 2
    bits = lax.bitcast_convert_type(x.astype(BF16).astype(F32), I32)
    return lax.shift_right_logical(bits[:, :w], 16) | bits[:, w:]


def _unpack_halves(p):
    lo = lax.bitcast_convert_type(lax.shift_left(p, 16), F32)
    hi = lax.bitcast_convert_type(p & jnp.int32(-65536), F32)
    return jnp.concatenate([lo, hi], axis=-1).astype(BF16)


def _silu(x):
    return x * jax.nn.sigmoid(x)


def _params(sem, flags=None):
    return pltpu.CompilerParams(dimension_semantics=sem, vmem_limit_bytes=VMEM_LIMIT, flags=flags)


def _mod_body(r_ref, w_ref, b_ref, o_ref):
    r = r_ref[...]
    o_ref[...] = _dot(_silu(r), w_ref[...], precision=HIGHEST) + b_ref[...]


def _modulation(rows, w, b):
    d, n = w.shape
    tn = n // 4
    return pl.pallas_call(
        _mod_body,
        grid=(n // tn,),
        in_specs=[pl.BlockSpec((8, d), lambda j: (0, 0)),
                  pl.BlockSpec((d, tn), lambda j: (0, j)),
                  pl.BlockSpec((1, tn), lambda j: (0, j))],
        out_specs=pl.BlockSpec((8, tn), lambda j: (0, j)),
        out_shape=jax.ShapeDtypeStruct((8, n), F32),
        compiler_params=_params(("arbitrary",)),
        name="modulation",
    )(rows, w, b)


def _log_sigmoid(x):
    return jnp.minimum(x, 0.0) - jnp.log(1.0 + jnp.exp(-jnp.abs(x)))


def _inproj_body(x_ref, sc_ref, sh_ref, nw_ref, w_ref, wg_ref, gb_ref, lbl_ref, *refs):
    hgq_ref, hgv_ref, hgg_ref, hgk_ref, hglf_ref, mlqk_ref, mlv_ref, mlo_ref, gates_ref = refs[-9:]
    kw = HEADS_W
    x = x_ref[0]
    y = x * lax.rsqrt(jnp.mean(x * x, axis=-1, keepdims=True) + NORM_EPS) * nw_ref[...]
    u = y * (1.0 + sc_ref[0]) + sh_ref[0]
    ub = u.astype(BF16)

    def proj(c0, c1):
        return _dot(ub, w_ref[:, c0:c1])

    hgq_ref[0] = _silu(proj(0, kw)).astype(BF16)
    hgv_ref[0] = proj(kw, 2 * kw).astype(BF16)
    hgg_ref[0] = _silu(proj(2 * kw, 3 * kw)).astype(BF16)

    lbl = lbl_ref[...]
    mx = jnp.max(lbl, axis=0)
    ex = jnp.exp(lbl - mx[None])
    lb = ex[0] / jnp.sum(ex, axis=0)
    for d in range(2):
        p = proj((3 + d) * kw, (4 + d) * kw)
        lbd = lb[d:d + 1]
        f = lbd + (1.0 - lbd) * _sigmoid(p)
        hgk_ref[0, :, d * kw:(d + 1) * kw] = (1.0 - f).astype(BF16)
        hglf_ref[0, :, d * kw:(d + 1) * kw] = jnp.log(f)

    mlqk_ref[0, :, 0:kw] = proj(5 * kw, 6 * kw)
    mlqk_ref[0, :, kw:2 * kw] = proj(6 * kw, 7 * kw)
    mlv_ref[0] = proj(7 * kw, 8 * kw).astype(BF16)
    mlo_ref[0] = _sigmoid(proj(8 * kw, 9 * kw)).astype(BF16)

    g = _dot_split(wg_ref[...], u, ub) + gb_ref[...]
    row = lax.broadcasted_iota(I32, g.shape, 0)
    gates_ref[0] = jnp.where((row % 8) >= N_HEADS, _log_sigmoid(g), g)


def _inproj_shapes(bsz, tall):
    kw = HEADS_W
    return [
        jax.ShapeDtypeStruct((bsz, tall, kw), BF16),
        jax.ShapeDtypeStruct((bsz, tall, kw), BF16),
        jax.ShapeDtypeStruct((bsz, tall, kw), BF16),
        jax.ShapeDtypeStruct((bsz, tall, 2 * kw), BF16),
        jax.ShapeDtypeStruct((bsz, tall, 2 * kw), F32),
        jax.ShapeDtypeStruct((bsz, tall, 2 * kw), F32),
        jax.ShapeDtypeStruct((bsz, tall, kw), BF16),
        jax.ShapeDtypeStruct((bsz, tall, kw), BF16),
        jax.ShapeDtypeStruct((bsz, 4 * N_HEADS, tall), F32),
    ]


def _inproj(tokens, scale, shift, nw, w_main, w_gt, gate_b, lb_logits, tall, tm, blk0, prev=None):
    bsz, n, d = tokens.shape
    kw = HEADS_W
    nt = n // tm
    per_sample = scale.shape[0] == bsz
    mod_map = (lambda b, i: (b, 0, 0)) if per_sample else (lambda b, i: (0, 0, 0))
    const2 = lambda b, i: (0, 0)
    in_specs = [
        pl.BlockSpec((1, tm, d), lambda b, i: (b, i, 0)),
        pl.BlockSpec((1, 1, d), mod_map),
        pl.BlockSpec((1, 1, d), mod_map),
        pl.BlockSpec((1, d), const2),
        pl.BlockSpec(w_main.shape, const2, pipeline_mode=pl.Buffered(1)),
        pl.BlockSpec(w_gt.shape, const2),
        pl.BlockSpec(gate_b.shape, const2),
        pl.BlockSpec(lb_logits.shape, lambda b, i: (0, 0, 0)),
    ]
    args = [tokens, scale, shift, nw, w_main, w_gt, gate_b, lb_logits]
    aliases = {}
    if prev is not None:
        for k, a in enumerate(prev):
            in_specs.append(pl.BlockSpec(memory_space=pl.ANY))
            aliases[len(args)] = k
            args.append(a)
    row_map = lambda b, i: (b, blk0 + i, 0)
    widths = [kw, kw, kw, 2 * kw, 2 * kw, 2 * kw, kw, kw]
    out_specs = [pl.BlockSpec((1, tm, w), row_map) for w in widths]
    out_specs.append(pl.BlockSpec((1, 4 * N_HEADS, tm), lambda b, i: (b, 0, blk0 + i)))
    return pl.pallas_call(
        _inproj_body,
        grid=(bsz, nt),
        in_specs=in_specs,
        out_specs=out_specs,
        out_shape=_inproj_shapes(bsz, tall),
        input_output_aliases=aliases,
        compiler_params=_params(("arbitrary", "arbitrary")),
        name="inproj_ctx" if prev is not None else "inproj_x",
    )(*args)


def _conv_body(x_ref, w_ref, b_ref, o_ref, pad_ref, cpad_ref, *, t, nctx, scale_from):
    halo = CONV_HALO
    rows = 512
    win = rows + 2 * halo
    ch = x_ref.shape[-1]
    scale = jnp.where(pl.program_id(1) >= scale_from, HEAD_DIM ** -0.5, 1.0).astype(F32)
    w = w_ref[...]
    bias = b_ref[...]

    pad_ref[0:halo, :] = jnp.zeros((halo, ch), F32)
    pad_ref[halo + t:halo + t + halo, :] = jnp.zeros((halo, ch), F32)
    pad_ref[halo:halo + t, :] = x_ref[0, 0:t, :]
    col = (lax.broadcasted_iota(I32, (win, ch), 0) + (GRID_W - halo % GRID_W)) % GRID_W
    left_ok = col > 0
    right_ok = col < GRID_W - 1

    def chunk(c, carry):
        o = pl.multiple_of(c * rows, rows)
        xw = pad_ref[pl.ds(o, win), :]
        xm = jnp.where(left_ok, pltpu.roll(xw, 1, 0), 0.0)
        xp = jnp.where(right_ok, pltpu.roll(xw, win - 1, 0), 0.0)
        def taps(dr, lo):
            sl = slice(lo, lo + rows)
            return xm[sl] * w[dr, 0:1] + xw[sl] * w[dr, 1:2] + xp[sl] * w[dr, 2:3]

        y = taps(1, halo) + taps(0, halo - GRID_W) + taps(2, halo + GRID_W)
        o_ref[0, pl.ds(o, rows), :] = (_silu(y + bias) * scale).astype(o_ref.dtype)
        return carry

    lax.fori_loop(0, t // rows, chunk, 0)

    cpad_ref[0:8, :] = jnp.zeros((8, ch), F32)
    cpad_ref[8 + nctx:16 + nctx, :] = jnp.zeros((8, ch), F32)
    cpad_ref[8:8 + nctx, :] = x_ref[0, t:t + nctx, :]
    xw = cpad_ref[...]
    n = nctx + 16
    y = (pltpu.roll(xw, 1, 0) * w[1, 0:1] + xw * w[1, 1:2] + pltpu.roll(xw, n - 1, 0) * w[1, 2:3])[8:8 + nctx]
    o_ref[0, t:t + nctx, :] = (_silu(y + bias) * scale).astype(o_ref.dtype)


def _conv(qk_pre, conv_w, conv_b, t, nctx):
    bsz, tall, c = qk_pre.shape
    ch = 128
    body = functools.partial(_conv_body, t=t, nctx=nctx, scale_from=(c // 2) // ch)
    return pl.pallas_call(
        body,
        grid=(bsz, c // ch),
        in_specs=[pl.BlockSpec((1, tall, ch), lambda b, j: (b, 0, j)),
                  pl.BlockSpec((3, 3, ch), lambda b, j: (0, 0, j)),
                  pl.BlockSpec((1, ch), lambda b, j: (0, j))],
        out_specs=pl.BlockSpec((1, tall, ch), lambda b, j: (b, 0, j)),
        out_shape=jax.ShapeDtypeStruct((bsz, tall, c), BF16),
        scratch_shapes=[pltpu.VMEM((t + 2 * CONV_HALO, ch), F32),
                        pltpu.VMEM((nctx + 16, ch), F32)],
        compiler_params=_params(("arbitrary", "arbitrary")),
        name="qk_conv",
    )(qk_pre, conv_w, conv_b)


def _fwd_blk(s, nx):
    return jnp.where(s == 0, nx, s - 1)


def _bwd_blk(s, nx):
    return jnp.where(s == 0, nx, nx - s)


def _hg_constants(rev):
    c = HG_CHUNK
    i = np.arange(c)[:, None]
    j = np.arange(c)[None, :]
    blocks = [(j >= i) if rev else (j <= i)]
    masks = [i == j]
    m = c // 2
    while m >= 1:
        b0 = (i // (2 * m)) * (2 * m)
        same = (i // (2 * m)) == (j // (2 * m))
        if rev:
            beta = b0 + m
            qrow = (i % (2 * m)) < m
            g = np.where(qrow, (j >= i) & (j < beta), (j >= beta) & (j < i))
            mask = same & qrow & ((j % (2 * m)) >= m)
        else:
            beta = b0 + m - 1
            qrow = (i % (2 * m)) >= m
            g = np.where(qrow, (j > beta) & (j <= i), (j > i) & (j <= beta))
            mask = same & qrow & ((j % (2 * m)) < m)
        if m < HG_VPU_LEVEL_MIN:
            blocks.append(g)
        masks.append(mask)
        m //= 2
    g = np.concatenate(blocks, axis=0).astype(np.float32)
    g3 = np.concatenate([g, g, g], axis=1)
    m2 = np.concatenate([np.stack(masks), np.stack(masks)], axis=2)
    return (jnp.asarray(g3, BF16), jnp.asarray(m2, F32))


def _hg_level_decay(a, m, rev):
    c = a.shape[0]
    parts = []
    for b0 in range(0, c, 2 * m):
        beta = b0 + m if rev else b0 + m - 1
        ref = a[beta:beta + 1]
        if m % 8 == 0:
            first, second = a[b0:b0 + m], a[b0 + m:b0 + 2 * m]
            parts += [first - ref, ref - second] if rev else [ref - first, second - ref]
        else:
            d = a[b0:b0 + 2 * m] - ref
            parts.append(jnp.minimum(d, -d))
    return jnp.concatenate(parts, axis=0)


def _block_diag(x, zero):
    w = x.shape[1] // 2
    return jnp.concatenate([jnp.concatenate([x[:, :w], zero], axis=1),
                            jnp.concatenate([zero, x[:, w:]], axis=1)], axis=0)


def _hg_chunk(dirs):
    c = HG_CHUNK
    w = 2 * HEAD_DIM
    zero = jnp.zeros((c, HEAD_DIM), BF16)
    units = []
    for rev, r0, bb, q_ref, k_ref, v_ref, lf_ref, g_ref, msk_ref, o_ref, st_ref in dirs:
        rows = pl.ds(r0, c)
        lf = lf_ref[bb, rows, :] * LOG2_E
        p1 = lf.astype(BF16)
        r1 = lf - p1.astype(F32)
        p2 = r1.astype(BF16)
        p3 = (r1 - p2.astype(F32)).astype(BF16)
        dall = _dot(g_ref[...], jnp.concatenate([p1, p2, p3], axis=0))
        for hp in range(N_HEADS // 2):
            cs = slice(hp * w, (hp + 1) * w)
            units.append(dict(rev=rev, rows=rows, cs=cs, hp=hp, bb=bb, msk_ref=msk_ref, o_ref=o_ref, st_ref=st_ref,
                              q=q_ref[bb, rows, cs], k=k_ref[bb, rows, cs], v=v_ref[bb, rows, cs], dall=dall[:, cs]))
    for u in units:
        msk_ref = u["msk_ref"]
        nlev = msk_ref.shape[0] - 1
        att = _dot(u["q"], _block_diag(u["k"], zero), _NT) * msk_ref[0]
        a = u["dall"][0:c]
        row = 1
        for l in range(nlev):
            m = c >> (l + 1)
            if m >= HG_VPU_LEVEL_MIN:
                dec = _hg_level_decay(a, m, u["rev"])
            else:
                dec = u["dall"][row * c:(row + 1) * c]
                row += 1
            e = jnp.exp2(dec).astype(BF16)
            att = att + _dot(u["q"] * e, _block_diag(u["k"] * e, zero), _NT) * msk_ref[l + 1]
        u["att"] = att.astype(BF16)
    for u in units:
        a = u["dall"][0:c]
        u["a_tot"] = a[0:1] if u["rev"] else a[c - 1:c]
        u["st"] = [u["st_ref"][2 * u["hp"] + i] for i in range(2)]
        zf = jnp.zeros((HEAD_DIM, HEAD_DIM), BF16)
        st2 = jnp.concatenate([jnp.concatenate([u["st"][0].astype(BF16), zf], axis=1),
                               jnp.concatenate([zf, u["st"][1].astype(BF16)], axis=1)], axis=0)
        qbar = (u["q"].astype(F32) * jnp.exp2(a)).astype(BF16)
        u["o"] = _dot(u["att"], _block_diag(u["v"], zero)) + _dot(qbar, st2, _NT)
        u["khat"] = (u["k"].astype(F32) * jnp.exp2(u["a_tot"] - a)).astype(BF16)
    for u in units:
        u["o_ref"][u["bb"], u["rows"], u["cs"]] = u["o"].astype(BF16)
        for i in range(2):
            hs = slice(i * HEAD_DIM, (i + 1) * HEAD_DIM)
            upd = _dot(u["v"][:, hs], u["khat"][:, hs], _TN)
            u["st_ref"][2 * u["hp"] + i] = u["st"][i] * jnp.exp2(u["a_tot"][:, hs]) + upd


def _hg_body(qf_ref, kf_ref, vf_ref, lff_ref, qb_ref, kb_ref, vb_ref, lfb_ref,
             gf_ref, mf_ref, gb_ref, mb_ref, of_ref, ob_ref, st_ref):
    @pl.when(pl.program_id(1) == 0)
    def _():
        st_ref[...] = jnp.zeros(st_ref.shape, F32)

    nsub = SCAN_STEP // HG_CHUNK
    for c in range(nsub):
        dirs = []
        for bb in range(qf_ref.shape[0]):
            dirs.append((False, c * HG_CHUNK, bb, qf_ref, kf_ref, vf_ref, lff_ref, gf_ref, mf_ref, of_ref,
                         st_ref.at[0, bb]))
            dirs.append((True, (nsub - 1 - c) * HG_CHUNK, bb, qb_ref, kb_ref, vb_ref, lfb_ref, gb_ref, mb_ref, ob_ref,
                         st_ref.at[1, bb]))
        _hg_chunk(dirs)


def _hgrn2(hgq, hgk, hgv, hglf, nx):
    bsz, tall, kw = hgq.shape
    steps = tall // SCAN_STEP
    cf = _hg_constants(False)
    cb = _hg_constants(True)
    nb = HG_SAMPLES if bsz % HG_SAMPLES == 0 else 1
    blk = (nb, SCAN_STEP, kw)
    fwd = lambda col: (lambda b, s: (b, _fwd_blk(s, nx), col))
    bwd = lambda col: (lambda b, s: (b, _bwd_blk(s, nx), col))
    const = lambda a: pl.BlockSpec(a.shape, lambda b, s: (0,) * a.ndim)
    in_specs = [pl.BlockSpec(blk, fwd(0)), pl.BlockSpec(blk, fwd(0)), pl.BlockSpec(blk, fwd(0)), pl.BlockSpec(blk, fwd(0)),
                pl.BlockSpec(blk, bwd(0)), pl.BlockSpec(blk, bwd(1)), pl.BlockSpec(blk, bwd(0)), pl.BlockSpec(blk, bwd(1))]
    in_specs += [const(a) for a in cf + cb]
    out_sds = jax.ShapeDtypeStruct((bsz, tall, kw), BF16)
    return pl.pallas_call(
        _hg_body,
        grid=(bsz // nb, steps),
        in_specs=in_specs,
        out_specs=[pl.BlockSpec(blk, fwd(0)), pl.BlockSpec(blk, bwd(0))],
        out_shape=[out_sds, out_sds],
        scratch_shapes=[pltpu.VMEM((2, nb, N_HEADS, HEAD_DIM, HEAD_DIM), F32)],
        compiler_params=_params(("arbitrary", "arbitrary")),
        name="hgrn2_scan",
    )(hgq, hgk, hgv, hglf, hgq, hgk, hgv, hglf, *cf, *cb)


def _ml_constants(rev):
    k = np.arange(ML_CHUNK)
    tri = (k[:, None] >= k[None, :]) if rev else (k[:, None] <= k[None, :])
    return jnp.asarray(np.concatenate([tri, tri, tri], axis=0), BF16)


def _ml_chunk(dirs):
    c = ML_CHUNK
    ii = lax.broadcasted_iota(I32, (c, c), 0)
    jj = lax.broadcasted_iota(I32, (c, c), 1)
    ones = jnp.ones((c, HEAD_DIM), BF16)
    units = []
    for rev, r0, d, bb, q_ref, k_ref, v_ref, g_ref, tri3_ref, o_ref, st_ref, m_ref in dirs:
        gates = g_ref[bb, :, pl.ds(r0, c)] * LOG2_E
        p1 = gates.astype(BF16)
        r1 = gates - p1.astype(F32)
        p2 = r1.astype(BF16)
        p3 = (r1 - p2.astype(F32)).astype(BF16)
        csum = _dot(jnp.concatenate([p1, p2, p3], axis=1), tri3_ref[...])
        for h in range(N_HEADS):
            cs = slice(h * HEAD_DIM, (h + 1) * HEAD_DIM)
            u = dict(rev=rev, o_ref=o_ref, bb=bb, rows=pl.ds(r0, c), cs=cs, st_ref=st_ref, m_ref=m_ref, h=h)
            u["qb"] = q_ref[bb, pl.ds(r0, c), cs]
            u["kb"] = k_ref[bb, pl.ds(r0, c), cs]
            u["v1"] = jnp.concatenate([v_ref[bb, pl.ds(r0, c), cs], ones], axis=1)
            irow = gates[d * 8 + h:d * 8 + h + 1]
            u["brow"] = csum[d * 8 + N_HEADS + h:d * 8 + N_HEADS + h + 1]
            u["rrow"] = irow - u["brow"]
            units.append(u)
    for u in units:
        u["st"] = u["st_ref"][u["h"]]
        a = _dot(jnp.concatenate([u["kb"], u["st"].astype(BF16)], axis=0), u["qb"], _NT)
        u["s"] = a[:c]
        u["sq"] = a[c:]
    for u in units:
        last = 0 if u["rev"] else c - 1
        tri_t = (ii >= jj) if u["rev"] else (ii <= jj)
        rcol = jnp.concatenate([u["rrow"], jnp.zeros((7, c), F32)], axis=0).T[:, 0:1]
        u["mprev"] = u["m_ref"][u["h"]][:, 0:1]
        rmat = jnp.where(tri_t, rcol, -jnp.inf)
        u["grow"] = jnp.maximum(jnp.max(rmat, axis=0, keepdims=True), u["mprev"])
        qk = (u["s"] * jnp.exp2(rmat - u["grow"])).astype(BF16)
        blast = u["brow"][:, last:last + 1]
        u["mnew"] = blast + u["grow"][:, last:last + 1]
        kh = (u["kb"].astype(F32) * jnp.exp2(blast + rcol - u["mnew"])).astype(BF16)
        u["ws"] = jnp.exp2(blast + u["mprev"] - u["mnew"])
        u["qkh"] = jnp.concatenate([qk, kh], axis=1)
    for u in units:
        u["nu"] = _dot(u["v1"], u["qkh"], _TN)
    for u in units:
        both = u["nu"][:, :c] + jnp.exp2(u["mprev"] - u["grow"]) * u["sq"]
        den = both[HEAD_DIM:HEAD_DIM + 1]
        inv = 1.0 / jnp.maximum(jnp.abs(den), jnp.exp2(-(u["brow"] + u["grow"])))
        u["o_ref"][u["bb"], u["rows"], u["cs"]] = (both[:HEAD_DIM] * inv).T.astype(BF16)
        u["st_ref"][u["h"]] = u["ws"] * u["st"] + u["nu"][:, c:]
        u["m_ref"][u["h"]] = jnp.broadcast_to(u["mnew"], (1, HEAD_DIM))


def _ml_body(qf_ref, kf_ref, vf_ref, gf_ref, qb_ref, kb_ref, vb_ref, gb_ref, tf_ref, tb_ref,
             of_ref, ob_ref, st_ref, m_ref):
    @pl.when(pl.program_id(1) == 0)
    def _():
        st_ref[...] = jnp.zeros(st_ref.shape, F32)
        m_ref[...] = jnp.zeros(m_ref.shape, F32)

    nsub = SCAN_STEP // ML_CHUNK
    for c in range(nsub):
        dirs = []
        for bb in range(qf_ref.shape[0]):
            dirs.append((False, c * ML_CHUNK, 0, bb, qf_ref, kf_ref, vf_ref, gf_ref, tf_ref, of_ref,
                         st_ref.at[0, bb], m_ref.at[0, bb]))
            dirs.append((True, (nsub - 1 - c) * ML_CHUNK, 1, bb, qb_ref, kb_ref, vb_ref, gb_ref, tb_ref, ob_ref,
                         st_ref.at[1, bb], m_ref.at[1, bb]))
        _ml_chunk(dirs)


def _mlstm(mlqk, mlv, gates, nx):
    bsz, tall, kw = mlv.shape
    steps = tall // SCAN_STEP
    nb = ML_SAMPLES if bsz % ML_SAMPLES == 0 else 1
    blk = (nb, SCAN_STEP, kw)
    gblk = (nb, 4 * N_HEADS, SCAN_STEP)
    tf, tb = _ml_constants(False), _ml_constants(True)
    fwd = lambda col: (lambda b, s: (b, _fwd_blk(s, nx), col))
    bwd = lambda col: (lambda b, s: (b, _bwd_blk(s, nx), col))
    const = pl.BlockSpec(tf.shape, lambda b, s: (0, 0))
    in_specs = [pl.BlockSpec(blk, fwd(0)), pl.BlockSpec(blk, fwd(1)), pl.BlockSpec(blk, fwd(0)),
                pl.BlockSpec(gblk, lambda b, s: (b, 0, _fwd_blk(s, nx))),
                pl.BlockSpec(blk, bwd(0)), pl.BlockSpec(blk, bwd(1)), pl.BlockSpec(blk, bwd(0)),
                pl.BlockSpec(gblk, lambda b, s: (b, 0, _bwd_blk(s, nx))), const, const]
    out_sds = jax.ShapeDtypeStruct((bsz, tall, kw), BF16)
    return pl.pallas_call(
        _ml_body,
        grid=(bsz // nb, steps),
        in_specs=in_specs,
        out_specs=[pl.BlockSpec(blk, fwd(0)), pl.BlockSpec(blk, bwd(0))],
        out_shape=[out_sds, out_sds],
        scratch_shapes=[pltpu.VMEM((2, nb, N_HEADS, 2 * HEAD_DIM, HEAD_DIM), F32),
                        pltpu.VMEM((2, nb, N_HEADS, 1, HEAD_DIM), F32)],
        compiler_params=_params(("arbitrary", "arbitrary")),
        name="mlstm_scan",
    )(mlqk, mlqk, mlv, gates, mlqk, mlqk, mlv, gates, tf, tb)


def _out_body(hof_ref, hob_ref, mhf_ref, mhb_ref, hgg_ref, mlo_ref, x_ref, g1_ref, sc2_ref, sh2_ref,
              hnw_ref, mnw_ref, wout_ref, n2w_ref, rwt_ref, x1_ref, vt_ref, aff_ref):
    hg = hof_ref[0].astype(F32) + hob_ref[0].astype(F32)
    ml = mhf_ref[0].astype(F32) + mhb_ref[0].astype(F32)
    hparts, mparts = [], []
    for h in range(N_HEADS):
        cs = slice(h * HEAD_DIM, (h + 1) * HEAD_DIM)
        t = hg[:, cs]
        hparts.append(t * lax.rsqrt(jnp.mean(t * t, axis=-1, keepdims=True) + NORM_EPS))
        t = ml[:, cs]
        t = t - jnp.mean(t, axis=-1, keepdims=True)
        mparts.append(t * lax.rsqrt(jnp.mean(t * t, axis=-1, keepdims=True) + NORM_EPS))
    hgn = jnp.concatenate(hparts, axis=-1) * hnw_ref[...] * hgg_ref[0].astype(F32)
    mln = jnp.concatenate(mparts, axis=-1) * mnw_ref[...] * mlo_ref[0].astype(F32)
    mix = jnp.concatenate([hgn, mln], axis=-1).astype(BF16)
    x1 = x_ref[0] + g1_ref[0] * _dot(mix, wout_ref[...])
    x1_ref[0] = x1
    v = x1 * lax.rsqrt(jnp.mean(x1 * x1, axis=-1, keepdims=True) + NORM_EPS) * n2w_ref[...]
    v = v * (1.0 + sc2_ref[0]) + sh2_ref[0]
    vt_ref[0] = _pack_halves(v)
    logits = _dot_split(rwt_ref[...], v, v.astype(BF16))
    ex = jnp.exp(logits - jnp.max(logits, axis=0, keepdims=True))
    aff_ref[0] = ex / jnp.sum(ex, axis=0, keepdims=True)


def _mixer_out(hg_f, hg_b, ml_f, ml_b, hgg, mlo, x, g1, sc2, sh2, hnw, mnw, w_out, n2w, rwt, tm):
    bsz, t, d = x.shape
    kw = HEADS_W
    ne = rwt.shape[0]
    row = lambda b, i: (b, i, 0)
    mod = lambda b, i: (b, 0, 0)
    const2 = lambda b, i: (0, 0)
    act = pl.BlockSpec((1, tm, kw), row)
    in_specs = [act, act, act, act, act, act,
                pl.BlockSpec((1, tm, d), row),
                pl.BlockSpec((1, 1, d), mod), pl.BlockSpec((1, 1, d), mod), pl.BlockSpec((1, 1, d), mod),
                pl.BlockSpec((1, kw), const2), pl.BlockSpec((1, kw), const2),
                pl.BlockSpec(w_out.shape, const2), pl.BlockSpec((1, d), const2), pl.BlockSpec(rwt.shape, const2)]
    return pl.pallas_call(
        _out_body,
        grid=(bsz, t // tm),
        in_specs=in_specs,
        out_specs=[pl.BlockSpec((1, tm, d), row),
                   pl.BlockSpec((1, tm, d // 2), row),
                   pl.BlockSpec((1, ne, tm), lambda b, i: (b, 0, i))],
        out_shape=[jax.ShapeDtypeStruct((bsz, t, d), F32),
                   jax.ShapeDtypeStruct((bsz, t, d // 2), I32),
                   jax.ShapeDtypeStruct((bsz, ne, t), F32)],
        compiler_params=_params(("arbitrary", "arbitrary")),
        name="mixer_out",
    )(hg_f, hg_b, ml_f, ml_b, hgg, mlo, x, g1, sc2, sh2, hnw, mnw, w_out, n2w, rwt)


def _prefix_count(maskf, u_ref, ones_ref, bl_ref):
    e, nb, ln = maskf.shape
    x = maskf.reshape(e * nb, ln)
    xb = x.astype(BF16)
    incl = _dot(xb, u_ref[...])
    tot = _dot(xb, ones_ref[...])
    off = _dot(bl_ref[...], tot.astype(BF16))
    return (incl - x + off).reshape(e, nb, ln), off.reshape(e, nb, ln)


def _topk_body(aff_ref, u_ref, ones_ref, bl_ref, bl1_ref, pos_ref, base_ref, *, cap):
    x = aff_ref[0]
    ne = x.shape[0]

    def count(m):
        return jnp.sum(jnp.sum(jnp.where(m, 1.0, 0.0), axis=1, keepdims=True), axis=2, keepdims=True)

    def halve(_, carry):
        lo, hi = carry
        mid = 0.5 * (lo + hi)
        up = count(x > mid) >= cap
        return jnp.where(up, mid, lo), jnp.where(up, hi, mid)

    lo, hi = lax.fori_loop(0, TOPK_BISECTIONS, halve,
                           (jnp.full((ne, 1, 1), -1.0, F32), jnp.full((ne, 1, 1), 1.0, F32)))
    gt = jnp.where(x > hi, 1.0, 0.0)
    eq = jnp.where(x > lo, 1.0, 0.0) - gt
    need = cap - count(x > hi)
    eq_rank, _ = _prefix_count(eq, u_ref, ones_ref, bl_ref)
    sel = gt + eq * jnp.where(eq_rank < need, 1.0, 0.0)
    pos, _ = _prefix_count(sel, u_ref, ones_ref, bl_ref)
    pos_ref[0] = jnp.where(sel > 0, pos, -1.0).astype(I32)
    nsel = jnp.sum(sel, axis=0)
    nb16 = nsel.astype(BF16)
    incl = _dot(nb16, u_ref[...])
    tot = _dot(nb16, ones_ref[...])
    off = _dot(bl1_ref[...], tot, precision=HIGHEST)
    base_ref[0] = (incl - nsel + off).astype(I32)


def _topk(aff4, cap):
    bsz, ne, nb, ln = aff4.shape
    k = np.arange(ln)
    u = jnp.asarray(k[:, None] <= k[None, :], BF16)
    ones = jnp.ones((ln, ln), BF16)
    r = np.arange(ne * nb)
    bl = jnp.asarray(((r[:, None] // nb) == (r[None, :] // nb)) & (r[None, :] < r[:, None]), BF16)
    r1 = np.arange(nb)
    bl1 = jnp.asarray(r1[None, :] < r1[:, None], F32)
    blk = pl.BlockSpec((1, ne, nb, ln), lambda b: (b, 0, 0, 0))
    const = lambda a: pl.BlockSpec(a.shape, lambda b: (0, 0))
    return pl.pallas_call(
        functools.partial(_topk_body, cap=cap),
        grid=(bsz,),
        in_specs=[blk, const(u), const(ones), const(bl), const(bl1)],
        out_specs=[blk, pl.BlockSpec((1, nb, ln), lambda b: (b, 0, 0))],
        out_shape=[jax.ShapeDtypeStruct((bsz, ne, nb, ln), I32), jax.ShapeDtypeStruct((bsz, nb, ln), I32)],
        compiler_params=_params(("arbitrary",)),
        name="expert_topk",
    )(aff4, u, ones, bl, bl1)


def _sc_setup():
    info = plsc.get_sparse_core_info()
    mesh = plsc.VectorSubcoreMesh(core_axis_name="c", subcore_axis_name="s")
    params = dataclasses.replace(pltpu.CompilerParams(), needs_layout_passes=False)
    return info.num_cores, info.num_cores * info.num_subcores, info.num_lanes, mesh, params


def _sc_worker_id(nc):
    return lax.axis_index("s") * nc + lax.axis_index("c")


def _sc_gather_scratch(width, dtype):
    one = [pltpu.VMEM((SC_GATHER_ROWS,), I32), pltpu.VMEM((SC_GATHER_ROWS, width), dtype), pltpu.SemaphoreType.DMA]
    return one + one


def _sc_gather_chunks(table_hbm, out_hbm, idx_v, bufs, out_row0, n, lanes):
    chunk = SC_GATHER_ROWS

    def gather(c, buf):
        ich_v, rows_v, sem = buf
        for q in range(chunk // lanes):
            src = pl.ds(pl.multiple_of(c * chunk + q * lanes, lanes), lanes)
            ich_v[pl.ds(q * lanes, lanes)] = idx_v[src]
        return pltpu.make_async_copy(table_hbm.at[ich_v], rows_v, sem)

    def finish(c, buf):
        ich_v, rows_v, sem = buf
        pltpu.make_async_copy(table_hbm.at[ich_v], rows_v, sem).wait()
        pltpu.sync_copy(rows_v, out_hbm.at[pl.ds(pl.multiple_of(out_row0 + c * chunk, chunk), chunk)])

    npair = n // (2 * chunk)
    gather(0, bufs[0]).start()

    @pl.loop(0, npair)
    def _(i):
        gather(2 * i + 1, bufs[1]).start()
        finish(2 * i, bufs[0])

        @pl.when(i + 1 < npair)
        def _():
            gather(2 * i + 2, bufs[0]).start()

        finish(2 * i + 1, bufs[1])


def _sc_invert_gather(table, pos_flat, aff_flat, npairs, ne, t, cap):
    nc, nw, lanes, mesh, params = _sc_setup()
    split = max(1, nw // npairs)
    seg = cap // split
    per_w = npairs * split // nw
    width = table.shape[1]

    @functools.partial(
        pl.kernel, mesh=mesh, compiler_params=params,
        out_type=[jax.ShapeDtypeStruct((npairs * cap, width), table.dtype),
                  jax.ShapeDtypeStruct((npairs * cap,), F32)],
        scratch_types=[pltpu.VMEM((t,), I32), pltpu.VMEM((t,), F32), pltpu.VMEM((seg,), I32), pltpu.VMEM((seg,), F32)]
        + _sc_gather_scratch(width, table.dtype),
    )
    def body(table_hbm, pos_hbm, aff_hbm, out_hbm, gate_hbm, pos_v, aff_v, idx_v, gate_v, *g):
        wid = _sc_worker_id(nc)

        @pl.loop(0, per_w)
        def _(kk):
            item = wid * per_w + kk
            p = item // split
            lo = (item % split) * seg
            tok0 = (p // ne) * t
            pltpu.sync_copy(pos_hbm.at[pl.ds(pl.multiple_of(p * t, t), t)], pos_v)
            pltpu.sync_copy(aff_hbm.at[pl.ds(pl.multiple_of(p * t, t), t)], aff_v)

            @pl.loop(0, t // lanes)
            def _(i):
                v = pos_v[pl.ds(pl.multiple_of(i * lanes, lanes), lanes)] - lo
                tok = lax.iota(I32, lanes) + i * lanes
                plsc.store_scatter(idx_v, [v], tok, mask=(v >= 0) & (v < seg))

            @pl.loop(0, seg // lanes)
            def _(j):
                sl = pl.ds(pl.multiple_of(j * lanes, lanes), lanes)
                ii = idx_v[sl]
                gate_v[sl] = plsc.load_gather(aff_v, [ii])
                idx_v[sl] = ii + tok0

            row0 = p * cap + lo
            pltpu.sync_copy(gate_v, gate_hbm.at[pl.ds(pl.multiple_of(row0, seg), seg)])
            _sc_gather_chunks(table_hbm, out_hbm, idx_v, (g[0:3], g[3:6]), row0, seg, lanes)

    return body(table, pos_flat, aff_flat)


def _sc_assignment_order(pos_flat, base_flat, bsz, ne, t, cap):
    nc, nw, lanes, mesh, params = _sc_setup()
    na = ne * cap
    per_b = nw // bsz
    rng = na // per_b

    @functools.partial(
        pl.kernel, mesh=mesh, compiler_params=params,
        out_type=[jax.ShapeDtypeStruct((bsz * na,), I32), jax.ShapeDtypeStruct((bsz * na,), I32)],
        scratch_types=[pltpu.VMEM((t,), I32), pltpu.VMEM((t,), I32), pltpu.VMEM((t,), I32),
                       pltpu.VMEM((rng,), I32), pltpu.VMEM((rng,), I32)],
    )
    def body(pos_hbm, base_hbm, perm_hbm, tok_hbm, pos_v, base_v, rank_v, perm_v, tok_v):
        wid = _sc_worker_id(nc)
        b = wid // per_b
        a0 = (wid % per_b) * rng
        pltpu.sync_copy(base_hbm.at[pl.ds(pl.multiple_of(b * t, t), t)], base_v)

        @pl.loop(0, t // lanes)
        def _(i):
            rank_v[pl.ds(pl.multiple_of(i * lanes, lanes), lanes)] = jnp.zeros((lanes,), I32)

        @pl.loop(0, ne)
        def _(e):
            p = b * ne + e
            pltpu.sync_copy(pos_hbm.at[pl.ds(pl.multiple_of(p * t, t), t)], pos_v)

            @pl.loop(0, t // lanes)
            def _(i):
                sl = pl.ds(pl.multiple_of(i * lanes, lanes), lanes)
                v = pos_v[sl]
                r = rank_v[sl]
                a = base_v[sl] + r - a0
                sel = v >= 0
                mine = sel & (a >= 0) & (a < rng)
                plsc.store_scatter(perm_v, [a], v + p * cap, mask=mine)
                plsc.store_scatter(tok_v, [a], lax.iota(I32, lanes) + i * lanes, mask=mine)
                rank_v[sl] = r + jnp.where(sel, 1, 0)

        dst = pl.ds(pl.multiple_of(b * na + a0, rng), rng)
        pltpu.sync_copy(perm_v, perm_hbm.at[dst])
        pltpu.sync_copy(tok_v, tok_hbm.at[dst])

    return body(pos_flat, base_flat)


def _sc_gather_rows(table, idx):
    nc, nw, lanes, mesh, params = _sc_setup()
    n = idx.shape[0]
    per_w = n // nw
    width = table.shape[1]
    chunk = SC_GATHER_ROWS

    @functools.partial(
        pl.kernel, mesh=mesh, compiler_params=params,
        out_type=jax.ShapeDtypeStruct((n, width), table.dtype),
        scratch_types=[pltpu.VMEM((per_w,), I32)] + _sc_gather_scratch(width, table.dtype),
    )
    def body(table_hbm, idx_hbm, out_hbm, idx_v, *g):
        row0 = _sc_worker_id(nc) * per_w
        pltpu.sync_copy(idx_hbm.at[pl.ds(pl.multiple_of(row0, per_w), per_w)], idx_v)
        _sc_gather_chunks(table_hbm, out_hbm, idx_v, (g[0:3], g[3:6]), row0, per_w, lanes)

    return body(table, idx)


def _moe1_body(xs_ref, gate_ref, wg_ref, wu_ref, wd_ref, y_ref):
    xs = _unpack_halves(xs_ref[0])
    hg = _dot(xs, wg_ref[0].astype(BF16))
    hu = _dot(xs, wu_ref[0].astype(BF16))
    h = (_silu(hg) * hu).astype(BF16)
    y = _dot(h, wd_ref[0].astype(BF16))
    gt = gate_ref[0].T
    y = jnp.concatenate([y[k * 128:(k + 1) * 128] * gt[:, k:k + 1] for k in range(gt.shape[1])], axis=0)
    y_ref[0] = _pack_halves(y)


def _moe_experts(xs, gate, wg, wu, wd, bsz):
    npairs, cap, half = xs.shape
    ne, d, f = wg.shape
    pair = lambda b, e: (b * ne + e, 0, 0)
    expert = lambda b, e: (e, 0, 0)
    return pl.pallas_call(
        _moe1_body,
        grid=(bsz, ne),
        in_specs=[pl.BlockSpec((1, cap, half), pair),
                  pl.BlockSpec((1, cap // 128, 128), pair),
                  pl.BlockSpec((1, d, f), expert),
                  pl.BlockSpec((1, d, f), expert),
                  pl.BlockSpec((1, f, d), expert)],
        out_specs=pl.BlockSpec((1, cap, half), pair),
        out_shape=jax.ShapeDtypeStruct((npairs, cap, half), I32),
        compiler_params=_params(("arbitrary", "arbitrary")),
        cost_estimate=pl.CostEstimate(flops=6 * npairs * cap * d * f, transcendentals=npairs * cap * f,
                                      bytes_accessed=12 * npairs * d * f + 16 * npairs * cap * half),
        name="moe_experts",
    )(xs, gate, wg, wu, wd)


def _combine_body(abase_ref, ys_hbm, tok_ref, x1_ref, g2_ref, fw_ref, o_ref, buf_ref, acc_ref, sem_ref, cnt_ref, *, na):
    n = ONEHOT_BLK
    ring = buf_ref.shape[0]
    nch = na // n
    b, i = pl.program_id(0), pl.program_id(1)

    @pl.when(i == 0)
    def _():
        cnt_ref[0] = 0
        cnt_ref[1] = 0

    lo = abase_ref[b, i]
    hi = abase_ref[b, i + 1]
    c0 = lo // n
    c1 = jnp.where(hi > lo, (hi + n - 1) // n, c0)

    def copy(c):
        slot = c % ring
        row0 = pl.multiple_of(b * na + c * n, n)
        return pltpu.make_async_copy(ys_hbm.at[pl.ds(row0, n)], buf_ref.at[slot], sem_ref.at[slot])

    acc_ref[...] = jnp.zeros(acc_ref.shape, F32)
    tok_ids = lax.broadcasted_iota(I32, (n, n), 0) + i * n

    def step(c, carry):
        started = cnt_ref[0]
        ahead = jnp.minimum(c + ring, nch)

        def start(k, _):
            copy(k).start()
            return 0

        lax.fori_loop(started, ahead, start, 0)
        cnt_ref[0] = jnp.maximum(started, ahead)

        @pl.when(cnt_ref[1] <= c)
        def _():
            copy(c).wait()
            cnt_ref[1] = c + 1

        seg = jnp.where(tok_ids == tok_ref[0, pl.ds(c, 1), :], 1.0, 0.0).astype(BF16)
        acc_ref[...] += _dot(seg, _unpack_halves(buf_ref[c % ring]))
        return carry

    lax.fori_loop(c0, c1, step, 0)
    x2 = x1_ref[0] + g2_ref[0] * acc_ref[...]
    o_ref[0] = x2 * lax.rsqrt(jnp.mean(x2 * x2, axis=-1, keepdims=True) + NORM_EPS) * fw_ref[...]


def _combine_final(abase, ys, tok, x1, g2, fw, na):
    bsz, t, d = x1.shape
    n = ONEHOT_BLK
    grid_spec = pltpu.PrefetchScalarGridSpec(
        num_scalar_prefetch=1,
        grid=(bsz, t // n),
        in_specs=[pl.BlockSpec(memory_space=pl.ANY),
                  pl.BlockSpec((1, na // n, n), lambda b, i, a: (b, 0, 0)),
                  pl.BlockSpec((1, n, d), lambda b, i, a: (b, i, 0)),
                  pl.BlockSpec((1, 1, d), lambda b, i, a: (b, 0, 0)),
                  pl.BlockSpec((1, d), lambda b, i, a: (0, 0))],
        out_specs=pl.BlockSpec((1, n, d), lambda b, i, a: (b, i, 0)),
        scratch_shapes=[pltpu.VMEM((COMBINE_RING, n, d // 2), I32), pltpu.VMEM((n, d), F32),
                        pltpu.SemaphoreType.DMA((COMBINE_RING,)), pltpu.SMEM((2,), I32)],
    )
    return pl.pallas_call(
        functools.partial(_combine_body, na=na),
        grid_spec=grid_spec,
        out_shape=jax.ShapeDtypeStruct((bsz, t, d), F32),
        compiler_params=_params(("arbitrary", "arbitrary")),
        name="combine_final",
    )(abase, ys, tok, x1, g2, fw)


def kernel(x, c, ctx, c_ctx, ada_w, ada_b, norm1_w, w_in, conv_w, conv_b, hg_lb_logits, ml_gate_b,
           hg_norm_w, ml_norm_w, w_out, norm2_w, router_w, exp_w_gate, exp_w_up, exp_w_down, final_norm_w):
    bsz, t, d = x.shape
    nctx = ctx.shape[1]
    assert ada_w.shape[0] == 1, "single-layer block"
    assert nctx == SCAN_STEP and t % SCAN_STEP == 0 and bsz + 1 <= 8
    tall = t + nctx
    nx = t // SCAN_STEP
    kw = HEADS_W
    ne = router_w.shape[-1]
    cap = EC_CAPACITY * t // ne
    tm = min(512, t)

    rows = jnp.concatenate([c, c_ctx[None], jnp.zeros((7 - bsz, d), F32)], axis=0)
    mod = _modulation(rows, ada_w[0], ada_b[0][None])
    mx = [m[:, None, :] for m in jnp.split(mod[:bsz], 6, axis=-1)]
    mc = [m[:, None, :] for m in jnp.split(mod[bsz:bsz + 1], 6, axis=-1)]
    sh1, sc1, g1, sh2, sc2, g2 = mx
    csh1, csc1 = mc[0], mc[1]

    main_w = 9 * kw
    w_main = w_in[0][:, :main_w].astype(BF16)
    w_gt = w_in[0][:, main_w:].T
    gate_b = ml_gate_b[0][:, None]
    nw1 = norm1_w[0][None]
    outs = _inproj(x, sc1, sh1, nw1, w_main, w_gt, gate_b, hg_lb_logits, tall, min(2 * tm, t), 0)
    outs = _inproj(ctx, csc1, csh1, nw1, w_main, w_gt, gate_b, hg_lb_logits, tall, nctx, t // nctx, prev=outs)
    hgq, hgv, hgg, hgk, hglf, mlqk_pre, mlv, mlo, gates = outs

    mlqk = _conv(mlqk_pre, conv_w[0], conv_b[0][None], t, nctx)
    hg_f, hg_b = _hgrn2(hgq, hgk, hgv, hglf, nx)
    ml_f, ml_b = _mlstm(mlqk, mlv, gates, nx)

    x1, vpk, aff = _mixer_out(hg_f, hg_b, ml_f, ml_b, hgg, mlo, x, g1, sc2, sh2,
                             hg_norm_w[0][None], ml_norm_w[0][None], w_out[0].astype(BF16),
                             norm2_w[0][None], router_w[0].T, min(2 * tm, t))

    pos4, base = _topk(aff.reshape(bsz, ne, t // 128, 128), cap)
    na = ne * cap
    pos_flat = pos4.reshape(-1)
    xs, gate = _sc_invert_gather(vpk.reshape(bsz * t, d // 2), pos_flat, aff.reshape(-1), bsz * ne, ne, t, cap)
    perm, tok = _sc_assignment_order(pos_flat, base.reshape(-1), bsz, ne, t, cap)
    y = _moe_experts(xs.reshape(bsz * ne, cap, d // 2), gate.reshape(bsz * ne, cap // 128, 128),
                     exp_w_gate[0], exp_w_up[0], exp_w_down[0], bsz)
    ys = _sc_gather_rows(y.reshape(bsz * na, d // 2), perm)
    abase = jnp.concatenate([base.reshape(bsz, t)[:, ::ONEHOT_BLK], jnp.full((bsz, 1), na, I32)], axis=1)
    return _combine_final(abase, ys, tok.reshape(bsz, na // ONEHOT_BLK, ONEHOT_BLK), x1, g2, final_norm_w[None], na)
```

```python
import dataclasses
import functools

import numpy as np
import jax
import jax.numpy as jnp
from jax import lax
from jax.experimental import pallas as pl
from jax.experimental.pallas import tpu as pltpu
from jax.experimental.pallas import tpu_sc as plsc

F32 = jnp.float32
BF16 = jnp.bfloat16
I32 = jnp.int32
HIGHEST = lax.Precision.HIGHEST
NORM_EPS = 1e-6
LOG2_E = 1.4426950408889634

LANES = 128
SUBLANES = 8
HEAD_DIM = 128
N_HEADS = 4
HEADS_W = N_HEADS * HEAD_DIM
GATE_ROWS = 2 * N_HEADS
GRID_W = 64
MOD_COL_TILES = 4
EC_CAPACITY = 2
HG_CHUNK = 64
HG_VPU_LEVEL_MIN = 4
ML_CHUNK = 256
SCAN_STEP = 256
HG_SAMPLES = 2
ML_SAMPLES = 4
ONEHOT_BLK = 256
CONV_HALO = 72
CONV_ROWS = 512
SC_GATHER_ROWS = 64
COMBINE_RING = 6
TOPK_BISECTIONS = 64
VMEM_LIMIT = 56 * 1024 * 1024

_NT = (((1,), (1,)), ((), ()))
_TN = (((0,), (0,)), ((), ()))


def _dot(a, b, dims=None, precision=None):
    if dims is None:
        return jnp.dot(a, b, preferred_element_type=F32, precision=precision)
    return lax.dot_general(a, b, dims, preferred_element_type=F32, precision=precision)


def _dot_split(w, x, x1):
    n = w.shape[0]
    x2 = (x - x1.astype(F32)).astype(BF16)
    w1 = w.astype(BF16)
    wr = w - w1.astype(F32)
    w2 = wr.astype(BF16)
    w3 = (wr - w2.astype(F32)).astype(BF16)
    pa = _dot(jnp.concatenate([w1, w2, w3], axis=0), x1, _NT)
    pb = _dot(jnp.concatenate([w1, w2], axis=0), x2, _NT)
    return pa[0:n] + pa[n:2 * n] + pa[2 * n:] + pb[0:n] + pb[n:]


def _sigmoid(x):
    return jax.nn.sigmoid(x)


def _pack_halves(x):
    w = x.shape[-1] // 2
    bits = lax.bitcast_convert_type(x.astype(BF16).astype(F32), I32)
    return lax.shift_right_logical(bits[:, :w], 16) | bits[:, w:]


def _unpack_halves(p):
    lo = lax.bitcast_convert_type(lax.shift_left(p, 16), F32)
    hi = lax.bitcast_convert_type(p & jnp.int32(-65536), F32)
    return jnp.concatenate([lo, hi], axis=-1).astype(BF16)


def _silu(x):
    return x * jax.nn.sigmoid(x)


def _params(sem, flags=None):
    return pltpu.CompilerParams(dimension_semantics=sem, vmem_limit_bytes=VMEM_LIMIT, flags=flags)


def _mod_body(r_ref, w_ref, b_ref, o_ref):
    r = r_ref[...]
    o_ref[...] = _dot(_silu(r), w_ref[...], precision=HIGHEST) + b_ref[...]


def _modulation(rows, w, b):
    d, n = w.shape
    nrows = rows.shape[0]
    tn = n // MOD_COL_TILES
    return pl.pallas_call(
        _mod_body,
        grid=(MOD_COL_TILES,),
        in_specs=[pl.BlockSpec((nrows, d), lambda j: (0, 0)),
                  pl.BlockSpec((d, tn), lambda j: (0, j)),
                  pl.BlockSpec((1, tn), lambda j: (0, j))],
        out_specs=pl.BlockSpec((nrows, tn), lambda j: (0, j)),
        out_shape=jax.ShapeDtypeStruct((nrows, n), F32),
        compiler_params=_params(("arbitrary",)),
        name="modulation",
    )(rows, w, b)


def _log_sigmoid(x):
    return jnp.minimum(x, 0.0) - jnp.log(1.0 + jnp.exp(-jnp.abs(x)))


def _inproj_body(x_ref, sc_ref, sh_ref, nw_ref, w_ref, wg_ref, gb_ref, lbl_ref, *refs):
    hgq_ref, hgv_ref, hgg_ref, hgk_ref, hglf_ref, mlqk_ref, mlv_ref, mlo_ref, gates_ref = refs[-9:]
    kw = HEADS_W
    x = x_ref[0]
    y = x * lax.rsqrt(jnp.mean(x * x, axis=-1, keepdims=True) + NORM_EPS) * nw_ref[...]
    u = y * (1.0 + sc_ref[0]) + sh_ref[0]
    ub = u.astype(BF16)

    def proj(c0, c1):
        return _dot(ub, w_ref[:, c0:c1])

    hgq_ref[0] = _silu(proj(0, kw)).astype(BF16)
    hgv_ref[0] = proj(kw, 2 * kw).astype(BF16)
    hgg_ref[0] = _silu(proj(2 * kw, 3 * kw)).astype(BF16)

    lbl = lbl_ref[...]
    mx = jnp.max(lbl, axis=0)
    ex = jnp.exp(lbl - mx[None])
    lb = ex[0] / jnp.sum(ex, axis=0)
    for d in range(2):
        p = proj((3 + d) * kw, (4 + d) * kw)
        lbd = lb[d:d + 1]
        f = lbd + (1.0 - lbd) * _sigmoid(p)
        hgk_ref[0, :, d * kw:(d + 1) * kw] = (1.0 - f).astype(BF16)
        hglf_ref[0, :, d * kw:(d + 1) * kw] = jnp.log(f)

    mlqk_ref[0, :, 0:kw] = proj(5 * kw, 6 * kw)
    mlqk_ref[0, :, kw:2 * kw] = proj(6 * kw, 7 * kw)
    mlv_ref[0] = proj(7 * kw, 8 * kw).astype(BF16)
    mlo_ref[0] = _sigmoid(proj(8 * kw, 9 * kw)).astype(BF16)

    g = _dot_split(wg_ref[...], u, ub) + gb_ref[...]
    row = lax.broadcasted_iota(I32, g.shape, 0)
    gates_ref[0] = jnp.where((row % GATE_ROWS) >= N_HEADS, _log_sigmoid(g), g)


def _inproj_shapes(bsz, tall):
    kw = HEADS_W
    return [
        jax.ShapeDtypeStruct((bsz, tall, kw), BF16),
        jax.ShapeDtypeStruct((bsz, tall, kw), BF16),
        jax.ShapeDtypeStruct((bsz, tall, kw), BF16),
        jax.ShapeDtypeStruct((bsz, tall, 2 * kw), BF16),
        jax.ShapeDtypeStruct((bsz, tall, 2 * kw), F32),
        jax.ShapeDtypeStruct((bsz, tall, 2 * kw), F32),
        jax.ShapeDtypeStruct((bsz, tall, kw), BF16),
        jax.ShapeDtypeStruct((bsz, tall, kw), BF16),
        jax.ShapeDtypeStruct((bsz, 2 * GATE_ROWS, tall), F32),
    ]


def _inproj(tokens, scale, shift, nw, w_main, w_gt, gate_b, lb_logits, tall, tm, blk0, prev=None):
    bsz, n, d = tokens.shape
    kw = HEADS_W
    nt = n // tm
    per_sample = scale.shape[0] == bsz
    mod_map = (lambda b, i: (b, 0, 0)) if per_sample else (lambda b, i: (0, 0, 0))
    const2 = lambda b, i: (0, 0)
    in_specs = [
        pl.BlockSpec((1, tm, d), lambda b, i: (b, i, 0)),
        pl.BlockSpec((1, 1, d), mod_map),
        pl.BlockSpec((1, 1, d), mod_map),
        pl.BlockSpec((1, d), const2),
        pl.BlockSpec(w_main.shape, const2, pipeline_mode=pl.Buffered(1)),
        pl.BlockSpec(w_gt.shape, const2),
        pl.BlockSpec(gate_b.shape, const2),
        pl.BlockSpec(lb_logits.shape, lambda b, i: (0, 0, 0)),
    ]
    args = [tokens, scale, shift, nw, w_main, w_gt, gate_b, lb_logits]
    aliases = {}
    if prev is not None:
        for k, a in enumerate(prev):
            in_specs.append(pl.BlockSpec(memory_space=pl.ANY))
            aliases[len(args)] = k
            args.append(a)
    row_map = lambda b, i: (b, blk0 + i, 0)
    widths = [kw, kw, kw, 2 * kw, 2 * kw, 2 * kw, kw, kw]
    out_specs = [pl.BlockSpec((1, tm, w), row_map) for w in widths]
    out_specs.append(pl.BlockSpec((1, 2 * GATE_ROWS, tm), lambda b, i: (b, 0, blk0 + i)))
    return pl.pallas_call(
        _inproj_body,
        grid=(bsz, nt),
        in_specs=in_specs,
        out_specs=out_specs,
        out_shape=_inproj_shapes(bsz, tall),
        input_output_aliases=aliases,
        compiler_params=_params(("arbitrary", "arbitrary")),
        name="inproj_ctx" if prev is not None else "inproj_x",
    )(*args)


def _conv_body(x_ref, w_ref, b_ref, o_ref, pad_ref, cpad_ref, *, t, nctx, scale_from):
    halo = CONV_HALO
    rows = CONV_ROWS
    win = rows + 2 * halo
    ch = x_ref.shape[-1]
    scale = jnp.where(pl.program_id(1) >= scale_from, HEAD_DIM ** -0.5, 1.0).astype(F32)
    w = w_ref[...]
    bias = b_ref[...]

    pad_ref[0:halo, :] = jnp.zeros((halo, ch), F32)
    pad_ref[halo + t:halo + t + halo, :] = jnp.zeros((halo, ch), F32)
    pad_ref[halo:halo + t, :] = x_ref[0, 0:t, :]
    col = (lax.broadcasted_iota(I32, (win, ch), 0) + (GRID_W - halo % GRID_W)) % GRID_W
    left_ok = col > 0
    right_ok = col < GRID_W - 1

    def chunk(c, carry):
        o = pl.multiple_of(c * rows, rows)
        xw = pad_ref[pl.ds(o, win), :]
        xm = jnp.where(left_ok, pltpu.roll(xw, 1, 0), 0.0)
        xp = jnp.where(right_ok, pltpu.roll(xw, win - 1, 0), 0.0)
        def taps(dr, lo):
            sl = slice(lo, lo + rows)
            return xm[sl] * w[dr, 0:1] + xw[sl] * w[dr, 1:2] + xp[sl] * w[dr, 2:3]

        y = taps(1, halo) + taps(0, halo - GRID_W) + taps(2, halo + GRID_W)
        o_ref[0, pl.ds(o, rows), :] = (_silu(y + bias) * scale).astype(o_ref.dtype)
        return carry

    lax.fori_loop(0, t // rows, chunk, 0)

    pad = SUBLANES
    cpad_ref[0:pad, :] = jnp.zeros((pad, ch), F32)
    cpad_ref[pad + nctx:2 * pad + nctx, :] = jnp.zeros((pad, ch), F32)
    cpad_ref[pad:pad + nctx, :] = x_ref[0, t:t + nctx, :]
    xw = cpad_ref[...]
    n = nctx + 2 * pad
    y = (pltpu.roll(xw, 1, 0) * w[1, 0:1] + xw * w[1, 1:2] + pltpu.roll(xw, n - 1, 0) * w[1, 2:3])[pad:pad + nctx]
    o_ref[0, t:t + nctx, :] = (_silu(y + bias) * scale).astype(o_ref.dtype)


def _conv(qk_pre, conv_w, conv_b, t, nctx):
    bsz, tall, c = qk_pre.shape
    ch = 128
    body = functools.partial(_conv_body, t=t, nctx=nctx, scale_from=(c // 2) // ch)
    return pl.pallas_call(
        body,
        grid=(bsz, c // ch),
        in_specs=[pl.BlockSpec((1, tall, ch), lambda b, j: (b, 0, j)),
                  pl.BlockSpec((3, 3, ch), lambda b, j: (0, 0, j)),
                  pl.BlockSpec((1, ch), lambda b, j: (0, j))],
        out_specs=pl.BlockSpec((1, tall, ch), lambda b, j: (b, 0, j)),
        out_shape=jax.ShapeDtypeStruct((bsz, tall, c), BF16),
        scratch_shapes=[pltpu.VMEM((t + 2 * CONV_HALO, ch), F32),
                        pltpu.VMEM((nctx + 2 * SUBLANES, ch), F32)],
        compiler_params=_params(("arbitrary", "arbitrary")),
        name="qk_conv",
    )(qk_pre, conv_w, conv_b)


def _fwd_blk(s, nx):
    return jnp.where(s == 0, nx, s - 1)


def _bwd_blk(s, nx):
    return jnp.where(s == 0, nx, nx - s)


def _hg_constants(rev):
    c = HG_CHUNK
    i = np.arange(c)[:, None]
    j = np.arange(c)[None, :]
    blocks = [(j >= i) if rev else (j <= i)]
    masks = [i == j]
    m = c // 2
    while m >= 1:
        b0 = (i // (2 * m)) * (2 * m)
        same = (i // (2 * m)) == (j // (2 * m))
        if rev:
            beta = b0 + m
            qrow = (i % (2 * m)) < m
            g = np.where(qrow, (j >= i) & (j < beta), (j >= beta) & (j < i))
            mask = same & qrow & ((j % (2 * m)) >= m)
        else:
            beta = b0 + m - 1
            qrow = (i % (2 * m)) >= m
            g = np.where(qrow, (j > beta) & (j <= i), (j > i) & (j <= beta))
            mask = same & qrow & ((j % (2 * m)) < m)
        if m < HG_VPU_LEVEL_MIN:
            blocks.append(g)
        masks.append(mask)
        m //= 2
    g = np.concatenate(blocks, axis=0).astype(np.float32)
    g3 = np.concatenate([g, g, g], axis=1)
    m2 = np.concatenate([np.stack(masks), np.stack(masks)], axis=2)
    return (jnp.asarray(g3, BF16), jnp.asarray(m2, F32))


def _hg_level_decay(a, m, rev):
    c = a.shape[0]
    parts = []
    for b0 in range(0, c, 2 * m):
        beta = b0 + m if rev else b0 + m - 1
        ref = a[beta:beta + 1]
        if m % 8 == 0:
            first, second = a[b0:b0 + m], a[b0 + m:b0 + 2 * m]
            parts += [first - ref, ref - second] if rev else [ref - first, second - ref]
        else:
            d = a[b0:b0 + 2 * m] - ref
            parts.append(jnp.minimum(d, -d))
    return jnp.concatenate(parts, axis=0)


def _block_diag(x, zero):
    w = x.shape[1] // 2
    return jnp.concatenate([jnp.concatenate([x[:, :w], zero], axis=1),
                            jnp.concatenate([zero, x[:, w:]], axis=1)], axis=0)


def _hg_chunk(dirs):
    c = HG_CHUNK
    w = 2 * HEAD_DIM
    zero = jnp.zeros((c, HEAD_DIM), BF16)
    units = []
    for rev, r0, bb, q_ref, k_ref, v_ref, lf_ref, g_ref, msk_ref, o_ref, st_ref in dirs:
        rows = pl.ds(r0, c)
        lf = lf_ref[bb, rows, :] * LOG2_E
        p1 = lf.astype(BF16)
        r1 = lf - p1.astype(F32)
        p2 = r1.astype(BF16)
        p3 = (r1 - p2.astype(F32)).astype(BF16)
        dall = _dot(g_ref[...], jnp.concatenate([p1, p2, p3], axis=0))
        for hp in range(N_HEADS // 2):
            cs = slice(hp * w, (hp + 1) * w)
            units.append(dict(rev=rev, rows=rows, cs=cs, hp=hp, bb=bb, msk_ref=msk_ref, o_ref=o_ref, st_ref=st_ref,
                              q=q_ref[bb, rows, cs], k=k_ref[bb, rows, cs], v=v_ref[bb, rows, cs], dall=dall[:, cs]))
    for u in units:
        msk_ref = u["msk_ref"]
        nlev = msk_ref.shape[0] - 1
        att = _dot(u["q"], _block_diag(u["k"], zero), _NT) * msk_ref[0]
        a = u["dall"][0:c]
        row = 1
        for l in range(nlev):
            m = c >> (l + 1)
            if m >= HG_VPU_LEVEL_MIN:
                dec = _hg_level_decay(a, m, u["rev"])
            else:
                dec = u["dall"][row * c:(row + 1) * c]
                row += 1
            e = jnp.exp2(dec).astype(BF16)
            att = att + _dot(u["q"] * e, _block_diag(u["k"] * e, zero), _NT) * msk_ref[l + 1]
        u["att"] = att.astype(BF16)
    for u in units:
        a = u["dall"][0:c]
        u["a_tot"] = a[0:1] if u["rev"] else a[c - 1:c]
        u["st"] = [u["st_ref"][2 * u["hp"] + i] for i in range(2)]
        zf = jnp.zeros((HEAD_DIM, HEAD_DIM), BF16)
        st2 = jnp.concatenate([jnp.concatenate([u["st"][0].astype(BF16), zf], axis=1),
                               jnp.concatenate([zf, u["st"][1].astype(BF16)], axis=1)], axis=0)
        qbar = (u["q"].astype(F32) * jnp.exp2(a)).astype(BF16)
        u["o"] = _dot(u["att"], _block_diag(u["v"], zero)) + _dot(qbar, st2, _NT)
        u["khat"] = (u["k"].astype(F32) * jnp.exp2(u["a_tot"] - a)).astype(BF16)
    for u in units:
        u["o_ref"][u["bb"], u["rows"], u["cs"]] = u["o"].astype(BF16)
        for i in range(2):
            hs = slice(i * HEAD_DIM, (i + 1) * HEAD_DIM)
            upd = _dot(u["v"][:, hs], u["khat"][:, hs], _TN)
            u["st_ref"][2 * u["hp"] + i] = u["st"][i] * jnp.exp2(u["a_tot"][:, hs]) + upd


def _hg_body(qf_ref, kf_ref, vf_ref, lff_ref, qb_ref, kb_ref, vb_ref, lfb_ref,
             gf_ref, mf_ref, gb_ref, mb_ref, of_ref, ob_ref, st_ref):
    @pl.when(pl.program_id(1) == 0)
    def _():
        st_ref[...] = jnp.zeros(st_ref.shape, F32)

    nsub = SCAN_STEP // HG_CHUNK
    for c in range(nsub):
        dirs = []
        for bb in range(qf_ref.shape[0]):
            dirs.append((False, c * HG_CHUNK, bb, qf_ref, kf_ref, vf_ref, lff_ref, gf_ref, mf_ref, of_ref,
                         st_ref.at[0, bb]))
            dirs.append((True, (nsub - 1 - c) * HG_CHUNK, bb, qb_ref, kb_ref, vb_ref, lfb_ref, gb_ref, mb_ref, ob_ref,
                         st_ref.at[1, bb]))
        _hg_chunk(dirs)


def _hgrn2(hgq, hgk, hgv, hglf, nx):
    bsz, tall, kw = hgq.shape
    steps = tall // SCAN_STEP
    cf = _hg_constants(False)
    cb = _hg_constants(True)
    nb = HG_SAMPLES if bsz % HG_SAMPLES == 0 else 1
    blk = (nb, SCAN_STEP, kw)
    fwd = lambda col: (lambda b, s: (b, _fwd_blk(s, nx), col))
    bwd = lambda col: (lambda b, s: (b, _bwd_blk(s, nx), col))
    const = lambda a: pl.BlockSpec(a.shape, lambda b, s: (0,) * a.ndim)
    in_specs = [pl.BlockSpec(blk, fwd(0)), pl.BlockSpec(blk, fwd(0)), pl.BlockSpec(blk, fwd(0)), pl.BlockSpec(blk, fwd(0)),
                pl.BlockSpec(blk, bwd(0)), pl.BlockSpec(blk, bwd(1)), pl.BlockSpec(blk, bwd(0)), pl.BlockSpec(blk, bwd(1))]
    in_specs += [const(a) for a in cf + cb]
    out_sds = jax.ShapeDtypeStruct((bsz, tall, kw), BF16)
    return pl.pallas_call(
        _hg_body,
        grid=(bsz // nb, steps),
        in_specs=in_specs,
        out_specs=[pl.BlockSpec(blk, fwd(0)), pl.BlockSpec(blk, bwd(0))],
        out_shape=[out_sds, out_sds],
        scratch_shapes=[pltpu.VMEM((2, nb, N_HEADS, HEAD_DIM, HEAD_DIM), F32)],
        compiler_params=_params(("arbitrary", "arbitrary")),
        name="hgrn2_scan",
    )(hgq, hgk, hgv, hglf, hgq, hgk, hgv, hglf, *cf, *cb)


def _ml_constants(rev):
    k = np.arange(ML_CHUNK)
    tri = (k[:, None] >= k[None, :]) if rev else (k[:, None] <= k[None, :])
    return jnp.asarray(np.concatenate([tri, tri, tri], axis=0), BF16)


def _ml_chunk(dirs):
    c = ML_CHUNK
    ii = lax.broadcasted_iota(I32, (c, c), 0)
    jj = lax.broadcasted_iota(I32, (c, c), 1)
    ones = jnp.ones((c, HEAD_DIM), BF16)
    units = []
    for rev, r0, d, bb, q_ref, k_ref, v_ref, g_ref, tri3_ref, o_ref, st_ref, m_ref in dirs:
        gates = g_ref[bb, :, pl.ds(r0, c)] * LOG2_E
        p1 = gates.astype(BF16)
        r1 = gates - p1.astype(F32)
        p2 = r1.astype(BF16)
        p3 = (r1 - p2.astype(F32)).astype(BF16)
        csum = _dot(jnp.concatenate([p1, p2, p3], axis=1), tri3_ref[...])
        for h in range(N_HEADS):
            cs = slice(h * HEAD_DIM, (h + 1) * HEAD_DIM)
            u = dict(rev=rev, o_ref=o_ref, bb=bb, rows=pl.ds(r0, c), cs=cs, st_ref=st_ref, m_ref=m_ref, h=h)
            u["qb"] = q_ref[bb, pl.ds(r0, c), cs]
            u["kb"] = k_ref[bb, pl.ds(r0, c), cs]
            u["v1"] = jnp.concatenate([v_ref[bb, pl.ds(r0, c), cs], ones], axis=1)
            gi = d * GATE_ROWS + h
            irow = gates[gi:gi + 1]
            u["brow"] = csum[gi + N_HEADS:gi + N_HEADS + 1]
            u["rrow"] = irow - u["brow"]
            units.append(u)
    for u in units:
        u["st"] = u["st_ref"][u["h"]]
        a = _dot(jnp.concatenate([u["kb"], u["st"].astype(BF16)], axis=0), u["qb"], _NT)
        u["s"] = a[:c]
        u["sq"] = a[c:]
    for u in units:
        last = 0 if u["rev"] else c - 1
        tri_t = (ii >= jj) if u["rev"] else (ii <= jj)
        rcol = jnp.concatenate([u["rrow"], jnp.zeros((SUBLANES - 1, c), F32)], axis=0).T[:, 0:1]
        u["mprev"] = u["m_ref"][u["h"]][:, 0:1]
        rmat = jnp.where(tri_t, rcol, -jnp.inf)
        u["grow"] = jnp.maximum(jnp.max(rmat, axis=0, keepdims=True), u["mprev"])
        qk = (u["s"] * jnp.exp2(rmat - u["grow"])).astype(BF16)
        blast = u["brow"][:, last:last + 1]
        u["mnew"] = blast + u["grow"][:, last:last + 1]
        kh = (u["kb"].astype(F32) * jnp.exp2(blast + rcol - u["mnew"])).astype(BF16)
        u["ws"] = jnp.exp2(blast + u["mprev"] - u["mnew"])
        u["qkh"] = jnp.concatenate([qk, kh], axis=1)
    for u in units:
        u["nu"] = _dot(u["v1"], u["qkh"], _TN)
    for u in units:
        both = u["nu"][:, :c] + jnp.exp2(u["mprev"] - u["grow"]) * u["sq"]
        den = both[HEAD_DIM:HEAD_DIM + 1]
        inv = 1.0 / jnp.maximum(jnp.abs(den), jnp.exp2(-(u["brow"] + u["grow"])))
        u["o_ref"][u["bb"], u["rows"], u["cs"]] = (both[:HEAD_DIM] * inv).T.astype(BF16)
        u["st_ref"][u["h"]] = u["ws"] * u["st"] + u["nu"][:, c:]
        u["m_ref"][u["h"]] = jnp.broadcast_to(u["mnew"], (1, HEAD_DIM))


def _ml_body(qf_ref, kf_ref, vf_ref, gf_ref, qb_ref, kb_ref, vb_ref, gb_ref, tf_ref, tb_ref,
             of_ref, ob_ref, st_ref, m_ref):
    @pl.when(pl.program_id(1) == 0)
    def _():
        st_ref[...] = jnp.zeros(st_ref.shape, F32)
        m_ref[...] = jnp.zeros(m_ref.shape, F32)

    nsub = SCAN_STEP // ML_CHUNK
    for c in range(nsub):
        dirs = []
        for bb in range(qf_ref.shape[0]):
            dirs.append((False, c * ML_CHUNK, 0, bb, qf_ref, kf_ref, vf_ref, gf_ref, tf_ref, of_ref,
                         st_ref.at[0, bb], m_ref.at[0, bb]))
            dirs.append((True, (nsub - 1 - c) * ML_CHUNK, 1, bb, qb_ref, kb_ref, vb_ref, gb_ref, tb_ref, ob_ref,
                         st_ref.at[1, bb], m_ref.at[1, bb]))
        _ml_chunk(dirs)


def _mlstm(mlqk, mlv, gates, nx):
    bsz, tall, kw = mlv.shape
    steps = tall // SCAN_STEP
    nb = ML_SAMPLES if bsz % ML_SAMPLES == 0 else 1
    blk = (nb, SCAN_STEP, kw)
    gblk = (nb, 2 * GATE_ROWS, SCAN_STEP)
    tf, tb = _ml_constants(False), _ml_constants(True)
    fwd = lambda col: (lambda b, s: (b, _fwd_blk(s, nx), col))
    bwd = lambda col: (lambda b, s: (b, _bwd_blk(s, nx), col))
    const = pl.BlockSpec(tf.shape, lambda b, s: (0, 0))
    in_specs = [pl.BlockSpec(blk, fwd(0)), pl.BlockSpec(blk, fwd(1)), pl.BlockSpec(blk, fwd(0)),
                pl.BlockSpec(gblk, lambda b, s: (b, 0, _fwd_blk(s, nx))),
                pl.BlockSpec(blk, bwd(0)), pl.BlockSpec(blk, bwd(1)), pl.BlockSpec(blk, bwd(0)),
                pl.BlockSpec(gblk, lambda b, s: (b, 0, _bwd_blk(s, nx))), const, const]
    out_sds = jax.ShapeDtypeStruct((bsz, tall, kw), BF16)
    return pl.pallas_call(
        _ml_body,
        grid=(bsz // nb, steps),
        in_specs=in_specs,
        out_specs=[pl.BlockSpec(blk, fwd(0)), pl.BlockSpec(blk, bwd(0))],
        out_shape=[out_sds, out_sds],
        scratch_shapes=[pltpu.VMEM((2, nb, N_HEADS, 2 * HEAD_DIM, HEAD_DIM), F32),
                        pltpu.VMEM((2, nb, N_HEADS, 1, HEAD_DIM), F32)],
        compiler_params=_params(("arbitrary", "arbitrary")),
        name="mlstm_scan",
    )(mlqk, mlqk, mlv, gates, mlqk, mlqk, mlv, gates, tf, tb)


def _out_body(hof_ref, hob_ref, mhf_ref, mhb_ref, hgg_ref, mlo_ref, x_ref, g1_ref, sc2_ref, sh2_ref,
              hnw_ref, mnw_ref, wout_ref, n2w_ref, rwt_ref, x1_ref, vt_ref, aff_ref):
    hg = hof_ref[0].astype(F32) + hob_ref[0].astype(F32)
    ml = mhf_ref[0].astype(F32) + mhb_ref[0].astype(F32)
    hparts, mparts = [], []
    for h in range(N_HEADS):
        cs = slice(h * HEAD_DIM, (h + 1) * HEAD_DIM)
        t = hg[:, cs]
        hparts.append(t * lax.rsqrt(jnp.mean(t * t, axis=-1, keepdims=True) + NORM_EPS))
        t = ml[:, cs]
        t = t - jnp.mean(t, axis=-1, keepdims=True)
        mparts.append(t * lax.rsqrt(jnp.mean(t * t, axis=-1, keepdims=True) + NORM_EPS))
    hgn = jnp.concatenate(hparts, axis=-1) * hnw_ref[...] * hgg_ref[0].astype(F32)
    mln = jnp.concatenate(mparts, axis=-1) * mnw_ref[...] * mlo_ref[0].astype(F32)
    mix = jnp.concatenate([hgn, mln], axis=-1).astype(BF16)
    x1 = x_ref[0] + g1_ref[0] * _dot(mix, wout_ref[...])
    x1_ref[0] = x1
    v = x1 * lax.rsqrt(jnp.mean(x1 * x1, axis=-1, keepdims=True) + NORM_EPS) * n2w_ref[...]
    v = v * (1.0 + sc2_ref[0]) + sh2_ref[0]
    vt_ref[0] = _pack_halves(v)
    logits = _dot_split(rwt_ref[...], v, v.astype(BF16))
    ex = jnp.exp(logits - jnp.max(logits, axis=0, keepdims=True))
    aff_ref[0] = ex / jnp.sum(ex, axis=0, keepdims=True)


def _mixer_out(hg_f, hg_b, ml_f, ml_b, hgg, mlo, x, g1, sc2, sh2, hnw, mnw, w_out, n2w, rwt, tm):
    bsz, t, d = x.shape
    kw = HEADS_W
    ne = rwt.shape[0]
    row = lambda b, i: (b, i, 0)
    mod = lambda b, i: (b, 0, 0)
    const2 = lambda b, i: (0, 0)
    act = pl.BlockSpec((1, tm, kw), row)
    in_specs = [act, act, act, act, act, act,
                pl.BlockSpec((1, tm, d), row),
                pl.BlockSpec((1, 1, d), mod), pl.BlockSpec((1, 1, d), mod), pl.BlockSpec((1, 1, d), mod),
                pl.BlockSpec((1, kw), const2), pl.BlockSpec((1, kw), const2),
                pl.BlockSpec(w_out.shape, const2), pl.BlockSpec((1, d), const2), pl.BlockSpec(rwt.shape, const2)]
    return pl.pallas_call(
        _out_body,
        grid=(bsz, t // tm),
        in_specs=in_specs,
        out_specs=[pl.BlockSpec((1, tm, d), row),
                   pl.BlockSpec((1, tm, d // 2), row),
                   pl.BlockSpec((1, ne, tm), lambda b, i: (b, 0, i))],
        out_shape=[jax.ShapeDtypeStruct((bsz, t, d), F32),
                   jax.ShapeDtypeStruct((bsz, t, d // 2), I32),
                   jax.ShapeDtypeStruct((bsz, ne, t), F32)],
        compiler_params=_params(("arbitrary", "arbitrary")),
        name="mixer_out",
    )(hg_f, hg_b, ml_f, ml_b, hgg, mlo, x, g1, sc2, sh2, hnw, mnw, w_out, n2w, rwt)


def _prefix_count(maskf, u_ref, ones_ref, bl_ref):
    e, nb, ln = maskf.shape
    x = maskf.reshape(e * nb, ln)
    xb = x.astype(BF16)
    incl = _dot(xb, u_ref[...])
    tot = _dot(xb, ones_ref[...])
    off = _dot(bl_ref[...], tot.astype(BF16))
    return (incl - x + off).reshape(e, nb, ln), off.reshape(e, nb, ln)


def _topk_body(aff_ref, u_ref, ones_ref, bl_ref, bl1_ref, pos_ref, base_ref, *, cap):
    x = aff_ref[0]
    ne = x.shape[0]

    def count(m):
        return jnp.sum(jnp.sum(jnp.where(m, 1.0, 0.0), axis=1, keepdims=True), axis=2, keepdims=True)

    def halve(_, carry):
        lo, hi = carry
        mid = 0.5 * (lo + hi)
        up = count(x > mid) >= cap
        return jnp.where(up, mid, lo), jnp.where(up, hi, mid)

    lo, hi = lax.fori_loop(0, TOPK_BISECTIONS, halve,
                           (jnp.full((ne, 1, 1), -1.0, F32), jnp.full((ne, 1, 1), 1.0, F32)))
    gt = jnp.where(x > hi, 1.0, 0.0)
    eq = jnp.where(x > lo, 1.0, 0.0) - gt
    need = cap - count(x > hi)
    eq_rank, _ = _prefix_count(eq, u_ref, ones_ref, bl_ref)
    sel = gt + eq * jnp.where(eq_rank < need, 1.0, 0.0)
    pos, _ = _prefix_count(sel, u_ref, ones_ref, bl_ref)
    pos_ref[0] = jnp.where(sel > 0, pos, -1.0).astype(I32)
    nsel = jnp.sum(sel, axis=0)
    nb16 = nsel.astype(BF16)
    incl = _dot(nb16, u_ref[...])
    tot = _dot(nb16, ones_ref[...])
    off = _dot(bl1_ref[...], tot, precision=HIGHEST)
    base_ref[0] = (incl - nsel + off).astype(I32)


def _topk(aff4, cap):
    bsz, ne, nb, ln = aff4.shape
    k = np.arange(ln)
    u = jnp.asarray(k[:, None] <= k[None, :], BF16)
    ones = jnp.ones((ln, ln), BF16)
    r = np.arange(ne * nb)
    bl = jnp.asarray(((r[:, None] // nb) == (r[None, :] // nb)) & (r[None, :] < r[:, None]), BF16)
    r1 = np.arange(nb)
    bl1 = jnp.asarray(r1[None, :] < r1[:, None], F32)
    blk = pl.BlockSpec((1, ne, nb, ln), lambda b: (b, 0, 0, 0))
    const = lambda a: pl.BlockSpec(a.shape, lambda b: (0, 0))
    return pl.pallas_call(
        functools.partial(_topk_body, cap=cap),
        grid=(bsz,),
        in_specs=[blk, const(u), const(ones), const(bl), const(bl1)],
        out_specs=[blk, pl.BlockSpec((1, nb, ln), lambda b: (b, 0, 0))],
        out_shape=[jax.ShapeDtypeStruct((bsz, ne, nb, ln), I32), jax.ShapeDtypeStruct((bsz, nb, ln), I32)],
        compiler_params=_params(("arbitrary",)),
        name="expert_topk",
    )(aff4, u, ones, bl, bl1)


def _sc_setup():
    info = plsc.get_sparse_core_info()
    mesh = plsc.VectorSubcoreMesh(core_axis_name="c", subcore_axis_name="s")
    params = dataclasses.replace(pltpu.CompilerParams(), needs_layout_passes=False)
    return info.num_cores, info.num_cores * info.num_subcores, info.num_lanes, mesh, params


def _sc_worker_id(nc):
    return lax.axis_index("s") * nc + lax.axis_index("c")


def _sc_gather_scratch(width, dtype):
    one = [pltpu.VMEM((SC_GATHER_ROWS,), I32), pltpu.VMEM((SC_GATHER_ROWS, width), dtype), pltpu.SemaphoreType.DMA]
    return one + one


def _sc_gather_chunks(table_hbm, out_hbm, idx_v, bufs, out_row0, n, lanes):
    chunk = SC_GATHER_ROWS

    def gather(c, buf):
        ich_v, rows_v, sem = buf
        for q in range(chunk // lanes):
            src = pl.ds(pl.multiple_of(c * chunk + q * lanes, lanes), lanes)
            ich_v[pl.ds(q * lanes, lanes)] = idx_v[src]
        return pltpu.make_async_copy(table_hbm.at[ich_v], rows_v, sem)

    def finish(c, buf):
        ich_v, rows_v, sem = buf
        pltpu.make_async_copy(table_hbm.at[ich_v], rows_v, sem).wait()
        pltpu.sync_copy(rows_v, out_hbm.at[pl.ds(pl.multiple_of(out_row0 + c * chunk, chunk), chunk)])

    npair = n // (2 * chunk)
    gather(0, bufs[0]).start()

    @pl.loop(0, npair)
    def _(i):
        gather(2 * i + 1, bufs[1]).start()
        finish(2 * i, bufs[0])

        @pl.when(i + 1 < npair)
        def _():
            gather(2 * i + 2, bufs[0]).start()

        finish(2 * i + 1, bufs[1])


def _sc_invert_gather(table, pos_flat, aff_flat, npairs, ne, t, cap):
    nc, nw, lanes, mesh, params = _sc_setup()
    split = max(1, nw // npairs)
    seg = cap // split
    per_w = npairs * split // nw
    width = table.shape[1]

    @functools.partial(
        pl.kernel, mesh=mesh, compiler_params=params,
        out_type=[jax.ShapeDtypeStruct((npairs * cap, width), table.dtype),
                  jax.ShapeDtypeStruct((npairs * cap,), F32)],
        scratch_types=[pltpu.VMEM((t,), I32), pltpu.VMEM((t,), F32), pltpu.VMEM((seg,), I32), pltpu.VMEM((seg,), F32)]
        + _sc_gather_scratch(width, table.dtype),
    )
    def body(table_hbm, pos_hbm, aff_hbm, out_hbm, gate_hbm, pos_v, aff_v, idx_v, gate_v, *g):
        wid = _sc_worker_id(nc)

        @pl.loop(0, per_w)
        def _(kk):
            item = wid * per_w + kk
            p = item // split
            lo = (item % split) * seg
            tok0 = (p // ne) * t
            pltpu.sync_copy(pos_hbm.at[pl.ds(pl.multiple_of(p * t, t), t)], pos_v)
            pltpu.sync_copy(aff_hbm.at[pl.ds(pl.multiple_of(p * t, t), t)], aff_v)

            @pl.loop(0, t // lanes)
            def _(i):
                v = pos_v[pl.ds(pl.multiple_of(i * lanes, lanes), lanes)] - lo
                tok = lax.iota(I32, lanes) + i * lanes
                plsc.store_scatter(idx_v, [v], tok, mask=(v >= 0) & (v < seg))

            @pl.loop(0, seg // lanes)
            def _(j):
                sl = pl.ds(pl.multiple_of(j * lanes, lanes), lanes)
                ii = idx_v[sl]
                gate_v[sl] = plsc.load_gather(aff_v, [ii])
                idx_v[sl] = ii + tok0

            row0 = p * cap + lo
            pltpu.sync_copy(gate_v, gate_hbm.at[pl.ds(pl.multiple_of(row0, seg), seg)])
            _sc_gather_chunks(table_hbm, out_hbm, idx_v, (g[0:3], g[3:6]), row0, seg, lanes)

    return body(table, pos_flat, aff_flat)


def _sc_assignment_order(pos_flat, base_flat, bsz, ne, t, cap):
    nc, nw, lanes, mesh, params = _sc_setup()
    na = ne * cap
    per_b = nw // bsz
    rng = na // per_b

    @functools.partial(
        pl.kernel, mesh=mesh, compiler_params=params,
        out_type=[jax.ShapeDtypeStruct((bsz * na,), I32), jax.ShapeDtypeStruct((bsz * na,), I32)],
        scratch_types=[pltpu.VMEM((t,), I32), pltpu.VMEM((t,), I32), pltpu.VMEM((t,), I32),
                       pltpu.VMEM((rng,), I32), pltpu.VMEM((rng,), I32)],
    )
    def body(pos_hbm, base_hbm, perm_hbm, tok_hbm, pos_v, base_v, rank_v, perm_v, tok_v):
        wid = _sc_worker_id(nc)
        b = wid // per_b
        a0 = (wid % per_b) * rng
        pltpu.sync_copy(base_hbm.at[pl.ds(pl.multiple_of(b * t, t), t)], base_v)

        @pl.loop(0, t // lanes)
        def _(i):
            rank_v[pl.ds(pl.multiple_of(i * lanes, lanes), lanes)] = jnp.zeros((lanes,), I32)

        @pl.loop(0, ne)
        def _(e):
            p = b * ne + e
            pltpu.sync_copy(pos_hbm.at[pl.ds(pl.multiple_of(p * t, t), t)], pos_v)

            @pl.loop(0, t // lanes)
            def _(i):
                sl = pl.ds(pl.multiple_of(i * lanes, lanes), lanes)
                v = pos_v[sl]
                r = rank_v[sl]
                a = base_v[sl] + r - a0
                sel = v >= 0
                mine = sel & (a >= 0) & (a < rng)
                plsc.store_scatter(perm_v, [a], v + p * cap, mask=mine)
                plsc.store_scatter(tok_v, [a], lax.iota(I32, lanes) + i * lanes, mask=mine)
                rank_v[sl] = r + jnp.where(sel, 1, 0)

        dst = pl.ds(pl.multiple_of(b * na + a0, rng), rng)
        pltpu.sync_copy(perm_v, perm_hbm.at[dst])
        pltpu.sync_copy(tok_v, tok_hbm.at[dst])

    return body(pos_flat, base_flat)


def _sc_gather_rows(table, idx):
    nc, nw, lanes, mesh, params = _sc_setup()
    n = idx.shape[0]
    per_w = n // nw
    width = table.shape[1]
    chunk = SC_GATHER_ROWS

    @functools.partial(
        pl.kernel, mesh=mesh, compiler_params=params,
        out_type=jax.ShapeDtypeStruct((n, width), table.dtype),
        scratch_types=[pltpu.VMEM((per_w,), I32)] + _sc_gather_scratch(width, table.dtype),
    )
    def body(table_hbm, idx_hbm, out_hbm, idx_v, *g):
        row0 = _sc_worker_id(nc) * per_w
        pltpu.sync_copy(idx_hbm.at[pl.ds(pl.multiple_of(row0, per_w), per_w)], idx_v)
        _sc_gather_chunks(table_hbm, out_hbm, idx_v, (g[0:3], g[3:6]), row0, per_w, lanes)

    return body(table, idx)


def _moe1_body(xs_ref, gate_ref, wg_ref, wu_ref, wd_ref, y_ref):
    xs = _unpack_halves(xs_ref[0])
    hg = _dot(xs, wg_ref[0].astype(BF16))
    hu = _dot(xs, wu_ref[0].astype(BF16))
    h = (_silu(hg) * hu).astype(BF16)
    y = _dot(h, wd_ref[0].astype(BF16))
    gt = gate_ref[0].T
    y = jnp.concatenate([y[k * LANES:(k + 1) * LANES] * gt[:, k:k + 1] for k in range(gt.shape[1])], axis=0)
    y_ref[0] = _pack_halves(y)


def _moe_experts(xs, gate, wg, wu, wd, bsz):
    npairs, cap, half = xs.shape
    ne, d, f = wg.shape
    pair = lambda b, e: (b * ne + e, 0, 0)
    expert = lambda b, e: (e, 0, 0)
    return pl.pallas_call(
        _moe1_body,
        grid=(bsz, ne),
        in_specs=[pl.BlockSpec((1, cap, half), pair),
                  pl.BlockSpec((1, cap // LANES, LANES), pair),
                  pl.BlockSpec((1, d, f), expert),
                  pl.BlockSpec((1, d, f), expert),
                  pl.BlockSpec((1, f, d), expert)],
        out_specs=pl.BlockSpec((1, cap, half), pair),
        out_shape=jax.ShapeDtypeStruct((npairs, cap, half), I32),
        compiler_params=_params(("arbitrary", "arbitrary")),
        name="moe_experts",
    )(xs, gate, wg, wu, wd)


def _combine_body(abase_ref, ys_hbm, tok_ref, x1_ref, g2_ref, fw_ref, o_ref, buf_ref, acc_ref, sem_ref, cnt_ref, *, na):
    n = ONEHOT_BLK
    ring = buf_ref.shape[0]
    nch = na // n
    b, i = pl.program_id(0), pl.program_id(1)

    @pl.when(i == 0)
    def _():
        cnt_ref[0] = 0
        cnt_ref[1] = 0

    lo = abase_ref[b, i]
    hi = abase_ref[b, i + 1]
    c0 = lo // n
    c1 = jnp.where(hi > lo, (hi + n - 1) // n, c0)

    def copy(c):
        slot = c % ring
        row0 = pl.multiple_of(b * na + c * n, n)
        return pltpu.make_async_copy(ys_hbm.at[pl.ds(row0, n)], buf_ref.at[slot], sem_ref.at[slot])

    acc_ref[...] = jnp.zeros(acc_ref.shape, F32)
    tok_ids = lax.broadcasted_iota(I32, (n, n), 0) + i * n

    def step(c, carry):
        started = cnt_ref[0]
        ahead = jnp.minimum(c + ring, nch)

        def start(k, _):
            copy(k).start()
            return 0

        lax.fori_loop(started, ahead, start, 0)
        cnt_ref[0] = jnp.maximum(started, ahead)

        @pl.when(cnt_ref[1] <= c)
        def _():
            copy(c).wait()
            cnt_ref[1] = c + 1

        seg = jnp.where(tok_ids == tok_ref[0, pl.ds(c, 1), :], 1.0, 0.0).astype(BF16)
        acc_ref[...] += _dot(seg, _unpack_halves(buf_ref[c % ring]))
        return carry

    lax.fori_loop(c0, c1, step, 0)
    x2 = x1_ref[0] + g2_ref[0] * acc_ref[...]
    o_ref[0] = x2 * lax.rsqrt(jnp.mean(x2 * x2, axis=-1, keepdims=True) + NORM_EPS) * fw_ref[...]


def _combine_final(abase, ys, tok, x1, g2, fw, na):
    bsz, t, d = x1.shape
    n = ONEHOT_BLK
    grid_spec = pltpu.PrefetchScalarGridSpec(
        num_scalar_prefetch=1,
        grid=(bsz, t // n),
        in_specs=[pl.BlockSpec(memory_space=pl.ANY),
                  pl.BlockSpec((1, na // n, n), lambda b, i, a: (b, 0, 0)),
                  pl.BlockSpec((1, n, d), lambda b, i, a: (b, i, 0)),
                  pl.BlockSpec((1, 1, d), lambda b, i, a: (b, 0, 0)),
                  pl.BlockSpec((1, d), lambda b, i, a: (0, 0))],
        out_specs=pl.BlockSpec((1, n, d), lambda b, i, a: (b, i, 0)),
        scratch_shapes=[pltpu.VMEM((COMBINE_RING, n, d // 2), I32), pltpu.VMEM((n, d), F32),
                        pltpu.SemaphoreType.DMA((COMBINE_RING,)), pltpu.SMEM((2,), I32)],
    )
    return pl.pallas_call(
        functools.partial(_combine_body, na=na),
        grid_spec=grid_spec,
        out_shape=jax.ShapeDtypeStruct((bsz, t, d), F32),
        compiler_params=_params(("arbitrary", "arbitrary")),
        name="combine_final",
    )(abase, ys, tok, x1, g2, fw)


def kernel(x, c, ctx, c_ctx, ada_w, ada_b, norm1_w, w_in, conv_w, conv_b, hg_lb_logits, ml_gate_b,
           hg_norm_w, ml_norm_w, w_out, norm2_w, router_w, exp_w_gate, exp_w_up, exp_w_down, final_norm_w):
    bsz, t, d = x.shape
    nctx = ctx.shape[1]
    assert ada_w.shape[0] == 1, "single-layer block"
    assert nctx == SCAN_STEP and t % SCAN_STEP == 0 and bsz + 1 <= SUBLANES
    tall = t + nctx
    nx = t // SCAN_STEP
    kw = HEADS_W
    ne = router_w.shape[-1]
    cap = EC_CAPACITY * t // ne
    tm = min(512, t)

    rows = jnp.concatenate([c, c_ctx[None], jnp.zeros((SUBLANES - 1 - bsz, d), F32)], axis=0)
    mod = _modulation(rows, ada_w[0], ada_b[0][None])
    mx = [m[:, None, :] for m in jnp.split(mod[:bsz], 6, axis=-1)]
    mc = [m[:, None, :] for m in jnp.split(mod[bsz:bsz + 1], 6, axis=-1)]
    sh1, sc1, g1, sh2, sc2, g2 = mx
    csh1, csc1 = mc[0], mc[1]

    main_w = 9 * kw
    w_main = w_in[0][:, :main_w].astype(BF16)
    w_gt = w_in[0][:, main_w:].T
    gate_b = ml_gate_b[0][:, None]
    nw1 = norm1_w[0][None]
    outs = _inproj(x, sc1, sh1, nw1, w_main, w_gt, gate_b, hg_lb_logits, tall, min(2 * tm, t), 0)
    outs = _inproj(ctx, csc1, csh1, nw1, w_main, w_gt, gate_b, hg_lb_logits, tall, nctx, t // nctx, prev=outs)
    hgq, hgv, hgg, hgk, hglf, mlqk_pre, mlv, mlo, gates = outs

    mlqk = _conv(mlqk_pre, conv_w[0], conv_b[0][None], t, nctx)
    hg_f, hg_b = _hgrn2(hgq, hgk, hgv, hglf, nx)
    ml_f, ml_b = _mlstm(mlqk, mlv, gates, nx)

    x1, vpk, aff = _mixer_out(hg_f, hg_b, ml_f, ml_b, hgg, mlo, x, g1, sc2, sh2,
                             hg_norm_w[0][None], ml_norm_w[0][None], w_out[0].astype(BF16),
                             norm2_w[0][None], router_w[0].T, min(2 * tm, t))

    pos4, base = _topk(aff.reshape(bsz, ne, t // LANES, LANES), cap)
    na = ne * cap
    pos_flat = pos4.reshape(-1)
    xs, gate = _sc_invert_gather(vpk.reshape(bsz * t, d // 2), pos_flat, aff.reshape(-1), bsz * ne, ne, t, cap)
    perm, tok = _sc_assignment_order(pos_flat, base.reshape(-1), bsz, ne, t, cap)
    y = _moe_experts(xs.reshape(bsz * ne, cap, d // 2), gate.reshape(bsz * ne, cap // LANES, LANES),
                     exp_w_gate[0], exp_w_up[0], exp_w_down[0], bsz)
    ys = _sc_gather_rows(y.reshape(bsz * na, d // 2), perm)
    abase = jnp.concatenate([base.reshape(bsz, t)[:, ::ONEHOT_BLK], jnp.full((bsz, 1), na, I32)], axis=1)
    return _combine_final(abase, ys, tok.reshape(bsz, na // ONEHOT_BLK, ONEHOT_BLK), x1, g2, final_norm_w[None], na)
```

```python
import dataclasses
import functools

import numpy as np
import jax
import jax.numpy as jnp
from jax import lax
from jax.experimental import pallas as pl
from jax.experimental.pallas import tpu as pltpu
from jax.experimental.pallas import tpu_sc as plsc

F32 = jnp.float32
BF16 = jnp.bfloat16
I32 = jnp.int32
HIGHEST = lax.Precision.HIGHEST
NORM_EPS = 1e-6
LOG2_E = 1.4426950408889634

LANES = 128
SUBLANES = 8
HEAD_DIM = 128
N_HEADS = 4
HEADS_W = N_HEADS * HEAD_DIM
GATE_ROWS = 2 * N_HEADS
GRID_W = 64
MOD_COL_TILES = 4
EC_CAPACITY = 2
HG_CHUNK = 64
HG_VPU_LEVEL_MIN = 4
ML_CHUNK = 256
SCAN_STEP = 256
HG_SAMPLES = 2
ML_SAMPLES = 4
ONEHOT_BLK = 256
CONV_HALO = 72
CONV_ROWS = 512
SC_GATHER_ROWS = 64
COMBINE_RING = 6
COMBINE_BLOCKS = 2
TOPK_BISECTIONS = 64
VMEM_LIMIT = 56 * 1024 * 1024

_NT = (((1,), (1,)), ((), ()))
_TN = (((0,), (0,)), ((), ()))


def _dot(a, b, dims=None, precision=None):
    if dims is None:
        return jnp.dot(a, b, preferred_element_type=F32, precision=precision)
    return lax.dot_general(a, b, dims, preferred_element_type=F32, precision=precision)


def _dot_split(w, x, x1):
    n = w.shape[0]
    x2 = (x - x1.astype(F32)).astype(BF16)
    w1 = w.astype(BF16)
    wr = w - w1.astype(F32)
    w2 = wr.astype(BF16)
    w3 = (wr - w2.astype(F32)).astype(BF16)
    pa = _dot(jnp.concatenate([w1, w2, w3], axis=0), x1, _NT)
    pb = _dot(jnp.concatenate([w1, w2], axis=0), x2, _NT)
    return pa[0:n] + pa[n:2 * n] + pa[2 * n:] + pb[0:n] + pb[n:]


def _sigmoid(x):
    return jax.nn.sigmoid(x)


def _pack_halves(x):
    w = x.shape[-1] // 2
    bits = lax.bitcast_convert_type(x.astype(BF16).astype(F32), I32)
    return lax.shift_right_logical(bits[:, :w], 16) | bits[:, w:]


def _unpack_halves(p):
    lo = lax.bitcast_convert_type(lax.shift_left(p, 16), F32)
    hi = lax.bitcast_convert_type(p & jnp.int32(-65536), F32)
    return jnp.concatenate([lo, hi], axis=-1).astype(BF16)


def _silu(x):
    return x * jax.nn.sigmoid(x)


def _params(sem, flags=None):
    return pltpu.CompilerParams(dimension_semantics=sem, vmem_limit_bytes=VMEM_LIMIT, flags=flags)


def _mod_body(r_ref, w_ref, b_ref, o_ref):
    r = r_ref[...]
    o_ref[...] = _dot(_silu(r), w_ref[...], precision=HIGHEST) + b_ref[...]


def _modulation(rows, w, b):
    d, n = w.shape
    nrows = rows.shape[0]
    tn = n // MOD_COL_TILES
    return pl.pallas_call(
        _mod_body,
        grid=(MOD_COL_TILES,),
        in_specs=[pl.BlockSpec((nrows, d), lambda j: (0, 0)),
                  pl.BlockSpec((d, tn), lambda j: (0, j)),
                  pl.BlockSpec((1, tn), lambda j: (0, j))],
        out_specs=pl.BlockSpec((nrows, tn), lambda j: (0, j)),
        out_shape=jax.ShapeDtypeStruct((nrows, n), F32),
        compiler_params=_params(("arbitrary",)),
        name="modulation",
    )(rows, w, b)


def _log_sigmoid(x):
    return jnp.minimum(x, 0.0) - jnp.log(1.0 + jnp.exp(-jnp.abs(x)))


def _inproj_body(x_ref, sc_ref, sh_ref, nw_ref, w_ref, wg_ref, gb_ref, lbl_ref, *refs):
    hgq_ref, hgv_ref, hgg_ref, hgk_ref, hglf_ref, mlqk_ref, mlv_ref, mlo_ref, gates_ref = refs[-9:]
    kw = HEADS_W
    x = x_ref[0]
    y = x * lax.rsqrt(jnp.mean(x * x, axis=-1, keepdims=True) + NORM_EPS) * nw_ref[...]
    u = y * (1.0 + sc_ref[0]) + sh_ref[0]
    ub = u.astype(BF16)

    def proj(c0, c1):
        return _dot(ub, w_ref[:, c0:c1])

    hgq_ref[0] = _silu(proj(0, kw)).astype(BF16)
    hgv_ref[0] = proj(kw, 2 * kw).astype(BF16)
    hgg_ref[0] = _silu(proj(2 * kw, 3 * kw)).astype(BF16)

    lbl = lbl_ref[...]
    mx = jnp.max(lbl, axis=0)
    ex = jnp.exp(lbl - mx[None])
    lb = ex[0] / jnp.sum(ex, axis=0)
    for d in range(2):
        p = proj((3 + d) * kw, (4 + d) * kw)
        lbd = lb[d:d + 1]
        f = lbd + (1.0 - lbd) * _sigmoid(p)
        hgk_ref[0, :, d * kw:(d + 1) * kw] = (1.0 - f).astype(BF16)
        hglf_ref[0, :, d * kw:(d + 1) * kw] = jnp.log(f)

    mlqk_ref[0, :, 0:kw] = proj(5 * kw, 6 * kw)
    mlqk_ref[0, :, kw:2 * kw] = proj(6 * kw, 7 * kw)
    mlv_ref[0] = proj(7 * kw, 8 * kw).astype(BF16)
    mlo_ref[0] = _sigmoid(proj(8 * kw, 9 * kw)).astype(BF16)

    g = _dot_split(wg_ref[...], u, ub) + gb_ref[...]
    row = lax.broadcasted_iota(I32, g.shape, 0)
    gates_ref[0] = jnp.where((row % GATE_ROWS) >= N_HEADS, _log_sigmoid(g), g)


def _inproj_shapes(bsz, tall):
    kw = HEADS_W
    return [
        jax.ShapeDtypeStruct((bsz, tall, kw), BF16),
        jax.ShapeDtypeStruct((bsz, tall, kw), BF16),
        jax.ShapeDtypeStruct((bsz, tall, kw), BF16),
        jax.ShapeDtypeStruct((bsz, tall, 2 * kw), BF16),
        jax.ShapeDtypeStruct((bsz, tall, 2 * kw), F32),
        jax.ShapeDtypeStruct((bsz, tall, 2 * kw), F32),
        jax.ShapeDtypeStruct((bsz, tall, kw), BF16),
        jax.ShapeDtypeStruct((bsz, tall, kw), BF16),
        jax.ShapeDtypeStruct((bsz, 2 * GATE_ROWS, tall), F32),
    ]


def _inproj(tokens, scale, shift, nw, w_main, w_gt, gate_b, lb_logits, tall, tm, blk0, prev=None):
    bsz, n, d = tokens.shape
    kw = HEADS_W
    nt = n // tm
    per_sample = scale.shape[0] == bsz
    mod_map = (lambda b, i: (b, 0, 0)) if per_sample else (lambda b, i: (0, 0, 0))
    const2 = lambda b, i: (0, 0)
    in_specs = [
        pl.BlockSpec((1, tm, d), lambda b, i: (b, i, 0)),
        pl.BlockSpec((1, 1, d), mod_map),
        pl.BlockSpec((1, 1, d), mod_map),
        pl.BlockSpec((1, d), const2),
        pl.BlockSpec(w_main.shape, const2, pipeline_mode=pl.Buffered(1)),
        pl.BlockSpec(w_gt.shape, const2),
        pl.BlockSpec(gate_b.shape, const2),
        pl.BlockSpec(lb_logits.shape, lambda b, i: (0, 0, 0)),
    ]
    args = [tokens, scale, shift, nw, w_main, w_gt, gate_b, lb_logits]
    aliases = {}
    if prev is not None:
        for k, a in enumerate(prev):
            in_specs.append(pl.BlockSpec(memory_space=pl.ANY))
            aliases[len(args)] = k
            args.append(a)
    row_map = lambda b, i: (b, blk0 + i, 0)
    widths = [kw, kw, kw, 2 * kw, 2 * kw, 2 * kw, kw, kw]
    out_specs = [pl.BlockSpec((1, tm, w), row_map) for w in widths]
    out_specs.append(pl.BlockSpec((1, 2 * GATE_ROWS, tm), lambda b, i: (b, 0, blk0 + i)))
    return pl.pallas_call(
        _inproj_body,
        grid=(bsz, nt),
        in_specs=in_specs,
        out_specs=out_specs,
        out_shape=_inproj_shapes(bsz, tall),
        input_output_aliases=aliases,
        compiler_params=_params(("arbitrary", "arbitrary")),
        name="inproj_ctx" if prev is not None else "inproj_x",
    )(*args)


def _conv_body(x_ref, w_ref, b_ref, o_ref, pad_ref, cpad_ref, *, t, nctx, scale_from):
    halo = CONV_HALO
    rows = CONV_ROWS
    win = rows + 2 * halo
    ch = x_ref.shape[-1]
    scale = jnp.where(pl.program_id(1) >= scale_from, HEAD_DIM ** -0.5, 1.0).astype(F32)
    w = w_ref[...]
    bias = b_ref[...]

    pad_ref[0:halo, :] = jnp.zeros((halo, ch), F32)
    pad_ref[halo + t:halo + t + halo, :] = jnp.zeros((halo, ch), F32)
    pad_ref[halo:halo + t, :] = x_ref[0, 0:t, :]
    col = (lax.broadcasted_iota(I32, (win, ch), 0) + (GRID_W - halo % GRID_W)) % GRID_W
    left_ok = col > 0
    right_ok = col < GRID_W - 1

    def chunk(c, carry):
        o = pl.multiple_of(c * rows, rows)
        xw = pad_ref[pl.ds(o, win), :]
        xm = jnp.where(left_ok, pltpu.roll(xw, 1, 0), 0.0)
        xp = jnp.where(right_ok, pltpu.roll(xw, win - 1, 0), 0.0)
        def taps(dr, lo):
            sl = slice(lo, lo + rows)
            return xm[sl] * w[dr, 0:1] + xw[sl] * w[dr, 1:2] + xp[sl] * w[dr, 2:3]

        y = taps(1, halo) + taps(0, halo - GRID_W) + taps(2, halo + GRID_W)
        o_ref[0, pl.ds(o, rows), :] = (_silu(y + bias) * scale).astype(o_ref.dtype)
        return carry

    lax.fori_loop(0, t // rows, chunk, 0)

    pad = SUBLANES
    cpad_ref[0:pad, :] = jnp.zeros((pad, ch), F32)
    cpad_ref[pad + nctx:2 * pad + nctx, :] = jnp.zeros((pad, ch), F32)
    cpad_ref[pad:pad + nctx, :] = x_ref[0, t:t + nctx, :]
    xw = cpad_ref[...]
    n = nctx + 2 * pad
    y = (pltpu.roll(xw, 1, 0) * w[1, 0:1] + xw * w[1, 1:2] + pltpu.roll(xw, n - 1, 0) * w[1, 2:3])[pad:pad + nctx]
    o_ref[0, t:t + nctx, :] = (_silu(y + bias) * scale).astype(o_ref.dtype)


def _conv(qk_pre, conv_w, conv_b, t, nctx):
    bsz, tall, c = qk_pre.shape
    ch = 128
    body = functools.partial(_conv_body, t=t, nctx=nctx, scale_from=(c // 2) // ch)
    return pl.pallas_call(
        body,
        grid=(bsz, c // ch),
        in_specs=[pl.BlockSpec((1, tall, ch), lambda b, j: (b, 0, j)),
                  pl.BlockSpec((3, 3, ch), lambda b, j: (0, 0, j)),
                  pl.BlockSpec((1, ch), lambda b, j: (0, j))],
        out_specs=pl.BlockSpec((1, tall, ch), lambda b, j: (b, 0, j)),
        out_shape=jax.ShapeDtypeStruct((bsz, tall, c), BF16),
        scratch_shapes=[pltpu.VMEM((t + 2 * CONV_HALO, ch), F32),
                        pltpu.VMEM((nctx + 2 * SUBLANES, ch), F32)],
        compiler_params=_params(("arbitrary", "arbitrary")),
        name="qk_conv",
    )(qk_pre, conv_w, conv_b)


def _fwd_blk(s, nx):
    return jnp.where(s == 0, nx, s - 1)


def _bwd_blk(s, nx):
    return jnp.where(s == 0, nx, nx - s)


def _hg_constants(rev):
    c = HG_CHUNK
    i = np.arange(c)[:, None]
    j = np.arange(c)[None, :]
    blocks = [(j >= i) if rev else (j <= i)]
    masks = [i == j]
    m = c // 2
    while m >= 1:
        b0 = (i // (2 * m)) * (2 * m)
        same = (i // (2 * m)) == (j // (2 * m))
        if rev:
            beta = b0 + m
            qrow = (i % (2 * m)) < m
            g = np.where(qrow, (j >= i) & (j < beta), (j >= beta) & (j < i))
            mask = same & qrow & ((j % (2 * m)) >= m)
        else:
            beta = b0 + m - 1
            qrow = (i % (2 * m)) >= m
            g = np.where(qrow, (j > beta) & (j <= i), (j > i) & (j <= beta))
            mask = same & qrow & ((j % (2 * m)) < m)
        if m < HG_VPU_LEVEL_MIN:
            blocks.append(g)
        masks.append(mask)
        m //= 2
    g = np.concatenate(blocks, axis=0).astype(np.float32)
    g3 = np.concatenate([g, g, g], axis=1)
    m2 = np.concatenate([np.stack(masks), np.stack(masks)], axis=2)
    return (jnp.asarray(g3, BF16), jnp.asarray(m2, F32))


def _hg_level_decay(a, m, rev):
    c = a.shape[0]
    parts = []
    for b0 in range(0, c, 2 * m):
        beta = b0 + m if rev else b0 + m - 1
        ref = a[beta:beta + 1]
        if m % 8 == 0:
            first, second = a[b0:b0 + m], a[b0 + m:b0 + 2 * m]
            parts += [first - ref, ref - second] if rev else [ref - first, second - ref]
        else:
            d = a[b0:b0 + 2 * m] - ref
            parts.append(jnp.minimum(d, -d))
    return jnp.concatenate(parts, axis=0)


def _block_diag(x, zero):
    w = x.shape[1] // 2
    return jnp.concatenate([jnp.concatenate([x[:, :w], zero], axis=1),
                            jnp.concatenate([zero, x[:, w:]], axis=1)], axis=0)


def _hg_chunk(dirs):
    c = HG_CHUNK
    w = 2 * HEAD_DIM
    zero = jnp.zeros((c, HEAD_DIM), BF16)
    units = []
    for rev, r0, bb, q_ref, k_ref, v_ref, lf_ref, g_ref, msk_ref, o_ref, st_ref in dirs:
        rows = pl.ds(r0, c)
        lf = lf_ref[bb, rows, :] * LOG2_E
        p1 = lf.astype(BF16)
        r1 = lf - p1.astype(F32)
        p2 = r1.astype(BF16)
        p3 = (r1 - p2.astype(F32)).astype(BF16)
        dall = _dot(g_ref[...], jnp.concatenate([p1, p2, p3], axis=0))
        for hp in range(N_HEADS // 2):
            cs = slice(hp * w, (hp + 1) * w)
            units.append(dict(rev=rev, rows=rows, cs=cs, hp=hp, bb=bb, msk_ref=msk_ref, o_ref=o_ref, st_ref=st_ref,
                              q=q_ref[bb, rows, cs], k=k_ref[bb, rows, cs], v=v_ref[bb, rows, cs], dall=dall[:, cs]))
    for u in units:
        msk_ref = u["msk_ref"]
        nlev = msk_ref.shape[0] - 1
        att = _dot(u["q"], _block_diag(u["k"], zero), _NT) * msk_ref[0]
        a = u["dall"][0:c]
        row = 1
        for l in range(nlev):
            m = c >> (l + 1)
            if m >= HG_VPU_LEVEL_MIN:
                dec = _hg_level_decay(a, m, u["rev"])
            else:
                dec = u["dall"][row * c:(row + 1) * c]
                row += 1
            e = jnp.exp2(dec).astype(BF16)
            att = att + _dot(u["q"] * e, _block_diag(u["k"] * e, zero), _NT) * msk_ref[l + 1]
        u["att"] = att.astype(BF16)
    for u in units:
        a = u["dall"][0:c]
        u["a_tot"] = a[0:1] if u["rev"] else a[c - 1:c]
        u["st"] = [u["st_ref"][2 * u["hp"] + i] for i in range(2)]
        zf = jnp.zeros((HEAD_DIM, HEAD_DIM), BF16)
        st2 = jnp.concatenate([jnp.concatenate([u["st"][0].astype(BF16), zf], axis=1),
                               jnp.concatenate([zf, u["st"][1].astype(BF16)], axis=1)], axis=0)
        qbar = (u["q"].astype(F32) * jnp.exp2(a)).astype(BF16)
        u["o"] = _dot(u["att"], _block_diag(u["v"], zero)) + _dot(qbar, st2, _NT)
        u["khat"] = (u["k"].astype(F32) * jnp.exp2(u["a_tot"] - a)).astype(BF16)
    for u in units:
        u["o_ref"][u["bb"], u["rows"], u["cs"]] = u["o"].astype(BF16)
        for i in range(2):
            hs = slice(i * HEAD_DIM, (i + 1) * HEAD_DIM)
            upd = _dot(u["v"][:, hs], u["khat"][:, hs], _TN)
            u["st_ref"][2 * u["hp"] + i] = u["st"][i] * jnp.exp2(u["a_tot"][:, hs]) + upd


def _hg_body(qf_ref, kf_ref, vf_ref, lff_ref, qb_ref, kb_ref, vb_ref, lfb_ref,
             gf_ref, mf_ref, gb_ref, mb_ref, of_ref, ob_ref, st_ref):
    @pl.when(pl.program_id(1) == 0)
    def _():
        st_ref[...] = jnp.zeros(st_ref.shape, F32)

    nsub = SCAN_STEP // HG_CHUNK
    for c in range(nsub):
        dirs = []
        for bb in range(qf_ref.shape[0]):
            dirs.append((False, c * HG_CHUNK, bb, qf_ref, kf_ref, vf_ref, lff_ref, gf_ref, mf_ref, of_ref,
                         st_ref.at[0, bb]))
            dirs.append((True, (nsub - 1 - c) * HG_CHUNK, bb, qb_ref, kb_ref, vb_ref, lfb_ref, gb_ref, mb_ref, ob_ref,
                         st_ref.at[1, bb]))
        _hg_chunk(dirs)


def _hgrn2(hgq, hgk, hgv, hglf, nx):
    bsz, tall, kw = hgq.shape
    steps = tall // SCAN_STEP
    cf = _hg_constants(False)
    cb = _hg_constants(True)
    nb = HG_SAMPLES if bsz % HG_SAMPLES == 0 else 1
    blk = (nb, SCAN_STEP, kw)
    fwd = lambda col: (lambda b, s: (b, _fwd_blk(s, nx), col))
    bwd = lambda col: (lambda b, s: (b, _bwd_blk(s, nx), col))
    const = lambda a: pl.BlockSpec(a.shape, lambda b, s: (0,) * a.ndim)
    in_specs = [pl.BlockSpec(blk, fwd(0)), pl.BlockSpec(blk, fwd(0)), pl.BlockSpec(blk, fwd(0)), pl.BlockSpec(blk, fwd(0)),
                pl.BlockSpec(blk, bwd(0)), pl.BlockSpec(blk, bwd(1)), pl.BlockSpec(blk, bwd(0)), pl.BlockSpec(blk, bwd(1))]
    in_specs += [const(a) for a in cf + cb]
    out_sds = jax.ShapeDtypeStruct((bsz, tall, kw), BF16)
    return pl.pallas_call(
        _hg_body,
        grid=(bsz // nb, steps),
        in_specs=in_specs,
        out_specs=[pl.BlockSpec(blk, fwd(0)), pl.BlockSpec(blk, bwd(0))],
        out_shape=[out_sds, out_sds],
        scratch_shapes=[pltpu.VMEM((2, nb, N_HEADS, HEAD_DIM, HEAD_DIM), F32)],
        compiler_params=_params(("arbitrary", "arbitrary")),
        name="hgrn2_scan",
    )(hgq, hgk, hgv, hglf, hgq, hgk, hgv, hglf, *cf, *cb)


def _ml_constants(rev):
    k = np.arange(ML_CHUNK)
    tri = (k[:, None] >= k[None, :]) if rev else (k[:, None] <= k[None, :])
    return jnp.asarray(np.concatenate([tri, tri, tri], axis=0), BF16)


def _ml_chunk(dirs):
    c = ML_CHUNK
    ii = lax.broadcasted_iota(I32, (c, c), 0)
    jj = lax.broadcasted_iota(I32, (c, c), 1)
    ones = jnp.ones((c, HEAD_DIM), BF16)
    units = []
    for rev, r0, d, bb, q_ref, k_ref, v_ref, g_ref, tri3_ref, o_ref, st_ref, m_ref in dirs:
        gates = g_ref[bb, :, pl.ds(r0, c)] * LOG2_E
        p1 = gates.astype(BF16)
        r1 = gates - p1.astype(F32)
        p2 = r1.astype(BF16)
        p3 = (r1 - p2.astype(F32)).astype(BF16)
        csum = _dot(jnp.concatenate([p1, p2, p3], axis=1), tri3_ref[...])
        for h in range(N_HEADS):
            cs = slice(h * HEAD_DIM, (h + 1) * HEAD_DIM)
            u = dict(rev=rev, o_ref=o_ref, bb=bb, rows=pl.ds(r0, c), cs=cs, st_ref=st_ref, m_ref=m_ref, h=h)
            u["qb"] = q_ref[bb, pl.ds(r0, c), cs]
            u["kb"] = k_ref[bb, pl.ds(r0, c), cs]
            u["v1"] = jnp.concatenate([v_ref[bb, pl.ds(r0, c), cs], ones], axis=1)
            gi = d * GATE_ROWS + h
            irow = gates[gi:gi + 1]
            u["brow"] = csum[gi + N_HEADS:gi + N_HEADS + 1]
            u["rrow"] = irow - u["brow"]
            units.append(u)
    for u in units:
        u["st"] = u["st_ref"][u["h"]]
        a = _dot(jnp.concatenate([u["kb"], u["st"].astype(BF16)], axis=0), u["qb"], _NT)
        u["s"] = a[:c]
        u["sq"] = a[c:]
    for u in units:
        last = 0 if u["rev"] else c - 1
        tri_t = (ii >= jj) if u["rev"] else (ii <= jj)
        rcol = jnp.concatenate([u["rrow"], jnp.zeros((SUBLANES - 1, c), F32)], axis=0).T[:, 0:1]
        u["mprev"] = u["m_ref"][u["h"]][:, 0:1]
        rmat = jnp.where(tri_t, rcol, -jnp.inf)
        u["grow"] = jnp.maximum(jnp.max(rmat, axis=0, keepdims=True), u["mprev"])
        qk = (u["s"] * jnp.exp2(rmat - u["grow"])).astype(BF16)
        blast = u["brow"][:, last:last + 1]
        u["mnew"] = blast + u["grow"][:, last:last + 1]
        kh = (u["kb"].astype(F32) * jnp.exp2(blast + rcol - u["mnew"])).astype(BF16)
        u["ws"] = jnp.exp2(blast + u["mprev"] - u["mnew"])
        u["qkh"] = jnp.concatenate([qk, kh], axis=1)
    for u in units:
        u["nu"] = _dot(u["v1"], u["qkh"], _TN)
    for u in units:
        both = u["nu"][:, :c] + jnp.exp2(u["mprev"] - u["grow"]) * u["sq"]
        den = both[HEAD_DIM:HEAD_DIM + 1]
        inv = 1.0 / jnp.maximum(jnp.abs(den), jnp.exp2(-(u["brow"] + u["grow"])))
        u["o_ref"][u["bb"], u["rows"], u["cs"]] = (both[:HEAD_DIM] * inv).T.astype(BF16)
        u["st_ref"][u["h"]] = u["ws"] * u["st"] + u["nu"][:, c:]
        u["m_ref"][u["h"]] = jnp.broadcast_to(u["mnew"], (1, HEAD_DIM))


def _ml_body(qf_ref, kf_ref, vf_ref, gf_ref, qb_ref, kb_ref, vb_ref, gb_ref, tf_ref, tb_ref,
             of_ref, ob_ref, st_ref, m_ref):
    @pl.when(pl.program_id(1) == 0)
    def _():
        st_ref[...] = jnp.zeros(st_ref.shape, F32)
        m_ref[...] = jnp.zeros(m_ref.shape, F32)

    nsub = SCAN_STEP // ML_CHUNK
    for c in range(nsub):
        dirs = []
        for bb in range(qf_ref.shape[0]):
            dirs.append((False, c * ML_CHUNK, 0, bb, qf_ref, kf_ref, vf_ref, gf_ref, tf_ref, of_ref,
                         st_ref.at[0, bb], m_ref.at[0, bb]))
            dirs.append((True, (nsub - 1 - c) * ML_CHUNK, 1, bb, qb_ref, kb_ref, vb_ref, gb_ref, tb_ref, ob_ref,
                         st_ref.at[1, bb], m_ref.at[1, bb]))
        _ml_chunk(dirs)


def _mlstm(mlqk, mlv, gates, nx):
    bsz, tall, kw = mlv.shape
    steps = tall // SCAN_STEP
    nb = ML_SAMPLES if bsz % ML_SAMPLES == 0 else 1
    blk = (nb, SCAN_STEP, kw)
    gblk = (nb, 2 * GATE_ROWS, SCAN_STEP)
    tf, tb = _ml_constants(False), _ml_constants(True)
    fwd = lambda col: (lambda b, s: (b, _fwd_blk(s, nx), col))
    bwd = lambda col: (lambda b, s: (b, _bwd_blk(s, nx), col))
    const = pl.BlockSpec(tf.shape, lambda b, s: (0, 0))
    in_specs = [pl.BlockSpec(blk, fwd(0)), pl.BlockSpec(blk, fwd(1)), pl.BlockSpec(blk, fwd(0)),
                pl.BlockSpec(gblk, lambda b, s: (b, 0, _fwd_blk(s, nx))),
                pl.BlockSpec(blk, bwd(0)), pl.BlockSpec(blk, bwd(1)), pl.BlockSpec(blk, bwd(0)),
                pl.BlockSpec(gblk, lambda b, s: (b, 0, _bwd_blk(s, nx))), const, const]
    out_sds = jax.ShapeDtypeStruct((bsz, tall, kw), BF16)
    return pl.pallas_call(
        _ml_body,
        grid=(bsz // nb, steps),
        in_specs=in_specs,
        out_specs=[pl.BlockSpec(blk, fwd(0)), pl.BlockSpec(blk, bwd(0))],
        out_shape=[out_sds, out_sds],
        scratch_shapes=[pltpu.VMEM((2, nb, N_HEADS, 2 * HEAD_DIM, HEAD_DIM), F32),
                        pltpu.VMEM((2, nb, N_HEADS, 1, HEAD_DIM), F32)],
        compiler_params=_params(("arbitrary", "arbitrary")),
        name="mlstm_scan",
    )(mlqk, mlqk, mlv, gates, mlqk, mlqk, mlv, gates, tf, tb)


def _out_body(hof_ref, hob_ref, mhf_ref, mhb_ref, hgg_ref, mlo_ref, x_ref, g1_ref, sc2_ref, sh2_ref,
              hnw_ref, mnw_ref, wout_ref, n2w_ref, rwt_ref, x1_ref, vt_ref, aff_ref):
    hg = hof_ref[0].astype(F32) + hob_ref[0].astype(F32)
    ml = mhf_ref[0].astype(F32) + mhb_ref[0].astype(F32)
    hparts, mparts = [], []
    for h in range(N_HEADS):
        cs = slice(h * HEAD_DIM, (h + 1) * HEAD_DIM)
        t = hg[:, cs]
        hparts.append(t * lax.rsqrt(jnp.mean(t * t, axis=-1, keepdims=True) + NORM_EPS))
        t = ml[:, cs]
        t = t - jnp.mean(t, axis=-1, keepdims=True)
        mparts.append(t * lax.rsqrt(jnp.mean(t * t, axis=-1, keepdims=True) + NORM_EPS))
    hgn = jnp.concatenate(hparts, axis=-1) * hnw_ref[...] * hgg_ref[0].astype(F32)
    mln = jnp.concatenate(mparts, axis=-1) * mnw_ref[...] * mlo_ref[0].astype(F32)
    mix = jnp.concatenate([hgn, mln], axis=-1).astype(BF16)
    x1 = x_ref[0] + g1_ref[0] * _dot(mix, wout_ref[...])
    x1_ref[0] = x1
    v = x1 * lax.rsqrt(jnp.mean(x1 * x1, axis=-1, keepdims=True) + NORM_EPS) * n2w_ref[...]
    v = v * (1.0 + sc2_ref[0]) + sh2_ref[0]
    vt_ref[0] = _pack_halves(v)
    logits = _dot_split(rwt_ref[...], v, v.astype(BF16))
    ex = jnp.exp(logits - jnp.max(logits, axis=0, keepdims=True))
    aff_ref[0] = ex / jnp.sum(ex, axis=0, keepdims=True)


def _mixer_out(hg_f, hg_b, ml_f, ml_b, hgg, mlo, x, g1, sc2, sh2, hnw, mnw, w_out, n2w, rwt, tm):
    bsz, t, d = x.shape
    kw = HEADS_W
    ne = rwt.shape[0]
    row = lambda b, i: (b, i, 0)
    mod = lambda b, i: (b, 0, 0)
    const2 = lambda b, i: (0, 0)
    act = pl.BlockSpec((1, tm, kw), row)
    in_specs = [act, act, act, act, act, act,
                pl.BlockSpec((1, tm, d), row),
                pl.BlockSpec((1, 1, d), mod), pl.BlockSpec((1, 1, d), mod), pl.BlockSpec((1, 1, d), mod),
                pl.BlockSpec((1, kw), const2), pl.BlockSpec((1, kw), const2),
                pl.BlockSpec(w_out.shape, const2), pl.BlockSpec((1, d), const2), pl.BlockSpec(rwt.shape, const2)]
    return pl.pallas_call(
        _out_body,
        grid=(bsz, t // tm),
        in_specs=in_specs,
        out_specs=[pl.BlockSpec((1, tm, d), row),
                   pl.BlockSpec((1, tm, d // 2), row),
                   pl.BlockSpec((1, ne, tm), lambda b, i: (b, 0, i))],
        out_shape=[jax.ShapeDtypeStruct((bsz, t, d), F32),
                   jax.ShapeDtypeStruct((bsz, t, d // 2), I32),
                   jax.ShapeDtypeStruct((bsz, ne, t), F32)],
        compiler_params=_params(("arbitrary", "arbitrary")),
        name="mixer_out",
    )(hg_f, hg_b, ml_f, ml_b, hgg, mlo, x, g1, sc2, sh2, hnw, mnw, w_out, n2w, rwt)


def _prefix_count(maskf, u_ref, ones_ref, bl_ref):
    e, nb, ln = maskf.shape
    x = maskf.reshape(e * nb, ln)
    xb = x.astype(BF16)
    incl = _dot(xb, u_ref[...])
    tot = _dot(xb, ones_ref[...])
    off = _dot(bl_ref[...], tot.astype(BF16))
    return (incl - x + off).reshape(e, nb, ln), off.reshape(e, nb, ln)


def _topk_body(aff_ref, u_ref, ones_ref, bl_ref, bl1_ref, pos_ref, base_ref, *, cap):
    x = aff_ref[0]
    ne = x.shape[0]

    def count(m):
        return jnp.sum(jnp.sum(jnp.where(m, 1.0, 0.0), axis=1, keepdims=True), axis=2, keepdims=True)

    def halve(_, carry):
        lo, hi = carry
        mid = 0.5 * (lo + hi)
        up = count(x > mid) >= cap
        return jnp.where(up, mid, lo), jnp.where(up, hi, mid)

    lo, hi = lax.fori_loop(0, TOPK_BISECTIONS, halve,
                           (jnp.full((ne, 1, 1), -1.0, F32), jnp.full((ne, 1, 1), 1.0, F32)))
    gt = jnp.where(x > hi, 1.0, 0.0)
    eq = jnp.where(x > lo, 1.0, 0.0) - gt
    need = cap - count(x > hi)
    eq_rank, _ = _prefix_count(eq, u_ref, ones_ref, bl_ref)
    sel = gt + eq * jnp.where(eq_rank < need, 1.0, 0.0)
    pos, _ = _prefix_count(sel, u_ref, ones_ref, bl_ref)
    pos_ref[0] = jnp.where(sel > 0, pos, -1.0).astype(I32)
    nsel = jnp.sum(sel, axis=0)
    nb16 = nsel.astype(BF16)
    incl = _dot(nb16, u_ref[...])
    tot = _dot(nb16, ones_ref[...])
    off = _dot(bl1_ref[...], tot, precision=HIGHEST)
    base_ref[0] = (incl - nsel + off).astype(I32)


def _topk(aff4, cap):
    bsz, ne, nb, ln = aff4.shape
    k = np.arange(ln)
    u = jnp.asarray(k[:, None] <= k[None, :], BF16)
    ones = jnp.ones((ln, ln), BF16)
    r = np.arange(ne * nb)
    bl = jnp.asarray(((r[:, None] // nb) == (r[None, :] // nb)) & (r[None, :] < r[:, None]), BF16)
    r1 = np.arange(nb)
    bl1 = jnp.asarray(r1[None, :] < r1[:, None], F32)
    blk = pl.BlockSpec((1, ne, nb, ln), lambda b: (b, 0, 0, 0))
    const = lambda a: pl.BlockSpec(a.shape, lambda b: (0, 0))
    return pl.pallas_call(
        functools.partial(_topk_body, cap=cap),
        grid=(bsz,),
        in_specs=[blk, const(u), const(ones), const(bl), const(bl1)],
        out_specs=[blk, pl.BlockSpec((1, nb, ln), lambda b: (b, 0, 0))],
        out_shape=[jax.ShapeDtypeStruct((bsz, ne, nb, ln), I32), jax.ShapeDtypeStruct((bsz, nb, ln), I32)],
        compiler_params=_params(("arbitrary",)),
        name="expert_topk",
    )(aff4, u, ones, bl, bl1)


def _sc_setup():
    info = plsc.get_sparse_core_info()
    mesh = plsc.VectorSubcoreMesh(core_axis_name="c", subcore_axis_name="s")
    params = dataclasses.replace(pltpu.CompilerParams(), needs_layout_passes=False)
    return info.num_cores, info.num_cores * info.num_subcores, info.num_lanes, mesh, params


def _sc_worker_id(nc):
    return lax.axis_index("s") * nc + lax.axis_index("c")


def _sc_gather_scratch(width, dtype):
    one = [pltpu.VMEM((SC_GATHER_ROWS,), I32), pltpu.VMEM((SC_GATHER_ROWS, width), dtype), pltpu.SemaphoreType.DMA]
    return one + one


def _sc_gather_chunks(table_hbm, out_hbm, idx_v, bufs, out_row0, n, lanes):
    chunk = SC_GATHER_ROWS

    def gather(c, buf):
        ich_v, rows_v, sem = buf
        for q in range(chunk // lanes):
            src = pl.ds(pl.multiple_of(c * chunk + q * lanes, lanes), lanes)
            ich_v[pl.ds(q * lanes, lanes)] = idx_v[src]
        return pltpu.make_async_copy(table_hbm.at[ich_v], rows_v, sem)

    def finish(c, buf):
        ich_v, rows_v, sem = buf
        pltpu.make_async_copy(table_hbm.at[ich_v], rows_v, sem).wait()
        pltpu.sync_copy(rows_v, out_hbm.at[pl.ds(pl.multiple_of(out_row0 + c * chunk, chunk), chunk)])

    npair = n // (2 * chunk)
    gather(0, bufs[0]).start()

    @pl.loop(0, npair)
    def _(i):
        gather(2 * i + 1, bufs[1]).start()
        finish(2 * i, bufs[0])

        @pl.when(i + 1 < npair)
        def _():
            gather(2 * i + 2, bufs[0]).start()

        finish(2 * i + 1, bufs[1])


def _sc_invert_gather(table, pos_flat, aff_flat, npairs, ne, t, cap):
    nc, nw, lanes, mesh, params = _sc_setup()
    split = max(1, nw // npairs)
    seg = cap // split
    per_w = npairs * split // nw
    width = table.shape[1]

    @functools.partial(
        pl.kernel, mesh=mesh, compiler_params=params,
        out_type=[jax.ShapeDtypeStruct((npairs * cap, width), table.dtype),
                  jax.ShapeDtypeStruct((npairs * cap,), F32)],
        scratch_types=[pltpu.VMEM((t,), I32), pltpu.VMEM((t,), F32), pltpu.VMEM((seg,), I32), pltpu.VMEM((seg,), F32)]
        + _sc_gather_scratch(width, table.dtype),
    )
    def body(table_hbm, pos_hbm, aff_hbm, out_hbm, gate_hbm, pos_v, aff_v, idx_v, gate_v, *g):
        wid = _sc_worker_id(nc)

        @pl.loop(0, per_w)
        def _(kk):
            item = wid * per_w + kk
            p = item // split
            lo = (item % split) * seg
            tok0 = (p // ne) * t
            pltpu.sync_copy(pos_hbm.at[pl.ds(pl.multiple_of(p * t, t), t)], pos_v)
            pltpu.sync_copy(aff_hbm.at[pl.ds(pl.multiple_of(p * t, t), t)], aff_v)

            @pl.loop(0, t // lanes)
            def _(i):
                v = pos_v[pl.ds(pl.multiple_of(i * lanes, lanes), lanes)] - lo
                tok = lax.iota(I32, lanes) + i * lanes
                plsc.store_scatter(idx_v, [v], tok, mask=(v >= 0) & (v < seg))

            @pl.loop(0, seg // lanes)
            def _(j):
                sl = pl.ds(pl.multiple_of(j * lanes, lanes), lanes)
                ii = idx_v[sl]
                gate_v[sl] = plsc.load_gather(aff_v, [ii])
                idx_v[sl] = ii + tok0

            row0 = p * cap + lo
            pltpu.sync_copy(gate_v, gate_hbm.at[pl.ds(pl.multiple_of(row0, seg), seg)])
            _sc_gather_chunks(table_hbm, out_hbm, idx_v, (g[0:3], g[3:6]), row0, seg, lanes)

    return body(table, pos_flat, aff_flat)


def _sc_assignment_order(pos_flat, base_flat, bsz, ne, t, cap):
    nc, nw, lanes, mesh, params = _sc_setup()
    na = ne * cap
    per_b = nw // bsz
    rng = na // per_b

    @functools.partial(
        pl.kernel, mesh=mesh, compiler_params=params,
        out_type=[jax.ShapeDtypeStruct((bsz * na,), I32), jax.ShapeDtypeStruct((bsz * na,), I32)],
        scratch_types=[pltpu.VMEM((t,), I32), pltpu.VMEM((t,), I32), pltpu.VMEM((t,), I32),
                       pltpu.VMEM((rng,), I32), pltpu.VMEM((rng,), I32)],
    )
    def body(pos_hbm, base_hbm, perm_hbm, tok_hbm, pos_v, base_v, rank_v, perm_v, tok_v):
        wid = _sc_worker_id(nc)
        b = wid // per_b
        a0 = (wid % per_b) * rng
        pltpu.sync_copy(base_hbm.at[pl.ds(pl.multiple_of(b * t, t), t)], base_v)

        @pl.loop(0, t // lanes)
        def _(i):
            rank_v[pl.ds(pl.multiple_of(i * lanes, lanes), lanes)] = jnp.zeros((lanes,), I32)

        @pl.loop(0, ne)
        def _(e):
            p = b * ne + e
            pltpu.sync_copy(pos_hbm.at[pl.ds(pl.multiple_of(p * t, t), t)], pos_v)

            @pl.loop(0, t // lanes)
            def _(i):
                sl = pl.ds(pl.multiple_of(i * lanes, lanes), lanes)
                v = pos_v[sl]
                r = rank_v[sl]
                a = base_v[sl] + r - a0
                sel = v >= 0
                mine = sel & (a >= 0) & (a < rng)
                plsc.store_scatter(perm_v, [a], v + p * cap, mask=mine)
                plsc.store_scatter(tok_v, [a], lax.iota(I32, lanes) + i * lanes, mask=mine)
                rank_v[sl] = r + jnp.where(sel, 1, 0)

        dst = pl.ds(pl.multiple_of(b * na + a0, rng), rng)
        pltpu.sync_copy(perm_v, perm_hbm.at[dst])
        pltpu.sync_copy(tok_v, tok_hbm.at[dst])

    return body(pos_flat, base_flat)


def _sc_gather_rows(table, idx):
    nc, nw, lanes, mesh, params = _sc_setup()
    n = idx.shape[0]
    per_w = n // nw
    width = table.shape[1]
    chunk = SC_GATHER_ROWS

    @functools.partial(
        pl.kernel, mesh=mesh, compiler_params=params,
        out_type=jax.ShapeDtypeStruct((n, width), table.dtype),
        scratch_types=[pltpu.VMEM((per_w,), I32)] + _sc_gather_scratch(width, table.dtype),
    )
    def body(table_hbm, idx_hbm, out_hbm, idx_v, *g):
        row0 = _sc_worker_id(nc) * per_w
        pltpu.sync_copy(idx_hbm.at[pl.ds(pl.multiple_of(row0, per_w), per_w)], idx_v)
        _sc_gather_chunks(table_hbm, out_hbm, idx_v, (g[0:3], g[3:6]), row0, per_w, lanes)

    return body(table, idx)


def _moe1_body(xs_ref, gate_ref, wg_ref, wu_ref, wd_ref, y_ref):
    xs = _unpack_halves(xs_ref[0])
    hg = _dot(xs, wg_ref[0].astype(BF16))
    hu = _dot(xs, wu_ref[0].astype(BF16))
    h = (_silu(hg) * hu).astype(BF16)
    y = _dot(h, wd_ref[0].astype(BF16))
    gt = gate_ref[0].T
    y = jnp.concatenate([y[k * LANES:(k + 1) * LANES] * gt[:, k:k + 1] for k in range(gt.shape[1])], axis=0)
    y_ref[0] = _pack_halves(y)


def _moe_experts(xs, gate, wg, wu, wd, bsz):
    npairs, cap, half = xs.shape
    ne, d, f = wg.shape
    pair = lambda b, e: (b * ne + e, 0, 0)
    expert = lambda b, e: (e, 0, 0)
    return pl.pallas_call(
        _moe1_body,
        grid=(bsz, ne),
        in_specs=[pl.BlockSpec((1, cap, half), pair),
                  pl.BlockSpec((1, cap // LANES, LANES), pair),
                  pl.BlockSpec((1, d, f), expert),
                  pl.BlockSpec((1, d, f), expert),
                  pl.BlockSpec((1, f, d), expert)],
        out_specs=pl.BlockSpec((1, cap, half), pair),
        out_shape=jax.ShapeDtypeStruct((npairs, cap, half), I32),
        compiler_params=_params(("arbitrary", "arbitrary")),
        name="moe_experts",
    )(xs, gate, wg, wu, wd)


def _combine_body(abase_ref, ys_hbm, tok_ref, x1_ref, g2_ref, fw_ref, o_ref, buf_ref, acc_ref, sem_ref, cnt_ref, *, na):
    n = ONEHOT_BLK
    ring = buf_ref.shape[0]
    nch = na // n
    b, i = pl.program_id(0), pl.program_id(1)

    @pl.when(i == 0)
    def _():
        cnt_ref[0] = 0
        cnt_ref[1] = 0

    def copy(c):
        slot = c % ring
        row0 = pl.multiple_of(b * na + c * n, n)
        return pltpu.make_async_copy(ys_hbm.at[pl.ds(row0, n)], buf_ref.at[slot], sem_ref.at[slot])

    for j in range(x1_ref.shape[1] // n):
        blk = i * (x1_ref.shape[1] // n) + j
        lo = abase_ref[b, blk]
        hi = abase_ref[b, blk + 1]
        c0 = lo // n
        c1 = jnp.where(hi > lo, (hi + n - 1) // n, c0)
        acc_ref[...] = jnp.zeros(acc_ref.shape, F32)
        tok_ids = lax.broadcasted_iota(I32, (n, n), 0) + blk * n

        def step(c, carry, tok_ids=tok_ids):
            started = cnt_ref[0]
            ahead = jnp.minimum(c + ring, nch)

            def start(k, _):
                copy(k).start()
                return 0

            lax.fori_loop(started, ahead, start, 0)
            cnt_ref[0] = jnp.maximum(started, ahead)

            @pl.when(cnt_ref[1] <= c)
            def _():
                copy(c).wait()
                cnt_ref[1] = c + 1

            seg = jnp.where(tok_ids == tok_ref[0, pl.ds(c, 1), :], 1.0, 0.0).astype(BF16)
            acc_ref[...] += _dot(seg, _unpack_halves(buf_ref[c % ring]))
            return carry

        lax.fori_loop(c0, c1, step, 0)
        rows = slice(j * n, (j + 1) * n)
        x2 = x1_ref[0, rows, :] + g2_ref[0] * acc_ref[...]
        o_ref[0, rows, :] = x2 * lax.rsqrt(jnp.mean(x2 * x2, axis=-1, keepdims=True) + NORM_EPS) * fw_ref[...]


def _combine_final(abase, ys, tok, x1, g2, fw, na):
    bsz, t, d = x1.shape
    n = ONEHOT_BLK
    tb = min(COMBINE_BLOCKS * n, t)
    grid_spec = pltpu.PrefetchScalarGridSpec(
        num_scalar_prefetch=1,
        grid=(bsz, t // tb),
        in_specs=[pl.BlockSpec(memory_space=pl.ANY),
                  pl.BlockSpec((1, na // n, n), lambda b, i, a: (b, 0, 0)),
                  pl.BlockSpec((1, tb, d), lambda b, i, a: (b, i, 0)),
                  pl.BlockSpec((1, 1, d), lambda b, i, a: (b, 0, 0)),
                  pl.BlockSpec((1, d), lambda b, i, a: (0, 0))],
        out_specs=pl.BlockSpec((1, tb, d), lambda b, i, a: (b, i, 0)),
        scratch_shapes=[pltpu.VMEM((COMBINE_RING, n, d // 2), I32), pltpu.VMEM((n, d), F32),
                        pltpu.SemaphoreType.DMA((COMBINE_RING,)), pltpu.SMEM((2,), I32)],
    )
    return pl.pallas_call(
        functools.partial(_combine_body, na=na),
        grid_spec=grid_spec,
        out_shape=jax.ShapeDtypeStruct((bsz, t, d), F32),
        compiler_params=_params(("arbitrary", "arbitrary")),
        name="combine_final",
    )(abase, ys, tok, x1, g2, fw)


def kernel(x, c, ctx, c_ctx, ada_w, ada_b, norm1_w, w_in, conv_w, conv_b, hg_lb_logits, ml_gate_b,
           hg_norm_w, ml_norm_w, w_out, norm2_w, router_w, exp_w_gate, exp_w_up, exp_w_down, final_norm_w):
    bsz, t, d = x.shape
    nctx = ctx.shape[1]
    assert ada_w.shape[0] == 1, "single-layer block"
    assert nctx == SCAN_STEP and t % SCAN_STEP == 0 and bsz + 1 <= SUBLANES
    tall = t + nctx
    nx = t // SCAN_STEP
    kw = HEADS_W
    ne = router_w.shape[-1]
    cap = EC_CAPACITY * t // ne
    tm = min(512, t)

    rows = jnp.concatenate([c, c_ctx[None], jnp.zeros((SUBLANES - 1 - bsz, d), F32)], axis=0)
    mod = _modulation(rows, ada_w[0], ada_b[0][None])
    mx = [m[:, None, :] for m in jnp.split(mod[:bsz], 6, axis=-1)]
    mc = [m[:, None, :] for m in jnp.split(mod[bsz:bsz + 1], 6, axis=-1)]
    sh1, sc1, g1, sh2, sc2, g2 = mx
    csh1, csc1 = mc[0], mc[1]

    main_w = 9 * kw
    w_main = w_in[0][:, :main_w].astype(BF16)
    w_gt = w_in[0][:, main_w:].T
    gate_b = ml_gate_b[0][:, None]
    nw1 = norm1_w[0][None]
    outs = _inproj(x, sc1, sh1, nw1, w_main, w_gt, gate_b, hg_lb_logits, tall, min(2 * tm, t), 0)
    outs = _inproj(ctx, csc1, csh1, nw1, w_main, w_gt, gate_b, hg_lb_logits, tall, nctx, t // nctx, prev=outs)
    hgq, hgv, hgg, hgk, hglf, mlqk_pre, mlv, mlo, gates = outs

    mlqk = _conv(mlqk_pre, conv_w[0], conv_b[0][None], t, nctx)
    hg_f, hg_b = _hgrn2(hgq, hgk, hgv, hglf, nx)
    ml_f, ml_b = _mlstm(mlqk, mlv, gates, nx)

    x1, vpk, aff = _mixer_out(hg_f, hg_b, ml_f, ml_b, hgg, mlo, x, g1, sc2, sh2,
                             hg_norm_w[0][None], ml_norm_w[0][None], w_out[0].astype(BF16),
                             norm2_w[0][None], router_w[0].T, min(2 * tm, t))

    pos4, base = _topk(aff.reshape(bsz, ne, t // LANES, LANES), cap)
    na = ne * cap
    pos_flat = pos4.reshape(-1)
    xs, gate = _sc_invert_gather(vpk.reshape(bsz * t, d // 2), pos_flat, aff.reshape(-1), bsz * ne, ne, t, cap)
    perm, tok = _sc_assignment_order(pos_flat, base.reshape(-1), bsz, ne, t, cap)
    y = _moe_experts(xs.reshape(bsz * ne, cap, d // 2), gate.reshape(bsz * ne, cap // LANES, LANES),
                     exp_w_gate[0], exp_w_up[0], exp_w_down[0], bsz)
    ys = _sc_gather_rows(y.reshape(bsz * na, d // 2), perm)
    abase = jnp.concatenate([base.reshape(bsz, t)[:, ::ONEHOT_BLK], jnp.full((bsz, 1), na, I32)], axis=1)
    return _combine_final(abase, ys, tok.reshape(bsz, na // ONEHOT_BLK, ONEHOT_BLK), x1, g2, final_norm_w[None], na)
```

```python
import dataclasses
import functools

import numpy as np
import jax
import jax.numpy as jnp
from jax import lax
from jax.experimental import pallas as pl
from jax.experimental.pallas import tpu as pltpu
from jax.experimental.pallas import tpu_sc as plsc

F32 = jnp.float32
BF16 = jnp.bfloat16
I32 = jnp.int32
HIGHEST = lax.Precision.HIGHEST
NORM_EPS = 1e-6
LOG2_E = 1.4426950408889634

LANES = 128
SUBLANES = 8
HEAD_DIM = 128
N_HEADS = 4
HEADS_W = N_HEADS * HEAD_DIM
GATE_ROWS = 2 * N_HEADS
GRID_W = 64
MOD_COL_TILES = 4
EC_CAPACITY = 2
HG_CHUNK = 64
HG_VPU_LEVEL_MIN = 4
ML_CHUNK = 256
SCAN_STEP = 256
HG_SAMPLES = 2
ML_SAMPLES = 4
ONEHOT_BLK = 256
CONV_HALO = 72
CONV_ROWS = 512
SC_GATHER_ROWS = 64
COMBINE_RING = 6
COMBINE_BLOCKS = 2
TOPK_BISECTIONS = 64
VMEM_LIMIT = 56 * 1024 * 1024

_NT = (((1,), (1,)), ((), ()))
_TN = (((0,), (0,)), ((), ()))


def _dot(a, b, dims=None, precision=None):
    if dims is None:
        return jnp.dot(a, b, preferred_element_type=F32, precision=precision)
    return lax.dot_general(a, b, dims, preferred_element_type=F32, precision=precision)


def _dot_split(w, x, x1):
    n = w.shape[0]
    x2 = (x - x1.astype(F32)).astype(BF16)
    w1 = w.astype(BF16)
    wr = w - w1.astype(F32)
    w2 = wr.astype(BF16)
    w3 = (wr - w2.astype(F32)).astype(BF16)
    pa = _dot(jnp.concatenate([w1, w2, w3], axis=0), x1, _NT)
    pb = _dot(jnp.concatenate([w1, w2], axis=0), x2, _NT)
    return pa[0:n] + pa[n:2 * n] + pa[2 * n:] + pb[0:n] + pb[n:]


def _sigmoid(x):
    return jax.nn.sigmoid(x)


def _pack_halves(x):
    w = x.shape[-1] // 2
    bits = lax.bitcast_convert_type(x.astype(BF16).astype(F32), I32)
    return lax.shift_right_logical(bits[:, :w], 16) | bits[:, w:]


def _unpack_halves(p):
    lo = lax.bitcast_convert_type(lax.shift_left(p, 16), F32)
    hi = lax.bitcast_convert_type(p & jnp.int32(-65536), F32)
    return jnp.concatenate([lo, hi], axis=-1).astype(BF16)


def _silu(x):
    return x * jax.nn.sigmoid(x)


def _params(sem, flags=None):
    return pltpu.CompilerParams(dimension_semantics=sem, vmem_limit_bytes=VMEM_LIMIT, flags=flags)


def _mod_body(r_ref, w_ref, b_ref, o_ref):
    r = r_ref[...]
    o_ref[...] = _dot(_silu(r), w_ref[...], precision=HIGHEST) + b_ref[...]


def _modulation(rows, w, b):
    d, n = w.shape
    nrows = rows.shape[0]
    tn = n // MOD_COL_TILES
    return pl.pallas_call(
        _mod_body,
        grid=(MOD_COL_TILES,),
        in_specs=[pl.BlockSpec((nrows, d), lambda j: (0, 0)),
                  pl.BlockSpec((d, tn), lambda j: (0, j)),
                  pl.BlockSpec((1, tn), lambda j: (0, j))],
        out_specs=pl.BlockSpec((nrows, tn), lambda j: (0, j)),
        out_shape=jax.ShapeDtypeStruct((nrows, n), F32),
        compiler_params=_params(("arbitrary",)),
        name="modulation",
    )(rows, w, b)


def _log_sigmoid(x):
    return jnp.minimum(x, 0.0) - jnp.log(1.0 + jnp.exp(-jnp.abs(x)))


def _inproj_body(x_ref, sc_ref, sh_ref, nw_ref, w_ref, wg_ref, gb_ref, lbl_ref, *refs):
    hgq_ref, hgv_ref, hgg_ref, hgk_ref, hglf_ref, mlqk_ref, mlv_ref, mlo_ref, gates_ref = refs[-9:]
    kw = HEADS_W
    x = x_ref[0]
    y = x * lax.rsqrt(jnp.mean(x * x, axis=-1, keepdims=True) + NORM_EPS) * nw_ref[...]
    u = y * (1.0 + sc_ref[0]) + sh_ref[0]
    ub = u.astype(BF16)

    def proj(c0, c1):
        return _dot(ub, w_ref[:, c0:c1])

    hgq_ref[0] = _silu(proj(0, kw)).astype(BF16)
    hgv_ref[0] = proj(kw, 2 * kw).astype(BF16)
    hgg_ref[0] = _silu(proj(2 * kw, 3 * kw)).astype(BF16)

    lbl = lbl_ref[...]
    mx = jnp.max(lbl, axis=0)
    ex = jnp.exp(lbl - mx[None])
    lb = ex[0] / jnp.sum(ex, axis=0)
    for d in range(2):
        p = proj((3 + d) * kw, (4 + d) * kw)
        lbd = lb[d:d + 1]
        f = lbd + (1.0 - lbd) * _sigmoid(p)
        hgk_ref[0, :, d * kw:(d + 1) * kw] = (1.0 - f).astype(BF16)
        hglf_ref[0, :, d * kw:(d + 1) * kw] = jnp.log(f)

    mlqk_ref[0, :, 0:kw] = proj(5 * kw, 6 * kw)
    mlqk_ref[0, :, kw:2 * kw] = proj(6 * kw, 7 * kw)
    mlv_ref[0] = proj(7 * kw, 8 * kw).astype(BF16)
    mlo_ref[0] = _sigmoid(proj(8 * kw, 9 * kw)).astype(BF16)

    g = _dot_split(wg_ref[...], u, ub) + gb_ref[...]
    row = lax.broadcasted_iota(I32, g.shape, 0)
    gates_ref[0] = jnp.where((row % GATE_ROWS) >= N_HEADS, _log_sigmoid(g), g)


def _inproj_shapes(bsz, tall):
    kw = HEADS_W
    return [
        jax.ShapeDtypeStruct((bsz, tall, kw), BF16),
        jax.ShapeDtypeStruct((bsz, tall, kw), BF16),
        jax.ShapeDtypeStruct((bsz, tall, kw), BF16),
        jax.ShapeDtypeStruct((bsz, tall, 2 * kw), BF16),
        jax.ShapeDtypeStruct((bsz, tall, 2 * kw), F32),
        jax.ShapeDtypeStruct((bsz, tall, 2 * kw), F32),
        jax.ShapeDtypeStruct((bsz, tall, kw), BF16),
        jax.ShapeDtypeStruct((bsz, tall, kw), BF16),
        jax.ShapeDtypeStruct((bsz, 2 * GATE_ROWS, tall), F32),
    ]


def _inproj(tokens, scale, shift, nw, w_main, w_gt, gate_b, lb_logits, tall, tm, blk0, prev=None):
    bsz, n, d = tokens.shape
    kw = HEADS_W
    nt = n // tm
    per_sample = scale.shape[0] == bsz
    mod_map = (lambda b, i: (b, 0, 0)) if per_sample else (lambda b, i: (0, 0, 0))
    const2 = lambda b, i: (0, 0)
    in_specs = [
        pl.BlockSpec((1, tm, d), lambda b, i: (b, i, 0)),
        pl.BlockSpec((1, 1, d), mod_map),
        pl.BlockSpec((1, 1, d), mod_map),
        pl.BlockSpec((1, d), const2),
        pl.BlockSpec(w_main.shape, const2, pipeline_mode=pl.Buffered(1)),
        pl.BlockSpec(w_gt.shape, const2),
        pl.BlockSpec(gate_b.shape, const2),
        pl.BlockSpec(lb_logits.shape, lambda b, i: (0, 0, 0)),
    ]
    args = [tokens, scale, shift, nw, w_main, w_gt, gate_b, lb_logits]
    aliases = {}
    if prev is not None:
        for k, a in enumerate(prev):
            in_specs.append(pl.BlockSpec(memory_space=pl.ANY))
            aliases[len(args)] = k
            args.append(a)
    row_map = lambda b, i: (b, blk0 + i, 0)
    widths = [kw, kw, kw, 2 * kw, 2 * kw, 2 * kw, kw, kw]
    out_specs = [pl.BlockSpec((1, tm, w), row_map) for w in widths]
    out_specs.append(pl.BlockSpec((1, 2 * GATE_ROWS, tm), lambda b, i: (b, 0, blk0 + i)))
    return pl.pallas_call(
        _inproj_body,
        grid=(bsz, nt),
        in_specs=in_specs,
        out_specs=out_specs,
        out_shape=_inproj_shapes(bsz, tall),
        input_output_aliases=aliases,
        compiler_params=_params(("arbitrary", "arbitrary")),
        name="inproj_ctx" if prev is not None else "inproj_x",
    )(*args)


def _conv_body(x_ref, w_ref, b_ref, o_ref, pad_ref, cpad_ref, *, t, nctx, scale_from):
    halo = CONV_HALO
    rows = CONV_ROWS
    win = rows + 2 * halo
    ch = x_ref.shape[-1]
    scale = jnp.where(pl.program_id(1) >= scale_from, HEAD_DIM ** -0.5, 1.0).astype(F32)
    w = w_ref[...]
    bias = b_ref[...]

    pad_ref[0:halo, :] = jnp.zeros((halo, ch), F32)
    pad_ref[halo + t:halo + t + halo, :] = jnp.zeros((halo, ch), F32)
    pad_ref[halo:halo + t, :] = x_ref[0, 0:t, :]
    col = (lax.broadcasted_iota(I32, (win, ch), 0) + (GRID_W - halo % GRID_W)) % GRID_W
    left_ok = col > 0
    right_ok = col < GRID_W - 1

    def chunk(c, carry):
        o = pl.multiple_of(c * rows, rows)
        xw = pad_ref[pl.ds(o, win), :]
        xm = jnp.where(left_ok, pltpu.roll(xw, 1, 0), 0.0)
        xp = jnp.where(right_ok, pltpu.roll(xw, win - 1, 0), 0.0)
        def taps(dr, lo):
            sl = slice(lo, lo + rows)
            return xm[sl] * w[dr, 0:1] + xw[sl] * w[dr, 1:2] + xp[sl] * w[dr, 2:3]

        y = taps(1, halo) + taps(0, halo - GRID_W) + taps(2, halo + GRID_W)
        o_ref[0, pl.ds(o, rows), :] = (_silu(y + bias) * scale).astype(o_ref.dtype)
        return carry

    lax.fori_loop(0, t // rows, chunk, 0)

    pad = SUBLANES
    cpad_ref[0:pad, :] = jnp.zeros((pad, ch), F32)
    cpad_ref[pad + nctx:2 * pad + nctx, :] = jnp.zeros((pad, ch), F32)
    cpad_ref[pad:pad + nctx, :] = x_ref[0, t:t + nctx, :]
    xw = cpad_ref[...]
    n = nctx + 2 * pad
    y = (pltpu.roll(xw, 1, 0) * w[1, 0:1] + xw * w[1, 1:2] + pltpu.roll(xw, n - 1, 0) * w[1, 2:3])[pad:pad + nctx]
    o_ref[0, t:t + nctx, :] = (_silu(y + bias) * scale).astype(o_ref.dtype)


def _conv(qk_pre, conv_w, conv_b, t, nctx):
    bsz, tall, c = qk_pre.shape
    ch = 128
    body = functools.partial(_conv_body, t=t, nctx=nctx, scale_from=(c // 2) // ch)
    return pl.pallas_call(
        body,
        grid=(bsz, c // ch),
        in_specs=[pl.BlockSpec((1, tall, ch), lambda b, j: (b, 0, j)),
                  pl.BlockSpec((3, 3, ch), lambda b, j: (0, 0, j)),
                  pl.BlockSpec((1, ch), lambda b, j: (0, j))],
        out_specs=pl.BlockSpec((1, tall, ch), lambda b, j: (b, 0, j)),
        out_shape=jax.ShapeDtypeStruct((bsz, tall, c), BF16),
        scratch_shapes=[pltpu.VMEM((t + 2 * CONV_HALO, ch), F32),
                        pltpu.VMEM((nctx + 2 * SUBLANES, ch), F32)],
        compiler_params=_params(("arbitrary", "arbitrary")),
        name="qk_conv",
    )(qk_pre, conv_w, conv_b)


def _fwd_blk(s, nx):
    return jnp.where(s == 0, nx, s - 1)


def _bwd_blk(s, nx):
    return jnp.where(s == 0, nx, nx - s)


def _hg_constants(rev):
    c = HG_CHUNK
    i = np.arange(c)[:, None]
    j = np.arange(c)[None, :]
    blocks = [(j >= i) if rev else (j <= i)]
    masks = [i == j]
    m = c // 2
    while m >= 1:
        b0 = (i // (2 * m)) * (2 * m)
        same = (i // (2 * m)) == (j // (2 * m))
        if rev:
            beta = b0 + m
            qrow = (i % (2 * m)) < m
            g = np.where(qrow, (j >= i) & (j < beta), (j >= beta) & (j < i))
            mask = same & qrow & ((j % (2 * m)) >= m)
        else:
            beta = b0 + m - 1
            qrow = (i % (2 * m)) >= m
            g = np.where(qrow, (j > beta) & (j <= i), (j > i) & (j <= beta))
            mask = same & qrow & ((j % (2 * m)) < m)
        if m < HG_VPU_LEVEL_MIN:
            blocks.append(g)
        masks.append(mask)
        m //= 2
    g = np.concatenate(blocks, axis=0).astype(np.float32)
    g3 = np.concatenate([g, g, g], axis=1)
    m2 = np.concatenate([np.stack(masks), np.stack(masks)], axis=2)
    return (jnp.asarray(g3, BF16), jnp.asarray(m2, F32))


def _hg_level_decay(a, m, rev):
    c = a.shape[0]
    parts = []
    for b0 in range(0, c, 2 * m):
        beta = b0 + m if rev else b0 + m - 1
        ref = a[beta:beta + 1]
        if m % 8 == 0:
            first, second = a[b0:b0 + m], a[b0 + m:b0 + 2 * m]
            parts += [first - ref, ref - second] if rev else [ref - first, second - ref]
        else:
            d = a[b0:b0 + 2 * m] - ref
            parts.append(jnp.minimum(d, -d))
    return jnp.concatenate(parts, axis=0)


def _block_diag(x, zero):
    w = x.shape[1] // 2
    return jnp.concatenate([jnp.concatenate([x[:, :w], zero], axis=1),
                            jnp.concatenate([zero, x[:, w:]], axis=1)], axis=0)


def _hg_chunk(dirs):
    c = HG_CHUNK
    w = 2 * HEAD_DIM
    zero = jnp.zeros((c, HEAD_DIM), BF16)
    units = []
    for rev, r0, bb, q_ref, k_ref, v_ref, lf_ref, g_ref, msk_ref, o_ref, st_ref in dirs:
        rows = pl.ds(r0, c)
        lf = lf_ref[bb, rows, :] * LOG2_E
        p1 = lf.astype(BF16)
        r1 = lf - p1.astype(F32)
        p2 = r1.astype(BF16)
        p3 = (r1 - p2.astype(F32)).astype(BF16)
        dall = _dot(g_ref[...], jnp.concatenate([p1, p2, p3], axis=0))
        for hp in range(N_HEADS // 2):
            cs = slice(hp * w, (hp + 1) * w)
            units.append(dict(rev=rev, rows=rows, cs=cs, hp=hp, bb=bb, msk_ref=msk_ref, o_ref=o_ref, st_ref=st_ref,
                              q=q_ref[bb, rows, cs], k=k_ref[bb, rows, cs], v=v_ref[bb, rows, cs], dall=dall[:, cs]))
    for u in units:
        msk_ref = u["msk_ref"]
        nlev = msk_ref.shape[0] - 1
        qk = u["q"].astype(F32) * u["k"].astype(F32)
        att = jnp.concatenate([jnp.sum(qk[:, :HEAD_DIM], axis=1, keepdims=True) * msk_ref[0][:, :c],
                               jnp.sum(qk[:, HEAD_DIM:], axis=1, keepdims=True) * msk_ref[0][:, c:]], axis=1)
        a = u["dall"][0:c]
        row = 1
        for l in range(nlev):
            m = c >> (l + 1)
            if m >= HG_VPU_LEVEL_MIN:
                dec = _hg_level_decay(a, m, u["rev"])
            else:
                dec = u["dall"][row * c:(row + 1) * c]
                row += 1
            e = jnp.exp2(dec).astype(BF16)
            att = att + _dot(u["q"] * e, _block_diag(u["k"] * e, zero), _NT) * msk_ref[l + 1]
        u["att"] = att.astype(BF16)
    for u in units:
        a = u["dall"][0:c]
        u["a_tot"] = a[0:1] if u["rev"] else a[c - 1:c]
        u["st"] = [u["st_ref"][2 * u["hp"] + i] for i in range(2)]
        zf = jnp.zeros((HEAD_DIM, HEAD_DIM), BF16)
        st2 = jnp.concatenate([jnp.concatenate([u["st"][0].astype(BF16), zf], axis=1),
                               jnp.concatenate([zf, u["st"][1].astype(BF16)], axis=1)], axis=0)
        qbar = (u["q"].astype(F32) * jnp.exp2(a)).astype(BF16)
        u["o"] = _dot(u["att"], _block_diag(u["v"], zero)) + _dot(qbar, st2, _NT)
        u["khat"] = (u["k"].astype(F32) * jnp.exp2(u["a_tot"] - a)).astype(BF16)
    for u in units:
        u["o_ref"][u["bb"], u["rows"], u["cs"]] = u["o"].astype(BF16)
        for i in range(2):
            hs = slice(i * HEAD_DIM, (i + 1) * HEAD_DIM)
            upd = _dot(u["v"][:, hs], u["khat"][:, hs], _TN)
            u["st_ref"][2 * u["hp"] + i] = u["st"][i] * jnp.exp2(u["a_tot"][:, hs]) + upd


def _hg_body(qf_ref, kf_ref, vf_ref, lff_ref, qb_ref, kb_ref, vb_ref, lfb_ref,
             gf_ref, mf_ref, gb_ref, mb_ref, of_ref, ob_ref, st_ref):
    @pl.when(pl.program_id(1) == 0)
    def _():
        st_ref[...] = jnp.zeros(st_ref.shape, F32)

    nsub = SCAN_STEP // HG_CHUNK
    for c in range(nsub):
        dirs = []
        for bb in range(qf_ref.shape[0]):
            dirs.append((False, c * HG_CHUNK, bb, qf_ref, kf_ref, vf_ref, lff_ref, gf_ref, mf_ref, of_ref,
                         st_ref.at[0, bb]))
            dirs.append((True, (nsub - 1 - c) * HG_CHUNK, bb, qb_ref, kb_ref, vb_ref, lfb_ref, gb_ref, mb_ref, ob_ref,
                         st_ref.at[1, bb]))
        _hg_chunk(dirs)


def _hgrn2(hgq, hgk, hgv, hglf, nx):
    bsz, tall, kw = hgq.shape
    steps = tall // SCAN_STEP
    cf = _hg_constants(False)
    cb = _hg_constants(True)
    nb = HG_SAMPLES if bsz % HG_SAMPLES == 0 else 1
    blk = (nb, SCAN_STEP, kw)
    fwd = lambda col: (lambda b, s: (b, _fwd_blk(s, nx), col))
    bwd = lambda col: (lambda b, s: (b, _bwd_blk(s, nx), col))
    const = lambda a: pl.BlockSpec(a.shape, lambda b, s: (0,) * a.ndim)
    in_specs = [pl.BlockSpec(blk, fwd(0)), pl.BlockSpec(blk, fwd(0)), pl.BlockSpec(blk, fwd(0)), pl.BlockSpec(blk, fwd(0)),
                pl.BlockSpec(blk, bwd(0)), pl.BlockSpec(blk, bwd(1)), pl.BlockSpec(blk, bwd(0)), pl.BlockSpec(blk, bwd(1))]
    in_specs += [const(a) for a in cf + cb]
    out_sds = jax.ShapeDtypeStruct((bsz, tall, kw), BF16)
    return pl.pallas_call(
        _hg_body,
        grid=(bsz // nb, steps),
        in_specs=in_specs,
        out_specs=[pl.BlockSpec(blk, fwd(0)), pl.BlockSpec(blk, bwd(0))],
        out_shape=[out_sds, out_sds],
        scratch_shapes=[pltpu.VMEM((2, nb, N_HEADS, HEAD_DIM, HEAD_DIM), F32)],
        compiler_params=_params(("arbitrary", "arbitrary")),
        name="hgrn2_scan",
    )(hgq, hgk, hgv, hglf, hgq, hgk, hgv, hglf, *cf, *cb)


def _ml_constants(rev):
    k = np.arange(ML_CHUNK)
    tri = (k[:, None] >= k[None, :]) if rev else (k[:, None] <= k[None, :])
    return jnp.asarray(np.concatenate([tri, tri, tri], axis=0), BF16)


def _ml_chunk(dirs):
    c = ML_CHUNK
    ii = lax.broadcasted_iota(I32, (c, c), 0)
    jj = lax.broadcasted_iota(I32, (c, c), 1)
    ones = jnp.ones((c, HEAD_DIM), BF16)
    units = []
    for rev, r0, d, bb, q_ref, k_ref, v_ref, g_ref, tri3_ref, o_ref, st_ref, m_ref in dirs:
        gates = g_ref[bb, :, pl.ds(r0, c)] * LOG2_E
        p1 = gates.astype(BF16)
        r1 = gates - p1.astype(F32)
        p2 = r1.astype(BF16)
        p3 = (r1 - p2.astype(F32)).astype(BF16)
        csum = _dot(jnp.concatenate([p1, p2, p3], axis=1), tri3_ref[...])
        for h in range(N_HEADS):
            cs = slice(h * HEAD_DIM, (h + 1) * HEAD_DIM)
            u = dict(rev=rev, o_ref=o_ref, bb=bb, rows=pl.ds(r0, c), cs=cs, st_ref=st_ref, m_ref=m_ref, h=h)
            u["qb"] = q_ref[bb, pl.ds(r0, c), cs]
            u["kb"] = k_ref[bb, pl.ds(r0, c), cs]
            u["v1"] = jnp.concatenate([v_ref[bb, pl.ds(r0, c), cs], ones], axis=1)
            gi = d * GATE_ROWS + h
            irow = gates[gi:gi + 1]
            u["brow"] = csum[gi + N_HEADS:gi + N_HEADS + 1]
            u["rrow"] = irow - u["brow"]
            units.append(u)
    for u in units:
        u["st"] = u["st_ref"][u["h"]]
        a = _dot(jnp.concatenate([u["kb"], u["st"].astype(BF16)], axis=0), u["qb"], _NT)
        u["s"] = a[:c]
        u["sq"] = a[c:]
    for u in units:
        last = 0 if u["rev"] else c - 1
        tri_t = (ii >= jj) if u["rev"] else (ii <= jj)
        rcol = jnp.concatenate([u["rrow"], jnp.zeros((SUBLANES - 1, c), F32)], axis=0).T[:, 0:1]
        u["mprev"] = u["m_ref"][u["h"]][:, 0:1]
        rmat = jnp.where(tri_t, rcol, -jnp.inf)
        u["grow"] = jnp.maximum(jnp.max(rmat, axis=0, keepdims=True), u["mprev"])
        qk = (u["s"] * jnp.exp2(rmat - u["grow"])).astype(BF16)
        blast = u["brow"][:, last:last + 1]
        u["mnew"] = blast + u["grow"][:, last:last + 1]
        kh = (u["kb"].astype(F32) * jnp.exp2(blast + rcol - u["mnew"])).astype(BF16)
        u["ws"] = jnp.exp2(blast + u["mprev"] - u["mnew"])
        u["qkh"] = jnp.concatenate([qk, kh], axis=1)
    for u in units:
        u["nu"] = _dot(u["v1"], u["qkh"], _TN)
    for u in units:
        both = u["nu"][:, :c] + jnp.exp2(u["mprev"] - u["grow"]) * u["sq"]
        den = both[HEAD_DIM:HEAD_DIM + 1]
        inv = 1.0 / jnp.maximum(jnp.abs(den), jnp.exp2(-(u["brow"] + u["grow"])))
        u["o_ref"][u["bb"], u["rows"], u["cs"]] = (both[:HEAD_DIM] * inv).T.astype(BF16)
        u["st_ref"][u["h"]] = u["ws"] * u["st"] + u["nu"][:, c:]
        u["m_ref"][u["h"]] = jnp.broadcast_to(u["mnew"], (1, HEAD_DIM))


def _ml_body(qf_ref, kf_ref, vf_ref, gf_ref, qb_ref, kb_ref, vb_ref, gb_ref, tf_ref, tb_ref,
             of_ref, ob_ref, st_ref, m_ref):
    @pl.when(pl.program_id(1) == 0)
    def _():
        st_ref[...] = jnp.zeros(st_ref.shape, F32)
        m_ref[...] = jnp.zeros(m_ref.shape, F32)

    nsub = SCAN_STEP // ML_CHUNK
    for c in range(nsub):
        dirs = []
        for bb in range(qf_ref.shape[0]):
            dirs.append((False, c * ML_CHUNK, 0, bb, qf_ref, kf_ref, vf_ref, gf_ref, tf_ref, of_ref,
                         st_ref.at[0, bb], m_ref.at[0, bb]))
            dirs.append((True, (nsub - 1 - c) * ML_CHUNK, 1, bb, qb_ref, kb_ref, vb_ref, gb_ref, tb_ref, ob_ref,
                         st_ref.at[1, bb], m_ref.at[1, bb]))
        _ml_chunk(dirs)


def _mlstm(mlqk, mlv, gates, nx):
    bsz, tall, kw = mlv.shape
    steps = tall // SCAN_STEP
    nb = ML_SAMPLES if bsz % ML_SAMPLES == 0 else 1
    blk = (nb, SCAN_STEP, kw)
    gblk = (nb, 2 * GATE_ROWS, SCAN_STEP)
    tf, tb = _ml_constants(False), _ml_constants(True)
    fwd = lambda col: (lambda b, s: (b, _fwd_blk(s, nx), col))
    bwd = lambda col: (lambda b, s: (b, _bwd_blk(s, nx), col))
    const = pl.BlockSpec(tf.shape, lambda b, s: (0, 0))
    in_specs = [pl.BlockSpec(blk, fwd(0)), pl.BlockSpec(blk, fwd(1)), pl.BlockSpec(blk, fwd(0)),
                pl.BlockSpec(gblk, lambda b, s: (b, 0, _fwd_blk(s, nx))),
                pl.BlockSpec(blk, bwd(0)), pl.BlockSpec(blk, bwd(1)), pl.BlockSpec(blk, bwd(0)),
                pl.BlockSpec(gblk, lambda b, s: (b, 0, _bwd_blk(s, nx))), const, const]
    out_sds = jax.ShapeDtypeStruct((bsz, tall, kw), BF16)
    return pl.pallas_call(
        _ml_body,
        grid=(bsz // nb, steps),
        in_specs=in_specs,
        out_specs=[pl.BlockSpec(blk, fwd(0)), pl.BlockSpec(blk, bwd(0))],
        out_shape=[out_sds, out_sds],
        scratch_shapes=[pltpu.VMEM((2, nb, N_HEADS, 2 * HEAD_DIM, HEAD_DIM), F32),
                        pltpu.VMEM((2, nb, N_HEADS, 1, HEAD_DIM), F32)],
        compiler_params=_params(("arbitrary", "arbitrary")),
        name="mlstm_scan",
    )(mlqk, mlqk, mlv, gates, mlqk, mlqk, mlv, gates, tf, tb)


def _out_body(hof_ref, hob_ref, mhf_ref, mhb_ref, hgg_ref, mlo_ref, x_ref, g1_ref, sc2_ref, sh2_ref,
              hnw_ref, mnw_ref, wout_ref, n2w_ref, rwt_ref, x1_ref, vt_ref, aff_ref):
    hg = hof_ref[0].astype(F32) + hob_ref[0].astype(F32)
    ml = mhf_ref[0].astype(F32) + mhb_ref[0].astype(F32)
    hparts, mparts = [], []
    for h in range(N_HEADS):
        cs = slice(h * HEAD_DIM, (h + 1) * HEAD_DIM)
        t = hg[:, cs]
        hparts.append(t * lax.rsqrt(jnp.mean(t * t, axis=-1, keepdims=True) + NORM_EPS))
        t = ml[:, cs]
        t = t - jnp.mean(t, axis=-1, keepdims=True)
        mparts.append(t * lax.rsqrt(jnp.mean(t * t, axis=-1, keepdims=True) + NORM_EPS))
    hgn = jnp.concatenate(hparts, axis=-1) * hnw_ref[...] * hgg_ref[0].astype(F32)
    mln = jnp.concatenate(mparts, axis=-1) * mnw_ref[...] * mlo_ref[0].astype(F32)
    mix = jnp.concatenate([hgn, mln], axis=-1).astype(BF16)
    x1 = x_ref[0] + g1_ref[0] * _dot(mix, wout_ref[...])
    x1_ref[0] = x1
    v = x1 * lax.rsqrt(jnp.mean(x1 * x1, axis=-1, keepdims=True) + NORM_EPS) * n2w_ref[...]
    v = v * (1.0 + sc2_ref[0]) + sh2_ref[0]
    vt_ref[0] = _pack_halves(v)
    logits = _dot_split(rwt_ref[...], v, v.astype(BF16))
    ex = jnp.exp(logits - jnp.max(logits, axis=0, keepdims=True))
    aff_ref[0] = ex / jnp.sum(ex, axis=0, keepdims=True)


def _mixer_out(hg_f, hg_b, ml_f, ml_b, hgg, mlo, x, g1, sc2, sh2, hnw, mnw, w_out, n2w, rwt, tm):
    bsz, t, d = x.shape
    kw = HEADS_W
    ne = rwt.shape[0]
    row = lambda b, i: (b, i, 0)
    mod = lambda b, i: (b, 0, 0)
    const2 = lambda b, i: (0, 0)
    act = pl.BlockSpec((1, tm, kw), row)
    in_specs = [act, act, act, act, act, act,
                pl.BlockSpec((1, tm, d), row),
                pl.BlockSpec((1, 1, d), mod), pl.BlockSpec((1, 1, d), mod), pl.BlockSpec((1, 1, d), mod),
                pl.BlockSpec((1, kw), const2), pl.BlockSpec((1, kw), const2),
                pl.BlockSpec(w_out.shape, const2), pl.BlockSpec((1, d), const2), pl.BlockSpec(rwt.shape, const2)]
    return pl.pallas_call(
        _out_body,
        grid=(bsz, t // tm),
        in_specs=in_specs,
        out_specs=[pl.BlockSpec((1, tm, d), row),
                   pl.BlockSpec((1, tm, d // 2), row),
                   pl.BlockSpec((1, ne, tm), lambda b, i: (b, 0, i))],
        out_shape=[jax.ShapeDtypeStruct((bsz, t, d), F32),
                   jax.ShapeDtypeStruct((bsz, t, d // 2), I32),
                   jax.ShapeDtypeStruct((bsz, ne, t), F32)],
        compiler_params=_params(("arbitrary", "arbitrary")),
        name="mixer_out",
    )(hg_f, hg_b, ml_f, ml_b, hgg, mlo, x, g1, sc2, sh2, hnw, mnw, w_out, n2w, rwt)


def _prefix_count(maskf, u_ref, ones_ref, bl_ref):
    e, nb, ln = maskf.shape
    x = maskf.reshape(e * nb, ln)
    xb = x.astype(BF16)
    incl = _dot(xb, u_ref[...])
    tot = _dot(xb, ones_ref[...])
    off = _dot(bl_ref[...], tot.astype(BF16))
    return (incl - x + off).reshape(e, nb, ln), off.reshape(e, nb, ln)


def _topk_body(aff_ref, u_ref, ones_ref, bl_ref, bl1_ref, pos_ref, base_ref, *, cap):
    x = aff_ref[0]
    ne = x.shape[0]

    def count(m):
        return jnp.sum(jnp.sum(jnp.where(m, 1.0, 0.0), axis=1, keepdims=True), axis=2, keepdims=True)

    def halve(_, carry):
        lo, hi = carry
        mid = 0.5 * (lo + hi)
        up = count(x > mid) >= cap
        return jnp.where(up, mid, lo), jnp.where(up, hi, mid)

    lo, hi = lax.fori_loop(0, TOPK_BISECTIONS, halve,
                           (jnp.full((ne, 1, 1), -1.0, F32), jnp.full((ne, 1, 1), 1.0, F32)))
    gt = jnp.where(x > hi, 1.0, 0.0)
    eq = jnp.where(x > lo, 1.0, 0.0) - gt
    need = cap - count(x > hi)
    eq_rank, _ = _prefix_count(eq, u_ref, ones_ref, bl_ref)
    sel = gt + eq * jnp.where(eq_rank < need, 1.0, 0.0)
    pos, _ = _prefix_count(sel, u_ref, ones_ref, bl_ref)
    pos_ref[0] = jnp.where(sel > 0, pos, -1.0).astype(I32)
    nsel = jnp.sum(sel, axis=0)
    nb16 = nsel.astype(BF16)
    incl = _dot(nb16, u_ref[...])
    tot = _dot(nb16, ones_ref[...])
    off = _dot(bl1_ref[...], tot, precision=HIGHEST)
    base_ref[0] = (incl - nsel + off).astype(I32)


def _topk(aff4, cap):
    bsz, ne, nb, ln = aff4.shape
    k = np.arange(ln)
    u = jnp.asarray(k[:, None] <= k[None, :], BF16)
    ones = jnp.ones((ln, ln), BF16)
    r = np.arange(ne * nb)
    bl = jnp.asarray(((r[:, None] // nb) == (r[None, :] // nb)) & (r[None, :] < r[:, None]), BF16)
    r1 = np.arange(nb)
    bl1 = jnp.asarray(r1[None, :] < r1[:, None], F32)
    blk = pl.BlockSpec((1, ne, nb, ln), lambda b: (b, 0, 0, 0))
    const = lambda a: pl.BlockSpec(a.shape, lambda b: (0, 0))
    return pl.pallas_call(
        functools.partial(_topk_body, cap=cap),
        grid=(bsz,),
        in_specs=[blk, const(u), const(ones), const(bl), const(bl1)],
        out_specs=[blk, pl.BlockSpec((1, nb, ln), lambda b: (b, 0, 0))],
        out_shape=[jax.ShapeDtypeStruct((bsz, ne, nb, ln), I32), jax.ShapeDtypeStruct((bsz, nb, ln), I32)],
        compiler_params=_params(("arbitrary",)),
        name="expert_topk",
    )(aff4, u, ones, bl, bl1)


def _sc_setup():
    info = plsc.get_sparse_core_info()
    mesh = plsc.VectorSubcoreMesh(core_axis_name="c", subcore_axis_name="s")
    params = dataclasses.replace(pltpu.CompilerParams(), needs_layout_passes=False)
    return info.num_cores, info.num_cores * info.num_subcores, info.num_lanes, mesh, params


def _sc_worker_id(nc):
    return lax.axis_index("s") * nc + lax.axis_index("c")


def _sc_gather_scratch(width, dtype):
    one = [pltpu.VMEM((SC_GATHER_ROWS,), I32), pltpu.VMEM((SC_GATHER_ROWS, width), dtype), pltpu.SemaphoreType.DMA]
    return one + one


def _sc_gather_chunks(table_hbm, out_hbm, idx_v, bufs, out_row0, n, lanes):
    chunk = SC_GATHER_ROWS

    def gather(c, buf):
        ich_v, rows_v, sem = buf
        for q in range(chunk // lanes):
            src = pl.ds(pl.multiple_of(c * chunk + q * lanes, lanes), lanes)
            ich_v[pl.ds(q * lanes, lanes)] = idx_v[src]
        return pltpu.make_async_copy(table_hbm.at[ich_v], rows_v, sem)

    def finish(c, buf):
        ich_v, rows_v, sem = buf
        pltpu.make_async_copy(table_hbm.at[ich_v], rows_v, sem).wait()
        pltpu.sync_copy(rows_v, out_hbm.at[pl.ds(pl.multiple_of(out_row0 + c * chunk, chunk), chunk)])

    npair = n // (2 * chunk)
    gather(0, bufs[0]).start()

    @pl.loop(0, npair)
    def _(i):
        gather(2 * i + 1, bufs[1]).start()
        finish(2 * i, bufs[0])

        @pl.when(i + 1 < npair)
        def _():
            gather(2 * i + 2, bufs[0]).start()

        finish(2 * i + 1, bufs[1])


def _sc_invert_gather(table, pos_flat, aff_flat, npairs, ne, t, cap):
    nc, nw, lanes, mesh, params = _sc_setup()
    split = max(1, nw // npairs)
    seg = cap // split
    per_w = npairs * split // nw
    width = table.shape[1]

    @functools.partial(
        pl.kernel, mesh=mesh, compiler_params=params,
        out_type=[jax.ShapeDtypeStruct((npairs * cap, width), table.dtype),
                  jax.ShapeDtypeStruct((npairs * cap,), F32)],
        scratch_types=[pltpu.VMEM((t,), I32), pltpu.VMEM((t,), F32), pltpu.VMEM((seg,), I32), pltpu.VMEM((seg,), F32)]
        + _sc_gather_scratch(width, table.dtype),
    )
    def body(table_hbm, pos_hbm, aff_hbm, out_hbm, gate_hbm, pos_v, aff_v, idx_v, gate_v, *g):
        wid = _sc_worker_id(nc)

        @pl.loop(0, per_w)
        def _(kk):
            item = wid * per_w + kk
            p = item // split
            lo = (item % split) * seg
            tok0 = (p // ne) * t
            pltpu.sync_copy(pos_hbm.at[pl.ds(pl.multiple_of(p * t, t), t)], pos_v)
            pltpu.sync_copy(aff_hbm.at[pl.ds(pl.multiple_of(p * t, t), t)], aff_v)

            @pl.loop(0, t // lanes)
            def _(i):
                v = pos_v[pl.ds(pl.multiple_of(i * lanes, lanes), lanes)] - lo
                tok = lax.iota(I32, lanes) + i * lanes
                plsc.store_scatter(idx_v, [v], tok, mask=(v >= 0) & (v < seg))

            @pl.loop(0, seg // lanes)
            def _(j):
                sl = pl.ds(pl.multiple_of(j * lanes, lanes), lanes)
                ii = idx_v[sl]
                gate_v[sl] = plsc.load_gather(aff_v, [ii])
                idx_v[sl] = ii + tok0

            row0 = p * cap + lo
            pltpu.sync_copy(gate_v, gate_hbm.at[pl.ds(pl.multiple_of(row0, seg), seg)])
            _sc_gather_chunks(table_hbm, out_hbm, idx_v, (g[0:3], g[3:6]), row0, seg, lanes)

    return body(table, pos_flat, aff_flat)


def _sc_assignment_order(pos_flat, base_flat, bsz, ne, t, cap):
    nc, nw, lanes, mesh, params = _sc_setup()
    na = ne * cap
    per_b = nw // bsz
    rng = na // per_b

    @functools.partial(
        pl.kernel, mesh=mesh, compiler_params=params,
        out_type=[jax.ShapeDtypeStruct((bsz * na,), I32), jax.ShapeDtypeStruct((bsz * na,), I32)],
        scratch_types=[pltpu.VMEM((t,), I32), pltpu.VMEM((t,), I32), pltpu.VMEM((t,), I32),
                       pltpu.VMEM((rng,), I32), pltpu.VMEM((rng,), I32)],
    )
    def body(pos_hbm, base_hbm, perm_hbm, tok_hbm, pos_v, base_v, rank_v, perm_v, tok_v):
        wid = _sc_worker_id(nc)
        b = wid // per_b
        a0 = (wid % per_b) * rng
        pltpu.sync_copy(base_hbm.at[pl.ds(pl.multiple_of(b * t, t), t)], base_v)

        @pl.loop(0, t // lanes)
        def _(i):
            rank_v[pl.ds(pl.multiple_of(i * lanes, lanes), lanes)] = jnp.zeros((lanes,), I32)

        @pl.loop(0, ne)
        def _(e):
            p = b * ne + e
            pltpu.sync_copy(pos_hbm.at[pl.ds(pl.multiple_of(p * t, t), t)], pos_v)

            @pl.loop(0, t // lanes)
            def _(i):
                sl = pl.ds(pl.multiple_of(i * lanes, lanes), lanes)
                v = pos_v[sl]
                r = rank_v[sl]
                a = base_v[sl] + r - a0
                sel = v >= 0
                mine = sel & (a >= 0) & (a < rng)
                plsc.store_scatter(perm_v, [a], v + p * cap, mask=mine)
                plsc.store_scatter(tok_v, [a], lax.iota(I32, lanes) + i * lanes, mask=mine)
                rank_v[sl] = r + jnp.where(sel, 1, 0)

        dst = pl.ds(pl.multiple_of(b * na + a0, rng), rng)
        pltpu.sync_copy(perm_v, perm_hbm.at[dst])
        pltpu.sync_copy(tok_v, tok_hbm.at[dst])

    return body(pos_flat, base_flat)


def _sc_gather_rows(table, idx):
    nc, nw, lanes, mesh, params = _sc_setup()
    n = idx.shape[0]
    per_w = n // nw
    width = table.shape[1]
    chunk = SC_GATHER_ROWS

    @functools.partial(
        pl.kernel, mesh=mesh, compiler_params=params,
        out_type=jax.ShapeDtypeStruct((n, width), table.dtype),
        scratch_types=[pltpu.VMEM((per_w,), I32)] + _sc_gather_scratch(width, table.dtype),
    )
    def body(table_hbm, idx_hbm, out_hbm, idx_v, *g):
        row0 = _sc_worker_id(nc) * per_w
        pltpu.sync_copy(idx_hbm.at[pl.ds(pl.multiple_of(row0, per_w), per_w)], idx_v)
        _sc_gather_chunks(table_hbm, out_hbm, idx_v, (g[0:3], g[3:6]), row0, per_w, lanes)

    return body(table, idx)


def _moe1_body(xs_ref, gate_ref, wg_ref, wu_ref, wd_ref, y_ref):
    xs = _unpack_halves(xs_ref[0])
    hg = _dot(xs, wg_ref[0].astype(BF16))
    hu = _dot(xs, wu_ref[0].astype(BF16))
    h = (_silu(hg) * hu).astype(BF16)
    y = _dot(h, wd_ref[0].astype(BF16))
    gt = gate_ref[0].T
    y = jnp.concatenate([y[k * LANES:(k + 1) * LANES] * gt[:, k:k + 1] for k in range(gt.shape[1])], axis=0)
    y_ref[0] = _pack_halves(y)


def _moe_experts(xs, gate, wg, wu, wd, bsz):
    npairs, cap, half = xs.shape
    ne, d, f = wg.shape
    pair = lambda b, e: (b * ne + e, 0, 0)
    expert = lambda b, e: (e, 0, 0)
    return pl.pallas_call(
        _moe1_body,
        grid=(bsz, ne),
        in_specs=[pl.BlockSpec((1, cap, half), pair),
                  pl.BlockSpec((1, cap // LANES, LANES), pair),
                  pl.BlockSpec((1, d, f), expert),
                  pl.BlockSpec((1, d, f), expert),
                  pl.BlockSpec((1, f, d), expert)],
        out_specs=pl.BlockSpec((1, cap, half), pair),
        out_shape=jax.ShapeDtypeStruct((npairs, cap, half), I32),
        compiler_params=_params(("arbitrary", "arbitrary")),
        name="moe_experts",
    )(xs, gate, wg, wu, wd)


def _combine_body(abase_ref, ys_hbm, tok_ref, x1_ref, g2_ref, fw_ref, o_ref, buf_ref, acc_ref, sem_ref, cnt_ref, *, na):
    n = ONEHOT_BLK
    ring = buf_ref.shape[0]
    nch = na // n
    b, i = pl.program_id(0), pl.program_id(1)

    @pl.when(i == 0)
    def _():
        cnt_ref[0] = 0
        cnt_ref[1] = 0

    def copy(c):
        slot = c % ring
        row0 = pl.multiple_of(b * na + c * n, n)
        return pltpu.make_async_copy(ys_hbm.at[pl.ds(row0, n)], buf_ref.at[slot], sem_ref.at[slot])

    for j in range(x1_ref.shape[1] // n):
        blk = i * (x1_ref.shape[1] // n) + j
        lo = abase_ref[b, blk]
        hi = abase_ref[b, blk + 1]
        c0 = lo // n
        c1 = jnp.where(hi > lo, (hi + n - 1) // n, c0)
        acc_ref[...] = jnp.zeros(acc_ref.shape, F32)
        tok_ids = lax.broadcasted_iota(I32, (n, n), 0) + blk * n

        def step(c, carry, tok_ids=tok_ids):
            started = cnt_ref[0]
            ahead = jnp.minimum(c + ring, nch)

            def start(k, _):
                copy(k).start()
                return 0

            lax.fori_loop(started, ahead, start, 0)
            cnt_ref[0] = jnp.maximum(started, ahead)

            @pl.when(cnt_ref[1] <= c)
            def _():
                copy(c).wait()
                cnt_ref[1] = c + 1

            seg = jnp.where(tok_ids == tok_ref[0, pl.ds(c, 1), :], 1.0, 0.0).astype(BF16)
            acc_ref[...] += _dot(seg, _unpack_halves(buf_ref[c % ring]))
            return carry

        lax.fori_loop(c0, c1, step, 0)
        rows = slice(j * n, (j + 1) * n)
        x2 = x1_ref[0, rows, :] + g2_ref[0] * acc_ref[...]
        o_ref[0, rows, :] = x2 * lax.rsqrt(jnp.mean(x2 * x2, axis=-1, keepdims=True) + NORM_EPS) * fw_ref[...]


def _combine_final(abase, ys, tok, x1, g2, fw, na):
    bsz, t, d = x1.shape
    n = ONEHOT_BLK
    tb = min(COMBINE_BLOCKS * n, t)
    grid_spec = pltpu.PrefetchScalarGridSpec(
        num_scalar_prefetch=1,
        grid=(bsz, t // tb),
        in_specs=[pl.BlockSpec(memory_space=pl.ANY),
                  pl.BlockSpec((1, na // n, n), lambda b, i, a: (b, 0, 0)),
                  pl.BlockSpec((1, tb, d), lambda b, i, a: (b, i, 0)),
                  pl.BlockSpec((1, 1, d), lambda b, i, a: (b, 0, 0)),
                  pl.BlockSpec((1, d), lambda b, i, a: (0, 0))],
        out_specs=pl.BlockSpec((1, tb, d), lambda b, i, a: (b, i, 0)),
        scratch_shapes=[pltpu.VMEM((COMBINE_RING, n, d // 2), I32), pltpu.VMEM((n, d), F32),
                        pltpu.SemaphoreType.DMA((COMBINE_RING,)), pltpu.SMEM((2,), I32)],
    )
    return pl.pallas_call(
        functools.partial(_combine_body, na=na),
        grid_spec=grid_spec,
        out_shape=jax.ShapeDtypeStruct((bsz, t, d), F32),
        compiler_params=_params(("arbitrary", "arbitrary")),
        name="combine_final",
    )(abase, ys, tok, x1, g2, fw)


def kernel(x, c, ctx, c_ctx, ada_w, ada_b, norm1_w, w_in, conv_w, conv_b, hg_lb_logits, ml_gate_b,
           hg_norm_w, ml_norm_w, w_out, norm2_w, router_w, exp_w_gate, exp_w_up, exp_w_down, final_norm_w):
    bsz, t, d = x.shape
    nctx = ctx.shape[1]
    assert ada_w.shape[0] == 1, "single-layer block"
    assert nctx == SCAN_STEP and t % SCAN_STEP == 0 and bsz + 1 <= SUBLANES
    tall = t + nctx
    nx = t // SCAN_STEP
    kw = HEADS_W
    ne = router_w.shape[-1]
    cap = EC_CAPACITY * t // ne
    tm = min(512, t)

    rows = jnp.concatenate([c, c_ctx[None], jnp.zeros((SUBLANES - 1 - bsz, d), F32)], axis=0)
    mod = _modulation(rows, ada_w[0], ada_b[0][None])
    mx = [m[:, None, :] for m in jnp.split(mod[:bsz], 6, axis=-1)]
    mc = [m[:, None, :] for m in jnp.split(mod[bsz:bsz + 1], 6, axis=-1)]
    sh1, sc1, g1, sh2, sc2, g2 = mx
    csh1, csc1 = mc[0], mc[1]

    main_w = 9 * kw
    w_main = w_in[0][:, :main_w].astype(BF16)
    w_gt = w_in[0][:, main_w:].T
    gate_b = ml_gate_b[0][:, None]
    nw1 = norm1_w[0][None]
    outs = _inproj(x, sc1, sh1, nw1, w_main, w_gt, gate_b, hg_lb_logits, tall, min(2 * tm, t), 0)
    outs = _inproj(ctx, csc1, csh1, nw1, w_main, w_gt, gate_b, hg_lb_logits, tall, nctx, t // nctx, prev=outs)
    hgq, hgv, hgg, hgk, hglf, mlqk_pre, mlv, mlo, gates = outs

    mlqk = _conv(mlqk_pre, conv_w[0], conv_b[0][None], t, nctx)
    hg_f, hg_b = _hgrn2(hgq, hgk, hgv, hglf, nx)
    ml_f, ml_b = _mlstm(mlqk, mlv, gates, nx)

    x1, vpk, aff = _mixer_out(hg_f, hg_b, ml_f, ml_b, hgg, mlo, x, g1, sc2, sh2,
                             hg_norm_w[0][None], ml_norm_w[0][None], w_out[0].astype(BF16),
                             norm2_w[0][None], router_w[0].T, min(2 * tm, t))

    pos4, base = _topk(aff.reshape(bsz, ne, t // LANES, LANES), cap)
    na = ne * cap
    pos_flat = pos4.reshape(-1)
    xs, gate = _sc_invert_gather(vpk.reshape(bsz * t, d // 2), pos_flat, aff.reshape(-1), bsz * ne, ne, t, cap)
    perm, tok = _sc_assignment_order(pos_flat, base.reshape(-1), bsz, ne, t, cap)
    y = _moe_experts(xs.reshape(bsz * ne, cap, d // 2), gate.reshape(bsz * ne, cap // LANES, LANES),
                     exp_w_gate[0], exp_w_up[0], exp_w_down[0], bsz)
    ys = _sc_gather_rows(y.reshape(bsz * na, d // 2), perm)
    abase = jnp.concatenate([base.reshape(bsz, t)[:, ::ONEHOT_BLK], jnp.full((bsz, 1), na, I32)], axis=1)
    return _combine_final(abase, ys, tok.reshape(bsz, na // ONEHOT_BLK, ONEHOT_BLK), x1, g2, final_norm_w[None], na)
```

```python
import dataclasses
import functools

import numpy as np
import jax
import jax.numpy as jnp
from jax import lax
from jax.experimental import pallas as pl
from jax.experimental.pallas import tpu as pltpu
from jax.experimental.pallas import tpu_sc as plsc

F32 = jnp.float32
BF16 = jnp.bfloat16
I32 = jnp.int32
HIGHEST = lax.Precision.HIGHEST
NORM_EPS = 1e-6
LOG2_E = 1.4426950408889634

LANES = 128
SUBLANES = 8
HEAD_DIM = 128
N_HEADS = 4
HEADS_W = N_HEADS * HEAD_DIM
GATE_ROWS = 2 * N_HEADS
GRID_W = 64
MOD_COL_TILES = 4
EC_CAPACITY = 2
HG_CHUNK = 64
HG_VPU_LEVEL_MIN = 4
ML_CHUNK = 256
SCAN_STEP = 256
HG_SAMPLES = 4
ML_SAMPLES = 4
ONEHOT_BLK = 256
CONV_HALO = 72
CONV_ROWS = 512
SC_GATHER_ROWS = 64
COMBINE_RING = 6
COMBINE_BLOCKS = 2
TOPK_BISECTIONS = 64
VMEM_LIMIT = 56 * 1024 * 1024

_NT = (((1,), (1,)), ((), ()))
_TN = (((0,), (0,)), ((), ()))


def _dot(a, b, dims=None, precision=None):
    if dims is None:
        return jnp.dot(a, b, preferred_element_type=F32, precision=precision)
    return lax.dot_general(a, b, dims, preferred_element_type=F32, precision=precision)


def _dot_split(w, x, x1):
    n = w.shape[0]
    x2 = (x - x1.astype(F32)).astype(BF16)
    w1 = w.astype(BF16)
    wr = w - w1.astype(F32)
    w2 = wr.astype(BF16)
    w3 = (wr - w2.astype(F32)).astype(BF16)
    pa = _dot(jnp.concatenate([w1, w2, w3], axis=0), x1, _NT)
    pb = _dot(jnp.concatenate([w1, w2], axis=0), x2, _NT)
    return pa[0:n] + pa[n:2 * n] + pa[2 * n:] + pb[0:n] + pb[n:]


def _sigmoid(x):
    return jax.nn.sigmoid(x)


def _pack_halves(x):
    w = x.shape[-1] // 2
    bits = lax.bitcast_convert_type(x.astype(BF16).astype(F32), I32)
    return lax.shift_right_logical(bits[:, :w], 16) | bits[:, w:]


def _unpack_halves(p):
    lo = lax.bitcast_convert_type(lax.shift_left(p, 16), F32)
    hi = lax.bitcast_convert_type(p & jnp.int32(-65536), F32)
    return jnp.concatenate([lo, hi], axis=-1).astype(BF16)


def _silu(x):
    return x * jax.nn.sigmoid(x)


def _params(sem, flags=None):
    return pltpu.CompilerParams(dimension_semantics=sem, vmem_limit_bytes=VMEM_LIMIT, flags=flags)


def _mod_body(r_ref, w_ref, b_ref, o_ref):
    r = r_ref[...]
    o_ref[...] = _dot(_silu(r), w_ref[...], precision=HIGHEST) + b_ref[...]


def _modulation(rows, w, b):
    d, n = w.shape
    nrows = rows.shape[0]
    tn = n // MOD_COL_TILES
    return pl.pallas_call(
        _mod_body,
        grid=(MOD_COL_TILES,),
        in_specs=[pl.BlockSpec((nrows, d), lambda j: (0, 0)),
                  pl.BlockSpec((d, tn), lambda j: (0, j)),
                  pl.BlockSpec((1, tn), lambda j: (0, j))],
        out_specs=pl.BlockSpec((nrows, tn), lambda j: (0, j)),
        out_shape=jax.ShapeDtypeStruct((nrows, n), F32),
        compiler_params=_params(("arbitrary",)),
        name="modulation",
    )(rows, w, b)


def _log_sigmoid(x):
    return jnp.minimum(x, 0.0) - jnp.log(1.0 + jnp.exp(-jnp.abs(x)))


def _inproj_body(x_ref, sc_ref, sh_ref, nw_ref, w_ref, wg_ref, gb_ref, lbl_ref, *refs):
    hgq_ref, hgv_ref, hgg_ref, hgk_ref, hglf_ref, mlqk_ref, mlv_ref, mlo_ref, gates_ref = refs[-9:]
    kw = HEADS_W
    x = x_ref[0]
    y = x * lax.rsqrt(jnp.mean(x * x, axis=-1, keepdims=True) + NORM_EPS) * nw_ref[...]
    u = y * (1.0 + sc_ref[0]) + sh_ref[0]
    ub = u.astype(BF16)

    def proj(c0, c1):
        return _dot(ub, w_ref[:, c0:c1])

    hgq_ref[0] = _silu(proj(0, kw)).astype(BF16)
    hgv_ref[0] = proj(kw, 2 * kw).astype(BF16)
    hgg_ref[0] = _silu(proj(2 * kw, 3 * kw)).astype(BF16)

    lbl = lbl_ref[...]
    mx = jnp.max(lbl, axis=0)
    ex = jnp.exp(lbl - mx[None])
    lb = ex[0] / jnp.sum(ex, axis=0)
    for d in range(2):
        p = proj((3 + d) * kw, (4 + d) * kw)
        lbd = lb[d:d + 1]
        f = lbd + (1.0 - lbd) * _sigmoid(p)
        hgk_ref[0, :, d * kw:(d + 1) * kw] = (1.0 - f).astype(BF16)
        hglf_ref[0, :, d * kw:(d + 1) * kw] = jnp.log(f)

    mlqk_ref[0, :, 0:kw] = proj(5 * kw, 6 * kw)
    mlqk_ref[0, :, kw:2 * kw] = proj(6 * kw, 7 * kw)
    mlv_ref[0] = proj(7 * kw, 8 * kw).astype(BF16)
    mlo_ref[0] = _sigmoid(proj(8 * kw, 9 * kw)).astype(BF16)

    g = _dot_split(wg_ref[...], u, ub) + gb_ref[...]
    row = lax.broadcasted_iota(I32, g.shape, 0)
    gates_ref[0] = jnp.where((row % GATE_ROWS) >= N_HEADS, _log_sigmoid(g), g)


def _inproj_shapes(bsz, tall):
    kw = HEADS_W
    return [
        jax.ShapeDtypeStruct((bsz, tall, kw), BF16),
        jax.ShapeDtypeStruct((bsz, tall, kw), BF16),
        jax.ShapeDtypeStruct((bsz, tall, kw), BF16),
        jax.ShapeDtypeStruct((bsz, tall, 2 * kw), BF16),
        jax.ShapeDtypeStruct((bsz, tall, 2 * kw), F32),
        jax.ShapeDtypeStruct((bsz, tall, 2 * kw), F32),
        jax.ShapeDtypeStruct((bsz, tall, kw), BF16),
        jax.ShapeDtypeStruct((bsz, tall, kw), BF16),
        jax.ShapeDtypeStruct((bsz, 2 * GATE_ROWS, tall), F32),
    ]


def _inproj(tokens, scale, shift, nw, w_main, w_gt, gate_b, lb_logits, tall, tm, blk0, prev=None):
    bsz, n, d = tokens.shape
    kw = HEADS_W
    nt = n // tm
    per_sample = scale.shape[0] == bsz
    mod_map = (lambda b, i: (b, 0, 0)) if per_sample else (lambda b, i: (0, 0, 0))
    const2 = lambda b, i: (0, 0)
    in_specs = [
        pl.BlockSpec((1, tm, d), lambda b, i: (b, i, 0)),
        pl.BlockSpec((1, 1, d), mod_map),
        pl.BlockSpec((1, 1, d), mod_map),
        pl.BlockSpec((1, d), const2),
        pl.BlockSpec(w_main.shape, const2, pipeline_mode=pl.Buffered(1)),
        pl.BlockSpec(w_gt.shape, const2),
        pl.BlockSpec(gate_b.shape, const2),
        pl.BlockSpec(lb_logits.shape, lambda b, i: (0, 0, 0)),
    ]
    args = [tokens, scale, shift, nw, w_main, w_gt, gate_b, lb_logits]
    aliases = {}
    if prev is not None:
        for k, a in enumerate(prev):
            in_specs.append(pl.BlockSpec(memory_space=pl.ANY))
            aliases[len(args)] = k
            args.append(a)
    row_map = lambda b, i: (b, blk0 + i, 0)
    widths = [kw, kw, kw, 2 * kw, 2 * kw, 2 * kw, kw, kw]
    out_specs = [pl.BlockSpec((1, tm, w), row_map) for w in widths]
    out_specs.append(pl.BlockSpec((1, 2 * GATE_ROWS, tm), lambda b, i: (b, 0, blk0 + i)))
    return pl.pallas_call(
        _inproj_body,
        grid=(bsz, nt),
        in_specs=in_specs,
        out_specs=out_specs,
        out_shape=_inproj_shapes(bsz, tall),
        input_output_aliases=aliases,
        compiler_params=_params(("arbitrary", "arbitrary")),
        name="inproj_ctx" if prev is not None else "inproj_x",
    )(*args)


def _conv_body(x_ref, w_ref, b_ref, o_ref, pad_ref, cpad_ref, *, t, nctx, scale_from):
    halo = CONV_HALO
    rows = CONV_ROWS
    win = rows + 2 * halo
    ch = x_ref.shape[-1]
    scale = jnp.where(pl.program_id(1) >= scale_from, HEAD_DIM ** -0.5, 1.0).astype(F32)
    w = w_ref[...]
    bias = b_ref[...]

    pad_ref[0:halo, :] = jnp.zeros((halo, ch), F32)
    pad_ref[halo + t:halo + t + halo, :] = jnp.zeros((halo, ch), F32)
    pad_ref[halo:halo + t, :] = x_ref[0, 0:t, :]
    col = (lax.broadcasted_iota(I32, (win, ch), 0) + (GRID_W - halo % GRID_W)) % GRID_W
    left_ok = col > 0
    right_ok = col < GRID_W - 1

    def chunk(c, carry):
        o = pl.multiple_of(c * rows, rows)
        xw = pad_ref[pl.ds(o, win), :]
        xm = jnp.where(left_ok, pltpu.roll(xw, 1, 0), 0.0)
        xp = jnp.where(right_ok, pltpu.roll(xw, win - 1, 0), 0.0)
        def taps(dr, lo):
            sl = slice(lo, lo + rows)
            return xm[sl] * w[dr, 0:1] + xw[sl] * w[dr, 1:2] + xp[sl] * w[dr, 2:3]

        y = taps(1, halo) + taps(0, halo - GRID_W) + taps(2, halo + GRID_W)
        o_ref[0, pl.ds(o, rows), :] = (_silu(y + bias) * scale).astype(o_ref.dtype)
        return carry

    lax.fori_loop(0, t // rows, chunk, 0)

    pad = SUBLANES
    cpad_ref[0:pad, :] = jnp.zeros((pad, ch), F32)
    cpad_ref[pad + nctx:2 * pad + nctx, :] = jnp.zeros((pad, ch), F32)
    cpad_ref[pad:pad + nctx, :] = x_ref[0, t:t + nctx, :]
    xw = cpad_ref[...]
    n = nctx + 2 * pad
    y = (pltpu.roll(xw, 1, 0) * w[1, 0:1] + xw * w[1, 1:2] + pltpu.roll(xw, n - 1, 0) * w[1, 2:3])[pad:pad + nctx]
    o_ref[0, t:t + nctx, :] = (_silu(y + bias) * scale).astype(o_ref.dtype)


def _conv(qk_pre, conv_w, conv_b, t, nctx):
    bsz, tall, c = qk_pre.shape
    ch = 128
    body = functools.partial(_conv_body, t=t, nctx=nctx, scale_from=(c // 2) // ch)
    return pl.pallas_call(
        body,
        grid=(bsz, c // ch),
        in_specs=[pl.BlockSpec((1, tall, ch), lambda b, j: (b, 0, j)),
                  pl.BlockSpec((3, 3, ch), lambda b, j: (0, 0, j)),
                  pl.BlockSpec((1, ch), lambda b, j: (0, j))],
        out_specs=pl.BlockSpec((1, tall, ch), lambda b, j: (b, 0, j)),
        out_shape=jax.ShapeDtypeStruct((bsz, tall, c), BF16),
        scratch_shapes=[pltpu.VMEM((t + 2 * CONV_HALO, ch), F32),
                        pltpu.VMEM((nctx + 2 * SUBLANES, ch), F32)],
        compiler_params=_params(("arbitrary", "arbitrary")),
        name="qk_conv",
    )(qk_pre, conv_w, conv_b)


def _fwd_blk(s, nx):
    return jnp.where(s == 0, nx, s - 1)


def _bwd_blk(s, nx):
    return jnp.where(s == 0, nx, nx - s)


def _hg_constants(rev):
    c = HG_CHUNK
    i = np.arange(c)[:, None]
    j = np.arange(c)[None, :]
    blocks = [(j >= i) if rev else (j <= i)]
    masks = [i == j]
    m = c // 2
    while m >= 1:
        b0 = (i // (2 * m)) * (2 * m)
        same = (i // (2 * m)) == (j // (2 * m))
        if rev:
            beta = b0 + m
            qrow = (i % (2 * m)) < m
            g = np.where(qrow, (j >= i) & (j < beta), (j >= beta) & (j < i))
            mask = same & qrow & ((j % (2 * m)) >= m)
        else:
            beta = b0 + m - 1
            qrow = (i % (2 * m)) >= m
            g = np.where(qrow, (j > beta) & (j <= i), (j > i) & (j <= beta))
            mask = same & qrow & ((j % (2 * m)) < m)
        if m < HG_VPU_LEVEL_MIN:
            blocks.append(g)
        masks.append(mask)
        m //= 2
    g = np.concatenate(blocks, axis=0).astype(np.float32)
    g3 = np.concatenate([g, g, g], axis=1)
    m2 = np.concatenate([np.stack(masks), np.stack(masks)], axis=2)
    return (jnp.asarray(g3, BF16), jnp.asarray(m2, F32))


def _hg_level_decay(a, m, rev):
    c = a.shape[0]
    parts = []
    for b0 in range(0, c, 2 * m):
        beta = b0 + m if rev else b0 + m - 1
        ref = a[beta:beta + 1]
        if m % 8 == 0:
            first, second = a[b0:b0 + m], a[b0 + m:b0 + 2 * m]
            parts += [first - ref, ref - second] if rev else [ref - first, second - ref]
        else:
            d = a[b0:b0 + 2 * m] - ref
            parts.append(jnp.minimum(d, -d))
    return jnp.concatenate(parts, axis=0)


def _block_diag(x, zero):
    w = x.shape[1] // 2
    return jnp.concatenate([jnp.concatenate([x[:, :w], zero], axis=1),
                            jnp.concatenate([zero, x[:, w:]], axis=1)], axis=0)


def _hg_chunk(dirs):
    c = HG_CHUNK
    w = 2 * HEAD_DIM
    zero = jnp.zeros((c, HEAD_DIM), BF16)
    units = []
    for rev, r0, bb, q_ref, k_ref, v_ref, lf_ref, g_ref, msk_ref, o_ref, st_ref in dirs:
        rows = pl.ds(r0, c)
        lf = lf_ref[bb, rows, :] * LOG2_E
        p1 = lf.astype(BF16)
        r1 = lf - p1.astype(F32)
        p2 = r1.astype(BF16)
        p3 = (r1 - p2.astype(F32)).astype(BF16)
        dall = _dot(g_ref[...], jnp.concatenate([p1, p2, p3], axis=0))
        for hp in range(N_HEADS // 2):
            cs = slice(hp * w, (hp + 1) * w)
            units.append(dict(rev=rev, rows=rows, cs=cs, hp=hp, bb=bb, msk_ref=msk_ref, o_ref=o_ref, st_ref=st_ref,
                              q=q_ref[bb, rows, cs], k=k_ref[bb, rows, cs], v=v_ref[bb, rows, cs], dall=dall[:, cs]))
    for u in units:
        msk_ref = u["msk_ref"]
        nlev = msk_ref.shape[0] - 1
        qk = u["q"].astype(F32) * u["k"].astype(F32)
        att = jnp.concatenate([jnp.sum(qk[:, :HEAD_DIM], axis=1, keepdims=True) * msk_ref[0][:, :c],
                               jnp.sum(qk[:, HEAD_DIM:], axis=1, keepdims=True) * msk_ref[0][:, c:]], axis=1)
        a = u["dall"][0:c]
        row = 1
        for l in range(nlev):
            m = c >> (l + 1)
            if m >= HG_VPU_LEVEL_MIN:
                dec = _hg_level_decay(a, m, u["rev"])
            else:
                dec = u["dall"][row * c:(row + 1) * c]
                row += 1
            e = jnp.exp2(dec).astype(BF16)
            att = att + _dot(u["q"] * e, _block_diag(u["k"] * e, zero), _NT) * msk_ref[l + 1]
        u["att"] = att.astype(BF16)
    for u in units:
        a = u["dall"][0:c]
        u["a_tot"] = a[0:1] if u["rev"] else a[c - 1:c]
        u["st"] = [u["st_ref"][2 * u["hp"] + i] for i in range(2)]
        zf = jnp.zeros((HEAD_DIM, HEAD_DIM), BF16)
        st2 = jnp.concatenate([jnp.concatenate([u["st"][0].astype(BF16), zf], axis=1),
                               jnp.concatenate([zf, u["st"][1].astype(BF16)], axis=1)], axis=0)
        qbar = (u["q"].astype(F32) * jnp.exp2(a)).astype(BF16)
        u["o"] = _dot(u["att"], _block_diag(u["v"], zero)) + _dot(qbar, st2, _NT)
        u["khat"] = (u["k"].astype(F32) * jnp.exp2(u["a_tot"] - a)).astype(BF16)
    for u in units:
        u["o_ref"][u["bb"], u["rows"], u["cs"]] = u["o"].astype(BF16)
        for i in range(2):
            hs = slice(i * HEAD_DIM, (i + 1) * HEAD_DIM)
            upd = _dot(u["v"][:, hs], u["khat"][:, hs], _TN)
            u["st_ref"][2 * u["hp"] + i] = u["st"][i] * jnp.exp2(u["a_tot"][:, hs]) + upd


def _hg_body(qf_ref, kf_ref, vf_ref, lff_ref, qb_ref, kb_ref, vb_ref, lfb_ref,
             gf_ref, mf_ref, gb_ref, mb_ref, of_ref, ob_ref, st_ref):
    @pl.when(pl.program_id(1) == 0)
    def _():
        st_ref[...] = jnp.zeros(st_ref.shape, F32)

    nsub = SCAN_STEP // HG_CHUNK
    for c in range(nsub):
        dirs = []
        for bb in range(qf_ref.shape[0]):
            dirs.append((False, c * HG_CHUNK, bb, qf_ref, kf_ref, vf_ref, lff_ref, gf_ref, mf_ref, of_ref,
                         st_ref.at[0, bb]))
            dirs.append((True, (nsub - 1 - c) * HG_CHUNK, bb, qb_ref, kb_ref, vb_ref, lfb_ref, gb_ref, mb_ref, ob_ref,
                         st_ref.at[1, bb]))
        _hg_chunk(dirs)


def _hgrn2(hgq, hgk, hgv, hglf, nx):
    bsz, tall, kw = hgq.shape
    steps = tall // SCAN_STEP
    cf = _hg_constants(False)
    cb = _hg_constants(True)
    nb = HG_SAMPLES if bsz % HG_SAMPLES == 0 else 1
    blk = (nb, SCAN_STEP, kw)
    fwd = lambda col: (lambda b, s: (b, _fwd_blk(s, nx), col))
    bwd = lambda col: (lambda b, s: (b, _bwd_blk(s, nx), col))
    const = lambda a: pl.BlockSpec(a.shape, lambda b, s: (0,) * a.ndim)
    in_specs = [pl.BlockSpec(blk, fwd(0)), pl.BlockSpec(blk, fwd(0)), pl.BlockSpec(blk, fwd(0)), pl.BlockSpec(blk, fwd(0)),
                pl.BlockSpec(blk, bwd(0)), pl.BlockSpec(blk, bwd(1)), pl.BlockSpec(blk, bwd(0)), pl.BlockSpec(blk, bwd(1))]
    in_specs += [const(a) for a in cf + cb]
    out_sds = jax.ShapeDtypeStruct((bsz, tall, kw), BF16)
    return pl.pallas_call(
        _hg_body,
        grid=(bsz // nb, steps),
        in_specs=in_specs,
        out_specs=[pl.BlockSpec(blk, fwd(0)), pl.BlockSpec(blk, bwd(0))],
        out_shape=[out_sds, out_sds],
        scratch_shapes=[pltpu.VMEM((2, nb, N_HEADS, HEAD_DIM, HEAD_DIM), F32)],
        compiler_params=_params(("arbitrary", "arbitrary")),
        name="hgrn2_scan",
    )(hgq, hgk, hgv, hglf, hgq, hgk, hgv, hglf, *cf, *cb)


def _ml_constants(rev):
    k = np.arange(ML_CHUNK)
    tri = (k[:, None] >= k[None, :]) if rev else (k[:, None] <= k[None, :])
    return jnp.asarray(np.concatenate([tri, tri, tri], axis=0), BF16)


def _ml_chunk(dirs):
    c = ML_CHUNK
    ii = lax.broadcasted_iota(I32, (c, c), 0)
    jj = lax.broadcasted_iota(I32, (c, c), 1)
    ones = jnp.ones((c, HEAD_DIM), BF16)
    units = []
    for rev, r0, d, bb, q_ref, k_ref, v_ref, g_ref, tri3_ref, o_ref, st_ref, m_ref in dirs:
        gates = g_ref[bb, :, pl.ds(r0, c)] * LOG2_E
        p1 = gates.astype(BF16)
        r1 = gates - p1.astype(F32)
        p2 = r1.astype(BF16)
        p3 = (r1 - p2.astype(F32)).astype(BF16)
        csum = _dot(jnp.concatenate([p1, p2, p3], axis=1), tri3_ref[...])
        for h in range(N_HEADS):
            cs = slice(h * HEAD_DIM, (h + 1) * HEAD_DIM)
            u = dict(rev=rev, o_ref=o_ref, bb=bb, rows=pl.ds(r0, c), cs=cs, st_ref=st_ref, m_ref=m_ref, h=h)
            u["qb"] = q_ref[bb, pl.ds(r0, c), cs]
            u["kb"] = k_ref[bb, pl.ds(r0, c), cs]
            u["v1"] = jnp.concatenate([v_ref[bb, pl.ds(r0, c), cs], ones], axis=1)
            gi = d * GATE_ROWS + h
            irow = gates[gi:gi + 1]
            u["brow"] = csum[gi + N_HEADS:gi + N_HEADS + 1]
            u["rrow"] = irow - u["brow"]
            units.append(u)
    for u in units:
        u["st"] = u["st_ref"][u["h"]]
        a = _dot(jnp.concatenate([u["kb"], u["st"].astype(BF16)], axis=0), u["qb"], _NT)
        u["s"] = a[:c]
        u["sq"] = a[c:]
    for u in units:
        last = 0 if u["rev"] else c - 1
        tri_t = (ii >= jj) if u["rev"] else (ii <= jj)
        rcol = jnp.concatenate([u["rrow"], jnp.zeros((SUBLANES - 1, c), F32)], axis=0).T[:, 0:1]
        u["mprev"] = u["m_ref"][u["h"]][:, 0:1]
        rmat = jnp.where(tri_t, rcol, -jnp.inf)
        u["grow"] = jnp.maximum(jnp.max(rmat, axis=0, keepdims=True), u["mprev"])
        qk = (u["s"] * jnp.exp2(rmat - u["grow"])).astype(BF16)
        blast = u["brow"][:, last:last + 1]
        u["mnew"] = blast + u["grow"][:, last:last + 1]
        kh = (u["kb"].astype(F32) * jnp.exp2(blast + rcol - u["mnew"])).astype(BF16)
        u["ws"] = jnp.exp2(blast + u["mprev"] - u["mnew"])
        u["qkh"] = jnp.concatenate([qk, kh], axis=1)
    for u in units:
        u["nu"] = _dot(u["v1"], u["qkh"], _TN)
    for u in units:
        both = u["nu"][:, :c] + jnp.exp2(u["mprev"] - u["grow"]) * u["sq"]
        den = both[HEAD_DIM:HEAD_DIM + 1]
        inv = 1.0 / jnp.maximum(jnp.abs(den), jnp.exp2(-(u["brow"] + u["grow"])))
        u["o_ref"][u["bb"], u["rows"], u["cs"]] = (both[:HEAD_DIM] * inv).T.astype(BF16)
        u["st_ref"][u["h"]] = u["ws"] * u["st"] + u["nu"][:, c:]
        u["m_ref"][u["h"]] = jnp.broadcast_to(u["mnew"], (1, HEAD_DIM))


def _ml_body(qf_ref, kf_ref, vf_ref, gf_ref, qb_ref, kb_ref, vb_ref, gb_ref, tf_ref, tb_ref,
             of_ref, ob_ref, st_ref, m_ref):
    @pl.when(pl.program_id(1) == 0)
    def _():
        st_ref[...] = jnp.zeros(st_ref.shape, F32)
        m_ref[...] = jnp.zeros(m_ref.shape, F32)

    nsub = SCAN_STEP // ML_CHUNK
    for c in range(nsub):
        dirs = []
        for bb in range(qf_ref.shape[0]):
            dirs.append((False, c * ML_CHUNK, 0, bb, qf_ref, kf_ref, vf_ref, gf_ref, tf_ref, of_ref,
                         st_ref.at[0, bb], m_ref.at[0, bb]))
            dirs.append((True, (nsub - 1 - c) * ML_CHUNK, 1, bb, qb_ref, kb_ref, vb_ref, gb_ref, tb_ref, ob_ref,
                         st_ref.at[1, bb], m_ref.at[1, bb]))
        _ml_chunk(dirs)


def _mlstm(mlqk, mlv, gates, nx):
    bsz, tall, kw = mlv.shape
    steps = tall // SCAN_STEP
    nb = ML_SAMPLES if bsz % ML_SAMPLES == 0 else 1
    blk = (nb, SCAN_STEP, kw)
    gblk = (nb, 2 * GATE_ROWS, SCAN_STEP)
    tf, tb = _ml_constants(False), _ml_constants(True)
    fwd = lambda col: (lambda b, s: (b, _fwd_blk(s, nx), col))
    bwd = lambda col: (lambda b, s: (b, _bwd_blk(s, nx), col))
    const = pl.BlockSpec(tf.shape, lambda b, s: (0, 0))
    in_specs = [pl.BlockSpec(blk, fwd(0)), pl.BlockSpec(blk, fwd(1)), pl.BlockSpec(blk, fwd(0)),
                pl.BlockSpec(gblk, lambda b, s: (b, 0, _fwd_blk(s, nx))),
                pl.BlockSpec(blk, bwd(0)), pl.BlockSpec(blk, bwd(1)), pl.BlockSpec(blk, bwd(0)),
                pl.BlockSpec(gblk, lambda b, s: (b, 0, _bwd_blk(s, nx))), const, const]
    out_sds = jax.ShapeDtypeStruct((bsz, tall, kw), BF16)
    return pl.pallas_call(
        _ml_body,
        grid=(bsz // nb, steps),
        in_specs=in_specs,
        out_specs=[pl.BlockSpec(blk, fwd(0)), pl.BlockSpec(blk, bwd(0))],
        out_shape=[out_sds, out_sds],
        scratch_shapes=[pltpu.VMEM((2, nb, N_HEADS, 2 * HEAD_DIM, HEAD_DIM), F32),
                        pltpu.VMEM((2, nb, N_HEADS, 1, HEAD_DIM), F32)],
        compiler_params=_params(("arbitrary", "arbitrary")),
        name="mlstm_scan",
    )(mlqk, mlqk, mlv, gates, mlqk, mlqk, mlv, gates, tf, tb)


def _out_body(hof_ref, hob_ref, mhf_ref, mhb_ref, hgg_ref, mlo_ref, x_ref, g1_ref, sc2_ref, sh2_ref,
              hnw_ref, mnw_ref, wout_ref, n2w_ref, rwt_ref, x1_ref, vt_ref, aff_ref):
    hg = hof_ref[0].astype(F32) + hob_ref[0].astype(F32)
    ml = mhf_ref[0].astype(F32) + mhb_ref[0].astype(F32)
    hparts, mparts = [], []
    for h in range(N_HEADS):
        cs = slice(h * HEAD_DIM, (h + 1) * HEAD_DIM)
        t = hg[:, cs]
        hparts.append(t * lax.rsqrt(jnp.mean(t * t, axis=-1, keepdims=True) + NORM_EPS))
        t = ml[:, cs]
        t = t - jnp.mean(t, axis=-1, keepdims=True)
        mparts.append(t * lax.rsqrt(jnp.mean(t * t, axis=-1, keepdims=True) + NORM_EPS))
    hgn = jnp.concatenate(hparts, axis=-1) * hnw_ref[...] * hgg_ref[0].astype(F32)
    mln = jnp.concatenate(mparts, axis=-1) * mnw_ref[...] * mlo_ref[0].astype(F32)
    mix = jnp.concatenate([hgn, mln], axis=-1).astype(BF16)
    x1 = x_ref[0] + g1_ref[0] * _dot(mix, wout_ref[...])
    x1_ref[0] = x1
    v = x1 * lax.rsqrt(jnp.mean(x1 * x1, axis=-1, keepdims=True) + NORM_EPS) * n2w_ref[...]
    v = v * (1.0 + sc2_ref[0]) + sh2_ref[0]
    vt_ref[0] = _pack_halves(v)
    logits = _dot_split(rwt_ref[...], v, v.astype(BF16))
    ex = jnp.exp(logits - jnp.max(logits, axis=0, keepdims=True))
    aff_ref[0] = ex / jnp.sum(ex, axis=0, keepdims=True)


def _mixer_out(hg_f, hg_b, ml_f, ml_b, hgg, mlo, x, g1, sc2, sh2, hnw, mnw, w_out, n2w, rwt, tm):
    bsz, t, d = x.shape
    kw = HEADS_W
    ne = rwt.shape[0]
    row = lambda b, i: (b, i, 0)
    mod = lambda b, i: (b, 0, 0)
    const2 = lambda b, i: (0, 0)
    act = pl.BlockSpec((1, tm, kw), row)
    in_specs = [act, act, act, act, act, act,
                pl.BlockSpec((1, tm, d), row),
                pl.BlockSpec((1, 1, d), mod), pl.BlockSpec((1, 1, d), mod), pl.BlockSpec((1, 1, d), mod),
                pl.BlockSpec((1, kw), const2), pl.BlockSpec((1, kw), const2),
                pl.BlockSpec(w_out.shape, const2), pl.BlockSpec((1, d), const2), pl.BlockSpec(rwt.shape, const2)]
    return pl.pallas_call(
        _out_body,
        grid=(bsz, t // tm),
        in_specs=in_specs,
        out_specs=[pl.BlockSpec((1, tm, d), row),
                   pl.BlockSpec((1, tm, d // 2), row),
                   pl.BlockSpec((1, ne, tm), lambda b, i: (b, 0, i))],
        out_shape=[jax.ShapeDtypeStruct((bsz, t, d), F32),
                   jax.ShapeDtypeStruct((bsz, t, d // 2), I32),
                   jax.ShapeDtypeStruct((bsz, ne, t), F32)],
        compiler_params=_params(("arbitrary", "arbitrary")),
        name="mixer_out",
    )(hg_f, hg_b, ml_f, ml_b, hgg, mlo, x, g1, sc2, sh2, hnw, mnw, w_out, n2w, rwt)


def _prefix_count(maskf, u_ref, ones_ref, bl_ref):
    e, nb, ln = maskf.shape
    x = maskf.reshape(e * nb, ln)
    xb = x.astype(BF16)
    incl = _dot(xb, u_ref[...])
    tot = _dot(xb, ones_ref[...])
    off = _dot(bl_ref[...], tot.astype(BF16))
    return (incl - x + off).reshape(e, nb, ln), off.reshape(e, nb, ln)


def _topk_body(aff_ref, u_ref, ones_ref, bl_ref, bl1_ref, pos_ref, base_ref, *, cap):
    x = aff_ref[0]
    ne = x.shape[0]

    def count(m):
        return jnp.sum(jnp.sum(jnp.where(m, 1.0, 0.0), axis=1, keepdims=True), axis=2, keepdims=True)

    def halve(_, carry):
        lo, hi = carry
        mid = 0.5 * (lo + hi)
        up = count(x > mid) >= cap
        return jnp.where(up, mid, lo), jnp.where(up, hi, mid)

    lo, hi = lax.fori_loop(0, TOPK_BISECTIONS, halve,
                           (jnp.full((ne, 1, 1), -1.0, F32), jnp.full((ne, 1, 1), 1.0, F32)))
    gt = jnp.where(x > hi, 1.0, 0.0)
    eq = jnp.where(x > lo, 1.0, 0.0) - gt
    need = cap - count(x > hi)
    eq_rank, _ = _prefix_count(eq, u_ref, ones_ref, bl_ref)
    sel = gt + eq * jnp.where(eq_rank < need, 1.0, 0.0)
    pos, _ = _prefix_count(sel, u_ref, ones_ref, bl_ref)
    pos_ref[0] = jnp.where(sel > 0, pos, -1.0).astype(I32)
    nsel = jnp.sum(sel, axis=0)
    nb16 = nsel.astype(BF16)
    incl = _dot(nb16, u_ref[...])
    tot = _dot(nb16, ones_ref[...])
    off = _dot(bl1_ref[...], tot, precision=HIGHEST)
    base_ref[0] = (incl - nsel + off).astype(I32)


def _topk(aff4, cap):
    bsz, ne, nb, ln = aff4.shape
    k = np.arange(ln)
    u = jnp.asarray(k[:, None] <= k[None, :], BF16)
    ones = jnp.ones((ln, ln), BF16)
    r = np.arange(ne * nb)
    bl = jnp.asarray(((r[:, None] // nb) == (r[None, :] // nb)) & (r[None, :] < r[:, None]), BF16)
    r1 = np.arange(nb)
    bl1 = jnp.asarray(r1[None, :] < r1[:, None], F32)
    blk = pl.BlockSpec((1, ne, nb, ln), lambda b: (b, 0, 0, 0))
    const = lambda a: pl.BlockSpec(a.shape, lambda b: (0, 0))
    return pl.pallas_call(
        functools.partial(_topk_body, cap=cap),
        grid=(bsz,),
        in_specs=[blk, const(u), const(ones), const(bl), const(bl1)],
        out_specs=[blk, pl.BlockSpec((1, nb, ln), lambda b: (b, 0, 0))],
        out_shape=[jax.ShapeDtypeStruct((bsz, ne, nb, ln), I32), jax.ShapeDtypeStruct((bsz, nb, ln), I32)],
        compiler_params=_params(("arbitrary",)),
        name="expert_topk",
    )(aff4, u, ones, bl, bl1)


def _sc_setup():
    info = plsc.get_sparse_core_info()
    mesh = plsc.VectorSubcoreMesh(core_axis_name="c", subcore_axis_name="s")
    params = dataclasses.replace(pltpu.CompilerParams(), needs_layout_passes=False)
    return info.num_cores, info.num_cores * info.num_subcores, info.num_lanes, mesh, params


def _sc_worker_id(nc):
    return lax.axis_index("s") * nc + lax.axis_index("c")


def _sc_gather_scratch(width, dtype):
    one = [pltpu.VMEM((SC_GATHER_ROWS,), I32), pltpu.VMEM((SC_GATHER_ROWS, width), dtype), pltpu.SemaphoreType.DMA]
    return one + one


def _sc_gather_chunks(table_hbm, out_hbm, idx_v, bufs, out_row0, n, lanes):
    chunk = SC_GATHER_ROWS

    def gather(c, buf):
        ich_v, rows_v, sem = buf
        for q in range(chunk // lanes):
            src = pl.ds(pl.multiple_of(c * chunk + q * lanes, lanes), lanes)
            ich_v[pl.ds(q * lanes, lanes)] = idx_v[src]
        return pltpu.make_async_copy(table_hbm.at[ich_v], rows_v, sem)

    def finish(c, buf):
        ich_v, rows_v, sem = buf
        pltpu.make_async_copy(table_hbm.at[ich_v], rows_v, sem).wait()
        pltpu.sync_copy(rows_v, out_hbm.at[pl.ds(pl.multiple_of(out_row0 + c * chunk, chunk), chunk)])

    npair = n // (2 * chunk)
    gather(0, bufs[0]).start()

    @pl.loop(0, npair)
    def _(i):
        gather(2 * i + 1, bufs[1]).start()
        finish(2 * i, bufs[0])

        @pl.when(i + 1 < npair)
        def _():
            gather(2 * i + 2, bufs[0]).start()

        finish(2 * i + 1, bufs[1])


def _sc_invert_gather(table, pos_flat, aff_flat, npairs, ne, t, cap):
    nc, nw, lanes, mesh, params = _sc_setup()
    split = max(1, nw // npairs)
    seg = cap // split
    per_w = npairs * split // nw
    width = table.shape[1]

    @functools.partial(
        pl.kernel, mesh=mesh, compiler_params=params,
        out_type=[jax.ShapeDtypeStruct((npairs * cap, width), table.dtype),
                  jax.ShapeDtypeStruct((npairs * cap,), F32)],
        scratch_types=[pltpu.VMEM((t,), I32), pltpu.VMEM((t,), F32), pltpu.VMEM((seg,), I32), pltpu.VMEM((seg,), F32)]
        + _sc_gather_scratch(width, table.dtype),
    )
    def body(table_hbm, pos_hbm, aff_hbm, out_hbm, gate_hbm, pos_v, aff_v, idx_v, gate_v, *g):
        wid = _sc_worker_id(nc)

        @pl.loop(0, per_w)
        def _(kk):
            item = wid * per_w + kk
            p = item // split
            lo = (item % split) * seg
            tok0 = (p // ne) * t
            pltpu.sync_copy(pos_hbm.at[pl.ds(pl.multiple_of(p * t, t), t)], pos_v)
            pltpu.sync_copy(aff_hbm.at[pl.ds(pl.multiple_of(p * t, t), t)], aff_v)

            @pl.loop(0, t // lanes)
            def _(i):
                v = pos_v[pl.ds(pl.multiple_of(i * lanes, lanes), lanes)] - lo
                tok = lax.iota(I32, lanes) + i * lanes
                plsc.store_scatter(idx_v, [v], tok, mask=(v >= 0) & (v < seg))

            @pl.loop(0, seg // lanes)
            def _(j):
                sl = pl.ds(pl.multiple_of(j * lanes, lanes), lanes)
                ii = idx_v[sl]
                gate_v[sl] = plsc.load_gather(aff_v, [ii])
                idx_v[sl] = ii + tok0

            row0 = p * cap + lo
            pltpu.sync_copy(gate_v, gate_hbm.at[pl.ds(pl.multiple_of(row0, seg), seg)])
            _sc_gather_chunks(table_hbm, out_hbm, idx_v, (g[0:3], g[3:6]), row0, seg, lanes)

    return body(table, pos_flat, aff_flat)


def _sc_assignment_order(pos_flat, base_flat, bsz, ne, t, cap):
    nc, nw, lanes, mesh, params = _sc_setup()
    na = ne * cap
    per_b = nw // bsz
    rng = na // per_b

    @functools.partial(
        pl.kernel, mesh=mesh, compiler_params=params,
        out_type=[jax.ShapeDtypeStruct((bsz * na,), I32), jax.ShapeDtypeStruct((bsz * na,), I32)],
        scratch_types=[pltpu.VMEM((t,), I32), pltpu.VMEM((t,), I32), pltpu.VMEM((t,), I32),
                       pltpu.VMEM((rng,), I32), pltpu.VMEM((rng,), I32)],
    )
    def body(pos_hbm, base_hbm, perm_hbm, tok_hbm, pos_v, base_v, rank_v, perm_v, tok_v):
        wid = _sc_worker_id(nc)
        b = wid // per_b
        a0 = (wid % per_b) * rng
        pltpu.sync_copy(base_hbm.at[pl.ds(pl.multiple_of(b * t, t), t)], base_v)

        @pl.loop(0, t // lanes)
        def _(i):
            rank_v[pl.ds(pl.multiple_of(i * lanes, lanes), lanes)] = jnp.zeros((lanes,), I32)

        @pl.loop(0, ne)
        def _(e):
            p = b * ne + e
            pltpu.sync_copy(pos_hbm.at[pl.ds(pl.multiple_of(p * t, t), t)], pos_v)

            @pl.loop(0, t // lanes)
            def _(i):
                sl = pl.ds(pl.multiple_of(i * lanes, lanes), lanes)
                v = pos_v[sl]
                r = rank_v[sl]
                a = base_v[sl] + r - a0
                sel = v >= 0
                mine = sel & (a >= 0) & (a < rng)
                plsc.store_scatter(perm_v, [a], v + p * cap, mask=mine)
                plsc.store_scatter(tok_v, [a], lax.iota(I32, lanes) + i * lanes, mask=mine)
                rank_v[sl] = r + jnp.where(sel, 1, 0)

        dst = pl.ds(pl.multiple_of(b * na + a0, rng), rng)
        pltpu.sync_copy(perm_v, perm_hbm.at[dst])
        pltpu.sync_copy(tok_v, tok_hbm.at[dst])

    return body(pos_flat, base_flat)


def _sc_gather_rows(table, idx):
    nc, nw, lanes, mesh, params = _sc_setup()
    n = idx.shape[0]
    per_w = n // nw
    width = table.shape[1]
    chunk = SC_GATHER_ROWS

    @functools.partial(
        pl.kernel, mesh=mesh, compiler_params=params,
        out_type=jax.ShapeDtypeStruct((n, width), table.dtype),
        scratch_types=[pltpu.VMEM((per_w,), I32)] + _sc_gather_scratch(width, table.dtype),
    )
    def body(table_hbm, idx_hbm, out_hbm, idx_v, *g):
        row0 = _sc_worker_id(nc) * per_w
        pltpu.sync_copy(idx_hbm.at[pl.ds(pl.multiple_of(row0, per_w), per_w)], idx_v)
        _sc_gather_chunks(table_hbm, out_hbm, idx_v, (g[0:3], g[3:6]), row0, per_w, lanes)

    return body(table, idx)


def _moe1_body(xs_ref, gate_ref, wg_ref, wu_ref, wd_ref, y_ref):
    xs = _unpack_halves(xs_ref[0])
    hg = _dot(xs, wg_ref[0].astype(BF16))
    hu = _dot(xs, wu_ref[0].astype(BF16))
    h = (_silu(hg) * hu).astype(BF16)
    y = _dot(h, wd_ref[0].astype(BF16))
    gt = gate_ref[0].T
    y = jnp.concatenate([y[k * LANES:(k + 1) * LANES] * gt[:, k:k + 1] for k in range(gt.shape[1])], axis=0)
    y_ref[0] = _pack_halves(y)


def _moe_experts(xs, gate, wg, wu, wd, bsz):
    npairs, cap, half = xs.shape
    ne, d, f = wg.shape
    pair = lambda b, e: (b * ne + e, 0, 0)
    expert = lambda b, e: (e, 0, 0)
    return pl.pallas_call(
        _moe1_body,
        grid=(bsz, ne),
        in_specs=[pl.BlockSpec((1, cap, half), pair),
                  pl.BlockSpec((1, cap // LANES, LANES), pair),
                  pl.BlockSpec((1, d, f), expert),
                  pl.BlockSpec((1, d, f), expert),
                  pl.BlockSpec((1, f, d), expert)],
        out_specs=pl.BlockSpec((1, cap, half), pair),
        out_shape=jax.ShapeDtypeStruct((npairs, cap, half), I32),
        compiler_params=_params(("arbitrary", "arbitrary")),
        name="moe_experts",
    )(xs, gate, wg, wu, wd)


def _combine_body(abase_ref, ys_hbm, tok_ref, x1_ref, g2_ref, fw_ref, o_ref, buf_ref, acc_ref, sem_ref, cnt_ref, *, na):
    n = ONEHOT_BLK
    ring = buf_ref.shape[0]
    nch = na // n
    b, i = pl.program_id(0), pl.program_id(1)

    @pl.when(i == 0)
    def _():
        cnt_ref[0] = 0
        cnt_ref[1] = 0

    def copy(c):
        slot = c % ring
        row0 = pl.multiple_of(b * na + c * n, n)
        return pltpu.make_async_copy(ys_hbm.at[pl.ds(row0, n)], buf_ref.at[slot], sem_ref.at[slot])

    for j in range(x1_ref.shape[1] // n):
        blk = i * (x1_ref.shape[1] // n) + j
        lo = abase_ref[b, blk]
        hi = abase_ref[b, blk + 1]
        c0 = lo // n
        c1 = jnp.where(hi > lo, (hi + n - 1) // n, c0)
        acc_ref[...] = jnp.zeros(acc_ref.shape, F32)
        tok_ids = lax.broadcasted_iota(I32, (n, n), 0) + blk * n

        def step(c, carry, tok_ids=tok_ids):
            started = cnt_ref[0]
            ahead = jnp.minimum(c + ring, nch)

            def start(k, _):
                copy(k).start()
                return 0

            lax.fori_loop(started, ahead, start, 0)
            cnt_ref[0] = jnp.maximum(started, ahead)

            @pl.when(cnt_ref[1] <= c)
            def _():
                copy(c).wait()
                cnt_ref[1] = c + 1

            seg = jnp.where(tok_ids == tok_ref[0, pl.ds(c, 1), :], 1.0, 0.0).astype(BF16)
            acc_ref[...] += _dot(seg, _unpack_halves(buf_ref[c % ring]))
            return carry

        lax.fori_loop(c0, c1, step, 0)
        rows = slice(j * n, (j + 1) * n)
        x2 = x1_ref[0, rows, :] + g2_ref[0] * acc_ref[...]
        o_ref[0, rows, :] = x2 * lax.rsqrt(jnp.mean(x2 * x2, axis=-1, keepdims=True) + NORM_EPS) * fw_ref[...]


def _combine_final(abase, ys, tok, x1, g2, fw, na):
    bsz, t, d = x1.shape
    n = ONEHOT_BLK
    tb = min(COMBINE_BLOCKS * n, t)
    grid_spec = pltpu.PrefetchScalarGridSpec(
        num_scalar_prefetch=1,
        grid=(bsz, t // tb),
        in_specs=[pl.BlockSpec(memory_space=pl.ANY),
                  pl.BlockSpec((1, na // n, n), lambda b, i, a: (b, 0, 0)),
                  pl.BlockSpec((1, tb, d), lambda b, i, a: (b, i, 0)),
                  pl.BlockSpec((1, 1, d), lambda b, i, a: (b, 0, 0)),
                  pl.BlockSpec((1, d), lambda b, i, a: (0, 0))],
        out_specs=pl.BlockSpec((1, tb, d), lambda b, i, a: (b, i, 0)),
        scratch_shapes=[pltpu.VMEM((COMBINE_RING, n, d // 2), I32), pltpu.VMEM((n, d), F32),
                        pltpu.SemaphoreType.DMA((COMBINE_RING,)), pltpu.SMEM((2,), I32)],
    )
    return pl.pallas_call(
        functools.partial(_combine_body, na=na),
        grid_spec=grid_spec,
        out_shape=jax.ShapeDtypeStruct((bsz, t, d), F32),
        compiler_params=_params(("arbitrary", "arbitrary")),
        name="combine_final",
    )(abase, ys, tok, x1, g2, fw)


def kernel(x, c, ctx, c_ctx, ada_w, ada_b, norm1_w, w_in, conv_w, conv_b, hg_lb_logits, ml_gate_b,
           hg_norm_w, ml_norm_w, w_out, norm2_w, router_w, exp_w_gate, exp_w_up, exp_w_down, final_norm_w):
    bsz, t, d = x.shape
    nctx = ctx.shape[1]
    assert ada_w.shape[0] == 1, "single-layer block"
    assert nctx == SCAN_STEP and t % SCAN_STEP == 0 and bsz + 1 <= SUBLANES
    tall = t + nctx
    nx = t // SCAN_STEP
    kw = HEADS_W
    ne = router_w.shape[-1]
    cap = EC_CAPACITY * t // ne
    tm = min(512, t)

    rows = jnp.concatenate([c, c_ctx[None], jnp.zeros((SUBLANES - 1 - bsz, d), F32)], axis=0)
    mod = _modulation(rows, ada_w[0], ada_b[0][None])
    mx = [m[:, None, :] for m in jnp.split(mod[:bsz], 6, axis=-1)]
    mc = [m[:, None, :] for m in jnp.split(mod[bsz:bsz + 1], 6, axis=-1)]
    sh1, sc1, g1, sh2, sc2, g2 = mx
    csh1, csc1 = mc[0], mc[1]

    main_w = 9 * kw
    w_main = w_in[0][:, :main_w].astype(BF16)
    w_gt = w_in[0][:, main_w:].T
    gate_b = ml_gate_b[0][:, None]
    nw1 = norm1_w[0][None]
    outs = _inproj(x, sc1, sh1, nw1, w_main, w_gt, gate_b, hg_lb_logits, tall, min(2 * tm, t), 0)
    outs = _inproj(ctx, csc1, csh1, nw1, w_main, w_gt, gate_b, hg_lb_logits, tall, nctx, t // nctx, prev=outs)
    hgq, hgv, hgg, hgk, hglf, mlqk_pre, mlv, mlo, gates = outs

    mlqk = _conv(mlqk_pre, conv_w[0], conv_b[0][None], t, nctx)
    hg_f, hg_b = _hgrn2(hgq, hgk, hgv, hglf, nx)
    ml_f, ml_b = _mlstm(mlqk, mlv, gates, nx)

    x1, vpk, aff = _mixer_out(hg_f, hg_b, ml_f, ml_b, hgg, mlo, x, g1, sc2, sh2,
                             hg_norm_w[0][None], ml_norm_w[0][None], w_out[0].astype(BF16),
                             norm2_w[0][None], router_w[0].T, min(2 * tm, t))

    pos4, base = _topk(aff.reshape(bsz, ne, t // LANES, LANES), cap)
    na = ne * cap
    pos_flat = pos4.reshape(-1)
    xs, gate = _sc_invert_gather(vpk.reshape(bsz * t, d // 2), pos_flat, aff.reshape(-1), bsz * ne, ne, t, cap)
    perm, tok = _sc_assignment_order(pos_flat, base.reshape(-1), bsz, ne, t, cap)
    y = _moe_experts(xs.reshape(bsz * ne, cap, d // 2), gate.reshape(bsz * ne, cap // LANES, LANES),
                     exp_w_gate[0], exp_w_up[0], exp_w_down[0], bsz)
    ys = _sc_gather_rows(y.reshape(bsz * na, d // 2), perm)
    abase = jnp.concatenate([base.reshape(bsz, t)[:, ::ONEHOT_BLK], jnp.full((bsz, 1), na, I32)], axis=1)
    return _combine_final(abase, ys, tok.reshape(bsz, na // ONEHOT_BLK, ONEHOT_BLK), x1, g2, final_norm_w[None], na)
```

```python
import dataclasses
import functools

import numpy as np
import jax
import jax.numpy as jnp
from jax import lax
from jax.experimental import pallas as pl
from jax.experimental.pallas import tpu as pltpu
from jax.experimental.pallas import tpu_sc as plsc

F32 = jnp.float32
BF16 = jnp.bfloat16
I32 = jnp.int32
HIGHEST = lax.Precision.HIGHEST
NORM_EPS = 1e-6
LOG2_E = 1.4426950408889634

LANES = 128
SUBLANES = 8
HEAD_DIM = 128
N_HEADS = 4
HEADS_W = N_HEADS * HEAD_DIM
GATE_ROWS = 2 * N_HEADS
GRID_W = 64
MOD_COL_TILES = 4
EC_CAPACITY = 2
HG_CHUNK = 64
HG_VPU_LEVEL_MIN = 4
ML_CHUNK = 256
SCAN_STEP = 256
SCAN_SAMPLES = 2
ONEHOT_BLK = 256
CONV_HALO = 72
CONV_ROWS = 512
SC_GATHER_ROWS = 64
COMBINE_RING = 6
COMBINE_BLOCKS = 2
TOPK_BISECTIONS = 64
VMEM_LIMIT = 56 * 1024 * 1024

_NT = (((1,), (1,)), ((), ()))
_TN = (((0,), (0,)), ((), ()))


def _dot(a, b, dims=None, precision=None):
    if dims is None:
        return jnp.dot(a, b, preferred_element_type=F32, precision=precision)
    return lax.dot_general(a, b, dims, preferred_element_type=F32, precision=precision)


def _dot_split(w, x, x1):
    n = w.shape[0]
    x2 = (x - x1.astype(F32)).astype(BF16)
    w1 = w.astype(BF16)
    wr = w - w1.astype(F32)
    w2 = wr.astype(BF16)
    w3 = (wr - w2.astype(F32)).astype(BF16)
    pa = _dot(jnp.concatenate([w1, w2, w3], axis=0), x1, _NT)
    pb = _dot(jnp.concatenate([w1, w2], axis=0), x2, _NT)
    return pa[0:n] + pa[n:2 * n] + pa[2 * n:] + pb[0:n] + pb[n:]


def _sigmoid(x):
    return jax.nn.sigmoid(x)


def _pack_halves(x):
    w = x.shape[-1] // 2
    bits = lax.bitcast_convert_type(x.astype(BF16).astype(F32), I32)
    return lax.shift_right_logical(bits[:, :w], 16) | bits[:, w:]


def _unpack_halves(p):
    lo = lax.bitcast_convert_type(lax.shift_left(p, 16), F32)
    hi = lax.bitcast_convert_type(p & jnp.int32(-65536), F32)
    return jnp.concatenate([lo, hi], axis=-1).astype(BF16)


def _silu(x):
    return x * jax.nn.sigmoid(x)


def _params(sem, flags=None):
    return pltpu.CompilerParams(dimension_semantics=sem, vmem_limit_bytes=VMEM_LIMIT, flags=flags)


def _mod_body(r_ref, w_ref, b_ref, o_ref):
    r = r_ref[...]
    o_ref[...] = _dot(_silu(r), w_ref[...], precision=HIGHEST) + b_ref[...]


def _modulation(rows, w, b):
    d, n = w.shape
    nrows = rows.shape[0]
    tn = n // MOD_COL_TILES
    return pl.pallas_call(
        _mod_body,
        grid=(MOD_COL_TILES,),
        in_specs=[pl.BlockSpec((nrows, d), lambda j: (0, 0)),
                  pl.BlockSpec((d, tn), lambda j: (0, j)),
                  pl.BlockSpec((1, tn), lambda j: (0, j))],
        out_specs=pl.BlockSpec((nrows, tn), lambda j: (0, j)),
        out_shape=jax.ShapeDtypeStruct((nrows, n), F32),
        compiler_params=_params(("arbitrary",)),
        name="modulation",
    )(rows, w, b)


def _log_sigmoid(x):
    return jnp.minimum(x, 0.0) - jnp.log(1.0 + jnp.exp(-jnp.abs(x)))


def _inproj_body(x_ref, sc_ref, sh_ref, nw_ref, w_ref, wg_ref, gb_ref, lbl_ref, *refs):
    hgq_ref, hgv_ref, hgg_ref, hgk_ref, hglf_ref, mlqk_ref, mlv_ref, mlo_ref, gates_ref = refs[-9:]
    kw = HEADS_W
    x = x_ref[0]
    y = x * lax.rsqrt(jnp.mean(x * x, axis=-1, keepdims=True) + NORM_EPS) * nw_ref[...]
    u = y * (1.0 + sc_ref[0]) + sh_ref[0]
    ub = u.astype(BF16)

    def proj(c0, c1):
        return _dot(ub, w_ref[:, c0:c1])

    hgq_ref[0] = _silu(proj(0, kw)).astype(BF16)
    hgv_ref[0] = proj(kw, 2 * kw).astype(BF16)
    hgg_ref[0] = _silu(proj(2 * kw, 3 * kw)).astype(BF16)

    lbl = lbl_ref[...]
    mx = jnp.max(lbl, axis=0)
    ex = jnp.exp(lbl - mx[None])
    lb = ex[0] / jnp.sum(ex, axis=0)
    for d in range(2):
        p = proj((3 + d) * kw, (4 + d) * kw)
        lbd = lb[d:d + 1]
        f = lbd + (1.0 - lbd) * _sigmoid(p)
        hgk_ref[0, :, d * kw:(d + 1) * kw] = (1.0 - f).astype(BF16)
        hglf_ref[0, :, d * kw:(d + 1) * kw] = jnp.log(f)

    mlqk_ref[0, :, 0:kw] = proj(5 * kw, 6 * kw)
    mlqk_ref[0, :, kw:2 * kw] = proj(6 * kw, 7 * kw)
    mlv_ref[0] = proj(7 * kw, 8 * kw).astype(BF16)
    mlo_ref[0] = _sigmoid(proj(8 * kw, 9 * kw)).astype(BF16)

    g = _dot_split(wg_ref[...], u, ub) + gb_ref[...]
    row = lax.broadcasted_iota(I32, g.shape, 0)
    gates_ref[0] = jnp.where((row % GATE_ROWS) >= N_HEADS, _log_sigmoid(g), g)


def _inproj_shapes(bsz, tall):
    kw = HEADS_W
    return [
        jax.ShapeDtypeStruct((bsz, tall, kw), BF16),
        jax.ShapeDtypeStruct((bsz, tall, kw), BF16),
        jax.ShapeDtypeStruct((bsz, tall, kw), BF16),
        jax.ShapeDtypeStruct((bsz, tall, 2 * kw), BF16),
        jax.ShapeDtypeStruct((bsz, tall, 2 * kw), F32),
        jax.ShapeDtypeStruct((bsz, tall, 2 * kw), F32),
        jax.ShapeDtypeStruct((bsz, tall, kw), BF16),
        jax.ShapeDtypeStruct((bsz, tall, kw), BF16),
        jax.ShapeDtypeStruct((bsz, 2 * GATE_ROWS, tall), F32),
    ]


def _inproj(tokens, scale, shift, nw, w_main, w_gt, gate_b, lb_logits, tall, tm, blk0, prev=None):
    bsz, n, d = tokens.shape
    kw = HEADS_W
    nt = n // tm
    per_sample = scale.shape[0] == bsz
    mod_map = (lambda b, i: (b, 0, 0)) if per_sample else (lambda b, i: (0, 0, 0))
    const2 = lambda b, i: (0, 0)
    in_specs = [
        pl.BlockSpec((1, tm, d), lambda b, i: (b, i, 0)),
        pl.BlockSpec((1, 1, d), mod_map),
        pl.BlockSpec((1, 1, d), mod_map),
        pl.BlockSpec((1, d), const2),
        pl.BlockSpec(w_main.shape, const2, pipeline_mode=pl.Buffered(1)),
        pl.BlockSpec(w_gt.shape, const2),
        pl.BlockSpec(gate_b.shape, const2),
        pl.BlockSpec(lb_logits.shape, lambda b, i: (0, 0, 0)),
    ]
    args = [tokens, scale, shift, nw, w_main, w_gt, gate_b, lb_logits]
    aliases = {}
    if prev is not None:
        for k, a in enumerate(prev):
            in_specs.append(pl.BlockSpec(memory_space=pl.ANY))
            aliases[len(args)] = k
            args.append(a)
    row_map = lambda b, i: (b, blk0 + i, 0)
    widths = [kw, kw, kw, 2 * kw, 2 * kw, 2 * kw, kw, kw]
    out_specs = [pl.BlockSpec((1, tm, w), row_map) for w in widths]
    out_specs.append(pl.BlockSpec((1, 2 * GATE_ROWS, tm), lambda b, i: (b, 0, blk0 + i)))
    return pl.pallas_call(
        _inproj_body,
        grid=(bsz, nt),
        in_specs=in_specs,
        out_specs=out_specs,
        out_shape=_inproj_shapes(bsz, tall),
        input_output_aliases=aliases,
        compiler_params=_params(("arbitrary", "arbitrary")),
        name="inproj_ctx" if prev is not None else "inproj_x",
    )(*args)


def _conv_body(x_ref, w_ref, b_ref, o_ref, pad_ref, cpad_ref, *, t, nctx, scale_from):
    halo = CONV_HALO
    rows = CONV_ROWS
    win = rows + 2 * halo
    ch = x_ref.shape[-1]
    scale = jnp.where(pl.program_id(1) >= scale_from, HEAD_DIM ** -0.5, 1.0).astype(F32)
    w = w_ref[...]
    bias = b_ref[...]

    pad_ref[0:halo, :] = jnp.zeros((halo, ch), F32)
    pad_ref[halo + t:halo + t + halo, :] = jnp.zeros((halo, ch), F32)
    pad_ref[halo:halo + t, :] = x_ref[0, 0:t, :]
    col = (lax.broadcasted_iota(I32, (win, ch), 0) + (GRID_W - halo % GRID_W)) % GRID_W
    left_ok = col > 0
    right_ok = col < GRID_W - 1

    def chunk(c, carry):
        o = pl.multiple_of(c * rows, rows)
        xw = pad_ref[pl.ds(o, win), :]
        xm = jnp.where(left_ok, pltpu.roll(xw, 1, 0), 0.0)
        xp = jnp.where(right_ok, pltpu.roll(xw, win - 1, 0), 0.0)
        def taps(dr, lo):
            sl = slice(lo, lo + rows)
            return xm[sl] * w[dr, 0:1] + xw[sl] * w[dr, 1:2] + xp[sl] * w[dr, 2:3]

        y = taps(1, halo) + taps(0, halo - GRID_W) + taps(2, halo + GRID_W)
        o_ref[0, pl.ds(o, rows), :] = (_silu(y + bias) * scale).astype(o_ref.dtype)
        return carry

    lax.fori_loop(0, t // rows, chunk, 0)

    pad = SUBLANES
    cpad_ref[0:pad, :] = jnp.zeros((pad, ch), F32)
    cpad_ref[pad + nctx:2 * pad + nctx, :] = jnp.zeros((pad, ch), F32)
    cpad_ref[pad:pad + nctx, :] = x_ref[0, t:t + nctx, :]
    xw = cpad_ref[...]
    n = nctx + 2 * pad
    y = (pltpu.roll(xw, 1, 0) * w[1, 0:1] + xw * w[1, 1:2] + pltpu.roll(xw, n - 1, 0) * w[1, 2:3])[pad:pad + nctx]
    o_ref[0, t:t + nctx, :] = (_silu(y + bias) * scale).astype(o_ref.dtype)


def _conv(qk_pre, conv_w, conv_b, t, nctx):
    bsz, tall, c = qk_pre.shape
    ch = 128
    body = functools.partial(_conv_body, t=t, nctx=nctx, scale_from=(c // 2) // ch)
    return pl.pallas_call(
        body,
        grid=(bsz, c // ch),
        in_specs=[pl.BlockSpec((1, tall, ch), lambda b, j: (b, 0, j)),
                  pl.BlockSpec((3, 3, ch), lambda b, j: (0, 0, j)),
                  pl.BlockSpec((1, ch), lambda b, j: (0, j))],
        out_specs=pl.BlockSpec((1, tall, ch), lambda b, j: (b, 0, j)),
        out_shape=jax.ShapeDtypeStruct((bsz, tall, c), BF16),
        scratch_shapes=[pltpu.VMEM((t + 2 * CONV_HALO, ch), F32),
                        pltpu.VMEM((nctx + 2 * SUBLANES, ch), F32)],
        compiler_params=_params(("arbitrary", "arbitrary")),
        name="qk_conv",
    )(qk_pre, conv_w, conv_b)


def _fwd_blk(s, nx):
    return jnp.where(s == 0, nx, s - 1)


def _bwd_blk(s, nx):
    return jnp.where(s == 0, nx, nx - s)


def _hg_constants(rev):
    c = HG_CHUNK
    i = np.arange(c)[:, None]
    j = np.arange(c)[None, :]
    blocks = [(j >= i) if rev else (j <= i)]
    masks = [i == j]
    m = c // 2
    while m >= 1:
        b0 = (i // (2 * m)) * (2 * m)
        same = (i // (2 * m)) == (j // (2 * m))
        if rev:
            beta = b0 + m
            qrow = (i % (2 * m)) < m
            g = np.where(qrow, (j >= i) & (j < beta), (j >= beta) & (j < i))
            mask = same & qrow & ((j % (2 * m)) >= m)
        else:
            beta = b0 + m - 1
            qrow = (i % (2 * m)) >= m
            g = np.where(qrow, (j > beta) & (j <= i), (j > i) & (j <= beta))
            mask = same & qrow & ((j % (2 * m)) < m)
        if m < HG_VPU_LEVEL_MIN:
            blocks.append(g)
        masks.append(mask)
        m //= 2
    g = np.concatenate(blocks, axis=0).astype(np.float32)
    g3 = np.concatenate([g, g, g], axis=1)
    m2 = np.concatenate([np.stack(masks), np.stack(masks)], axis=2)
    return (jnp.asarray(g3, BF16), jnp.asarray(m2, F32))


def _hg_level_decay(a, m, rev):
    c = a.shape[0]
    parts = []
    for b0 in range(0, c, 2 * m):
        beta = b0 + m if rev else b0 + m - 1
        ref = a[beta:beta + 1]
        if m % 8 == 0:
            first, second = a[b0:b0 + m], a[b0 + m:b0 + 2 * m]
            parts += [first - ref, ref - second] if rev else [ref - first, second - ref]
        else:
            d = a[b0:b0 + 2 * m] - ref
            parts.append(jnp.minimum(d, -d))
    return jnp.concatenate(parts, axis=0)


def _block_diag(x, zero):
    w = x.shape[1] // 2
    return jnp.concatenate([jnp.concatenate([x[:, :w], zero], axis=1),
                            jnp.concatenate([zero, x[:, w:]], axis=1)], axis=0)


def _hg_chunk(dirs):
    c = HG_CHUNK
    w = 2 * HEAD_DIM
    zero = jnp.zeros((c, HEAD_DIM), BF16)
    units = []
    for rev, r0, bb, q_ref, k_ref, v_ref, lf_ref, g_ref, msk_ref, o_ref, st_ref in dirs:
        rows = pl.ds(r0, c)
        lf = lf_ref[bb, rows, :] * LOG2_E
        p1 = lf.astype(BF16)
        r1 = lf - p1.astype(F32)
        p2 = r1.astype(BF16)
        p3 = (r1 - p2.astype(F32)).astype(BF16)
        dall = _dot(g_ref[...], jnp.concatenate([p1, p2, p3], axis=0))
        for hp in range(N_HEADS // 2):
            cs = slice(hp * w, (hp + 1) * w)
            units.append(dict(rev=rev, rows=rows, cs=cs, hp=hp, bb=bb, msk_ref=msk_ref, o_ref=o_ref, st_ref=st_ref,
                              q=q_ref[bb, rows, cs], k=k_ref[bb, rows, cs], v=v_ref[bb, rows, cs], dall=dall[:, cs]))
    for u in units:
        msk_ref = u["msk_ref"]
        nlev = msk_ref.shape[0] - 1
        qk = u["q"].astype(F32) * u["k"].astype(F32)
        att = jnp.concatenate([jnp.sum(qk[:, :HEAD_DIM], axis=1, keepdims=True) * msk_ref[0][:, :c],
                               jnp.sum(qk[:, HEAD_DIM:], axis=1, keepdims=True) * msk_ref[0][:, c:]], axis=1)
        a = u["dall"][0:c]
        row = 1
        for l in range(nlev):
            m = c >> (l + 1)
            if m >= HG_VPU_LEVEL_MIN:
                dec = _hg_level_decay(a, m, u["rev"])
            else:
                dec = u["dall"][row * c:(row + 1) * c]
                row += 1
            e = jnp.exp2(dec).astype(BF16)
            att = att + _dot(u["q"] * e, _block_diag(u["k"] * e, zero), _NT) * msk_ref[l + 1]
        u["att"] = att.astype(BF16)
    for u in units:
        a = u["dall"][0:c]
        u["a_tot"] = a[0:1] if u["rev"] else a[c - 1:c]
        u["st"] = [u["st_ref"][2 * u["hp"] + i] for i in range(2)]
        zf = jnp.zeros((HEAD_DIM, HEAD_DIM), BF16)
        st2 = jnp.concatenate([jnp.concatenate([u["st"][0].astype(BF16), zf], axis=1),
                               jnp.concatenate([zf, u["st"][1].astype(BF16)], axis=1)], axis=0)
        qbar = (u["q"].astype(F32) * jnp.exp2(a)).astype(BF16)
        u["o"] = _dot(u["att"], _block_diag(u["v"], zero)) + _dot(qbar, st2, _NT)
        u["khat"] = (u["k"].astype(F32) * jnp.exp2(u["a_tot"] - a)).astype(BF16)
    for u in units:
        u["o_ref"][u["bb"], u["rows"], u["cs"]] = u["o"].astype(BF16)
        for i in range(2):
            hs = slice(i * HEAD_DIM, (i + 1) * HEAD_DIM)
            upd = _dot(u["v"][:, hs], u["khat"][:, hs], _TN)
            u["st_ref"][2 * u["hp"] + i] = u["st"][i] * jnp.exp2(u["a_tot"][:, hs]) + upd


def _ml_constants(rev):
    k = np.arange(ML_CHUNK)
    tri = (k[:, None] >= k[None, :]) if rev else (k[:, None] <= k[None, :])
    return jnp.asarray(np.concatenate([tri, tri, tri], axis=0), BF16)


def _ml_chunk(dirs):
    c = ML_CHUNK
    ii = lax.broadcasted_iota(I32, (c, c), 0)
    jj = lax.broadcasted_iota(I32, (c, c), 1)
    ones = jnp.ones((c, HEAD_DIM), BF16)
    units = []
    for rev, r0, d, bb, q_ref, k_ref, v_ref, g_ref, tri3_ref, o_ref, st_ref, m_ref in dirs:
        gates = g_ref[bb, :, pl.ds(r0, c)] * LOG2_E
        p1 = gates.astype(BF16)
        r1 = gates - p1.astype(F32)
        p2 = r1.astype(BF16)
        p3 = (r1 - p2.astype(F32)).astype(BF16)
        csum = _dot(jnp.concatenate([p1, p2, p3], axis=1), tri3_ref[...])
        for h in range(N_HEADS):
            cs = slice(h * HEAD_DIM, (h + 1) * HEAD_DIM)
            u = dict(rev=rev, o_ref=o_ref, bb=bb, rows=pl.ds(r0, c), cs=cs, st_ref=st_ref, m_ref=m_ref, h=h)
            u["qb"] = q_ref[bb, pl.ds(r0, c), cs]
            u["kb"] = k_ref[bb, pl.ds(r0, c), cs]
            u["v1"] = jnp.concatenate([v_ref[bb, pl.ds(r0, c), cs], ones], axis=1)
            gi = d * GATE_ROWS + h
            irow = gates[gi:gi + 1]
            u["brow"] = csum[gi + N_HEADS:gi + N_HEADS + 1]
            u["rrow"] = irow - u["brow"]
            units.append(u)
    yield
    for u in units:
        u["st"] = u["st_ref"][u["h"]]
        a = _dot(jnp.concatenate([u["kb"], u["st"].astype(BF16)], axis=0), u["qb"], _NT)
        u["s"] = a[:c]
        u["sq"] = a[c:]
    yield
    for u in units:
        last = 0 if u["rev"] else c - 1
        tri_t = (ii >= jj) if u["rev"] else (ii <= jj)
        rcol = jnp.concatenate([u["rrow"], jnp.zeros((SUBLANES - 1, c), F32)], axis=0).T[:, 0:1]
        u["mprev"] = u["m_ref"][u["h"]][:, 0:1]
        rmat = jnp.where(tri_t, rcol, -jnp.inf)
        u["grow"] = jnp.maximum(jnp.max(rmat, axis=0, keepdims=True), u["mprev"])
        qk = (u["s"] * jnp.exp2(rmat - u["grow"])).astype(BF16)
        blast = u["brow"][:, last:last + 1]
        u["mnew"] = blast + u["grow"][:, last:last + 1]
        kh = (u["kb"].astype(F32) * jnp.exp2(blast + rcol - u["mnew"])).astype(BF16)
        u["ws"] = jnp.exp2(blast + u["mprev"] - u["mnew"])
        u["qkh"] = jnp.concatenate([qk, kh], axis=1)
    yield
    for u in units:
        u["nu"] = _dot(u["v1"], u["qkh"], _TN)
    yield
    for u in units:
        both = u["nu"][:, :c] + jnp.exp2(u["mprev"] - u["grow"]) * u["sq"]
        den = both[HEAD_DIM:HEAD_DIM + 1]
        inv = 1.0 / jnp.maximum(jnp.abs(den), jnp.exp2(-(u["brow"] + u["grow"])))
        u["o_ref"][u["bb"], u["rows"], u["cs"]] = (both[:HEAD_DIM] * inv).T.astype(BF16)
        u["st_ref"][u["h"]] = u["ws"] * u["st"] + u["nu"][:, c:]
        u["m_ref"][u["h"]] = jnp.broadcast_to(u["mnew"], (1, HEAD_DIM))


def _scan_body(hqf_ref, hkf_ref, hvf_ref, hlf_ref, hqb_ref, hkb_ref, hvb_ref, hlb_ref, gf_ref, mf_ref, gb_ref, mb_ref,
               mqf_ref, mkf_ref, mvf_ref, mgf_ref, mqb_ref, mkb_ref, mvb_ref, mgb_ref, tf_ref, tb_ref,
               hof_ref, hob_ref, mof_ref, mob_ref, hst_ref, mst_ref, mm_ref):
    @pl.when(pl.program_id(1) == 0)
    def _():
        hst_ref[...] = jnp.zeros(hst_ref.shape, F32)
        mst_ref[...] = jnp.zeros(mst_ref.shape, F32)
        mm_ref[...] = jnp.zeros(mm_ref.shape, F32)

    samples = range(hqf_ref.shape[0])
    ml_dirs = []
    for bb in samples:
        ml_dirs.append((False, 0, 0, bb, mqf_ref, mkf_ref, mvf_ref, mgf_ref, tf_ref, mof_ref,
                        mst_ref.at[0, bb], mm_ref.at[0, bb]))
        ml_dirs.append((True, 0, 1, bb, mqb_ref, mkb_ref, mvb_ref, mgb_ref, tb_ref, mob_ref,
                        mst_ref.at[1, bb], mm_ref.at[1, bb]))
    ml_stages = _ml_chunk(ml_dirs)
    nsub = SCAN_STEP // HG_CHUNK
    for c in range(nsub):
        dirs = []
        for bb in samples:
            dirs.append((False, c * HG_CHUNK, bb, hqf_ref, hkf_ref, hvf_ref, hlf_ref, gf_ref, mf_ref, hof_ref,
                         hst_ref.at[0, bb]))
            dirs.append((True, (nsub - 1 - c) * HG_CHUNK, bb, hqb_ref, hkb_ref, hvb_ref, hlb_ref, gb_ref, mb_ref, hob_ref,
                         hst_ref.at[1, bb]))
        _hg_chunk(dirs)
        next(ml_stages, None)
    for _ in ml_stages:
        pass


def _scans(hgq, hgk, hgv, hglf, mlqk, mlv, gates, nx):
    bsz, tall, kw = hgq.shape
    assert ML_CHUNK == SCAN_STEP
    steps = tall // SCAN_STEP
    cf = _hg_constants(False)
    cb = _hg_constants(True)
    tf, tb = _ml_constants(False), _ml_constants(True)
    nb = SCAN_SAMPLES if bsz % SCAN_SAMPLES == 0 else 1
    blk = (nb, SCAN_STEP, kw)
    gblk = (nb, 2 * GATE_ROWS, SCAN_STEP)
    fwd = lambda col: pl.BlockSpec(blk, lambda b, s: (b, _fwd_blk(s, nx), col))
    bwd = lambda col: pl.BlockSpec(blk, lambda b, s: (b, _bwd_blk(s, nx), col))
    const = lambda a: pl.BlockSpec(a.shape, lambda b, s: (0,) * a.ndim)
    in_specs = [fwd(0), fwd(0), fwd(0), fwd(0), bwd(0), bwd(1), bwd(0), bwd(1)]
    in_specs += [const(a) for a in cf + cb]
    in_specs += [fwd(0), fwd(1), fwd(0), pl.BlockSpec(gblk, lambda b, s: (b, 0, _fwd_blk(s, nx))),
                 bwd(0), bwd(1), bwd(0), pl.BlockSpec(gblk, lambda b, s: (b, 0, _bwd_blk(s, nx))), const(tf), const(tb)]
    out_sds = jax.ShapeDtypeStruct((bsz, tall, kw), BF16)
    return pl.pallas_call(
        _scan_body,
        grid=(bsz // nb, steps),
        in_specs=in_specs,
        out_specs=[fwd(0), bwd(0), fwd(0), bwd(0)],
        out_shape=[out_sds] * 4,
        scratch_shapes=[pltpu.VMEM((2, nb, N_HEADS, HEAD_DIM, HEAD_DIM), F32),
                        pltpu.VMEM((2, nb, N_HEADS, 2 * HEAD_DIM, HEAD_DIM), F32),
                        pltpu.VMEM((2, nb, N_HEADS, 1, HEAD_DIM), F32)],
        compiler_params=_params(("arbitrary", "arbitrary")),
        name="bidirectional_scans",
    )(hgq, hgk, hgv, hglf, hgq, hgk, hgv, hglf, *cf, *cb, mlqk, mlqk, mlv, gates, mlqk, mlqk, mlv, gates, tf, tb)


def _out_body(hof_ref, hob_ref, mhf_ref, mhb_ref, hgg_ref, mlo_ref, x_ref, g1_ref, sc2_ref, sh2_ref,
              hnw_ref, mnw_ref, wout_ref, n2w_ref, rwt_ref, x1_ref, vt_ref, aff_ref):
    hg = hof_ref[0].astype(F32) + hob_ref[0].astype(F32)
    ml = mhf_ref[0].astype(F32) + mhb_ref[0].astype(F32)
    hparts, mparts = [], []
    for h in range(N_HEADS):
        cs = slice(h * HEAD_DIM, (h + 1) * HEAD_DIM)
        t = hg[:, cs]
        hparts.append(t * lax.rsqrt(jnp.mean(t * t, axis=-1, keepdims=True) + NORM_EPS))
        t = ml[:, cs]
        t = t - jnp.mean(t, axis=-1, keepdims=True)
        mparts.append(t * lax.rsqrt(jnp.mean(t * t, axis=-1, keepdims=True) + NORM_EPS))
    hgn = jnp.concatenate(hparts, axis=-1) * hnw_ref[...] * hgg_ref[0].astype(F32)
    mln = jnp.concatenate(mparts, axis=-1) * mnw_ref[...] * mlo_ref[0].astype(F32)
    mix = jnp.concatenate([hgn, mln], axis=-1).astype(BF16)
    x1 = x_ref[0] + g1_ref[0] * _dot(mix, wout_ref[...])
    x1_ref[0] = x1
    v = x1 * lax.rsqrt(jnp.mean(x1 * x1, axis=-1, keepdims=True) + NORM_EPS) * n2w_ref[...]
    v = v * (1.0 + sc2_ref[0]) + sh2_ref[0]
    vt_ref[0] = _pack_halves(v)
    logits = _dot_split(rwt_ref[...], v, v.astype(BF16))
    ex = jnp.exp(logits - jnp.max(logits, axis=0, keepdims=True))
    aff_ref[0] = ex / jnp.sum(ex, axis=0, keepdims=True)


def _mixer_out(hg_f, hg_b, ml_f, ml_b, hgg, mlo, x, g1, sc2, sh2, hnw, mnw, w_out, n2w, rwt, tm):
    bsz, t, d = x.shape
    kw = HEADS_W
    ne = rwt.shape[0]
    row = lambda b, i: (b, i, 0)
    mod = lambda b, i: (b, 0, 0)
    const2 = lambda b, i: (0, 0)
    act = pl.BlockSpec((1, tm, kw), row)
    in_specs = [act, act, act, act, act, act,
                pl.BlockSpec((1, tm, d), row),
                pl.BlockSpec((1, 1, d), mod), pl.BlockSpec((1, 1, d), mod), pl.BlockSpec((1, 1, d), mod),
                pl.BlockSpec((1, kw), const2), pl.BlockSpec((1, kw), const2),
                pl.BlockSpec(w_out.shape, const2), pl.BlockSpec((1, d), const2), pl.BlockSpec(rwt.shape, const2)]
    return pl.pallas_call(
        _out_body,
        grid=(bsz, t // tm),
        in_specs=in_specs,
        out_specs=[pl.BlockSpec((1, tm, d), row),
                   pl.BlockSpec((1, tm, d // 2), row),
                   pl.BlockSpec((1, ne, tm), lambda b, i: (b, 0, i))],
        out_shape=[jax.ShapeDtypeStruct((bsz, t, d), F32),
                   jax.ShapeDtypeStruct((bsz, t, d // 2), I32),
                   jax.ShapeDtypeStruct((bsz, ne, t), F32)],
        compiler_params=_params(("arbitrary", "arbitrary")),
        name="mixer_out",
    )(hg_f, hg_b, ml_f, ml_b, hgg, mlo, x, g1, sc2, sh2, hnw, mnw, w_out, n2w, rwt)


def _prefix_count(maskf, u_ref, ones_ref, bl_ref):
    e, nb, ln = maskf.shape
    x = maskf.reshape(e * nb, ln)
    xb = x.astype(BF16)
    incl = _dot(xb, u_ref[...])
    tot = _dot(xb, ones_ref[...])
    off = _dot(bl_ref[...], tot.astype(BF16))
    return (incl - x + off).reshape(e, nb, ln), off.reshape(e, nb, ln)


def _topk_body(aff_ref, u_ref, ones_ref, bl_ref, bl1_ref, pos_ref, base_ref, *, cap):
    x = aff_ref[0]
    ne = x.shape[0]

    def count(m):
        return jnp.sum(jnp.sum(jnp.where(m, 1.0, 0.0), axis=1, keepdims=True), axis=2, keepdims=True)

    def halve(_, carry):
        lo, hi = carry
        mid = 0.5 * (lo + hi)
        up = count(x > mid) >= cap
        return jnp.where(up, mid, lo), jnp.where(up, hi, mid)

    lo, hi = lax.fori_loop(0, TOPK_BISECTIONS, halve,
                           (jnp.full((ne, 1, 1), -1.0, F32), jnp.full((ne, 1, 1), 1.0, F32)))
    gt = jnp.where(x > hi, 1.0, 0.0)
    eq = jnp.where(x > lo, 1.0, 0.0) - gt
    need = cap - count(x > hi)
    eq_rank, _ = _prefix_count(eq, u_ref, ones_ref, bl_ref)
    sel = gt + eq * jnp.where(eq_rank < need, 1.0, 0.0)
    pos, _ = _prefix_count(sel, u_ref, ones_ref, bl_ref)
    pos_ref[0] = jnp.where(sel > 0, pos, -1.0).astype(I32)
    nsel = jnp.sum(sel, axis=0)
    nb16 = nsel.astype(BF16)
    incl = _dot(nb16, u_ref[...])
    tot = _dot(nb16, ones_ref[...])
    off = _dot(bl1_ref[...], tot, precision=HIGHEST)
    base_ref[0] = (incl - nsel + off).astype(I32)


def _topk(aff4, cap):
    bsz, ne, nb, ln = aff4.shape
    k = np.arange(ln)
    u = jnp.asarray(k[:, None] <= k[None, :], BF16)
    ones = jnp.ones((ln, ln), BF16)
    r = np.arange(ne * nb)
    bl = jnp.asarray(((r[:, None] // nb) == (r[None, :] // nb)) & (r[None, :] < r[:, None]), BF16)
    r1 = np.arange(nb)
    bl1 = jnp.asarray(r1[None, :] < r1[:, None], F32)
    blk = pl.BlockSpec((1, ne, nb, ln), lambda b: (b, 0, 0, 0))
    const = lambda a: pl.BlockSpec(a.shape, lambda b: (0, 0))
    return pl.pallas_call(
        functools.partial(_topk_body, cap=cap),
        grid=(bsz,),
        in_specs=[blk, const(u), const(ones), const(bl), const(bl1)],
        out_specs=[blk, pl.BlockSpec((1, nb, ln), lambda b: (b, 0, 0))],
        out_shape=[jax.ShapeDtypeStruct((bsz, ne, nb, ln), I32), jax.ShapeDtypeStruct((bsz, nb, ln), I32)],
        compiler_params=_params(("arbitrary",)),
        name="expert_topk",
    )(aff4, u, ones, bl, bl1)


def _sc_setup():
    info = plsc.get_sparse_core_info()
    mesh = plsc.VectorSubcoreMesh(core_axis_name="c", subcore_axis_name="s")
    params = dataclasses.replace(pltpu.CompilerParams(), needs_layout_passes=False)
    return info.num_cores, info.num_cores * info.num_subcores, info.num_lanes, mesh, params


def _sc_worker_id(nc):
    return lax.axis_index("s") * nc + lax.axis_index("c")


def _sc_gather_scratch(width, dtype):
    one = [pltpu.VMEM((SC_GATHER_ROWS,), I32), pltpu.VMEM((SC_GATHER_ROWS, width), dtype), pltpu.SemaphoreType.DMA]
    return one + one


def _sc_gather_chunks(table_hbm, out_hbm, idx_v, bufs, out_row0, n, lanes):
    chunk = SC_GATHER_ROWS

    def gather(c, buf):
        ich_v, rows_v, sem = buf
        for q in range(chunk // lanes):
            src = pl.ds(pl.multiple_of(c * chunk + q * lanes, lanes), lanes)
            ich_v[pl.ds(q * lanes, lanes)] = idx_v[src]
        return pltpu.make_async_copy(table_hbm.at[ich_v], rows_v, sem)

    def finish(c, buf):
        ich_v, rows_v, sem = buf
        pltpu.make_async_copy(table_hbm.at[ich_v], rows_v, sem).wait()
        pltpu.sync_copy(rows_v, out_hbm.at[pl.ds(pl.multiple_of(out_row0 + c * chunk, chunk), chunk)])

    npair = n // (2 * chunk)
    gather(0, bufs[0]).start()

    @pl.loop(0, npair)
    def _(i):
        gather(2 * i + 1, bufs[1]).start()
        finish(2 * i, bufs[0])

        @pl.when(i + 1 < npair)
        def _():
            gather(2 * i + 2, bufs[0]).start()

        finish(2 * i + 1, bufs[1])


def _sc_invert_gather(table, pos_flat, aff_flat, npairs, ne, t, cap):
    nc, nw, lanes, mesh, params = _sc_setup()
    split = max(1, nw // npairs)
    seg = cap // split
    per_w = npairs * split // nw
    width = table.shape[1]

    @functools.partial(
        pl.kernel, mesh=mesh, compiler_params=params,
        out_type=[jax.ShapeDtypeStruct((npairs * cap, width), table.dtype),
                  jax.ShapeDtypeStruct((npairs * cap,), F32)],
        scratch_types=[pltpu.VMEM((t,), I32), pltpu.VMEM((t,), F32), pltpu.VMEM((seg,), I32), pltpu.VMEM((seg,), F32)]
        + _sc_gather_scratch(width, table.dtype),
    )
    def body(table_hbm, pos_hbm, aff_hbm, out_hbm, gate_hbm, pos_v, aff_v, idx_v, gate_v, *g):
        wid = _sc_worker_id(nc)

        @pl.loop(0, per_w)
        def _(kk):
            item = wid * per_w + kk
            p = item // split
            lo = (item % split) * seg
            tok0 = (p // ne) * t
            pltpu.sync_copy(pos_hbm.at[pl.ds(pl.multiple_of(p * t, t), t)], pos_v)
            pltpu.sync_copy(aff_hbm.at[pl.ds(pl.multiple_of(p * t, t), t)], aff_v)

            @pl.loop(0, t // lanes)
            def _(i):
                v = pos_v[pl.ds(pl.multiple_of(i * lanes, lanes), lanes)] - lo
                tok = lax.iota(I32, lanes) + i * lanes
                plsc.store_scatter(idx_v, [v], tok, mask=(v >= 0) & (v < seg))

            @pl.loop(0, seg // lanes)
            def _(j):
                sl = pl.ds(pl.multiple_of(j * lanes, lanes), lanes)
                ii = idx_v[sl]
                gate_v[sl] = plsc.load_gather(aff_v, [ii])
                idx_v[sl] = ii + tok0

            row0 = p * cap + lo
            pltpu.sync_copy(gate_v, gate_hbm.at[pl.ds(pl.multiple_of(row0, seg), seg)])
            _sc_gather_chunks(table_hbm, out_hbm, idx_v, (g[0:3], g[3:6]), row0, seg, lanes)

    return body(table, pos_flat, aff_flat)


def _sc_assignment_order(pos_flat, base_flat, bsz, ne, t, cap):
    nc, nw, lanes, mesh, params = _sc_setup()
    na = ne * cap
    per_b = nw // bsz
    rng = na // per_b

    @functools.partial(
        pl.kernel, mesh=mesh, compiler_params=params,
        out_type=[jax.ShapeDtypeStruct((bsz * na,), I32), jax.ShapeDtypeStruct((bsz * na,), I32)],
        scratch_types=[pltpu.VMEM((t,), I32), pltpu.VMEM((t,), I32), pltpu.VMEM((t,), I32),
                       pltpu.VMEM((rng,), I32), pltpu.VMEM((rng,), I32)],
    )
    def body(pos_hbm, base_hbm, perm_hbm, tok_hbm, pos_v, base_v, rank_v, perm_v, tok_v):
        wid = _sc_worker_id(nc)
        b = wid // per_b
        a0 = (wid % per_b) * rng
        pltpu.sync_copy(base_hbm.at[pl.ds(pl.multiple_of(b * t, t), t)], base_v)

        @pl.loop(0, t // lanes)
        def _(i):
            rank_v[pl.ds(pl.multiple_of(i * lanes, lanes), lanes)] = jnp.zeros((lanes,), I32)

        @pl.loop(0, ne)
        def _(e):
            p = b * ne + e
            pltpu.sync_copy(pos_hbm.at[pl.ds(pl.multiple_of(p * t, t), t)], pos_v)

            @pl.loop(0, t // lanes)
            def _(i):
                sl = pl.ds(pl.multiple_of(i * lanes, lanes), lanes)
                v = pos_v[sl]
                r = rank_v[sl]
                a = base_v[sl] + r - a0
                sel = v >= 0
                mine = sel & (a >= 0) & (a < rng)
                plsc.store_scatter(perm_v, [a], v + p * cap, mask=mine)
                plsc.store_scatter(tok_v, [a], lax.iota(I32, lanes) + i * lanes, mask=mine)
                rank_v[sl] = r + jnp.where(sel, 1, 0)

        dst = pl.ds(pl.multiple_of(b * na + a0, rng), rng)
        pltpu.sync_copy(perm_v, perm_hbm.at[dst])
        pltpu.sync_copy(tok_v, tok_hbm.at[dst])

    return body(pos_flat, base_flat)


def _sc_gather_rows(table, idx):
    nc, nw, lanes, mesh, params = _sc_setup()
    n = idx.shape[0]
    per_w = n // nw
    width = table.shape[1]
    chunk = SC_GATHER_ROWS

    @functools.partial(
        pl.kernel, mesh=mesh, compiler_params=params,
        out_type=jax.ShapeDtypeStruct((n, width), table.dtype),
        scratch_types=[pltpu.VMEM((per_w,), I32)] + _sc_gather_scratch(width, table.dtype),
    )
    def body(table_hbm, idx_hbm, out_hbm, idx_v, *g):
        row0 = _sc_worker_id(nc) * per_w
        pltpu.sync_copy(idx_hbm.at[pl.ds(pl.multiple_of(row0, per_w), per_w)], idx_v)
        _sc_gather_chunks(table_hbm, out_hbm, idx_v, (g[0:3], g[3:6]), row0, per_w, lanes)

    return body(table, idx)


def _moe1_body(xs_ref, gate_ref, wg_ref, wu_ref, wd_ref, y_ref):
    xs = _unpack_halves(xs_ref[0])
    hg = _dot(xs, wg_ref[0].astype(BF16))
    hu = _dot(xs, wu_ref[0].astype(BF16))
    h = (_silu(hg) * hu).astype(BF16)
    y = _dot(h, wd_ref[0].astype(BF16))
    gt = gate_ref[0].T
    y = jnp.concatenate([y[k * LANES:(k + 1) * LANES] * gt[:, k:k + 1] for k in range(gt.shape[1])], axis=0)
    y_ref[0] = _pack_halves(y)


def _moe_experts(xs, gate, wg, wu, wd, bsz):
    npairs, cap, half = xs.shape
    ne, d, f = wg.shape
    pair = lambda b, e: (b * ne + e, 0, 0)
    expert = lambda b, e: (e, 0, 0)
    return pl.pallas_call(
        _moe1_body,
        grid=(bsz, ne),
        in_specs=[pl.BlockSpec((1, cap, half), pair),
                  pl.BlockSpec((1, cap // LANES, LANES), pair),
                  pl.BlockSpec((1, d, f), expert),
                  pl.BlockSpec((1, d, f), expert),
                  pl.BlockSpec((1, f, d), expert)],
        out_specs=pl.BlockSpec((1, cap, half), pair),
        out_shape=jax.ShapeDtypeStruct((npairs, cap, half), I32),
        compiler_params=_params(("arbitrary", "arbitrary")),
        name="moe_experts",
    )(xs, gate, wg, wu, wd)


def _combine_body(abase_ref, ys_hbm, tok_ref, x1_ref, g2_ref, fw_ref, o_ref, buf_ref, acc_ref, sem_ref, cnt_ref, *, na):
    n = ONEHOT_BLK
    ring = buf_ref.shape[0]
    nch = na // n
    b, i = pl.program_id(0), pl.program_id(1)

    @pl.when(i == 0)
    def _():
        cnt_ref[0] = 0
        cnt_ref[1] = 0

    def copy(c):
        slot = c % ring
        row0 = pl.multiple_of(b * na + c * n, n)
        return pltpu.make_async_copy(ys_hbm.at[pl.ds(row0, n)], buf_ref.at[slot], sem_ref.at[slot])

    for j in range(x1_ref.shape[1] // n):
        blk = i * (x1_ref.shape[1] // n) + j
        lo = abase_ref[b, blk]
        hi = abase_ref[b, blk + 1]
        c0 = lo // n
        c1 = jnp.where(hi > lo, (hi + n - 1) // n, c0)
        acc_ref[...] = jnp.zeros(acc_ref.shape, F32)
        tok_ids = lax.broadcasted_iota(I32, (n, n), 0) + blk * n

        def step(c, carry, tok_ids=tok_ids):
            started = cnt_ref[0]
            ahead = jnp.minimum(c + ring, nch)

            def start(k, _):
                copy(k).start()
                return 0

            lax.fori_loop(started, ahead, start, 0)
            cnt_ref[0] = jnp.maximum(started, ahead)

            @pl.when(cnt_ref[1] <= c)
            def _():
                copy(c).wait()
                cnt_ref[1] = c + 1

            seg = jnp.where(tok_ids == tok_ref[0, pl.ds(c, 1), :], 1.0, 0.0).astype(BF16)
            acc_ref[...] += _dot(seg, _unpack_halves(buf_ref[c % ring]))
            return carry

        lax.fori_loop(c0, c1, step, 0)
        rows = slice(j * n, (j + 1) * n)
        x2 = x1_ref[0, rows, :] + g2_ref[0] * acc_ref[...]
        o_ref[0, rows, :] = x2 * lax.rsqrt(jnp.mean(x2 * x2, axis=-1, keepdims=True) + NORM_EPS) * fw_ref[...]


def _combine_final(abase, ys, tok, x1, g2, fw, na):
    bsz, t, d = x1.shape
    n = ONEHOT_BLK
    tb = min(COMBINE_BLOCKS * n, t)
    grid_spec = pltpu.PrefetchScalarGridSpec(
        num_scalar_prefetch=1,
        grid=(bsz, t // tb),
        in_specs=[pl.BlockSpec(memory_space=pl.ANY),
                  pl.BlockSpec((1, na // n, n), lambda b, i, a: (b, 0, 0)),
                  pl.BlockSpec((1, tb, d), lambda b, i, a: (b, i, 0)),
                  pl.BlockSpec((1, 1, d), lambda b, i, a: (b, 0, 0)),
                  pl.BlockSpec((1, d), lambda b, i, a: (0, 0))],
        out_specs=pl.BlockSpec((1, tb, d), lambda b, i, a: (b, i, 0)),
        scratch_shapes=[pltpu.VMEM((COMBINE_RING, n, d // 2), I32), pltpu.VMEM((n, d), F32),
                        pltpu.SemaphoreType.DMA((COMBINE_RING,)), pltpu.SMEM((2,), I32)],
    )
    return pl.pallas_call(
        functools.partial(_combine_body, na=na),
        grid_spec=grid_spec,
        out_shape=jax.ShapeDtypeStruct((bsz, t, d), F32),
        compiler_params=_params(("arbitrary", "arbitrary")),
        name="combine_final",
    )(abase, ys, tok, x1, g2, fw)


def kernel(x, c, ctx, c_ctx, ada_w, ada_b, norm1_w, w_in, conv_w, conv_b, hg_lb_logits, ml_gate_b,
           hg_norm_w, ml_norm_w, w_out, norm2_w, router_w, exp_w_gate, exp_w_up, exp_w_down, final_norm_w):
    bsz, t, d = x.shape
    nctx = ctx.shape[1]
    assert ada_w.shape[0] == 1, "single-layer block"
    assert nctx == SCAN_STEP and t % SCAN_STEP == 0 and bsz + 1 <= SUBLANES
    tall = t + nctx
    nx = t // SCAN_STEP
    kw = HEADS_W
    ne = router_w.shape[-1]
    cap = EC_CAPACITY * t // ne
    tm = min(512, t)

    rows = jnp.concatenate([c, c_ctx[None], jnp.zeros((SUBLANES - 1 - bsz, d), F32)], axis=0)
    mod = _modulation(rows, ada_w[0], ada_b[0][None])
    mx = [m[:, None, :] for m in jnp.split(mod[:bsz], 6, axis=-1)]
    mc = [m[:, None, :] for m in jnp.split(mod[bsz:bsz + 1], 6, axis=-1)]
    sh1, sc1, g1, sh2, sc2, g2 = mx
    csh1, csc1 = mc[0], mc[1]

    main_w = 9 * kw
    w_main = w_in[0][:, :main_w].astype(BF16)
    w_gt = w_in[0][:, main_w:].T
    gate_b = ml_gate_b[0][:, None]
    nw1 = norm1_w[0][None]
    outs = _inproj(x, sc1, sh1, nw1, w_main, w_gt, gate_b, hg_lb_logits, tall, min(2 * tm, t), 0)
    outs = _inproj(ctx, csc1, csh1, nw1, w_main, w_gt, gate_b, hg_lb_logits, tall, nctx, t // nctx, prev=outs)
    hgq, hgv, hgg, hgk, hglf, mlqk_pre, mlv, mlo, gates = outs

    mlqk = _conv(mlqk_pre, conv_w[0], conv_b[0][None], t, nctx)
    hg_f, hg_b, ml_f, ml_b = _scans(hgq, hgk, hgv, hglf, mlqk, mlv, gates, nx)

    x1, vpk, aff = _mixer_out(hg_f, hg_b, ml_f, ml_b, hgg, mlo, x, g1, sc2, sh2,
                             hg_norm_w[0][None], ml_norm_w[0][None], w_out[0].astype(BF16),
                             norm2_w[0][None], router_w[0].T, min(2 * tm, t))

    pos4, base = _topk(aff.reshape(bsz, ne, t // LANES, LANES), cap)
    na = ne * cap
    pos_flat = pos4.reshape(-1)
    xs, gate = _sc_invert_gather(vpk.reshape(bsz * t, d // 2), pos_flat, aff.reshape(-1), bsz * ne, ne, t, cap)
    perm, tok = _sc_assignment_order(pos_flat, base.reshape(-1), bsz, ne, t, cap)
    y = _moe_experts(xs.reshape(bsz * ne, cap, d // 2), gate.reshape(bsz * ne, cap // LANES, LANES),
                     exp_w_gate[0], exp_w_up[0], exp_w_down[0], bsz)
    ys = _sc_gather_rows(y.reshape(bsz * na, d // 2), perm)
    abase = jnp.concatenate([base.reshape(bsz, t)[:, ::ONEHOT_BLK], jnp.full((bsz, 1), na, I32)], axis=1)
    return _combine_final(abase, ys, tok.reshape(bsz, na // ONEHOT_BLK, ONEHOT_BLK), x1, g2, final_norm_w[None], na)
```
